```python
import jax, jax.numpy as jnp
from jax import lax
import numpy as np

D_MODEL = 2048
BATCH = 4
SEQ = 2048
DEPTH = 1
DEC_BATCH = 128
DEC_SEQ = 1
PAST_LEN = 16384
PAGE_SIZE = 128

D_POOL = D_MODEL // 2
POOL_WINDOWS = (2, 4, 8, 16)
N_POOL_GROUPS = len(POOL_WINDOWS)
POOL_GROUP = D_POOL // N_POOL_GROUPS
D_SSM = D_MODEL // 2
SSM_GROUP = 16
N_SSM_GROUPS = D_SSM // SSM_GROUP
SSM_STATE = 64
DT_MIN = 1e-3
DT_MAX = 1e-1
D_FF = 3 * D_MODEL
CONV_W = 3
D_PLE = 256
D_IN = D_POOL + D_SSM + 2 * D_MODEL
EPS = 1e-6

kernel_name = 'pool_s5_gated_hybrid_step'


def _rmsnorm(x, g):
    xf = x.astype(jnp.float32)
    y = xf * lax.rsqrt(jnp.mean(xf * xf, axis=-1, keepdims=True) + EPS)
    return (y * g.astype(jnp.float32)).astype(x.dtype)


def _causal_pool(u, past, past_valid, w_pool, pool_scale):
    n, t, _ = u.shape
    lb = past.shape[1]
    ext = jnp.concatenate([past.astype(u.dtype), u], axis=1)
    extf = ext.astype(jnp.float32)
    valid = jnp.concatenate([jnp.full((lb,), past_valid, jnp.float32),
                             jnp.ones((t,), jnp.float32)])
    cc = jnp.concatenate([jnp.zeros((1,), jnp.float32), jnp.cumsum(valid)])
    hi = lb + 1
    outs = []
    for k, w in enumerate(POOL_WINDOWS):
        seg = extf[:, :, k * POOL_GROUP:(k + 1) * POOL_GROUP]
        cs = jnp.pad(jnp.cumsum(seg, axis=1), ((0, 0), (1, 0), (0, 0)))
        lo = lb + 1 - w
        win_sum = cs[:, hi:hi + t] - cs[:, lo:lo + t]
        count = (cc[hi:hi + t] - cc[lo:lo + t])[None, :, None]
        diff = win_sum / count - seg[:, lb:]
        outs.append(jnp.einsum('ntc,cd->ntd', diff.astype(u.dtype), w_pool[k]))
    y = jnp.concatenate(outs, axis=-1) * pool_scale
    return y, ext[:, -lb:]


def _causal_dwconv(a, past, w_conv, b_conv):
    t = a.shape[1]
    lb = past.shape[1]
    ext = jnp.concatenate([past.astype(a.dtype), a], axis=1)
    y = b_conv
    for j in range(CONV_W):
        y = y + w_conv[j] * ext[:, j:j + t]
    return y, ext[:, -lb:]


def _cmul_combine(e1, e2):
    a1r, a1i, b1r, b1i = e1
    a2r, a2i, b2r, b2i = e2
    ar = a2r * a1r - a2i * a1i
    ai = a2r * a1i + a2i * a1r
    br = a2r * b1r - a2i * b1i + b2r
    bi = a2r * b1i + a2i * b1r + b2i
    return (ar, ai, br, bi)


def _s5(u, h0_re, h0_im, lam_re, lam_im, log_dt, b_re, b_im, c_re, c_im, d_skip):
    n, t, _ = u.shape
    uf = u.astype(jnp.float32).reshape(n, t, N_SSM_GROUPS, SSM_GROUP)
    dt = jnp.exp(log_dt.astype(jnp.float32))[:, None]
    lr = lam_re.astype(jnp.float32)
    li = lam_im.astype(jnp.float32)
    mag = jnp.exp(lr * dt)
    abar_re = mag * jnp.cos(li * dt)
    abar_im = mag * jnp.sin(li * dt)
    nr = abar_re - 1.0
    ni = abar_im
    den = lr * lr + li * li
    coef_re = ((nr * lr + ni * li) / den)[..., None]
    coef_im = ((ni * lr - nr * li) / den)[..., None]
    br = b_re.astype(jnp.float32)
    bi = b_im.astype(jnp.float32)
    bbar_re = coef_re * br - coef_im * bi
    bbar_im = coef_re * bi + coef_im * br
    bu_re = jnp.einsum('ntgh,gph->ntgp', uf, bbar_re)
    bu_im = jnp.einsum('ntgh,gph->ntgp', uf, bbar_im)
    a_re = jnp.broadcast_to(abar_re, bu_re.shape)
    a_im = jnp.broadcast_to(abar_im, bu_re.shape)
    acr, aci, hzr, hzi = lax.associative_scan(_cmul_combine, (a_re, a_im, bu_re, bu_im), axis=1)
    h0r = h0_re.astype(jnp.float32)[:, None]
    h0i = h0_im.astype(jnp.float32)[:, None]
    h_re = acr * h0r - aci * h0i + hzr
    h_im = acr * h0i + aci * h0r + hzi
    y = (jnp.einsum('ntgp,ghp->ntgh', h_re, c_re.astype(jnp.float32))
         - jnp.einsum('ntgp,ghp->ntgh', h_im, c_im.astype(jnp.float32))
         + d_skip.astype(jnp.float32).reshape(N_SSM_GROUPS, SSM_GROUP) * uf)
    return y.reshape(n, t, D_SSM).astype(u.dtype), h_re[:, -1], h_im[:, -1]


def _layer(x, p, pool_past, pool_valid, h0_re, h0_im, conv_past,
           g_mix, w_in, w_pool, pool_scale, lam_re, lam_im, log_dt, b_re, b_im,
           c_re, c_im, d_skip, w_glu, w_branch_pool, w_branch_ssm, w_out,
           g_ffn, w_up, w_conv, b_conv, w_down, g_ple, w_ple_gate, w_ple):
    h = _rmsnorm(x, g_mix)
    z = h @ w_in
    u_pool, u_ssm, gate_pool, gate_ssm = jnp.split(
        z, [D_POOL, D_POOL + D_SSM, D_POOL + D_SSM + D_MODEL], axis=-1)
    y_pool, pool_new = _causal_pool(u_pool, pool_past, pool_valid, w_pool, pool_scale)
    y_ssm, h_re, h_im = _s5(u_ssm, h0_re, h0_im, lam_re, lam_im, log_dt,
                            b_re, b_im, c_re, c_im, d_skip)
    g = jax.nn.gelu(y_ssm) @ w_glu
    y_ssm = g[..., :D_SSM] * jax.nn.sigmoid(g[..., D_SSM:])
    merged = (jax.nn.sigmoid(gate_pool) * (y_pool @ w_branch_pool)
              + jax.nn.sigmoid(gate_ssm) * (y_ssm @ w_branch_ssm))
    x = x + merged @ w_out
    h = _rmsnorm(x, g_ffn)
    up = h @ w_up
    a, v = jnp.split(up, [D_FF], axis=-1)
    a, conv_new = _causal_dwconv(a, conv_past, w_conv, b_conv)
    x = x + (jax.nn.gelu(a) * v) @ w_down
    gate = jax.nn.sigmoid(_rmsnorm(x, g_ple) @ w_ple_gate)
    x = x + gate * (p.astype(x.dtype) @ w_ple)
    return x, pool_new, h_re.astype(x.dtype), h_im.astype(x.dtype), conv_new


def setup_inputs(seed: int = 0) -> dict:
    key = jax.random.key(seed)
    ks = jax.random.split(key, 40)
    f32 = jnp.float32

    def nrm(k, shape, scale):
        return jax.random.normal(k, shape, f32) * scale

    pool_buf = min(max(POOL_WINDOWS) - 1, PAST_LEN)
    conv_buf = min(CONV_W - 1, PAST_LEN)
    n_idx = jnp.arange(SSM_STATE, dtype=f32)
    lam_re = -0.5 + nrm(ks[12], (DEPTH, N_SSM_GROUPS, SSM_STATE), 0.01)
    lam_im = jnp.pi * n_idx + nrm(ks[13], (DEPTH, N_SSM_GROUPS, SSM_STATE), 0.01)
    log_dt = jax.random.uniform(ks[14], (DEPTH, N_SSM_GROUPS), f32,
                                float(np.log(DT_MIN)), float(np.log(DT_MAX)))
    return {
        'x_prompt': nrm(ks[0], (BATCH, SEQ, D_MODEL), 1.0),
        'x_sample': nrm(ks[1], (DEC_BATCH, DEC_SEQ, D_MODEL), 1.0),
        'cache_pool': nrm(ks[2], (DEPTH, DEC_BATCH, pool_buf, D_POOL), 1.0),
        'state_ssm_re': nrm(ks[3], (DEPTH, DEC_BATCH, N_SSM_GROUPS, SSM_STATE), 0.1),
        'state_ssm_im': nrm(ks[4], (DEPTH, DEC_BATCH, N_SSM_GROUPS, SSM_STATE), 0.1),
        'cache_conv': nrm(ks[5], (DEPTH, DEC_BATCH, conv_buf, D_FF), 1.0),
        'p_prompt': nrm(ks[6], (DEPTH, BATCH, SEQ, D_PLE), 1.0),
        'p_sample': nrm(ks[7], (DEPTH, DEC_BATCH, DEC_SEQ, D_PLE), 1.0),
        'g_mix': 1.0 + nrm(ks[8], (DEPTH, D_MODEL), 0.02),
        'w_in': nrm(ks[9], (DEPTH, D_MODEL, D_IN), D_MODEL ** -0.5),
        'w_pool': nrm(ks[10], (DEPTH, N_POOL_GROUPS, POOL_GROUP, POOL_GROUP), POOL_GROUP ** -0.5),
        'pool_scale': 1.0 + nrm(ks[11], (DEPTH, D_POOL), 0.02),
        'ssm_lam_re': lam_re,
        'ssm_lam_im': lam_im,
        'ssm_log_dt': log_dt,
        'ssm_b_re': nrm(ks[15], (DEPTH, N_SSM_GROUPS, SSM_STATE, SSM_GROUP), (2 * SSM_GROUP) ** -0.5),
        'ssm_b_im': nrm(ks[16], (DEPTH, N_SSM_GROUPS, SSM_STATE, SSM_GROUP), (2 * SSM_GROUP) ** -0.5),
        'ssm_c_re': nrm(ks[17], (DEPTH, N_SSM_GROUPS, SSM_GROUP, SSM_STATE), SSM_STATE ** -0.5),
        'ssm_c_im': nrm(ks[18], (DEPTH, N_SSM_GROUPS, SSM_GROUP, SSM_STATE), SSM_STATE ** -0.5),
        'ssm_d': nrm(ks[19], (DEPTH, D_SSM), 1.0),
        'w_glu': nrm(ks[20], (DEPTH, D_SSM, 2 * D_SSM), D_SSM ** -0.5),
        'w_branch_pool': nrm(ks[21], (DEPTH, D_POOL, D_MODEL), D_POOL ** -0.5),
        'w_branch_ssm': nrm(ks[22], (DEPTH, D_SSM, D_MODEL), D_SSM ** -0.5),
        'w_out': nrm(ks[23], (DEPTH, D_MODEL, D_MODEL), D_MODEL ** -0.5),
        'g_ffn': 1.0 + nrm(ks[24], (DEPTH, D_MODEL), 0.02),
        'w_up': nrm(ks[25], (DEPTH, D_MODEL, 2 * D_FF), D_MODEL ** -0.5),
        'w_conv': nrm(ks[26], (DEPTH, CONV_W, D_FF), CONV_W ** -0.5),
        'b_conv': nrm(ks[27], (DEPTH, D_FF), 0.01),
        'w_down': nrm(ks[28], (DEPTH, D_FF, D_MODEL), D_FF ** -0.5),
        'g_ple': 1.0 + nrm(ks[29], (DEPTH, D_MODEL), 0.02),
        'w_ple_gate': nrm(ks[30], (DEPTH, D_MODEL, D_MODEL), D_MODEL ** -0.5),
        'w_ple': nrm(ks[31], (DEPTH, D_PLE, D_MODEL), D_PLE ** -0.5),
        'g_final': 1.0 + nrm(ks[32], (D_MODEL,), 0.02),
    }


def reference(x_prompt, x_sample, cache_pool, state_ssm_re, state_ssm_im, cache_conv,
              p_prompt, p_sample, g_mix, w_in, w_pool, pool_scale, ssm_lam_re, ssm_lam_im,
              ssm_log_dt, ssm_b_re, ssm_b_im, ssm_c_re, ssm_c_im, ssm_d, w_glu,
              w_branch_pool, w_branch_ssm, w_out, g_ffn, w_up, w_conv, b_conv, w_down,
              g_ple, w_ple_gate, w_ple, g_final):
    xp, xs = x_prompt, x_sample
    nb = x_prompt.shape[0]
    pool_buf = cache_pool.shape[2]
    conv_buf = cache_conv.shape[2]
    pool_p, re_p, im_p, conv_p = [], [], [], []
    pool_s, re_s, im_s, conv_s = [], [], [], []
    for i in range(DEPTH):
        lw = (g_mix[i], w_in[i], w_pool[i], pool_scale[i], ssm_lam_re[i], ssm_lam_im[i],
              ssm_log_dt[i], ssm_b_re[i], ssm_b_im[i], ssm_c_re[i], ssm_c_im[i], ssm_d[i],
              w_glu[i], w_branch_pool[i], w_branch_ssm[i], w_out[i], g_ffn[i], w_up[i],
              w_conv[i], b_conv[i], w_down[i], g_ple[i], w_ple_gate[i], w_ple[i])
        xp, a0, a1, a2, a3 = _layer(
            xp, p_prompt[i],
            jnp.zeros((nb, pool_buf, D_POOL), xp.dtype), 0.0,
            jnp.zeros((nb, N_SSM_GROUPS, SSM_STATE), jnp.float32),
            jnp.zeros((nb, N_SSM_GROUPS, SSM_STATE), jnp.float32),
            jnp.zeros((nb, conv_buf, D_FF), xp.dtype), *lw)
        pool_p.append(a0); re_p.append(a1); im_p.append(a2); conv_p.append(a3)
        xs, b0, b1, b2, b3 = _layer(
            xs, p_sample[i], cache_pool[i], 1.0, state_ssm_re[i], state_ssm_im[i],
            cache_conv[i], *lw)
        pool_s.append(b0); re_s.append(b1); im_s.append(b2); conv_s.append(b3)
    y_prompt = _rmsnorm(xp, g_final)
    y_sample = _rmsnorm(xs, g_final)
    return (y_prompt, y_sample,
            jnp.stack(pool_p, axis=0), jnp.stack(re_p, axis=0), jnp.stack(im_p, axis=0),
            jnp.stack(conv_p, axis=0),
            jnp.stack(pool_s, axis=0), jnp.stack(re_s, axis=0), jnp.stack(im_s, axis=0),
            jnp.stack(conv_s, axis=0))
```

```python
import functools

import jax
import jax.numpy as jnp
from jax import lax
from jax.experimental import pallas as pl
from jax.experimental.pallas import tpu as pltpu

F32 = jnp.float32
BF16 = jnp.bfloat16

EPS = 1e-6
POOL_WINDOWS = (2, 4, 8, 16)
POOL_HALO = 16
SSM_GROUP = 16
SSM_STATE = 64
SSM_BLOCK_GROUPS = 16
SCAN_ROWS = 8
CONV_W = 3
CONV_HALO = 8
V7X_VMEM_BYTES = 64 * 1024 * 1024


def _vmem_limit(nbytes):
    return int(min(nbytes * 1.25 + (8 << 20), V7X_VMEM_BYTES - (6 << 20)))


def _params(sem, nbytes):
    return pltpu.CompilerParams(dimension_semantics=sem, vmem_limit_bytes=_vmem_limit(nbytes))


def _rmsnorm(x, g):
    return x * lax.rsqrt(jnp.mean(x * x, axis=-1, keepdims=True) + EPS) * g


def _dot(a, b):
    return jnp.dot(a, b, preferred_element_type=F32)


def _ssm_params_kernel(lr_ref, li_ref, logdt_ref, br_ref, bi_ref, pwr_ref, pwi_ref, bbr_ref, bbi_ref):
    lr = lr_ref[...]
    li = li_ref[...]
    dt = jnp.exp(logdt_ref[...])
    mag = jnp.exp(lr * dt)
    a_re = mag * jnp.cos(li * dt)
    a_im = mag * jnp.sin(li * dt)
    nr = a_re - 1.0
    ni = a_im
    den = lr * lr + li * li
    coef_re = (nr * lr + ni * li) / den
    coef_im = (ni * lr - nr * li) / den
    br = br_ref[...]
    bi = bi_ref[...]
    bbr_ref[...] = coef_re[None] * br - coef_im[None] * bi
    bbi_ref[...] = coef_re[None] * bi + coef_im[None] * br
    pr, pi = a_re, a_im
    pwr_ref[0] = pr
    pwi_ref[0] = pi
    for n in range(1, SCAN_ROWS):
        pr, pi = pr * a_re - pi * a_im, pr * a_im + pi * a_re
        pwr_ref[n] = pr
        pwi_ref[n] = pi


def _ssm_params(lam_re, lam_im, log_dt, b_re, b_im, c_re, c_im):
    g, p = lam_re.shape
    h = b_re.shape[-1]
    nkb = g // SSM_BLOCK_GROUPS
    bl = SSM_BLOCK_GROUPS
    pwr, pwi, bbr, bbi = pl.pallas_call(
        _ssm_params_kernel,
        out_shape=(jax.ShapeDtypeStruct((SCAN_ROWS, g, p), F32),
                   jax.ShapeDtypeStruct((SCAN_ROWS, g, p), F32),
                   jax.ShapeDtypeStruct((h, g, p), F32),
                   jax.ShapeDtypeStruct((h, g, p), F32)),
        name="ssm_params",
    )(lam_re, lam_im, log_dt.reshape(g, 1), jnp.transpose(b_re, (2, 0, 1)), jnp.transpose(b_im, (2, 0, 1)))

    eye = jnp.eye(bl, dtype=F32)
    bb = jnp.stack([bbr, bbi]).reshape(2, h, nkb, bl, p)
    b_cat = jnp.einsum('shkgp,gj->kghsjp', bb, eye).reshape(nkb, bl * h, 2 * bl * p).astype(BF16)
    cc = jnp.stack([c_re, -c_im]).reshape(2, nkb, bl, h, p)
    c_cat = jnp.einsum('skghp,gj->ksgpjh', cc, eye).reshape(nkb, 2 * bl * p, bl * h).astype(BF16)
    pw = jnp.stack([pwr, pwi], axis=1).reshape(SCAN_ROWS, 2, nkb, bl * p)
    pw = jnp.transpose(pw, (2, 0, 1, 3)).reshape(nkb, SCAN_ROWS, 2 * bl * p)
    rows = jnp.arange(SCAN_ROWS)[None, :, None]
    tbl = [jnp.where(rows >= k, pw[:, k - 1:k, :], 0.0) for k in (1, 2, 4)]
    tbl.append(pw)
    return b_cat, c_cat, jnp.concatenate(tbl, axis=1)


def _norm_matmul_kernel(x_ref, g_ref, w_ref, o_ref, h_ref):
    @pl.when(pl.program_id(1) == 0)
    def _():
        h_ref[...] = _rmsnorm(x_ref[...], g_ref[...]).astype(BF16)
    o_ref[...] = _dot(h_ref[...], w_ref[...])


def _norm_matmul(x, g, w, tm, tn):
    n, d = x.shape
    dout = w.shape[1]
    nbytes = 2 * tm * d * 4 + tm * d * 2 + 2 * d * tn * 2 + 2 * tm * tn * 4
    return pl.pallas_call(
        _norm_matmul_kernel,
        grid=(n // tm, dout // tn),
        in_specs=[pl.BlockSpec((tm, d), lambda i, j: (i, 0)),
                  pl.BlockSpec((1, d), lambda i, j: (0, 0)),
                  pl.BlockSpec((d, tn), lambda i, j: (0, j))],
        out_specs=pl.BlockSpec((tm, tn), lambda i, j: (i, j)),
        out_shape=jax.ShapeDtypeStruct((n, dout), F32),
        scratch_shapes=[pltpu.VMEM((tm, d), BF16)],
        compiler_params=_params(("arbitrary", "arbitrary"), nbytes),
        name="norm_matmul",
    )(x, g.reshape(1, d), w)


def _pool_project(diffs, wp_ref, scale_ref, wb_ref, y_ref):
    gw = wp_ref.shape[1]
    for k, diff in enumerate(diffs):
        yk = _dot(diff.astype(BF16), wp_ref[k]) * scale_ref[:, k * gw:(k + 1) * gw]
        y_ref[:, k * gw:(k + 1) * gw] = yk.astype(BF16)
    return _dot(y_ref[...], wb_ref[...])


def _pool_seq_kernel(u_ref, wp_ref, scale_ref, wb_ref, o_ref, new_ref, ext_ref, y_ref):
    tc, dp = u_ref.shape
    gw = wp_ref.shape[1]
    t = pl.program_id(1)

    @pl.when(t == 0)
    def _():
        ext_ref[0:POOL_HALO, :] = jnp.zeros((POOL_HALO, dp), F32)

    ext_ref[POOL_HALO:POOL_HALO + tc, :] = u_ref[...]
    pos = (t * tc + 1 + lax.broadcasted_iota(jnp.int32, (tc, 1), 0)).astype(F32)
    diffs = []
    for k, w in enumerate(POOL_WINDOWS):
        cols = slice(k * gw, (k + 1) * gw)
        u = ext_ref[POOL_HALO:POOL_HALO + tc, cols]
        s = u
        for j in range(1, w):
            s = s + ext_ref[POOL_HALO - j:POOL_HALO - j + tc, cols]
        count = jnp.minimum(pos, float(w))
        diffs.append(s / count - u)
    o_ref[...] = _pool_project(diffs, wp_ref, scale_ref, wb_ref, y_ref)
    nb = new_ref.shape[1]
    new_ref[0] = ext_ref[POOL_HALO + tc - nb:POOL_HALO + tc, :]
    ext_ref[0:POOL_HALO, :] = ext_ref[tc:tc + POOL_HALO, :]


def _pool_seq(z, w_pool, pool_scale, w_branch, nb, t_len, tc, pool_buf):
    n = z.shape[0]
    ng, gw, _ = w_pool.shape
    dp = ng * gw
    dm = w_branch.shape[1]
    nt = t_len // tc
    nbytes = (2 * tc * dp * 4 + 2 * ng * gw * gw * 2 + 2 * dp * dm * 2 + 2 * tc * dm * 4
              + (tc + POOL_HALO) * dp * 4 + tc * dp * 2 + 4 * tc * gw * 4)
    return pl.pallas_call(
        _pool_seq_kernel,
        grid=(nb, nt),
        in_specs=[pl.BlockSpec((tc, dp), lambda b, t: (b * nt + t, 0)),
                  pl.BlockSpec((ng, gw, gw), lambda b, t: (0, 0, 0)),
                  pl.BlockSpec((1, dp), lambda b, t: (0, 0)),
                  pl.BlockSpec((dp, dm), lambda b, t: (0, 0))],
        out_specs=(pl.BlockSpec((tc, dm), lambda b, t: (b * nt + t, 0)),
                   pl.BlockSpec((1, pool_buf, dp), lambda b, t: (b, 0, 0))),
        out_shape=(jax.ShapeDtypeStruct((n, dm), F32),
                   jax.ShapeDtypeStruct((nb, pool_buf, dp), F32)),
        scratch_shapes=[pltpu.VMEM((tc + POOL_HALO, dp), F32), pltpu.VMEM((tc, dp), BF16)],
        compiler_params=_params(("arbitrary", "arbitrary"), nbytes),
        name="pool_seq",
    )(z, w_pool, pool_scale.reshape(1, dp), w_branch)


def _pool_step_kernel(u_ref, cache_ref, wp_ref, scale_ref, wb_ref, o_ref, new_ref, y_ref):
    dp = u_ref.shape[1]
    gw = wp_ref.shape[1]
    lb = cache_ref.shape[1] // dp
    diffs = []
    for k, w in enumerate(POOL_WINDOWS):
        u = u_ref[:, k * gw:(k + 1) * gw]
        s = u
        for j in range(1, w):
            s = s + cache_ref[:, (lb - j) * dp + k * gw:(lb - j) * dp + (k + 1) * gw]
        diffs.append(s / float(w) - u)
    o_ref[...] = _pool_project(diffs, wp_ref, scale_ref, wb_ref, y_ref)
    new_ref[:, 0:(lb - 1) * dp] = cache_ref[:, dp:lb * dp]
    new_ref[:, (lb - 1) * dp:lb * dp] = u_ref[...]


def _pool_step(z, cache, w_pool, pool_scale, w_branch):
    n = z.shape[0]
    ng, gw, _ = w_pool.shape
    dp = ng * gw
    dm = w_branch.shape[1]
    lbdp = cache.shape[1]
    nbytes = 2 * (n * dp * 4 + 2 * n * lbdp * 4 + ng * gw * gw * 2 + dp * dm * 2 + n * dm * 4) + n * dp * 2
    return pl.pallas_call(
        _pool_step_kernel,
        grid=(1,),
        in_specs=[pl.BlockSpec((n, dp), lambda i: (0, 0)),
                  pl.BlockSpec((n, lbdp), lambda i: (0, 0)),
                  pl.BlockSpec((ng, gw, gw), lambda i: (0, 0, 0)),
                  pl.BlockSpec((1, dp), lambda i: (0, 0)),
                  pl.BlockSpec((dp, dm), lambda i: (0, 0))],
        out_specs=(pl.BlockSpec((n, dm), lambda i: (0, 0)),
                   pl.BlockSpec((n, lbdp), lambda i: (0, 0))),
        out_shape=(jax.ShapeDtypeStruct((n, dm), F32),
                   jax.ShapeDtypeStruct((n, lbdp), F32)),
        scratch_shapes=[pltpu.VMEM((n, dp), BF16)],
        compiler_params=_params(("arbitrary",), nbytes),
        name="pool_step",
    )(z, cache, w_pool, pool_scale.reshape(1, dp), w_branch)


def _cmul_add(xr, xi, ar, ai, br, bi):
    return xr + ar * br - ai * bi, xi + ar * bi + ai * br


def _ssm_seq_kernel(u_ref, bcat_ref, ccat_ref, tbl_ref, d_ref, gy_ref, st_ref, h_ref, carry_ref):
    tc = u_ref.shape[0]
    half = h_ref.shape[1] // 2
    t = pl.program_id(2)

    @pl.when(t == 0)
    def _():
        carry_ref[...] = jnp.zeros_like(carry_ref)

    u = u_ref[...]
    h_ref[...] = _dot(u.astype(BF16), bcat_ref[0])

    re = slice(0, half)
    im = slice(half, 2 * half)

    def tile_scan(i, carry):
        cr, ci = carry
        rows = pl.ds(pl.multiple_of(i * SCAN_ROWS, SCAN_ROWS), SCAN_ROWS)
        br = h_ref[rows, re]
        bi = h_ref[rows, im]
        for lvl, k in enumerate((1, 2, 4)):
            tr = pl.ds(lvl * SCAN_ROWS, SCAN_ROWS)
            br, bi = _cmul_add(br, bi, tbl_ref[0, tr, re], tbl_ref[0, tr, im],
                               pltpu.roll(br, k, 0), pltpu.roll(bi, k, 0))
        tr = pl.ds(3 * SCAN_ROWS, SCAN_ROWS)
        br, bi = _cmul_add(br, bi, tbl_ref[0, tr, re], tbl_ref[0, tr, im], cr, ci)
        h_ref[rows, re] = br
        h_ref[rows, im] = bi
        return br[SCAN_ROWS - 1:SCAN_ROWS, :], bi[SCAN_ROWS - 1:SCAN_ROWS, :]

    cr, ci = lax.fori_loop(0, tc // SCAN_ROWS, tile_scan, (carry_ref[:, re], carry_ref[:, im]))
    carry_ref[:, re] = cr
    carry_ref[:, im] = ci
    st_ref[0, 0] = carry_ref[...]
    y = _dot(h_ref[...].astype(BF16), ccat_ref[0]) + d_ref[...] * u
    gy_ref[...] = jax.nn.gelu(y).astype(BF16)


def _ssm_seq(z, col0, b_cat, c_cat, tbl, d_skip, nb, t_len, tc):
    n = z.shape[0]
    nkb, kw, sw = b_cat.shape
    nt = t_len // tc
    cb = col0 // kw
    nbytes = (2 * tc * kw * 4 + 4 * kw * sw * 2 + 2 * tbl.shape[1] * sw * 4 + 2 * tc * kw * 2
              + tc * sw * 4 + tc * sw * 2 + tc * sw * 4)
    gy, st = pl.pallas_call(
        _ssm_seq_kernel,
        grid=(nkb, nb, nt),
        in_specs=[pl.BlockSpec((tc, kw), lambda k, b, t: (b * nt + t, cb + k)),
                  pl.BlockSpec((1, kw, sw), lambda k, b, t: (k, 0, 0)),
                  pl.BlockSpec((1, sw, kw), lambda k, b, t: (k, 0, 0)),
                  pl.BlockSpec((1, tbl.shape[1], sw), lambda k, b, t: (k, 0, 0)),
                  pl.BlockSpec((1, kw), lambda k, b, t: (0, k))],
        out_specs=(pl.BlockSpec((tc, kw), lambda k, b, t: (b * nt + t, k)),
                   pl.BlockSpec((1, 1, 1, sw), lambda k, b, t: (b, k, 0, 0))),
        out_shape=(jax.ShapeDtypeStruct((n, nkb * kw), BF16),
                   jax.ShapeDtypeStruct((nb, nkb, 1, sw), F32)),
        scratch_shapes=[pltpu.VMEM((tc, sw), F32), pltpu.VMEM((1, sw), F32)],
        compiler_params=_params(("arbitrary", "arbitrary", "arbitrary"), nbytes),
        name="ssm_seq",
    )(z, b_cat, c_cat, tbl, d_skip.reshape(1, nkb * kw))
    st = st.reshape(nb, nkb, 2, sw // 2)
    return gy, st[:, :, 0].reshape(nb, -1), st[:, :, 1].reshape(nb, -1)


def _ssm_step_kernel(u_ref, h0r_ref, h0i_ref, bcat_ref, ccat_ref, tbl_ref, d_ref,
                     gy_ref, h1r_ref, h1i_ref, h_ref):
    half = h0r_ref.shape[1]
    u = u_ref[...]
    bu = _dot(u.astype(BF16), bcat_ref[0])
    a_row = pl.ds(3 * SCAN_ROWS, 1)
    hr, hi = _cmul_add(bu[:, 0:half], bu[:, half:2 * half],
                       tbl_ref[0, a_row, 0:half], tbl_ref[0, a_row, half:2 * half],
                       h0r_ref[...], h0i_ref[...])
    h1r_ref[...] = hr
    h1i_ref[...] = hi
    h_ref[:, 0:half] = hr.astype(BF16)
    h_ref[:, half:2 * half] = hi.astype(BF16)
    y = _dot(h_ref[...], ccat_ref[0]) + d_ref[...] * u
    gy_ref[...] = jax.nn.gelu(y).astype(BF16)


def _ssm_step(z, col0, h0_re, h0_im, b_cat, c_cat, tbl, d_skip):
    n = z.shape[0]
    nkb, kw, sw = b_cat.shape
    half = sw // 2
    cb = col0 // kw
    nbytes = 2 * (n * kw * 4 + 4 * n * half * 4 + 2 * kw * sw * 2 + tbl.shape[1] * sw * 4 + n * kw * 2) + n * sw * 6
    return pl.pallas_call(
        _ssm_step_kernel,
        grid=(nkb,),
        in_specs=[pl.BlockSpec((n, kw), lambda k: (0, cb + k)),
                  pl.BlockSpec((n, half), lambda k: (0, k)),
                  pl.BlockSpec((n, half), lambda k: (0, k)),
                  pl.BlockSpec((1, kw, sw), lambda k: (k, 0, 0)),
                  pl.BlockSpec((1, sw, kw), lambda k: (k, 0, 0)),
                  pl.BlockSpec((1, tbl.shape[1], sw), lambda k: (k, 0, 0)),
                  pl.BlockSpec((1, kw), lambda k: (0, k))],
        out_specs=(pl.BlockSpec((n, kw), lambda k: (0, k)),
                   pl.BlockSpec((n, half), lambda k: (0, k)),
                   pl.BlockSpec((n, half), lambda k: (0, k))),
        out_shape=(jax.ShapeDtypeStruct((n, nkb * kw), BF16),
                   jax.ShapeDtypeStruct((n, nkb * half), F32),
                   jax.ShapeDtypeStruct((n, nkb * half), F32)),
        scratch_shapes=[pltpu.VMEM((n, sw), BF16)],
        compiler_params=_params(("arbitrary",), nbytes),
        name="ssm_step",
    )(z, h0_re, h0_im, b_cat, c_cat, tbl, d_skip.reshape(1, nkb * kw))


def _glu_merge_kernel(gy_ref, bp_ref, gp_ref, gs_ref, wg_ref, wb_ref, o_ref):
    ds = wb_ref.shape[0]
    g = _dot(gy_ref[...], wg_ref[...])
    y = g[:, 0:ds] * jax.nn.sigmoid(g[:, ds:2 * ds])
    bs = _dot(y.astype(BF16), wb_ref[...])
    merged = jax.nn.sigmoid(gp_ref[...]) * bp_ref[...] + jax.nn.sigmoid(gs_ref[...]) * bs
    o_ref[...] = merged.astype(BF16)


def _glu_merge(gy, bp, z, gate_col0, w_glu, w_branch, tm):
    n, ds = gy.shape
    dm = w_branch.shape[1]
    gb = gate_col0 // dm
    nbytes = 2 * (tm * ds * 2 + 3 * tm * dm * 4 + ds * 2 * ds * 2 + ds * dm * 2 + tm * dm * 2) + 4 * tm * dm * 4
    return pl.pallas_call(
        _glu_merge_kernel,
        grid=(n // tm,),
        in_specs=[pl.BlockSpec((tm, ds), lambda i: (i, 0)),
                  pl.BlockSpec((tm, dm), lambda i: (i, 0)),
                  pl.BlockSpec((tm, dm), lambda i: (i, gb)),
                  pl.BlockSpec((tm, dm), lambda i: (i, gb + 1)),
                  pl.BlockSpec((ds, 2 * ds), lambda i: (0, 0)),
                  pl.BlockSpec((ds, dm), lambda i: (0, 0))],
        out_specs=pl.BlockSpec((tm, dm), lambda i: (i, 0)),
        out_shape=jax.ShapeDtypeStruct((n, dm), BF16),
        compiler_params=_params(("arbitrary",), nbytes),
        name="glu_merge",
    )(gy, bp, z, z, w_glu, w_branch)


def _matmul_res_kernel(m_ref, w_ref, x_ref, o_ref):
    o_ref[...] = x_ref[...] + _dot(m_ref[...], w_ref[...])


def _matmul_res(m, w, x, tm, tn):
    n, d = m.shape
    dout = w.shape[1]
    nbytes = 2 * (tm * d * 2 + d * tn * 2 + 2 * tm * tn * 4)
    return pl.pallas_call(
        _matmul_res_kernel,
        grid=(n // tm, dout // tn),
        in_specs=[pl.BlockSpec((tm, d), lambda i, j: (i, 0)),
                  pl.BlockSpec((d, tn), lambda i, j: (0, j)),
                  pl.BlockSpec((tm, tn), lambda i, j: (i, j))],
        out_specs=pl.BlockSpec((tm, tn), lambda i, j: (i, j)),
        out_shape=jax.ShapeDtypeStruct((n, dout), F32),
        compiler_params=_params(("arbitrary", "arbitrary"), nbytes),
        name="matmul_res",
    )(m, w, x)


def _ffn_gate_down(x_ref, conv, v, wd_ref, o_ref, acc_ref):
    c = pl.program_id(1)
    part = _dot((jax.nn.gelu(conv) * v).astype(BF16), wd_ref[...])

    @pl.when(c == 0)
    def _():
        acc_ref[...] = part

    @pl.when(c > 0)
    def _():
        acc_ref[...] += part

    @pl.when(c == pl.num_programs(1) - 1)
    def _():
        o_ref[...] = x_ref[...] + acc_ref[...]


def _ffn_seq_kernel(tiles_per_seq, x_ref, g_ref, wa_ref, wv_ref, wc_ref, bc_ref, wd_ref,
                    o_ref, new_ref, h_ref, acc_ref, ext_ref, carry_ref):
    tm = x_ref.shape[0]
    i = pl.program_id(0)
    c = pl.program_id(1)

    @pl.when(c == 0)
    def _():
        h_ref[...] = _rmsnorm(x_ref[...], g_ref[...]).astype(BF16)

    @pl.when(i % tiles_per_seq == 0)
    def _():
        carry_ref[c] = jnp.zeros(carry_ref.shape[1:], F32)

    a = _dot(h_ref[...], wa_ref[...])
    v = _dot(h_ref[...], wv_ref[...])
    ext_ref[0:CONV_HALO, :] = carry_ref[c]
    ext_ref[CONV_HALO:CONV_HALO + tm, :] = a
    conv = bc_ref[...] + wc_ref[CONV_W - 1:CONV_W, :] * a
    for j in range(CONV_W - 1):
        off = CONV_HALO - (CONV_W - 1) + j
        conv = conv + wc_ref[j:j + 1, :] * ext_ref[off:off + tm, :]
    carry_ref[c] = ext_ref[tm:tm + CONV_HALO, :]
    nb = new_ref.shape[1]
    new_ref[0] = ext_ref[CONV_HALO + tm - nb:CONV_HALO + tm, :]
    _ffn_gate_down(x_ref, conv, v, wd_ref, o_ref, acc_ref)


def _ffn_seq(x, g, w_up, w_conv, b_conv, w_down, nb, t_len, tm, tf, conv_buf):
    n, d = x.shape
    dff = w_down.shape[0]
    nc = dff // tf
    tps = t_len // tm
    nbytes = (4 * tm * d * 4 + 4 * d * tf * 2 + 2 * tf * d * 2 + tm * d * 2 + tm * d * 4
              + (tm + CONV_HALO) * tf * 4 + nc * CONV_HALO * tf * 4 + 4 * tm * tf * 4)
    out, new_tail = pl.pallas_call(
        functools.partial(_ffn_seq_kernel, tps),
        grid=(n // tm, nc),
        in_specs=[pl.BlockSpec((tm, d), lambda i, c: (i, 0)),
                  pl.BlockSpec((1, d), lambda i, c: (0, 0)),
                  pl.BlockSpec((d, tf), lambda i, c: (0, c)),
                  pl.BlockSpec((d, tf), lambda i, c: (0, nc + c)),
                  pl.BlockSpec((CONV_W, tf), lambda i, c: (0, c)),
                  pl.BlockSpec((1, tf), lambda i, c: (0, c)),
                  pl.BlockSpec((tf, d), lambda i, c: (c, 0))],
        out_specs=(pl.BlockSpec((tm, d), lambda i, c: (i, 0)),
                   pl.BlockSpec((1, conv_buf, tf), lambda i, c: (i, 0, c))),
        out_shape=(jax.ShapeDtypeStruct((n, d), F32),
                   jax.ShapeDtypeStruct((n // tm, conv_buf, dff), F32)),
        scratch_shapes=[pltpu.VMEM((tm, d), BF16), pltpu.VMEM((tm, d), F32),
                        pltpu.VMEM((tm + CONV_HALO, tf), F32), pltpu.VMEM((nc, CONV_HALO, tf), F32)],
        compiler_params=_params(("arbitrary", "arbitrary"), nbytes),
        name="ffn_seq",
    )(x, g.reshape(1, d), w_up, w_up, w_conv, b_conv.reshape(1, dff), w_down)
    return out, new_tail[tps - 1::tps]


def _ffn_step_kernel(x_ref, g_ref, wa_ref, wv_ref, wc_ref, bc_ref, wd_ref, p0_ref, p1_ref,
                     o_ref, a_ref, h_ref, acc_ref):
    @pl.when(pl.program_id(1) == 0)
    def _():
        h_ref[...] = _rmsnorm(x_ref[...], g_ref[...]).astype(BF16)

    a = _dot(h_ref[...], wa_ref[...])
    v = _dot(h_ref[...], wv_ref[...])
    conv = bc_ref[...] + wc_ref[2:3, :] * a + wc_ref[1:2, :] * p1_ref[...] + wc_ref[0:1, :] * p0_ref[...]
    a_ref[...] = a
    _ffn_gate_down(x_ref, conv, v, wd_ref, o_ref, acc_ref)


def _ffn_step(x, g, w_up, w_conv, b_conv, w_down, cache, tf):
    n, d = x.shape
    dff = w_down.shape[0]
    nc = dff // tf
    nbytes = 4 * n * d * 4 + 4 * d * tf * 2 + 2 * tf * d * 2 + n * d * 6 + 12 * n * tf * 4
    return pl.pallas_call(
        _ffn_step_kernel,
        grid=(1, nc),
        in_specs=[pl.BlockSpec((n, d), lambda i, c: (0, 0)),
                  pl.BlockSpec((1, d), lambda i, c: (0, 0)),
                  pl.BlockSpec((d, tf), lambda i, c: (0, c)),
                  pl.BlockSpec((d, tf), lambda i, c: (0, nc + c)),
                  pl.BlockSpec((CONV_W, tf), lambda i, c: (0, c)),
                  pl.BlockSpec((1, tf), lambda i, c: (0, c)),
                  pl.BlockSpec((tf, d), lambda i, c: (c, 0)),
                  pl.BlockSpec((n, tf), lambda i, c: (0, c)),
                  pl.BlockSpec((n, tf), lambda i, c: (0, nc + c))],
        out_specs=(pl.BlockSpec((n, d), lambda i, c: (0, 0)),
                   pl.BlockSpec((n, tf), lambda i, c: (0, c))),
        out_shape=(jax.ShapeDtypeStruct((n, d), F32),
                   jax.ShapeDtypeStruct((n, dff), F32)),
        scratch_shapes=[pltpu.VMEM((n, d), BF16), pltpu.VMEM((n, d), F32)],
        compiler_params=_params(("arbitrary", "arbitrary"), nbytes),
        name="ffn_step",
    )(x, g.reshape(1, d), w_up, w_up, w_conv, b_conv.reshape(1, dff), w_down, cache, cache)


def _ple_final_kernel(x_ref, p_ref, gp_ref, wg_ref, wp_ref, gf_ref, o_ref):
    x = x_ref[...]
    gate = jax.nn.sigmoid(_dot(_rmsnorm(x, gp_ref[...]).astype(BF16), wg_ref[...]))
    x = x + gate * _dot(p_ref[...].astype(BF16), wp_ref[...])
    o_ref[...] = _rmsnorm(x, gf_ref[...])


def _ple_final(x, p, g_ple, w_gate, w_ple, g_final, tm):
    n, d = x.shape
    dp = p.shape[1]
    nbytes = 2 * (2 * tm * d * 4 + tm * dp * 4 + d * d * 2 + dp * d * 2) + 4 * tm * d * 4
    return pl.pallas_call(
        _ple_final_kernel,
        grid=(n // tm,),
        in_specs=[pl.BlockSpec((tm, d), lambda i: (i, 0)),
                  pl.BlockSpec((tm, dp), lambda i: (i, 0)),
                  pl.BlockSpec((1, d), lambda i: (0, 0)),
                  pl.BlockSpec((d, d), lambda i: (0, 0)),
                  pl.BlockSpec((dp, d), lambda i: (0, 0)),
                  pl.BlockSpec((1, d), lambda i: (0, 0))],
        out_specs=pl.BlockSpec((tm, d), lambda i: (i, 0)),
        out_shape=jax.ShapeDtypeStruct((n, d), F32),
        compiler_params=_params(("arbitrary",), nbytes),
        name="ple_final",
    )(x, p, g_ple.reshape(1, d), w_gate, w_ple, g_final.reshape(1, d))


def _tile(n, pref):
    return pref if n % pref == 0 else n


def kernel(x_prompt, x_sample, cache_pool, state_ssm_re, state_ssm_im, cache_conv, p_prompt, p_sample, g_mix, w_in, w_pool, pool_scale, ssm_lam_re, ssm_lam_im, ssm_log_dt, ssm_b_re, ssm_b_im, ssm_c_re, ssm_c_im, ssm_d, w_glu, w_branch_pool, w_branch_ssm, w_out, g_ffn, w_up, w_conv, b_conv, w_down, g_ple, w_ple_gate, w_ple, g_final):
    depth = g_mix.shape[0]
    nb, t_len, d = x_prompt.shape
    ns = x_sample.shape[0]
    assert x_sample.shape[1] == 1, "the sample group advances one step per call"
    pool_buf, d_pool = cache_pool.shape[2], cache_pool.shape[3]
    conv_buf, d_ff = cache_conv.shape[2], cache_conv.shape[3]
    n_grp, n_state = ssm_lam_re.shape[1], ssm_lam_re.shape[2]
    d_ssm = ssm_d.shape[1]
    assert pool_buf == max(POOL_WINDOWS) - 1 and conv_buf == CONV_W - 1
    assert n_state == SSM_STATE and d_ssm == n_grp * SSM_GROUP

    xp = x_prompt.reshape(nb * t_len, d)
    xs = x_sample.reshape(ns, d)
    outs = [[] for _ in range(8)]
    for i in range(depth):
        w_in_b, w_pool_b, w_glu_b = w_in[i].astype(BF16), w_pool[i].astype(BF16), w_glu[i].astype(BF16)
        w_bp_b, w_bs_b, w_out_b = w_branch_pool[i].astype(BF16), w_branch_ssm[i].astype(BF16), w_out[i].astype(BF16)
        w_up_b, w_down_b = w_up[i].astype(BF16), w_down[i].astype(BF16)
        w_pg_b, w_ple_b = w_ple_gate[i].astype(BF16), w_ple[i].astype(BF16)
        b_cat, c_cat, tbl = _ssm_params(ssm_lam_re[i], ssm_lam_im[i], ssm_log_dt[i], ssm_b_re[i], ssm_b_im[i],
                                        ssm_c_re[i], ssm_c_im[i])

        def mix_tail(x, z, bp, gy, tm):
            merged = _glu_merge(gy, bp, z, d_pool + d_ssm, w_glu_b, w_bs_b, tm)
            return _matmul_res(merged, w_out_b, x, _tile(x.shape[0], 1024), 512)

        z = _norm_matmul(xp, g_mix[i], w_in_b, 1024, 512)
        bp, pool_new = _pool_seq(z, w_pool_b, pool_scale[i], w_bp_b, nb, t_len, 512, pool_buf)
        gy, st_re, st_im = _ssm_seq(z, d_pool, b_cat, c_cat, tbl, ssm_d[i], nb, t_len, 512)
        xp = mix_tail(xp, z, bp, gy, 256)
        xp, conv_new = _ffn_seq(xp, g_ffn[i], w_up_b, w_conv[i], b_conv[i], w_down_b, nb, t_len, 512, 512, conv_buf)
        xp_out = _ple_final(xp, p_prompt[i].reshape(nb * t_len, -1), g_ple[i], w_pg_b, w_ple_b, g_final, 256)
        for lst, val in zip(outs[:4], (pool_new, st_re.reshape(nb, n_grp, n_state),
                                       st_im.reshape(nb, n_grp, n_state), conv_new)):
            lst.append(val)

        z = _norm_matmul(xs, g_mix[i], w_in_b, ns, 512)
        bp, pool_new = _pool_step(z, cache_pool[i].reshape(ns, pool_buf * d_pool), w_pool_b, pool_scale[i], w_bp_b)
        gy, st_re, st_im = _ssm_step(z, d_pool, state_ssm_re[i].reshape(ns, -1), state_ssm_im[i].reshape(ns, -1),
                                     b_cat, c_cat, tbl, ssm_d[i])
        xs = mix_tail(xs, z, bp, gy, ns)
        xs, a_new = _ffn_step(xs, g_ffn[i], w_up_b, w_conv[i], b_conv[i], w_down_b,
                              cache_conv[i].reshape(ns, conv_buf * d_ff), 512)
        conv_new = jnp.concatenate([cache_conv[i][:, 1:], a_new[:, None, :]], axis=1)
        xs_out = _ple_final(xs, p_sample[i].reshape(ns, -1), g_ple[i], w_pg_b, w_ple_b, g_final, ns)
        for lst, val in zip(outs[4:], (pool_new.reshape(ns, pool_buf, d_pool), st_re.reshape(ns, n_grp, n_state),
                                       st_im.reshape(ns, n_grp, n_state), conv_new)):
            lst.append(val)

    assert depth == 1
    y_prompt = xp_out.reshape(nb, t_len, d)
    y_sample = xs_out.reshape(ns, 1, d)
    return (y_prompt, y_sample) + tuple(jnp.stack(o, axis=0) for o in outs)
```

```python
import functools

import jax
import jax.numpy as jnp
from jax import lax
from jax.experimental import pallas as pl
from jax.experimental.pallas import tpu as pltpu

F32 = jnp.float32
BF16 = jnp.bfloat16

EPS = 1e-6
POOL_WINDOWS = (2, 4, 8, 16)
POOL_HALO = 16
SSM_GROUP = 16
SSM_STATE = 64
SSM_BLOCK_GROUPS = 16
SCAN_ROWS = 8
CONV_W = 3
CONV_HALO = 8
V7X_VMEM_BYTES = 64 * 1024 * 1024


def _vmem_limit(nbytes):
    return int(min(nbytes * 1.25 + (8 << 20), V7X_VMEM_BYTES - (6 << 20)))


def _params(sem, nbytes):
    return pltpu.CompilerParams(dimension_semantics=sem, vmem_limit_bytes=_vmem_limit(nbytes))


def _rmsnorm(x, g):
    return x * lax.rsqrt(jnp.mean(x * x, axis=-1, keepdims=True) + EPS) * g


def _dot(a, b):
    return jnp.dot(a, b, preferred_element_type=F32)


def _ssm_params_kernel(lr_ref, li_ref, logdt_ref, br_ref, bi_ref, pwr_ref, pwi_ref, bbr_ref, bbi_ref):
    lr = lr_ref[...]
    li = li_ref[...]
    dt = jnp.exp(logdt_ref[...])
    mag = jnp.exp(lr * dt)
    a_re = mag * jnp.cos(li * dt)
    a_im = mag * jnp.sin(li * dt)
    nr = a_re - 1.0
    ni = a_im
    den = lr * lr + li * li
    coef_re = (nr * lr + ni * li) / den
    coef_im = (ni * lr - nr * li) / den
    br = br_ref[...]
    bi = bi_ref[...]
    bbr_ref[...] = coef_re[None] * br - coef_im[None] * bi
    bbi_ref[...] = coef_re[None] * bi + coef_im[None] * br
    pr, pi = a_re, a_im
    pwr_ref[0] = pr
    pwi_ref[0] = pi
    for n in range(1, SCAN_ROWS):
        pr, pi = pr * a_re - pi * a_im, pr * a_im + pi * a_re
        pwr_ref[n] = pr
        pwi_ref[n] = pi


def _ssm_params(lam_re, lam_im, log_dt, b_re, b_im, c_re, c_im):
    g, p = lam_re.shape
    h = b_re.shape[-1]
    nkb = g // SSM_BLOCK_GROUPS
    bl = SSM_BLOCK_GROUPS
    pwr, pwi, bbr, bbi = pl.pallas_call(
        _ssm_params_kernel,
        out_shape=(jax.ShapeDtypeStruct((SCAN_ROWS, g, p), F32),
                   jax.ShapeDtypeStruct((SCAN_ROWS, g, p), F32),
                   jax.ShapeDtypeStruct((h, g, p), F32),
                   jax.ShapeDtypeStruct((h, g, p), F32)),
        name="ssm_params",
    )(lam_re, lam_im, log_dt.reshape(g, 1), jnp.transpose(b_re, (2, 0, 1)), jnp.transpose(b_im, (2, 0, 1)))

    eye = jnp.eye(bl, dtype=F32)
    bb = jnp.stack([bbr, bbi]).reshape(2, h, nkb, bl, p)
    b_cat = jnp.einsum('shkgp,gj->kghsjp', bb, eye).reshape(nkb, bl * h, 2 * bl * p).astype(BF16)
    cc = jnp.stack([c_re, -c_im]).reshape(2, nkb, bl, h, p)
    c_cat = jnp.einsum('skghp,gj->ksgpjh', cc, eye).reshape(nkb, 2 * bl * p, bl * h).astype(BF16)
    pw = jnp.stack([pwr, pwi], axis=1).reshape(SCAN_ROWS, 2, nkb, bl * p)
    pw = jnp.transpose(pw, (2, 0, 1, 3)).reshape(nkb, SCAN_ROWS, 2 * bl * p)
    rows = jnp.arange(SCAN_ROWS)[None, :, None]
    tbl = [jnp.where(rows >= k, pw[:, k - 1:k, :], 0.0) for k in (1, 2, 4)]
    tbl.append(pw)
    return b_cat, c_cat, jnp.concatenate(tbl, axis=1)


def _norm_matmul_kernel(x_ref, g_ref, w_ref, o_ref, h_ref):
    @pl.when(pl.program_id(1) == 0)
    def _():
        h_ref[...] = _rmsnorm(x_ref[...], g_ref[...]).astype(BF16)
    o_ref[...] = _dot(h_ref[...], w_ref[...])


def _norm_matmul(x, g, w, tm, tn):
    n, d = x.shape
    dout = w.shape[1]
    nbytes = 2 * tm * d * 4 + tm * d * 2 + 2 * d * tn * 2 + 2 * tm * tn * 4
    return pl.pallas_call(
        _norm_matmul_kernel,
        grid=(n // tm, dout // tn),
        in_specs=[pl.BlockSpec((tm, d), lambda i, j: (i, 0)),
                  pl.BlockSpec((1, d), lambda i, j: (0, 0)),
                  pl.BlockSpec((d, tn), lambda i, j: (0, j))],
        out_specs=pl.BlockSpec((tm, tn), lambda i, j: (i, j)),
        out_shape=jax.ShapeDtypeStruct((n, dout), F32),
        scratch_shapes=[pltpu.VMEM((tm, d), BF16)],
        compiler_params=_params(("arbitrary", "arbitrary"), nbytes),
        name="norm_matmul",
    )(x, g.reshape(1, d), w)


def _pool_project(diffs, wp_ref, scale_ref, wb_ref, y_ref):
    gw = wp_ref.shape[1]
    for k, diff in enumerate(diffs):
        yk = _dot(diff.astype(BF16), wp_ref[k]) * scale_ref[:, k * gw:(k + 1) * gw]
        y_ref[:, k * gw:(k + 1) * gw] = yk.astype(BF16)
    return _dot(y_ref[...], wb_ref[...])


def _pool_seq_kernel(u_ref, wp_ref, scale_ref, wb_ref, o_ref, new_ref, ext_ref, y_ref):
    tc, dp = u_ref.shape
    gw = wp_ref.shape[1]
    t = pl.program_id(1)

    @pl.when(t == 0)
    def _():
        ext_ref[0:POOL_HALO, :] = jnp.zeros((POOL_HALO, dp), F32)

    ext_ref[POOL_HALO:POOL_HALO + tc, :] = u_ref[...]
    pos = (t * tc + 1 + lax.broadcasted_iota(jnp.int32, (tc, 1), 0)).astype(F32)
    diffs = []
    for k, w in enumerate(POOL_WINDOWS):
        cols = slice(k * gw, (k + 1) * gw)
        u = ext_ref[POOL_HALO:POOL_HALO + tc, cols]
        s = u
        for j in range(1, w):
            s = s + ext_ref[POOL_HALO - j:POOL_HALO - j + tc, cols]
        count = jnp.minimum(pos, float(w))
        diffs.append(s / count - u)
    o_ref[...] = _pool_project(diffs, wp_ref, scale_ref, wb_ref, y_ref)
    nb = new_ref.shape[1]
    new_ref[0] = ext_ref[POOL_HALO + tc - nb:POOL_HALO + tc, :]
    ext_ref[0:POOL_HALO, :] = ext_ref[tc:tc + POOL_HALO, :]


def _pool_seq(z, w_pool, pool_scale, w_branch, nb, t_len, tc, pool_buf):
    n = z.shape[0]
    ng, gw, _ = w_pool.shape
    dp = ng * gw
    dm = w_branch.shape[1]
    nt = t_len // tc
    nbytes = (2 * tc * dp * 4 + 2 * ng * gw * gw * 2 + 2 * dp * dm * 2 + 2 * tc * dm * 4
              + (tc + POOL_HALO) * dp * 4 + tc * dp * 2 + 4 * tc * gw * 4)
    return pl.pallas_call(
        _pool_seq_kernel,
        grid=(nb, nt),
        in_specs=[pl.BlockSpec((tc, dp), lambda b, t: (b * nt + t, 0)),
                  pl.BlockSpec((ng, gw, gw), lambda b, t: (0, 0, 0)),
                  pl.BlockSpec((1, dp), lambda b, t: (0, 0)),
                  pl.BlockSpec((dp, dm), lambda b, t: (0, 0))],
        out_specs=(pl.BlockSpec((tc, dm), lambda b, t: (b * nt + t, 0)),
                   pl.BlockSpec((1, pool_buf, dp), lambda b, t: (b, 0, 0))),
        out_shape=(jax.ShapeDtypeStruct((n, dm), F32),
                   jax.ShapeDtypeStruct((nb, pool_buf, dp), F32)),
        scratch_shapes=[pltpu.VMEM((tc + POOL_HALO, dp), F32), pltpu.VMEM((tc, dp), BF16)],
        compiler_params=_params(("arbitrary", "arbitrary"), nbytes),
        name="pool_seq",
    )(z, w_pool, pool_scale.reshape(1, dp), w_branch)


def _pool_step_kernel(u_ref, cache_ref, wp_ref, scale_ref, wb_ref, o_ref, new_ref, y_ref):
    dp = u_ref.shape[1]
    gw = wp_ref.shape[1]
    lb = cache_ref.shape[1] // dp
    diffs = []
    for k, w in enumerate(POOL_WINDOWS):
        u = u_ref[:, k * gw:(k + 1) * gw]
        s = u
        for j in range(1, w):
            s = s + cache_ref[:, (lb - j) * dp + k * gw:(lb - j) * dp + (k + 1) * gw]
        diffs.append(s / float(w) - u)
    o_ref[...] = _pool_project(diffs, wp_ref, scale_ref, wb_ref, y_ref)
    new_ref[:, 0:(lb - 1) * dp] = cache_ref[:, dp:lb * dp]
    new_ref[:, (lb - 1) * dp:lb * dp] = u_ref[...]


def _pool_step(z, cache, w_pool, pool_scale, w_branch):
    n = z.shape[0]
    ng, gw, _ = w_pool.shape
    dp = ng * gw
    dm = w_branch.shape[1]
    lbdp = cache.shape[1]
    nbytes = 2 * (n * dp * 4 + 2 * n * lbdp * 4 + ng * gw * gw * 2 + dp * dm * 2 + n * dm * 4) + n * dp * 2
    return pl.pallas_call(
        _pool_step_kernel,
        grid=(1,),
        in_specs=[pl.BlockSpec((n, dp), lambda i: (0, 0)),
                  pl.BlockSpec((n, lbdp), lambda i: (0, 0)),
                  pl.BlockSpec((ng, gw, gw), lambda i: (0, 0, 0)),
                  pl.BlockSpec((1, dp), lambda i: (0, 0)),
                  pl.BlockSpec((dp, dm), lambda i: (0, 0))],
        out_specs=(pl.BlockSpec((n, dm), lambda i: (0, 0)),
                   pl.BlockSpec((n, lbdp), lambda i: (0, 0))),
        out_shape=(jax.ShapeDtypeStruct((n, dm), F32),
                   jax.ShapeDtypeStruct((n, lbdp), F32)),
        scratch_shapes=[pltpu.VMEM((n, dp), BF16)],
        compiler_params=_params(("arbitrary",), nbytes),
        name="pool_step",
    )(z, cache, w_pool, pool_scale.reshape(1, dp), w_branch)


def _cmul_add(xr, xi, ar, ai, br, bi):
    return xr + ar * br - ai * bi, xi + ar * bi + ai * br


def _ssm_seq_kernel(u_ref, bcat_ref, ccat_ref, tbl_ref, d_ref, gy_ref, st_ref, h_ref, carry_ref):
    tc = u_ref.shape[0]
    half = h_ref.shape[1] // 2
    t = pl.program_id(2)

    @pl.when(t == 0)
    def _():
        carry_ref[...] = jnp.zeros_like(carry_ref)

    u = u_ref[...]
    h_ref[...] = _dot(u.astype(BF16), bcat_ref[0])

    re = slice(0, half)
    im = slice(half, 2 * half)

    def tile_scan(i, carry):
        cr, ci = carry
        rows = pl.ds(pl.multiple_of(i * SCAN_ROWS, SCAN_ROWS), SCAN_ROWS)
        br = h_ref[rows, re]
        bi = h_ref[rows, im]
        for lvl, k in enumerate((1, 2, 4)):
            tr = pl.ds(lvl * SCAN_ROWS, SCAN_ROWS)
            br, bi = _cmul_add(br, bi, tbl_ref[0, tr, re], tbl_ref[0, tr, im],
                               pltpu.roll(br, k, 0), pltpu.roll(bi, k, 0))
        tr = pl.ds(3 * SCAN_ROWS, SCAN_ROWS)
        br, bi = _cmul_add(br, bi, tbl_ref[0, tr, re], tbl_ref[0, tr, im], cr, ci)
        h_ref[rows, re] = br
        h_ref[rows, im] = bi
        return br[SCAN_ROWS - 1:SCAN_ROWS, :], bi[SCAN_ROWS - 1:SCAN_ROWS, :]

    cr, ci = lax.fori_loop(0, tc // SCAN_ROWS, tile_scan, (carry_ref[:, re], carry_ref[:, im]))
    carry_ref[:, re] = cr
    carry_ref[:, im] = ci
    st_ref[0, 0] = carry_ref[...]
    y = _dot(h_ref[...].astype(BF16), ccat_ref[0]) + d_ref[...] * u
    gy_ref[...] = jax.nn.gelu(y).astype(BF16)


def _ssm_seq(z, col0, b_cat, c_cat, tbl, d_skip, nb, t_len, tc):
    n = z.shape[0]
    nkb, kw, sw = b_cat.shape
    nt = t_len // tc
    cb = col0 // kw
    nbytes = (2 * tc * kw * 4 + 4 * kw * sw * 2 + 2 * tbl.shape[1] * sw * 4 + 2 * tc * kw * 2
              + tc * sw * 4 + tc * sw * 2 + tc * sw * 4)
    gy, st = pl.pallas_call(
        _ssm_seq_kernel,
        grid=(nkb, nb, nt),
        in_specs=[pl.BlockSpec((tc, kw), lambda k, b, t: (b * nt + t, cb + k)),
                  pl.BlockSpec((1, kw, sw), lambda k, b, t: (k, 0, 0)),
                  pl.BlockSpec((1, sw, kw), lambda k, b, t: (k, 0, 0)),
                  pl.BlockSpec((1, tbl.shape[1], sw), lambda k, b, t: (k, 0, 0)),
                  pl.BlockSpec((1, kw), lambda k, b, t: (0, k))],
        out_specs=(pl.BlockSpec((tc, kw), lambda k, b, t: (b * nt + t, k)),
                   pl.BlockSpec((1, 1, 1, sw), lambda k, b, t: (b, k, 0, 0))),
        out_shape=(jax.ShapeDtypeStruct((n, nkb * kw), BF16),
                   jax.ShapeDtypeStruct((nb, nkb, 1, sw), F32)),
        scratch_shapes=[pltpu.VMEM((tc, sw), F32), pltpu.VMEM((1, sw), F32)],
        compiler_params=_params(("arbitrary", "arbitrary", "arbitrary"), nbytes),
        name="ssm_seq",
    )(z, b_cat, c_cat, tbl, d_skip.reshape(1, nkb * kw))
    st = st.reshape(nb, nkb, 2, sw // 2)
    return gy, st[:, :, 0].reshape(nb, -1), st[:, :, 1].reshape(nb, -1)


def _ssm_step_kernel(u_ref, h0r_ref, h0i_ref, bcat_ref, ccat_ref, tbl_ref, d_ref,
                     gy_ref, h1r_ref, h1i_ref, h_ref):
    half = h0r_ref.shape[1]
    u = u_ref[...]
    bu = _dot(u.astype(BF16), bcat_ref[0])
    a_row = pl.ds(3 * SCAN_ROWS, 1)
    hr, hi = _cmul_add(bu[:, 0:half], bu[:, half:2 * half],
                       tbl_ref[0, a_row, 0:half], tbl_ref[0, a_row, half:2 * half],
                       h0r_ref[...], h0i_ref[...])
    h1r_ref[...] = hr
    h1i_ref[...] = hi
    h_ref[:, 0:half] = hr.astype(BF16)
    h_ref[:, half:2 * half] = hi.astype(BF16)
    y = _dot(h_ref[...], ccat_ref[0]) + d_ref[...] * u
    gy_ref[...] = jax.nn.gelu(y).astype(BF16)


def _ssm_step(z, col0, h0_re, h0_im, b_cat, c_cat, tbl, d_skip):
    n = z.shape[0]
    nkb, kw, sw = b_cat.shape
    half = sw // 2
    cb = col0 // kw
    nbytes = 2 * (n * kw * 4 + 4 * n * half * 4 + 2 * kw * sw * 2 + tbl.shape[1] * sw * 4 + n * kw * 2) + n * sw * 6
    return pl.pallas_call(
        _ssm_step_kernel,
        grid=(nkb,),
        in_specs=[pl.BlockSpec((n, kw), lambda k: (0, cb + k)),
                  pl.BlockSpec((n, half), lambda k: (0, k)),
                  pl.BlockSpec((n, half), lambda k: (0, k)),
                  pl.BlockSpec((1, kw, sw), lambda k: (k, 0, 0)),
                  pl.BlockSpec((1, sw, kw), lambda k: (k, 0, 0)),
                  pl.BlockSpec((1, tbl.shape[1], sw), lambda k: (k, 0, 0)),
                  pl.BlockSpec((1, kw), lambda k: (0, k))],
        out_specs=(pl.BlockSpec((n, kw), lambda k: (0, k)),
                   pl.BlockSpec((n, half), lambda k: (0, k)),
                   pl.BlockSpec((n, half), lambda k: (0, k))),
        out_shape=(jax.ShapeDtypeStruct((n, nkb * kw), BF16),
                   jax.ShapeDtypeStruct((n, nkb * half), F32),
                   jax.ShapeDtypeStruct((n, nkb * half), F32)),
        scratch_shapes=[pltpu.VMEM((n, sw), BF16)],
        compiler_params=_params(("arbitrary",), nbytes),
        name="ssm_step",
    )(z, h0_re, h0_im, b_cat, c_cat, tbl, d_skip.reshape(1, nkb * kw))


def _glu_merge_kernel(gy_ref, bp_ref, gp_ref, gs_ref, wg_ref, wb_ref, o_ref):
    ds = wb_ref.shape[0]
    g = _dot(gy_ref[...], wg_ref[...])
    y = g[:, 0:ds] * jax.nn.sigmoid(g[:, ds:2 * ds])
    bs = _dot(y.astype(BF16), wb_ref[...])
    merged = jax.nn.sigmoid(gp_ref[...]) * bp_ref[...] + jax.nn.sigmoid(gs_ref[...]) * bs
    o_ref[...] = merged.astype(BF16)


def _glu_merge(gy, bp, z, gate_col0, w_glu, w_branch, tm):
    n, ds = gy.shape
    dm = w_branch.shape[1]
    gb = gate_col0 // dm
    nbytes = 2 * (tm * ds * 2 + 3 * tm * dm * 4 + ds * 2 * ds * 2 + ds * dm * 2 + tm * dm * 2) + 4 * tm * dm * 4
    return pl.pallas_call(
        _glu_merge_kernel,
        grid=(n // tm,),
        in_specs=[pl.BlockSpec((tm, ds), lambda i: (i, 0)),
                  pl.BlockSpec((tm, dm), lambda i: (i, 0)),
                  pl.BlockSpec((tm, dm), lambda i: (i, gb)),
                  pl.BlockSpec((tm, dm), lambda i: (i, gb + 1)),
                  pl.BlockSpec((ds, 2 * ds), lambda i: (0, 0)),
                  pl.BlockSpec((ds, dm), lambda i: (0, 0))],
        out_specs=pl.BlockSpec((tm, dm), lambda i: (i, 0)),
        out_shape=jax.ShapeDtypeStruct((n, dm), BF16),
        compiler_params=_params(("arbitrary",), nbytes),
        name="glu_merge",
    )(gy, bp, z, z, w_glu, w_branch)


def _matmul_res_kernel(m_ref, w_ref, x_ref, o_ref):
    o_ref[...] = x_ref[...] + _dot(m_ref[...], w_ref[...])


def _matmul_res(m, w, x, tm, tn):
    n, d = m.shape
    dout = w.shape[1]
    nbytes = 2 * (tm * d * 2 + d * tn * 2 + 2 * tm * tn * 4)
    return pl.pallas_call(
        _matmul_res_kernel,
        grid=(n // tm, dout // tn),
        in_specs=[pl.BlockSpec((tm, d), lambda i, j: (i, 0)),
                  pl.BlockSpec((d, tn), lambda i, j: (0, j)),
                  pl.BlockSpec((tm, tn), lambda i, j: (i, j))],
        out_specs=pl.BlockSpec((tm, tn), lambda i, j: (i, j)),
        out_shape=jax.ShapeDtypeStruct((n, dout), F32),
        compiler_params=_params(("arbitrary", "arbitrary"), nbytes),
        name="matmul_res",
    )(m, w, x)


def _ffn_gate_down(x_ref, conv, v, wd_ref, o_ref, acc_ref):
    c = pl.program_id(1)
    part = _dot((jax.nn.gelu(conv) * v).astype(BF16), wd_ref[...])

    @pl.when(c == 0)
    def _():
        acc_ref[...] = part

    @pl.when(c > 0)
    def _():
        acc_ref[...] += part

    @pl.when(c == pl.num_programs(1) - 1)
    def _():
        o_ref[...] = x_ref[...] + acc_ref[...]


def _ffn_seq_kernel(nc, tiles_per_seq, n_steps, x_ref, g_ref, wa_ref, wv_ref, wc_ref, bc_ref, wd_ref,
                    o_ref, new_ref, h_ref, even_ref, odd_ref, ext_ref, carry_ref, gate_ref):
    tm = x_ref.shape[0]
    s = pl.program_id(0)
    sp = jnp.maximum(s - 1, 0)
    cp = sp % nc
    ip = sp // nc

    @pl.when(s == 0)
    def _():
        odd_ref[...] = jnp.zeros(odd_ref.shape, F32)
        o_ref[...] = jnp.zeros(o_ref.shape, F32)
        carry_ref[...] = jnp.zeros(carry_ref.shape, F32)

    @pl.when(jnp.logical_and(s % nc == 0, s < n_steps))
    def _():
        h_ref[...] = _rmsnorm(x_ref[...], g_ref[...]).astype(BF16)

    def step(cur_ref, prev_ref):
        cur_ref[0] = _dot(h_ref[...], wa_ref[...])
        cur_ref[1] = _dot(h_ref[...], wv_ref[...])

        a = prev_ref[0]
        v = prev_ref[1]
        seq_start = ip % tiles_per_seq == 0
        ext_ref[0:CONV_HALO, :] = jnp.where(seq_start, 0.0, carry_ref[cp])
        ext_ref[CONV_HALO:CONV_HALO + tm, :] = a
        conv = bc_ref[...] + wc_ref[CONV_W - 1:CONV_W, :] * a
        for j in range(CONV_W - 1):
            off = CONV_HALO - (CONV_W - 1) + j
            conv = conv + wc_ref[j:j + 1, :] * ext_ref[off:off + tm, :]
        carry_ref[cp] = ext_ref[tm:tm + CONV_HALO, :]
        nb = new_ref.shape[1]
        new_ref[0] = ext_ref[CONV_HALO + tm - nb:CONV_HALO + tm, :]
        gate_ref[...] = (jax.nn.gelu(conv) * v).astype(BF16)
        part = _dot(gate_ref[...], wd_ref[...])
        o_ref[...] = part + jnp.where(cp == 0, x_ref[...], o_ref[...])

    @pl.when(s % 2 == 0)
    def _():
        step(even_ref, odd_ref)

    @pl.when(s % 2 == 1)
    def _():
        step(odd_ref, even_ref)


def _ffn_seq(x, g, w_up, w_conv, b_conv, w_down, nb, t_len, tm, tf, conv_buf):
    n, d = x.shape
    dff = w_down.shape[0]
    nc = dff // tf
    tps = t_len // tm
    n_steps = (n // tm) * nc
    assert nc >= 2, "the residual tile must still be resident when its first chunk is finished"
    nbytes = (4 * tm * d * 4 + 4 * d * tf * 2 + 2 * tf * d * 2 + tm * d * 2 + 4 * tm * tf * 4
              + (tm + CONV_HALO) * tf * 4 + nc * CONV_HALO * tf * 4 + 4 * tm * tf * 4)

    def cur(s):
        return jnp.minimum(s, n_steps - 1)

    def prev(s):
        return jnp.maximum(s - 1, 0)

    out, new_tail = pl.pallas_call(
        functools.partial(_ffn_seq_kernel, nc, tps, n_steps),
        grid=(n_steps + 1,),
        in_specs=[pl.BlockSpec((tm, d), lambda s: (cur(s) // nc, 0)),
                  pl.BlockSpec((1, d), lambda s: (0, 0)),
                  pl.BlockSpec((d, tf), lambda s: (0, cur(s) % nc)),
                  pl.BlockSpec((d, tf), lambda s: (0, nc + cur(s) % nc)),
                  pl.BlockSpec((CONV_W, tf), lambda s: (0, prev(s) % nc)),
                  pl.BlockSpec((1, tf), lambda s: (0, prev(s) % nc)),
                  pl.BlockSpec((tf, d), lambda s: (prev(s) % nc, 0))],
        out_specs=(pl.BlockSpec((tm, d), lambda s: (prev(s) // nc, 0)),
                   pl.BlockSpec((1, conv_buf, tf), lambda s: (prev(s) // nc, 0, prev(s) % nc))),
        out_shape=(jax.ShapeDtypeStruct((n, d), F32),
                   jax.ShapeDtypeStruct((n // tm, conv_buf, dff), F32)),
        scratch_shapes=[pltpu.VMEM((tm, d), BF16), pltpu.VMEM((2, tm, tf), F32), pltpu.VMEM((2, tm, tf), F32),
                        pltpu.VMEM((tm + CONV_HALO, tf), F32), pltpu.VMEM((nc, CONV_HALO, tf), F32),
                        pltpu.VMEM((tm, tf), BF16)],
        compiler_params=_params(("arbitrary",), nbytes),
        name="ffn_seq",
    )(x, g.reshape(1, d), w_up, w_up, w_conv, b_conv.reshape(1, dff), w_down)
    return out, new_tail[tps - 1::tps]


def _ffn_step_kernel(x_ref, g_ref, wa_ref, wv_ref, wc_ref, bc_ref, wd_ref, p0_ref, p1_ref,
                     o_ref, a_ref, h_ref, acc_ref):
    @pl.when(pl.program_id(1) == 0)
    def _():
        h_ref[...] = _rmsnorm(x_ref[...], g_ref[...]).astype(BF16)

    a = _dot(h_ref[...], wa_ref[...])
    v = _dot(h_ref[...], wv_ref[...])
    conv = bc_ref[...] + wc_ref[2:3, :] * a + wc_ref[1:2, :] * p1_ref[...] + wc_ref[0:1, :] * p0_ref[...]
    a_ref[...] = a
    _ffn_gate_down(x_ref, conv, v, wd_ref, o_ref, acc_ref)


def _ffn_step(x, g, w_up, w_conv, b_conv, w_down, cache, tf):
    n, d = x.shape
    dff = w_down.shape[0]
    nc = dff // tf
    nbytes = 4 * n * d * 4 + 4 * d * tf * 2 + 2 * tf * d * 2 + n * d * 6 + 12 * n * tf * 4
    return pl.pallas_call(
        _ffn_step_kernel,
        grid=(1, nc),
        in_specs=[pl.BlockSpec((n, d), lambda i, c: (0, 0)),
                  pl.BlockSpec((1, d), lambda i, c: (0, 0)),
                  pl.BlockSpec((d, tf), lambda i, c: (0, c)),
                  pl.BlockSpec((d, tf), lambda i, c: (0, nc + c)),
                  pl.BlockSpec((CONV_W, tf), lambda i, c: (0, c)),
                  pl.BlockSpec((1, tf), lambda i, c: (0, c)),
                  pl.BlockSpec((tf, d), lambda i, c: (c, 0)),
                  pl.BlockSpec((n, tf), lambda i, c: (0, c)),
                  pl.BlockSpec((n, tf), lambda i, c: (0, nc + c))],
        out_specs=(pl.BlockSpec((n, d), lambda i, c: (0, 0)),
                   pl.BlockSpec((n, tf), lambda i, c: (0, c))),
        out_shape=(jax.ShapeDtypeStruct((n, d), F32),
                   jax.ShapeDtypeStruct((n, dff), F32)),
        scratch_shapes=[pltpu.VMEM((n, d), BF16), pltpu.VMEM((n, d), F32)],
        compiler_params=_params(("arbitrary", "arbitrary"), nbytes),
        name="ffn_step",
    )(x, g.reshape(1, d), w_up, w_up, w_conv, b_conv.reshape(1, dff), w_down, cache, cache)


def _ple_final_kernel(x_ref, p_ref, gp_ref, wg_ref, wp_ref, gf_ref, o_ref):
    x = x_ref[...]
    gate = jax.nn.sigmoid(_dot(_rmsnorm(x, gp_ref[...]).astype(BF16), wg_ref[...]))
    x = x + gate * _dot(p_ref[...].astype(BF16), wp_ref[...])
    o_ref[...] = _rmsnorm(x, gf_ref[...])


def _ple_final(x, p, g_ple, w_gate, w_ple, g_final, tm):
    n, d = x.shape
    dp = p.shape[1]
    nbytes = 2 * (2 * tm * d * 4 + tm * dp * 4 + d * d * 2 + dp * d * 2) + 4 * tm * d * 4
    return pl.pallas_call(
        _ple_final_kernel,
        grid=(n // tm,),
        in_specs=[pl.BlockSpec((tm, d), lambda i: (i, 0)),
                  pl.BlockSpec((tm, dp), lambda i: (i, 0)),
                  pl.BlockSpec((1, d), lambda i: (0, 0)),
                  pl.BlockSpec((d, d), lambda i: (0, 0)),
                  pl.BlockSpec((dp, d), lambda i: (0, 0)),
                  pl.BlockSpec((1, d), lambda i: (0, 0))],
        out_specs=pl.BlockSpec((tm, d), lambda i: (i, 0)),
        out_shape=jax.ShapeDtypeStruct((n, d), F32),
        compiler_params=_params(("arbitrary",), nbytes),
        name="ple_final",
    )(x, p, g_ple.reshape(1, d), w_gate, w_ple, g_final.reshape(1, d))


def _tile(n, pref):
    return pref if n % pref == 0 else n


def kernel(x_prompt, x_sample, cache_pool, state_ssm_re, state_ssm_im, cache_conv, p_prompt, p_sample, g_mix, w_in, w_pool, pool_scale, ssm_lam_re, ssm_lam_im, ssm_log_dt, ssm_b_re, ssm_b_im, ssm_c_re, ssm_c_im, ssm_d, w_glu, w_branch_pool, w_branch_ssm, w_out, g_ffn, w_up, w_conv, b_conv, w_down, g_ple, w_ple_gate, w_ple, g_final):
    depth = g_mix.shape[0]
    nb, t_len, d = x_prompt.shape
    ns = x_sample.shape[0]
    assert x_sample.shape[1] == 1, "the sample group advances one step per call"
    pool_buf, d_pool = cache_pool.shape[2], cache_pool.shape[3]
    conv_buf, d_ff = cache_conv.shape[2], cache_conv.shape[3]
    n_grp, n_state = ssm_lam_re.shape[1], ssm_lam_re.shape[2]
    d_ssm = ssm_d.shape[1]
    assert pool_buf == max(POOL_WINDOWS) - 1 and conv_buf == CONV_W - 1
    assert n_state == SSM_STATE and d_ssm == n_grp * SSM_GROUP

    xp = x_prompt.reshape(nb * t_len, d)
    xs = x_sample.reshape(ns, d)
    outs = [[] for _ in range(8)]
    for i in range(depth):
        w_in_b, w_pool_b, w_glu_b = w_in[i].astype(BF16), w_pool[i].astype(BF16), w_glu[i].astype(BF16)
        w_bp_b, w_bs_b, w_out_b = w_branch_pool[i].astype(BF16), w_branch_ssm[i].astype(BF16), w_out[i].astype(BF16)
        w_up_b, w_down_b = w_up[i].astype(BF16), w_down[i].astype(BF16)
        w_pg_b, w_ple_b = w_ple_gate[i].astype(BF16), w_ple[i].astype(BF16)
        b_cat, c_cat, tbl = _ssm_params(ssm_lam_re[i], ssm_lam_im[i], ssm_log_dt[i], ssm_b_re[i], ssm_b_im[i],
                                        ssm_c_re[i], ssm_c_im[i])

        def mix_tail(x, z, bp, gy, tm):
            merged = _glu_merge(gy, bp, z, d_pool + d_ssm, w_glu_b, w_bs_b, tm)
            return _matmul_res(merged, w_out_b, x, _tile(x.shape[0], 1024), 512)

        z = _norm_matmul(xp, g_mix[i], w_in_b, 1024, 512)
        bp, pool_new = _pool_seq(z, w_pool_b, pool_scale[i], w_bp_b, nb, t_len, 512, pool_buf)
        gy, st_re, st_im = _ssm_seq(z, d_pool, b_cat, c_cat, tbl, ssm_d[i], nb, t_len, 512)
        xp = mix_tail(xp, z, bp, gy, 256)
        xp, conv_new = _ffn_seq(xp, g_ffn[i], w_up_b, w_conv[i], b_conv[i], w_down_b, nb, t_len, 512, 512, conv_buf)
        xp_out = _ple_final(xp, p_prompt[i].reshape(nb * t_len, -1), g_ple[i], w_pg_b, w_ple_b, g_final, 256)
        for lst, val in zip(outs[:4], (pool_new, st_re.reshape(nb, n_grp, n_state),
                                       st_im.reshape(nb, n_grp, n_state), conv_new)):
            lst.append(val)

        z = _norm_matmul(xs, g_mix[i], w_in_b, ns, 512)
        bp, pool_new = _pool_step(z, cache_pool[i].reshape(ns, pool_buf * d_pool), w_pool_b, pool_scale[i], w_bp_b)
        gy, st_re, st_im = _ssm_step(z, d_pool, state_ssm_re[i].reshape(ns, -1), state_ssm_im[i].reshape(ns, -1),
                                     b_cat, c_cat, tbl, ssm_d[i])
        xs = mix_tail(xs, z, bp, gy, ns)
        xs, a_new = _ffn_step(xs, g_ffn[i], w_up_b, w_conv[i], b_conv[i], w_down_b,
                              cache_conv[i].reshape(ns, conv_buf * d_ff), 512)
        conv_new = jnp.concatenate([cache_conv[i][:, 1:], a_new[:, None, :]], axis=1)
        xs_out = _ple_final(xs, p_sample[i].reshape(ns, -1), g_ple[i], w_pg_b, w_ple_b, g_final, ns)
        for lst, val in zip(outs[4:], (pool_new.reshape(ns, pool_buf, d_pool), st_re.reshape(ns, n_grp, n_state),
                                       st_im.reshape(ns, n_grp, n_state), conv_new)):
            lst.append(val)

    assert depth == 1
    y_prompt = xp_out.reshape(nb, t_len, d)
    y_sample = xs_out.reshape(ns, 1, d)
    return (y_prompt, y_sample) + tuple(jnp.stack(o, axis=0) for o in outs)
```

```python
import functools

import jax
import jax.numpy as jnp
from jax import lax
from jax.experimental import pallas as pl
from jax.experimental.pallas import tpu as pltpu

F32 = jnp.float32
BF16 = jnp.bfloat16

EPS = 1e-6
POOL_WINDOWS = (2, 4, 8, 16)
POOL_HALO = 16
SSM_GROUP = 16
SSM_STATE = 64
SSM_BLOCK_GROUPS = 16
SCAN_ROWS = 8
CONV_W = 3
CONV_HALO = 8
W_TILE = 512
V7X_VMEM_BYTES = 64 * 1024 * 1024


def _vmem_limit(nbytes):
    return int(min(nbytes * 1.25 + (8 << 20), V7X_VMEM_BYTES - (6 << 20)))


def _params(sem, nbytes):
    return pltpu.CompilerParams(dimension_semantics=sem, vmem_limit_bytes=_vmem_limit(nbytes))


def _rmsnorm(x, g):
    return x * lax.rsqrt(jnp.mean(x * x, axis=-1, keepdims=True) + EPS) * g


def _dot(a, b):
    return jnp.dot(a, b, preferred_element_type=F32)


def _ssm_params_kernel(lr_ref, li_ref, logdt_ref, br_ref, bi_ref, pwr_ref, pwi_ref, bbr_ref, bbi_ref):
    lr = lr_ref[...]
    li = li_ref[...]
    dt = jnp.exp(logdt_ref[...])
    mag = jnp.exp(lr * dt)
    a_re = mag * jnp.cos(li * dt)
    a_im = mag * jnp.sin(li * dt)
    nr = a_re - 1.0
    ni = a_im
    den = lr * lr + li * li
    coef_re = (nr * lr + ni * li) / den
    coef_im = (ni * lr - nr * li) / den
    br = br_ref[...]
    bi = bi_ref[...]
    bbr_ref[...] = coef_re[None] * br - coef_im[None] * bi
    bbi_ref[...] = coef_re[None] * bi + coef_im[None] * br
    pr, pi = a_re, a_im
    pwr_ref[0] = pr
    pwi_ref[0] = pi
    for n in range(1, SCAN_ROWS):
        pr, pi = pr * a_re - pi * a_im, pr * a_im + pi * a_re
        pwr_ref[n] = pr
        pwi_ref[n] = pi


def _ssm_params(lam_re, lam_im, log_dt, b_re, b_im, c_re, c_im):
    g, p = lam_re.shape
    h = b_re.shape[-1]
    nkb = g // SSM_BLOCK_GROUPS
    bl = SSM_BLOCK_GROUPS
    pwr, pwi, bbr, bbi = pl.pallas_call(
        _ssm_params_kernel,
        out_shape=(jax.ShapeDtypeStruct((SCAN_ROWS, g, p), F32),
                   jax.ShapeDtypeStruct((SCAN_ROWS, g, p), F32),
                   jax.ShapeDtypeStruct((h, g, p), F32),
                   jax.ShapeDtypeStruct((h, g, p), F32)),
        name="ssm_params",
    )(lam_re, lam_im, log_dt.reshape(g, 1), jnp.transpose(b_re, (2, 0, 1)), jnp.transpose(b_im, (2, 0, 1)))

    eye = jnp.eye(bl, dtype=F32)
    bb = jnp.stack([bbr, bbi]).reshape(2, h, nkb, bl, p)
    b_cat = jnp.einsum('shkgp,gj->kghsjp', bb, eye).reshape(nkb, bl * h, 2 * bl * p).astype(BF16)
    cc = jnp.stack([c_re, -c_im]).reshape(2, nkb, bl, h, p)
    c_cat = jnp.einsum('skghp,gj->ksgpjh', cc, eye).reshape(nkb, 2 * bl * p, bl * h).astype(BF16)
    pw = jnp.stack([pwr, pwi], axis=1).reshape(SCAN_ROWS, 2, nkb, bl * p)
    pw = jnp.transpose(pw, (2, 0, 1, 3)).reshape(nkb, SCAN_ROWS, 2 * bl * p)
    rows = jnp.arange(SCAN_ROWS)[None, :, None]
    tbl = [jnp.where(rows >= k, pw[:, k - 1:k, :], 0.0) for k in (1, 2, 4)]
    tbl.append(pw)
    return b_cat, c_cat, jnp.concatenate(tbl, axis=1)


def _norm_matmul_kernel(x_ref, g_ref, w_ref, o_ref, h_ref):
    @pl.when(pl.program_id(1) == 0)
    def _():
        h_ref[...] = _rmsnorm(x_ref[...], g_ref[...]).astype(BF16)
    o_ref[...] = _dot(h_ref[...], w_ref[0])


def _col_tiles(w, tn):
    d, dout = w.shape
    return jnp.transpose(w.astype(BF16).reshape(d, dout // tn, tn), (1, 0, 2))


def _norm_matmul(x, g, w, tm):
    n, d = x.shape
    nj, _, tn = w.shape
    dout = nj * tn
    nbytes = 2 * tm * d * 4 + tm * d * 2 + 2 * d * tn * 2 + 2 * tm * tn * 4
    return pl.pallas_call(
        _norm_matmul_kernel,
        grid=(n // tm, nj),
        in_specs=[pl.BlockSpec((tm, d), lambda i, j: (i, 0)),
                  pl.BlockSpec((1, d), lambda i, j: (0, 0)),
                  pl.BlockSpec((1, d, tn), lambda i, j: (j, 0, 0))],
        out_specs=pl.BlockSpec((tm, tn), lambda i, j: (i, j)),
        out_shape=jax.ShapeDtypeStruct((n, dout), F32),
        scratch_shapes=[pltpu.VMEM((tm, d), BF16)],
        compiler_params=_params(("arbitrary", "arbitrary"), nbytes),
        name="norm_matmul",
    )(x, g.reshape(1, d), w)


def _pool_project(diffs, wp_ref, scale_ref, wb_ref, y_ref):
    gw = wp_ref.shape[1]
    for k, diff in enumerate(diffs):
        yk = _dot(diff.astype(BF16), wp_ref[k]) * scale_ref[:, k * gw:(k + 1) * gw]
        y_ref[:, k * gw:(k + 1) * gw] = yk.astype(BF16)
    return _dot(y_ref[...], wb_ref[...])


def _pool_seq_kernel(u_ref, wp_ref, scale_ref, wb_ref, o_ref, new_ref, ext_ref, y_ref):
    tc, dp = u_ref.shape
    gw = wp_ref.shape[1]
    t = pl.program_id(1)

    @pl.when(t == 0)
    def _():
        ext_ref[0:POOL_HALO, :] = jnp.zeros((POOL_HALO, dp), F32)

    ext_ref[POOL_HALO:POOL_HALO + tc, :] = u_ref[...]
    pos = (t * tc + 1 + lax.broadcasted_iota(jnp.int32, (tc, 1), 0)).astype(F32)
    diffs = []
    for k, w in enumerate(POOL_WINDOWS):
        cols = slice(k * gw, (k + 1) * gw)
        u = ext_ref[POOL_HALO:POOL_HALO + tc, cols]
        s = u
        for j in range(1, w):
            s = s + ext_ref[POOL_HALO - j:POOL_HALO - j + tc, cols]
        count = jnp.minimum(pos, float(w))
        diffs.append(s / count - u)
    o_ref[...] = _pool_project(diffs, wp_ref, scale_ref, wb_ref, y_ref)
    nb = new_ref.shape[1]
    new_ref[0] = ext_ref[POOL_HALO + tc - nb:POOL_HALO + tc, :]
    ext_ref[0:POOL_HALO, :] = ext_ref[tc:tc + POOL_HALO, :]


def _pool_seq(z, w_pool, pool_scale, w_branch, nb, t_len, tc, pool_buf):
    n = z.shape[0]
    ng, gw, _ = w_pool.shape
    dp = ng * gw
    dm = w_branch.shape[1]
    nt = t_len // tc
    nbytes = (2 * tc * dp * 4 + 2 * ng * gw * gw * 2 + 2 * dp * dm * 2 + 2 * tc * dm * 4
              + (tc + POOL_HALO) * dp * 4 + tc * dp * 2 + 4 * tc * gw * 4)
    return pl.pallas_call(
        _pool_seq_kernel,
        grid=(nb, nt),
        in_specs=[pl.BlockSpec((tc, dp), lambda b, t: (b * nt + t, 0)),
                  pl.BlockSpec((ng, gw, gw), lambda b, t: (0, 0, 0)),
                  pl.BlockSpec((1, dp), lambda b, t: (0, 0)),
                  pl.BlockSpec((dp, dm), lambda b, t: (0, 0))],
        out_specs=(pl.BlockSpec((tc, dm), lambda b, t: (b * nt + t, 0)),
                   pl.BlockSpec((1, pool_buf, dp), lambda b, t: (b, 0, 0))),
        out_shape=(jax.ShapeDtypeStruct((n, dm), F32),
                   jax.ShapeDtypeStruct((nb, pool_buf, dp), F32)),
        scratch_shapes=[pltpu.VMEM((tc + POOL_HALO, dp), F32), pltpu.VMEM((tc, dp), BF16)],
        compiler_params=_params(("arbitrary", "arbitrary"), nbytes),
        name="pool_seq",
    )(z, w_pool, pool_scale.reshape(1, dp), w_branch)


def _pool_step_kernel(u_ref, cache_ref, wp_ref, scale_ref, wb_ref, o_ref, new_ref, y_ref):
    dp = u_ref.shape[1]
    gw = wp_ref.shape[1]
    lb = cache_ref.shape[1] // dp
    diffs = []
    for k, w in enumerate(POOL_WINDOWS):
        u = u_ref[:, k * gw:(k + 1) * gw]
        s = u
        for j in range(1, w):
            s = s + cache_ref[:, (lb - j) * dp + k * gw:(lb - j) * dp + (k + 1) * gw]
        diffs.append(s / float(w) - u)
    o_ref[...] = _pool_project(diffs, wp_ref, scale_ref, wb_ref, y_ref)
    new_ref[:, 0:(lb - 1) * dp] = cache_ref[:, dp:lb * dp]
    new_ref[:, (lb - 1) * dp:lb * dp] = u_ref[...]


def _pool_step(z, cache, w_pool, pool_scale, w_branch):
    n = z.shape[0]
    ng, gw, _ = w_pool.shape
    dp = ng * gw
    dm = w_branch.shape[1]
    lbdp = cache.shape[1]
    nbytes = 2 * (n * dp * 4 + 2 * n * lbdp * 4 + ng * gw * gw * 2 + dp * dm * 2 + n * dm * 4) + n * dp * 2
    return pl.pallas_call(
        _pool_step_kernel,
        grid=(1,),
        in_specs=[pl.BlockSpec((n, dp), lambda i: (0, 0)),
                  pl.BlockSpec((n, lbdp), lambda i: (0, 0)),
                  pl.BlockSpec((ng, gw, gw), lambda i: (0, 0, 0)),
                  pl.BlockSpec((1, dp), lambda i: (0, 0)),
                  pl.BlockSpec((dp, dm), lambda i: (0, 0))],
        out_specs=(pl.BlockSpec((n, dm), lambda i: (0, 0)),
                   pl.BlockSpec((n, lbdp), lambda i: (0, 0))),
        out_shape=(jax.ShapeDtypeStruct((n, dm), F32),
                   jax.ShapeDtypeStruct((n, lbdp), F32)),
        scratch_shapes=[pltpu.VMEM((n, dp), BF16)],
        compiler_params=_params(("arbitrary",), nbytes),
        name="pool_step",
    )(z, cache, w_pool, pool_scale.reshape(1, dp), w_branch)


def _cmul_add(xr, xi, ar, ai, br, bi):
    return xr + ar * br - ai * bi, xi + ar * bi + ai * br


def _ssm_seq_kernel(u_ref, bcat_ref, ccat_ref, tbl_ref, d_ref, gy_ref, st_ref, h_ref, carry_ref):
    tc = u_ref.shape[0]
    half = h_ref.shape[1] // 2
    t = pl.program_id(2)

    @pl.when(t == 0)
    def _():
        carry_ref[...] = jnp.zeros_like(carry_ref)

    u = u_ref[...]
    h_ref[...] = _dot(u.astype(BF16), bcat_ref[0])

    re = slice(0, half)
    im = slice(half, 2 * half)

    def tile_scan(i, carry):
        cr, ci = carry
        rows = pl.ds(pl.multiple_of(i * SCAN_ROWS, SCAN_ROWS), SCAN_ROWS)
        br = h_ref[rows, re]
        bi = h_ref[rows, im]
        for lvl, k in enumerate((1, 2, 4)):
            tr = pl.ds(lvl * SCAN_ROWS, SCAN_ROWS)
            br, bi = _cmul_add(br, bi, tbl_ref[0, tr, re], tbl_ref[0, tr, im],
                               pltpu.roll(br, k, 0), pltpu.roll(bi, k, 0))
        tr = pl.ds(3 * SCAN_ROWS, SCAN_ROWS)
        br, bi = _cmul_add(br, bi, tbl_ref[0, tr, re], tbl_ref[0, tr, im], cr, ci)
        h_ref[rows, re] = br
        h_ref[rows, im] = bi
        return br[SCAN_ROWS - 1:SCAN_ROWS, :], bi[SCAN_ROWS - 1:SCAN_ROWS, :]

    cr, ci = lax.fori_loop(0, tc // SCAN_ROWS, tile_scan, (carry_ref[:, re], carry_ref[:, im]))
    carry_ref[:, re] = cr
    carry_ref[:, im] = ci
    st_ref[0, 0] = carry_ref[...]
    y = _dot(h_ref[...].astype(BF16), ccat_ref[0]) + d_ref[...] * u
    gy_ref[...] = jax.nn.gelu(y).astype(BF16)


def _ssm_seq(z, col0, b_cat, c_cat, tbl, d_skip, nb, t_len, tc):
    n = z.shape[0]
    nkb, kw, sw = b_cat.shape
    nt = t_len // tc
    cb = col0 // kw
    nbytes = (2 * tc * kw * 4 + 4 * kw * sw * 2 + 2 * tbl.shape[1] * sw * 4 + 2 * tc * kw * 2
              + tc * sw * 4 + tc * sw * 2 + tc * sw * 4)
    gy, st = pl.pallas_call(
        _ssm_seq_kernel,
        grid=(nkb, nb, nt),
        in_specs=[pl.BlockSpec((tc, kw), lambda k, b, t: (b * nt + t, cb + k)),
                  pl.BlockSpec((1, kw, sw), lambda k, b, t: (k, 0, 0)),
                  pl.BlockSpec((1, sw, kw), lambda k, b, t: (k, 0, 0)),
                  pl.BlockSpec((1, tbl.shape[1], sw), lambda k, b, t: (k, 0, 0)),
                  pl.BlockSpec((1, kw), lambda k, b, t: (0, k))],
        out_specs=(pl.BlockSpec((tc, kw), lambda k, b, t: (b * nt + t, k)),
                   pl.BlockSpec((1, 1, 1, sw), lambda k, b, t: (b, k, 0, 0))),
        out_shape=(jax.ShapeDtypeStruct((n, nkb * kw), BF16),
                   jax.ShapeDtypeStruct((nb, nkb, 1, sw), F32)),
        scratch_shapes=[pltpu.VMEM((tc, sw), F32), pltpu.VMEM((1, sw), F32)],
        compiler_params=_params(("arbitrary", "arbitrary", "arbitrary"), nbytes),
        name="ssm_seq",
    )(z, b_cat, c_cat, tbl, d_skip.reshape(1, nkb * kw))
    st = st.reshape(nb, nkb, 2, sw // 2)
    return gy, st[:, :, 0].reshape(nb, -1), st[:, :, 1].reshape(nb, -1)


def _ssm_step_kernel(u_ref, h0r_ref, h0i_ref, bcat_ref, ccat_ref, tbl_ref, d_ref,
                     gy_ref, h1r_ref, h1i_ref, h_ref):
    half = h0r_ref.shape[1]
    u = u_ref[...]
    bu = _dot(u.astype(BF16), bcat_ref[0])
    a_row = pl.ds(3 * SCAN_ROWS, 1)
    hr, hi = _cmul_add(bu[:, 0:half], bu[:, half:2 * half],
                       tbl_ref[0, a_row, 0:half], tbl_ref[0, a_row, half:2 * half],
                       h0r_ref[...], h0i_ref[...])
    h1r_ref[...] = hr
    h1i_ref[...] = hi
    h_ref[:, 0:half] = hr.astype(BF16)
    h_ref[:, half:2 * half] = hi.astype(BF16)
    y = _dot(h_ref[...], ccat_ref[0]) + d_ref[...] * u
    gy_ref[...] = jax.nn.gelu(y).astype(BF16)


def _ssm_step(z, col0, h0_re, h0_im, b_cat, c_cat, tbl, d_skip):
    n = z.shape[0]
    nkb, kw, sw = b_cat.shape
    half = sw // 2
    cb = col0 // kw
    nbytes = 2 * (n * kw * 4 + 4 * n * half * 4 + 2 * kw * sw * 2 + tbl.shape[1] * sw * 4 + n * kw * 2) + n * sw * 6
    return pl.pallas_call(
        _ssm_step_kernel,
        grid=(nkb,),
        in_specs=[pl.BlockSpec((n, kw), lambda k: (0, cb + k)),
                  pl.BlockSpec((n, half), lambda k: (0, k)),
                  pl.BlockSpec((n, half), lambda k: (0, k)),
                  pl.BlockSpec((1, kw, sw), lambda k: (k, 0, 0)),
                  pl.BlockSpec((1, sw, kw), lambda k: (k, 0, 0)),
                  pl.BlockSpec((1, tbl.shape[1], sw), lambda k: (k, 0, 0)),
                  pl.BlockSpec((1, kw), lambda k: (0, k))],
        out_specs=(pl.BlockSpec((n, kw), lambda k: (0, k)),
                   pl.BlockSpec((n, half), lambda k: (0, k)),
                   pl.BlockSpec((n, half), lambda k: (0, k))),
        out_shape=(jax.ShapeDtypeStruct((n, nkb * kw), BF16),
                   jax.ShapeDtypeStruct((n, nkb * half), F32),
                   jax.ShapeDtypeStruct((n, nkb * half), F32)),
        scratch_shapes=[pltpu.VMEM((n, sw), BF16)],
        compiler_params=_params(("arbitrary",), nbytes),
        name="ssm_step",
    )(z, h0_re, h0_im, b_cat, c_cat, tbl, d_skip.reshape(1, nkb * kw))


def _glu_merge_kernel(gy_ref, bp_ref, gp_ref, gs_ref, wg_ref, wb_ref, o_ref):
    ds = wb_ref.shape[0]
    g = _dot(gy_ref[...], wg_ref[...])
    y = g[:, 0:ds] * jax.nn.sigmoid(g[:, ds:2 * ds])
    bs = _dot(y.astype(BF16), wb_ref[...])
    merged = jax.nn.sigmoid(gp_ref[...]) * bp_ref[...] + jax.nn.sigmoid(gs_ref[...]) * bs
    o_ref[...] = merged.astype(BF16)


def _glu_merge(gy, bp, z, gate_col0, w_glu, w_branch, tm):
    n, ds = gy.shape
    dm = w_branch.shape[1]
    gb = gate_col0 // dm
    nbytes = 2 * (tm * ds * 2 + 3 * tm * dm * 4 + ds * 2 * ds * 2 + ds * dm * 2 + tm * dm * 2) + 4 * tm * dm * 4
    return pl.pallas_call(
        _glu_merge_kernel,
        grid=(n // tm,),
        in_specs=[pl.BlockSpec((tm, ds), lambda i: (i, 0)),
                  pl.BlockSpec((tm, dm), lambda i: (i, 0)),
                  pl.BlockSpec((tm, dm), lambda i: (i, gb)),
                  pl.BlockSpec((tm, dm), lambda i: (i, gb + 1)),
                  pl.BlockSpec((ds, 2 * ds), lambda i: (0, 0)),
                  pl.BlockSpec((ds, dm), lambda i: (0, 0))],
        out_specs=pl.BlockSpec((tm, dm), lambda i: (i, 0)),
        out_shape=jax.ShapeDtypeStruct((n, dm), BF16),
        compiler_params=_params(("arbitrary",), nbytes),
        name="glu_merge",
    )(gy, bp, z, z, w_glu, w_branch)


def _matmul_res_kernel(m_ref, w_ref, x_ref, o_ref):
    o_ref[...] = x_ref[...] + _dot(m_ref[...], w_ref[0])


def _matmul_res(m, w, x, tm):
    n, d = m.shape
    nj, _, tn = w.shape
    dout = nj * tn
    nbytes = 2 * (tm * d * 2 + d * tn * 2 + 2 * tm * tn * 4)
    return pl.pallas_call(
        _matmul_res_kernel,
        grid=(n // tm, nj),
        in_specs=[pl.BlockSpec((tm, d), lambda i, j: (i, 0)),
                  pl.BlockSpec((1, d, tn), lambda i, j: (j, 0, 0)),
                  pl.BlockSpec((tm, tn), lambda i, j: (i, j))],
        out_specs=pl.BlockSpec((tm, tn), lambda i, j: (i, j)),
        out_shape=jax.ShapeDtypeStruct((n, dout), F32),
        compiler_params=_params(("arbitrary", "arbitrary"), nbytes),
        name="matmul_res",
    )(m, w, x)


def _ffn_gate_down(x_ref, conv, v, wd_ref, o_ref, acc_ref):
    c = pl.program_id(1)
    part = _dot((jax.nn.gelu(conv) * v).astype(BF16), wd_ref[...])

    @pl.when(c == 0)
    def _():
        acc_ref[...] = part

    @pl.when(c > 0)
    def _():
        acc_ref[...] += part

    @pl.when(c == pl.num_programs(1) - 1)
    def _():
        o_ref[...] = x_ref[...] + acc_ref[...]


def _ffn_seq_kernel(nc, tiles_per_seq, n_steps, x_ref, g_ref, wa_ref, wv_ref, wc_ref, bc_ref, wd_ref,
                    o_ref, new_ref, h_ref, even_ref, odd_ref, ext_ref, carry_ref, gate_ref):
    tm = x_ref.shape[0]
    s = pl.program_id(0)
    sp = jnp.maximum(s - 1, 0)
    cp = sp % nc
    ip = sp // nc

    @pl.when(s == 0)
    def _():
        odd_ref[...] = jnp.zeros(odd_ref.shape, F32)
        o_ref[...] = jnp.zeros(o_ref.shape, F32)
        carry_ref[...] = jnp.zeros(carry_ref.shape, F32)

    @pl.when(jnp.logical_and(s % nc == 0, s < n_steps))
    def _():
        h_ref[...] = _rmsnorm(x_ref[...], g_ref[...]).astype(BF16)

    def step(cur_ref, prev_ref):
        cur_ref[0] = _dot(h_ref[...], wa_ref[0])
        cur_ref[1] = _dot(h_ref[...], wv_ref[0])

        a = prev_ref[0]
        v = prev_ref[1]
        seq_start = ip % tiles_per_seq == 0
        ext_ref[0:CONV_HALO, :] = jnp.where(seq_start, 0.0, carry_ref[cp])
        ext_ref[CONV_HALO:CONV_HALO + tm, :] = a
        conv = bc_ref[...] + wc_ref[CONV_W - 1:CONV_W, :] * a
        for j in range(CONV_W - 1):
            off = CONV_HALO - (CONV_W - 1) + j
            conv = conv + wc_ref[j:j + 1, :] * ext_ref[off:off + tm, :]
        carry_ref[cp] = ext_ref[tm:tm + CONV_HALO, :]
        nb = new_ref.shape[1]
        new_ref[0] = ext_ref[CONV_HALO + tm - nb:CONV_HALO + tm, :]
        gate_ref[...] = (jax.nn.gelu(conv) * v).astype(BF16)
        part = _dot(gate_ref[...], wd_ref[...])
        o_ref[...] = part + jnp.where(cp == 0, x_ref[...], o_ref[...])

    @pl.when(s % 2 == 0)
    def _():
        step(even_ref, odd_ref)

    @pl.when(s % 2 == 1)
    def _():
        step(odd_ref, even_ref)


def _ffn_seq(x, g, w_up, w_conv, b_conv, w_down, nb, t_len, tm, tf, conv_buf):
    n, d = x.shape
    dff = w_down.shape[0]
    nc = dff // tf
    tps = t_len // tm
    n_steps = (n // tm) * nc
    assert nc >= 2, "the residual tile must still be resident when its first chunk is finished"
    nbytes = (4 * tm * d * 4 + 4 * d * tf * 2 + 2 * tf * d * 2 + tm * d * 2 + 4 * tm * tf * 4
              + (tm + CONV_HALO) * tf * 4 + nc * CONV_HALO * tf * 4 + 4 * tm * tf * 4)

    def cur(s):
        return jnp.minimum(s, n_steps - 1)

    def prev(s):
        return jnp.maximum(s - 1, 0)

    out, new_tail = pl.pallas_call(
        functools.partial(_ffn_seq_kernel, nc, tps, n_steps),
        grid=(n_steps + 1,),
        in_specs=[pl.BlockSpec((tm, d), lambda s: (cur(s) // nc, 0)),
                  pl.BlockSpec((1, d), lambda s: (0, 0)),
                  pl.BlockSpec((1, d, tf), lambda s: (cur(s) % nc, 0, 0)),
                  pl.BlockSpec((1, d, tf), lambda s: (nc + cur(s) % nc, 0, 0)),
                  pl.BlockSpec((CONV_W, tf), lambda s: (0, prev(s) % nc)),
                  pl.BlockSpec((1, tf), lambda s: (0, prev(s) % nc)),
                  pl.BlockSpec((tf, d), lambda s: (prev(s) % nc, 0))],
        out_specs=(pl.BlockSpec((tm, d), lambda s: (prev(s) // nc, 0)),
                   pl.BlockSpec((1, conv_buf, tf), lambda s: (prev(s) // nc, 0, prev(s) % nc))),
        out_shape=(jax.ShapeDtypeStruct((n, d), F32),
                   jax.ShapeDtypeStruct((n // tm, conv_buf, dff), F32)),
        scratch_shapes=[pltpu.VMEM((tm, d), BF16), pltpu.VMEM((2, tm, tf), F32), pltpu.VMEM((2, tm, tf), F32),
                        pltpu.VMEM((tm + CONV_HALO, tf), F32), pltpu.VMEM((nc, CONV_HALO, tf), F32),
                        pltpu.VMEM((tm, tf), BF16)],
        compiler_params=_params(("arbitrary",), nbytes),
        name="ffn_seq",
    )(x, g.reshape(1, d), w_up, w_up, w_conv, b_conv.reshape(1, dff), w_down)
    return out, new_tail[tps - 1::tps]


def _ffn_step_kernel(x_ref, g_ref, wa_ref, wv_ref, wc_ref, bc_ref, wd_ref, p0_ref, p1_ref,
                     o_ref, a_ref, h_ref, acc_ref):
    @pl.when(pl.program_id(1) == 0)
    def _():
        h_ref[...] = _rmsnorm(x_ref[...], g_ref[...]).astype(BF16)

    a = _dot(h_ref[...], wa_ref[0])
    v = _dot(h_ref[...], wv_ref[0])
    conv = bc_ref[...] + wc_ref[2:3, :] * a + wc_ref[1:2, :] * p1_ref[...] + wc_ref[0:1, :] * p0_ref[...]
    a_ref[...] = a
    _ffn_gate_down(x_ref, conv, v, wd_ref, o_ref, acc_ref)


def _ffn_step(x, g, w_up, w_conv, b_conv, w_down, cache, tf):
    n, d = x.shape
    dff = w_down.shape[0]
    nc = dff // tf
    nbytes = 4 * n * d * 4 + 4 * d * tf * 2 + 2 * tf * d * 2 + n * d * 6 + 12 * n * tf * 4
    return pl.pallas_call(
        _ffn_step_kernel,
        grid=(1, nc),
        in_specs=[pl.BlockSpec((n, d), lambda i, c: (0, 0)),
                  pl.BlockSpec((1, d), lambda i, c: (0, 0)),
                  pl.BlockSpec((1, d, tf), lambda i, c: (c, 0, 0)),
                  pl.BlockSpec((1, d, tf), lambda i, c: (nc + c, 0, 0)),
                  pl.BlockSpec((CONV_W, tf), lambda i, c: (0, c)),
                  pl.BlockSpec((1, tf), lambda i, c: (0, c)),
                  pl.BlockSpec((tf, d), lambda i, c: (c, 0)),
                  pl.BlockSpec((n, tf), lambda i, c: (0, c)),
                  pl.BlockSpec((n, tf), lambda i, c: (0, nc + c))],
        out_specs=(pl.BlockSpec((n, d), lambda i, c: (0, 0)),
                   pl.BlockSpec((n, tf), lambda i, c: (0, c))),
        out_shape=(jax.ShapeDtypeStruct((n, d), F32),
                   jax.ShapeDtypeStruct((n, dff), F32)),
        scratch_shapes=[pltpu.VMEM((n, d), BF16), pltpu.VMEM((n, d), F32)],
        compiler_params=_params(("arbitrary", "arbitrary"), nbytes),
        name="ffn_step",
    )(x, g.reshape(1, d), w_up, w_up, w_conv, b_conv.reshape(1, dff), w_down, cache, cache)


def _ple_final_kernel(x_ref, p_ref, gp_ref, wg_ref, wp_ref, gf_ref, o_ref):
    x = x_ref[...]
    gate = jax.nn.sigmoid(_dot(_rmsnorm(x, gp_ref[...]).astype(BF16), wg_ref[...]))
    x = x + gate * _dot(p_ref[...].astype(BF16), wp_ref[...])
    o_ref[...] = _rmsnorm(x, gf_ref[...])


def _ple_final(x, p, g_ple, w_gate, w_ple, g_final, tm):
    n, d = x.shape
    dp = p.shape[1]
    nbytes = 2 * (2 * tm * d * 4 + tm * dp * 4 + d * d * 2 + dp * d * 2) + 4 * tm * d * 4
    return pl.pallas_call(
        _ple_final_kernel,
        grid=(n // tm,),
        in_specs=[pl.BlockSpec((tm, d), lambda i: (i, 0)),
                  pl.BlockSpec((tm, dp), lambda i: (i, 0)),
                  pl.BlockSpec((1, d), lambda i: (0, 0)),
                  pl.BlockSpec((d, d), lambda i: (0, 0)),
                  pl.BlockSpec((dp, d), lambda i: (0, 0)),
                  pl.BlockSpec((1, d), lambda i: (0, 0))],
        out_specs=pl.BlockSpec((tm, d), lambda i: (i, 0)),
        out_shape=jax.ShapeDtypeStruct((n, d), F32),
        compiler_params=_params(("arbitrary",), nbytes),
        name="ple_final",
    )(x, p, g_ple.reshape(1, d), w_gate, w_ple, g_final.reshape(1, d))


def _tile(n, pref):
    return pref if n % pref == 0 else n


def kernel(x_prompt, x_sample, cache_pool, state_ssm_re, state_ssm_im, cache_conv, p_prompt, p_sample, g_mix, w_in, w_pool, pool_scale, ssm_lam_re, ssm_lam_im, ssm_log_dt, ssm_b_re, ssm_b_im, ssm_c_re, ssm_c_im, ssm_d, w_glu, w_branch_pool, w_branch_ssm, w_out, g_ffn, w_up, w_conv, b_conv, w_down, g_ple, w_ple_gate, w_ple, g_final):
    depth = g_mix.shape[0]
    nb, t_len, d = x_prompt.shape
    ns = x_sample.shape[0]
    assert x_sample.shape[1] == 1, "the sample group advances one step per call"
    pool_buf, d_pool = cache_pool.shape[2], cache_pool.shape[3]
    conv_buf, d_ff = cache_conv.shape[2], cache_conv.shape[3]
    n_grp, n_state = ssm_lam_re.shape[1], ssm_lam_re.shape[2]
    d_ssm = ssm_d.shape[1]
    assert pool_buf == max(POOL_WINDOWS) - 1 and conv_buf == CONV_W - 1
    assert n_state == SSM_STATE and d_ssm == n_grp * SSM_GROUP

    xp = x_prompt.reshape(nb * t_len, d)
    xs = x_sample.reshape(ns, d)
    outs = [[] for _ in range(8)]
    for i in range(depth):
        w_in_b, w_out_b, w_up_b = (_col_tiles(w[i], W_TILE) for w in (w_in, w_out, w_up))
        w_pool_b, w_glu_b, w_down_b = w_pool[i].astype(BF16), w_glu[i].astype(BF16), w_down[i].astype(BF16)
        w_bp_b, w_bs_b = w_branch_pool[i].astype(BF16), w_branch_ssm[i].astype(BF16)
        w_pg_b, w_ple_b = w_ple_gate[i].astype(BF16), w_ple[i].astype(BF16)
        b_cat, c_cat, tbl = _ssm_params(ssm_lam_re[i], ssm_lam_im[i], ssm_log_dt[i], ssm_b_re[i], ssm_b_im[i],
                                        ssm_c_re[i], ssm_c_im[i])

        def mix_tail(x, z, bp, gy, tm):
            merged = _glu_merge(gy, bp, z, d_pool + d_ssm, w_glu_b, w_bs_b, tm)
            return _matmul_res(merged, w_out_b, x, _tile(x.shape[0], 1024))

        z = _norm_matmul(xp, g_mix[i], w_in_b, 1024)
        bp, pool_new = _pool_seq(z, w_pool_b, pool_scale[i], w_bp_b, nb, t_len, 512, pool_buf)
        gy, st_re, st_im = _ssm_seq(z, d_pool, b_cat, c_cat, tbl, ssm_d[i], nb, t_len, 512)
        xp = mix_tail(xp, z, bp, gy, 256)
        xp, conv_new = _ffn_seq(xp, g_ffn[i], w_up_b, w_conv[i], b_conv[i], w_down_b, nb, t_len, 512, W_TILE, conv_buf)
        xp_out = _ple_final(xp, p_prompt[i].reshape(nb * t_len, -1), g_ple[i], w_pg_b, w_ple_b, g_final, 256)
        for lst, val in zip(outs[:4], (pool_new, st_re.reshape(nb, n_grp, n_state),
                                       st_im.reshape(nb, n_grp, n_state), conv_new)):
            lst.append(val)

        z = _norm_matmul(xs, g_mix[i], w_in_b, ns)
        bp, pool_new = _pool_step(z, cache_pool[i].reshape(ns, pool_buf * d_pool), w_pool_b, pool_scale[i], w_bp_b)
        gy, st_re, st_im = _ssm_step(z, d_pool, state_ssm_re[i].reshape(ns, -1), state_ssm_im[i].reshape(ns, -1),
                                     b_cat, c_cat, tbl, ssm_d[i])
        xs = mix_tail(xs, z, bp, gy, ns)
        xs, a_new = _ffn_step(xs, g_ffn[i], w_up_b, w_conv[i], b_conv[i], w_down_b,
                              cache_conv[i].reshape(ns, conv_buf * d_ff), W_TILE)
        conv_new = jnp.concatenate([cache_conv[i][:, 1:], a_new[:, None, :]], axis=1)
        xs_out = _ple_final(xs, p_sample[i].reshape(ns, -1), g_ple[i], w_pg_b, w_ple_b, g_final, ns)
        for lst, val in zip(outs[4:], (pool_new.reshape(ns, pool_buf, d_pool), st_re.reshape(ns, n_grp, n_state),
                                       st_im.reshape(ns, n_grp, n_state), conv_new)):
            lst.append(val)

    assert depth == 1
    y_prompt = xp_out.reshape(nb, t_len, d)
    y_sample = xs_out.reshape(ns, 1, d)
    return (y_prompt, y_sample) + tuple(jnp.stack(o, axis=0) for o in outs)
```

```python
import functools

import jax
import jax.numpy as jnp
from jax import lax
from jax.experimental import pallas as pl
from jax.experimental.pallas import tpu as pltpu

F32 = jnp.float32
BF16 = jnp.bfloat16

EPS = 1e-6
POOL_WINDOWS = (2, 4, 8, 16)
POOL_HALO = 16
SSM_GROUP = 16
SSM_STATE = 64
SSM_BLOCK_GROUPS = 16
LANES = 128
SUBLANES = 8
SEG_LEN = 64
SEG_PITCH = SEG_LEN + 8
CONV_W = 3
CONV_HALO = 8
W_TILE = 512
V7X_VMEM_BYTES = 64 * 1024 * 1024


def _vmem_limit(nbytes):
    return int(min(nbytes * 1.25 + (8 << 20), V7X_VMEM_BYTES - (6 << 20)))


def _params(sem, nbytes):
    return pltpu.CompilerParams(dimension_semantics=sem, vmem_limit_bytes=_vmem_limit(nbytes))


def _rmsnorm(x, g):
    return x * lax.rsqrt(jnp.mean(x * x, axis=-1, keepdims=True) + EPS) * g


def _dot(a, b):
    return jnp.dot(a, b, preferred_element_type=F32)


def _ssm_params_kernel(lr_ref, li_ref, logdt_ref, br_ref, bi_ref, pwr_ref, pwi_ref, bbr_ref, bbi_ref):
    lr = lr_ref[...]
    li = li_ref[...]
    dt = jnp.exp(logdt_ref[...])
    mag = jnp.exp(lr * dt)
    a_re = mag * jnp.cos(li * dt)
    a_im = mag * jnp.sin(li * dt)
    nr = a_re - 1.0
    ni = a_im
    den = lr * lr + li * li
    coef_re = (nr * lr + ni * li) / den
    coef_im = (ni * lr - nr * li) / den
    br = br_ref[...]
    bi = bi_ref[...]
    bbr_ref[...] = coef_re[None] * br - coef_im[None] * bi
    bbi_ref[...] = coef_re[None] * bi + coef_im[None] * br
    pr, pi = a_re, a_im
    pwr_ref[0] = pr
    pwi_ref[0] = pi
    for n in range(1, SEG_LEN):
        pr, pi = pr * a_re - pi * a_im, pr * a_im + pi * a_re
        pwr_ref[n] = pr
        pwi_ref[n] = pi


def _ssm_params(lam_re, lam_im, log_dt, b_re, b_im, c_re, c_im):
    g, p = lam_re.shape
    h = b_re.shape[-1]
    nkb = g // SSM_BLOCK_GROUPS
    bl = SSM_BLOCK_GROUPS
    pwr, pwi, bbr, bbi = pl.pallas_call(
        _ssm_params_kernel,
        out_shape=(jax.ShapeDtypeStruct((SEG_LEN, g, p), F32),
                   jax.ShapeDtypeStruct((SEG_LEN, g, p), F32),
                   jax.ShapeDtypeStruct((h, g, p), F32),
                   jax.ShapeDtypeStruct((h, g, p), F32)),
        name="ssm_params",
    )(lam_re, lam_im, log_dt.reshape(g, 1), jnp.transpose(b_re, (2, 0, 1)), jnp.transpose(b_im, (2, 0, 1)))

    eye = jnp.eye(bl, dtype=F32)
    bb = jnp.stack([bbr, bbi]).reshape(2, h, nkb, bl, p)
    b_cat = jnp.einsum('shkgp,gj->kghsjp', bb, eye).reshape(nkb, bl * h, 2 * bl * p).astype(BF16)
    cc = jnp.stack([c_re, -c_im]).reshape(2, nkb, bl, h, p)
    c_cat = jnp.einsum('skghp,gj->ksgpjh', cc, eye).reshape(nkb, 2 * bl * p, bl * h).astype(BF16)
    nslab = 2 * bl * p // LANES
    pw = jnp.stack([pwr, pwi], axis=1).reshape(SEG_LEN, 2, nkb, bl * p)
    a_row = jnp.transpose(pw[0], (1, 0, 2)).reshape(nkb, 1, 2 * bl * p)
    pw = jnp.transpose(pw.reshape(SEG_LEN, 2, nkb, nslab // 2, LANES), (2, 1, 3, 0, 4))
    pw = pw.reshape(nkb, nslab, SEG_LEN, LANES)
    a_tile = jnp.broadcast_to(pw[:, :, 0:1, :], (nkb, nslab, SUBLANES, LANES))
    return b_cat, c_cat, pw, a_tile, a_row


def _norm_matmul_kernel(x_ref, g_ref, w_ref, o_ref, h_ref):
    @pl.when(pl.program_id(1) == 0)
    def _():
        h_ref[...] = _rmsnorm(x_ref[...], g_ref[...]).astype(BF16)
    o_ref[...] = _dot(h_ref[...], w_ref[...])


def _norm_matmul(x, g, w, tm, tn):
    n, d = x.shape
    dout = w.shape[1]
    nbytes = 2 * tm * d * 4 + tm * d * 2 + 2 * d * tn * 2 + 2 * tm * tn * 4
    return pl.pallas_call(
        _norm_matmul_kernel,
        grid=(n // tm, dout // tn),
        in_specs=[pl.BlockSpec((tm, d), lambda i, j: (i, 0)),
                  pl.BlockSpec((1, d), lambda i, j: (0, 0)),
                  pl.BlockSpec((d, tn), lambda i, j: (0, j))],
        out_specs=pl.BlockSpec((tm, tn), lambda i, j: (i, j)),
        out_shape=jax.ShapeDtypeStruct((n, dout), F32),
        scratch_shapes=[pltpu.VMEM((tm, d), BF16)],
        compiler_params=_params(("arbitrary", "arbitrary"), nbytes),
        name="norm_matmul",
    )(x, g.reshape(1, d), w)


def _pool_project(diffs, wp_ref, scale_ref, wb_ref, y_ref):
    gw = wp_ref.shape[1]
    for k, diff in enumerate(diffs):
        yk = _dot(diff.astype(BF16), wp_ref[k]) * scale_ref[:, k * gw:(k + 1) * gw]
        y_ref[:, k * gw:(k + 1) * gw] = yk.astype(BF16)
    return _dot(y_ref[...], wb_ref[...])


def _pool_seq_kernel(u_ref, wp_ref, scale_ref, wb_ref, o_ref, new_ref, ext_ref, y_ref):
    tc, dp = u_ref.shape
    gw = wp_ref.shape[1]
    t = pl.program_id(1)

    @pl.when(t == 0)
    def _():
        ext_ref[0:POOL_HALO, :] = jnp.zeros((POOL_HALO, dp), F32)

    ext_ref[POOL_HALO:POOL_HALO + tc, :] = u_ref[...]
    pos = (t * tc + 1 + lax.broadcasted_iota(jnp.int32, (tc, 1), 0)).astype(F32)
    diffs = []
    for k, w in enumerate(POOL_WINDOWS):
        cols = slice(k * gw, (k + 1) * gw)
        u = ext_ref[POOL_HALO:POOL_HALO + tc, cols]
        s = u
        for j in range(1, w):
            s = s + ext_ref[POOL_HALO - j:POOL_HALO - j + tc, cols]
        count = jnp.minimum(pos, float(w))
        diffs.append(s / count - u)
    o_ref[...] = _pool_project(diffs, wp_ref, scale_ref, wb_ref, y_ref)
    nb = new_ref.shape[1]
    new_ref[0] = ext_ref[POOL_HALO + tc - nb:POOL_HALO + tc, :]
    ext_ref[0:POOL_HALO, :] = ext_ref[tc:tc + POOL_HALO, :]


def _pool_seq(z, w_pool, pool_scale, w_branch, nb, t_len, tc, pool_buf):
    n = z.shape[0]
    ng, gw, _ = w_pool.shape
    dp = ng * gw
    dm = w_branch.shape[1]
    nt = t_len // tc
    nbytes = (2 * tc * dp * 4 + 2 * ng * gw * gw * 2 + 2 * dp * dm * 2 + 2 * tc * dm * 4
              + (tc + POOL_HALO) * dp * 4 + tc * dp * 2 + 4 * tc * gw * 4)
    return pl.pallas_call(
        _pool_seq_kernel,
        grid=(nb, nt),
        in_specs=[pl.BlockSpec((tc, dp), lambda b, t: (b * nt + t, 0)),
                  pl.BlockSpec((ng, gw, gw), lambda b, t: (0, 0, 0)),
                  pl.BlockSpec((1, dp), lambda b, t: (0, 0)),
                  pl.BlockSpec((dp, dm), lambda b, t: (0, 0))],
        out_specs=(pl.BlockSpec((tc, dm), lambda b, t: (b * nt + t, 0)),
                   pl.BlockSpec((1, pool_buf, dp), lambda b, t: (b, 0, 0))),
        out_shape=(jax.ShapeDtypeStruct((n, dm), F32),
                   jax.ShapeDtypeStruct((nb, pool_buf, dp), F32)),
        scratch_shapes=[pltpu.VMEM((tc + POOL_HALO, dp), F32), pltpu.VMEM((tc, dp), BF16)],
        compiler_params=_params(("arbitrary", "arbitrary"), nbytes),
        name="pool_seq",
    )(z, w_pool, pool_scale.reshape(1, dp), w_branch)


def _pool_step_kernel(u_ref, cache_ref, wp_ref, scale_ref, wb_ref, o_ref, new_ref, y_ref):
    dp = u_ref.shape[1]
    gw = wp_ref.shape[1]
    lb = cache_ref.shape[1] // dp
    diffs = []
    for k, w in enumerate(POOL_WINDOWS):
        u = u_ref[:, k * gw:(k + 1) * gw]
        s = u
        for j in range(1, w):
            s = s + cache_ref[:, (lb - j) * dp + k * gw:(lb - j) * dp + (k + 1) * gw]
        diffs.append(s / float(w) - u)
    o_ref[...] = _pool_project(diffs, wp_ref, scale_ref, wb_ref, y_ref)
    new_ref[:, 0:(lb - 1) * dp] = cache_ref[:, dp:lb * dp]
    new_ref[:, (lb - 1) * dp:lb * dp] = u_ref[...]


def _pool_step(z, cache, w_pool, pool_scale, w_branch):
    n = z.shape[0]
    ng, gw, _ = w_pool.shape
    dp = ng * gw
    dm = w_branch.shape[1]
    lbdp = cache.shape[1]
    nbytes = 2 * (n * dp * 4 + 2 * n * lbdp * 4 + ng * gw * gw * 2 + dp * dm * 2 + n * dm * 4) + n * dp * 2
    return pl.pallas_call(
        _pool_step_kernel,
        grid=(1,),
        in_specs=[pl.BlockSpec((n, dp), lambda i: (0, 0)),
                  pl.BlockSpec((n, lbdp), lambda i: (0, 0)),
                  pl.BlockSpec((ng, gw, gw), lambda i: (0, 0, 0)),
                  pl.BlockSpec((1, dp), lambda i: (0, 0)),
                  pl.BlockSpec((dp, dm), lambda i: (0, 0))],
        out_specs=(pl.BlockSpec((n, dm), lambda i: (0, 0)),
                   pl.BlockSpec((n, lbdp), lambda i: (0, 0))),
        out_shape=(jax.ShapeDtypeStruct((n, dm), F32),
                   jax.ShapeDtypeStruct((n, lbdp), F32)),
        scratch_shapes=[pltpu.VMEM((n, dp), BF16)],
        compiler_params=_params(("arbitrary",), nbytes),
        name="pool_step",
    )(z, cache, w_pool, pool_scale.reshape(1, dp), w_branch)


def _cmul_add(xr, xi, ar, ai, br, bi):
    return xr + ar * br - ai * bi, xi + ar * bi + ai * br


def _ssm_seq_kernel(u_ref, bcat_ref, ccat_ref, pw_ref, at_ref, d_ref, gy_ref, st_ref,
                    h_ref, hb_ref, seed_ref, carry_ref):
    nslab = h_ref.shape[0]
    npair = nslab // 2
    t = pl.program_id(2)

    @pl.when(t == 0)
    def _():
        carry_ref[...] = jnp.zeros_like(carry_ref)

    u = u_ref[...]
    bu = _dot(u.astype(BF16), bcat_ref[0])
    for j in range(nslab):
        for r in range(SUBLANES):
            h_ref[j, r * SEG_PITCH:r * SEG_PITCH + SEG_LEN, :] = (
                bu[r * SEG_LEN:(r + 1) * SEG_LEN, j * LANES:(j + 1) * LANES])

    def seg_step(n, state):
        rows = pl.ds(n, SUBLANES, stride=SEG_PITCH)
        new = []
        for j in range(npair):
            sr, si = _cmul_add(h_ref[j, rows, :], h_ref[npair + j, rows, :],
                               at_ref[0, j], at_ref[0, npair + j], state[2 * j], state[2 * j + 1])
            h_ref[j, rows, :] = sr
            h_ref[npair + j, rows, :] = si
            new += [sr, si]
        return tuple(new)

    ends = lax.fori_loop(0, SEG_LEN, seg_step, (jnp.zeros((SUBLANES, LANES), F32),) * nslab)

    last = slice(SEG_LEN - 1, SEG_LEN)
    for j in range(npair):
        cr, ci = carry_ref[j], carry_ref[npair + j]
        for r in range(SUBLANES):
            seed_ref[j, r:r + 1, :] = cr
            seed_ref[npair + j, r:r + 1, :] = ci
            cr, ci = _cmul_add(ends[2 * j][r:r + 1, :], ends[2 * j + 1][r:r + 1, :],
                               pw_ref[0, j, last, :], pw_ref[0, npair + j, last, :], cr, ci)
        carry_ref[j] = cr
        carry_ref[npair + j] = ci
    st_ref[0, 0] = carry_ref[...]

    rows_bf16 = 2 * SUBLANES
    for r in range(SUBLANES):
        for i in range(SEG_LEN // rows_bf16):
            src = slice(r * SEG_PITCH + i * rows_bf16, r * SEG_PITCH + (i + 1) * rows_bf16)
            dst = slice(r * SEG_LEN + i * rows_bf16, r * SEG_LEN + (i + 1) * rows_bf16)
            pws = slice(i * rows_bf16, (i + 1) * rows_bf16)
            for j in range(npair):
                hr, hi = _cmul_add(h_ref[j, src, :], h_ref[npair + j, src, :],
                                   pw_ref[0, j, pws, :], pw_ref[0, npair + j, pws, :],
                                   seed_ref[j, r:r + 1, :], seed_ref[npair + j, r:r + 1, :])
                hb_ref[dst, j * LANES:(j + 1) * LANES] = hr.astype(BF16)
                hb_ref[dst, (npair + j) * LANES:(npair + j + 1) * LANES] = hi.astype(BF16)

    y = _dot(hb_ref[...], ccat_ref[0]) + d_ref[...] * u
    gy_ref[...] = jax.nn.gelu(y).astype(BF16)


def _ssm_seq(z, col0, b_cat, c_cat, pw, a_tile, d_skip, nb, t_len):
    n = z.shape[0]
    nkb, kw, sw = b_cat.shape
    nslab = sw // LANES
    tc = SUBLANES * SEG_LEN
    nt = t_len // tc
    cb = col0 // kw
    nbytes = (2 * tc * kw * 4 + 4 * kw * sw * 2 + 2 * (SEG_LEN + SUBLANES) * sw * 4 + 2 * tc * kw * 2
              + SUBLANES * SEG_PITCH * sw * 4 + tc * sw * 2 + tc * sw * 4)
    gy, st = pl.pallas_call(
        _ssm_seq_kernel,
        grid=(nkb, nb, nt),
        in_specs=[pl.BlockSpec((tc, kw), lambda k, b, t: (b * nt + t, cb + k)),
                  pl.BlockSpec((1, kw, sw), lambda k, b, t: (k, 0, 0)),
                  pl.BlockSpec((1, sw, kw), lambda k, b, t: (k, 0, 0)),
                  pl.BlockSpec((1, nslab, SEG_LEN, LANES), lambda k, b, t: (k, 0, 0, 0)),
                  pl.BlockSpec((1, nslab, SUBLANES, LANES), lambda k, b, t: (k, 0, 0, 0)),
                  pl.BlockSpec((1, kw), lambda k, b, t: (0, k))],
        out_specs=(pl.BlockSpec((tc, kw), lambda k, b, t: (b * nt + t, k)),
                   pl.BlockSpec((1, 1, nslab, 1, LANES), lambda k, b, t: (b, k, 0, 0, 0))),
        out_shape=(jax.ShapeDtypeStruct((n, nkb * kw), BF16),
                   jax.ShapeDtypeStruct((nb, nkb, nslab, 1, LANES), F32)),
        scratch_shapes=[pltpu.VMEM((nslab, SUBLANES * SEG_PITCH, LANES), F32), pltpu.VMEM((tc, sw), BF16),
                        pltpu.VMEM((nslab, SUBLANES, LANES), F32), pltpu.VMEM((nslab, 1, LANES), F32)],
        compiler_params=_params(("arbitrary", "arbitrary", "arbitrary"), nbytes),
        name="ssm_seq",
    )(z, b_cat, c_cat, pw, a_tile, d_skip.reshape(1, nkb * kw))
    st = st.reshape(nb, nkb, 2, sw // 2)
    return gy, st[:, :, 0].reshape(nb, -1), st[:, :, 1].reshape(nb, -1)


def _ssm_step_kernel(u_ref, h0r_ref, h0i_ref, bcat_ref, ccat_ref, a_ref, d_ref,
                     gy_ref, h1r_ref, h1i_ref, h_ref):
    half = h0r_ref.shape[1]
    u = u_ref[...]
    bu = _dot(u.astype(BF16), bcat_ref[0])
    hr, hi = _cmul_add(bu[:, 0:half], bu[:, half:2 * half],
                       a_ref[0, :, 0:half], a_ref[0, :, half:2 * half],
                       h0r_ref[...], h0i_ref[...])
    h1r_ref[...] = hr
    h1i_ref[...] = hi
    h_ref[:, 0:half] = hr.astype(BF16)
    h_ref[:, half:2 * half] = hi.astype(BF16)
    y = _dot(h_ref[...], ccat_ref[0]) + d_ref[...] * u
    gy_ref[...] = jax.nn.gelu(y).astype(BF16)


def _ssm_step(z, col0, h0_re, h0_im, b_cat, c_cat, a_row, d_skip):
    n = z.shape[0]
    nkb, kw, sw = b_cat.shape
    half = sw // 2
    cb = col0 // kw
    nbytes = 2 * (n * kw * 4 + 4 * n * half * 4 + 2 * kw * sw * 2 + sw * 4 + n * kw * 2) + n * sw * 6
    return pl.pallas_call(
        _ssm_step_kernel,
        grid=(nkb,),
        in_specs=[pl.BlockSpec((n, kw), lambda k: (0, cb + k)),
                  pl.BlockSpec((n, half), lambda k: (0, k)),
                  pl.BlockSpec((n, half), lambda k: (0, k)),
                  pl.BlockSpec((1, kw, sw), lambda k: (k, 0, 0)),
                  pl.BlockSpec((1, sw, kw), lambda k: (k, 0, 0)),
                  pl.BlockSpec((1, 1, sw), lambda k: (k, 0, 0)),
                  pl.BlockSpec((1, kw), lambda k: (0, k))],
        out_specs=(pl.BlockSpec((n, kw), lambda k: (0, k)),
                   pl.BlockSpec((n, half), lambda k: (0, k)),
                   pl.BlockSpec((n, half), lambda k: (0, k))),
        out_shape=(jax.ShapeDtypeStruct((n, nkb * kw), BF16),
                   jax.ShapeDtypeStruct((n, nkb * half), F32),
                   jax.ShapeDtypeStruct((n, nkb * half), F32)),
        scratch_shapes=[pltpu.VMEM((n, sw), BF16)],
        compiler_params=_params(("arbitrary",), nbytes),
        name="ssm_step",
    )(z, h0_re, h0_im, b_cat, c_cat, a_row, d_skip.reshape(1, nkb * kw))


def _glu_merge_kernel(gy_ref, bp_ref, gp_ref, gs_ref, wg_ref, wb_ref, o_ref):
    ds = wb_ref.shape[0]
    g = _dot(gy_ref[...], wg_ref[...])
    y = g[:, 0:ds] * jax.nn.sigmoid(g[:, ds:2 * ds])
    bs = _dot(y.astype(BF16), wb_ref[...])
    merged = jax.nn.sigmoid(gp_ref[...]) * bp_ref[...] + jax.nn.sigmoid(gs_ref[...]) * bs
    o_ref[...] = merged.astype(BF16)


def _glu_merge(gy, bp, z, gate_col0, w_glu, w_branch, tm):
    n, ds = gy.shape
    dm = w_branch.shape[1]
    gb = gate_col0 // dm
    nbytes = 2 * (tm * ds * 2 + 3 * tm * dm * 4 + ds * 2 * ds * 2 + ds * dm * 2 + tm * dm * 2) + 4 * tm * dm * 4
    return pl.pallas_call(
        _glu_merge_kernel,
        grid=(n // tm,),
        in_specs=[pl.BlockSpec((tm, ds), lambda i: (i, 0)),
                  pl.BlockSpec((tm, dm), lambda i: (i, 0)),
                  pl.BlockSpec((tm, dm), lambda i: (i, gb)),
                  pl.BlockSpec((tm, dm), lambda i: (i, gb + 1)),
                  pl.BlockSpec((ds, 2 * ds), lambda i: (0, 0)),
                  pl.BlockSpec((ds, dm), lambda i: (0, 0))],
        out_specs=pl.BlockSpec((tm, dm), lambda i: (i, 0)),
        out_shape=jax.ShapeDtypeStruct((n, dm), BF16),
        compiler_params=_params(("arbitrary",), nbytes),
        name="glu_merge",
    )(gy, bp, z, z, w_glu, w_branch)


def _matmul_res_kernel(m_ref, w_ref, x_ref, o_ref):
    o_ref[...] = x_ref[...] + _dot(m_ref[...], w_ref[...])


def _matmul_res(m, w, x, tm, tn):
    n, d = m.shape
    dout = w.shape[1]
    nbytes = 2 * (tm * d * 2 + d * tn * 2 + 2 * tm * tn * 4)
    return pl.pallas_call(
        _matmul_res_kernel,
        grid=(n // tm, dout // tn),
        in_specs=[pl.BlockSpec((tm, d), lambda i, j: (i, 0)),
                  pl.BlockSpec((d, tn), lambda i, j: (0, j)),
                  pl.BlockSpec((tm, tn), lambda i, j: (i, j))],
        out_specs=pl.BlockSpec((tm, tn), lambda i, j: (i, j)),
        out_shape=jax.ShapeDtypeStruct((n, dout), F32),
        compiler_params=_params(("arbitrary", "arbitrary"), nbytes),
        name="matmul_res",
    )(m, w, x)


def _ffn_gate_down(x_ref, conv, v, wd_ref, o_ref, acc_ref):
    c = pl.program_id(1)
    part = _dot((jax.nn.gelu(conv) * v).astype(BF16), wd_ref[...])

    @pl.when(c == 0)
    def _():
        acc_ref[...] = part

    @pl.when(c > 0)
    def _():
        acc_ref[...] += part

    @pl.when(c == pl.num_programs(1) - 1)
    def _():
        o_ref[...] = x_ref[...] + acc_ref[...]


def _ffn_seq_kernel(nc, tiles_per_seq, n_steps, x_ref, g_ref, wa_ref, wv_ref, wc_ref, bc_ref, wd_ref,
                    o_ref, new_ref, h_ref, even_ref, odd_ref, ext_ref, carry_ref, gate_ref):
    tm = x_ref.shape[0]
    s = pl.program_id(0)
    sp = jnp.maximum(s - 1, 0)
    cp = sp % nc
    ip = sp // nc

    @pl.when(s == 0)
    def _():
        odd_ref[...] = jnp.zeros(odd_ref.shape, F32)
        o_ref[...] = jnp.zeros(o_ref.shape, F32)
        carry_ref[...] = jnp.zeros(carry_ref.shape, F32)

    @pl.when(jnp.logical_and(s % nc == 0, s < n_steps))
    def _():
        h_ref[...] = _rmsnorm(x_ref[...], g_ref[...]).astype(BF16)

    def step(cur_ref, prev_ref):
        cur_ref[0] = _dot(h_ref[...], wa_ref[...])
        cur_ref[1] = _dot(h_ref[...], wv_ref[...])

        a = prev_ref[0]
        v = prev_ref[1]
        seq_start = ip % tiles_per_seq == 0
        ext_ref[0:CONV_HALO, :] = jnp.where(seq_start, 0.0, carry_ref[cp])
        ext_ref[CONV_HALO:CONV_HALO + tm, :] = a
        conv = bc_ref[...] + wc_ref[CONV_W - 1:CONV_W, :] * a
        for j in range(CONV_W - 1):
            off = CONV_HALO - (CONV_W - 1) + j
            conv = conv + wc_ref[j:j + 1, :] * ext_ref[off:off + tm, :]
        carry_ref[cp] = ext_ref[tm:tm + CONV_HALO, :]
        nb = new_ref.shape[1]
        new_ref[0] = ext_ref[CONV_HALO + tm - nb:CONV_HALO + tm, :]
        gate_ref[...] = (jax.nn.gelu(conv) * v).astype(BF16)
        part = _dot(gate_ref[...], wd_ref[...])
        o_ref[...] = part + jnp.where(cp == 0, x_ref[...], o_ref[...])

    @pl.when(s % 2 == 0)
    def _():
        step(even_ref, odd_ref)

    @pl.when(s % 2 == 1)
    def _():
        step(odd_ref, even_ref)


def _ffn_seq(x, g, w_up, w_conv, b_conv, w_down, nb, t_len, tm, tf, conv_buf):
    n, d = x.shape
    dff = w_down.shape[0]
    nc = dff // tf
    tps = t_len // tm
    n_steps = (n // tm) * nc
    assert nc >= 2, "the residual tile must still be resident when its first chunk is finished"
    nbytes = (4 * tm * d * 4 + 4 * d * tf * 2 + 2 * tf * d * 2 + tm * d * 2 + 4 * tm * tf * 4
              + (tm + CONV_HALO) * tf * 4 + nc * CONV_HALO * tf * 4 + 4 * tm * tf * 4)

    def cur(s):
        return jnp.minimum(s, n_steps - 1)

    def prev(s):
        return jnp.maximum(s - 1, 0)

    out, new_tail = pl.pallas_call(
        functools.partial(_ffn_seq_kernel, nc, tps, n_steps),
        grid=(n_steps + 1,),
        in_specs=[pl.BlockSpec((tm, d), lambda s: (cur(s) // nc, 0)),
                  pl.BlockSpec((1, d), lambda s: (0, 0)),
                  pl.BlockSpec((d, tf), lambda s: (0, cur(s) % nc)),
                  pl.BlockSpec((d, tf), lambda s: (0, nc + cur(s) % nc)),
                  pl.BlockSpec((CONV_W, tf), lambda s: (0, prev(s) % nc)),
                  pl.BlockSpec((1, tf), lambda s: (0, prev(s) % nc)),
                  pl.BlockSpec((tf, d), lambda s: (prev(s) % nc, 0))],
        out_specs=(pl.BlockSpec((tm, d), lambda s: (prev(s) // nc, 0)),
                   pl.BlockSpec((1, conv_buf, tf), lambda s: (prev(s) // nc, 0, prev(s) % nc))),
        out_shape=(jax.ShapeDtypeStruct((n, d), F32),
                   jax.ShapeDtypeStruct((n // tm, conv_buf, dff), F32)),
        scratch_shapes=[pltpu.VMEM((tm, d), BF16), pltpu.VMEM((2, tm, tf), F32), pltpu.VMEM((2, tm, tf), F32),
                        pltpu.VMEM((tm + CONV_HALO, tf), F32), pltpu.VMEM((nc, CONV_HALO, tf), F32),
                        pltpu.VMEM((tm, tf), BF16)],
        compiler_params=_params(("arbitrary",), nbytes),
        name="ffn_seq",
    )(x, g.reshape(1, d), w_up, w_up, w_conv, b_conv.reshape(1, dff), w_down)
    return out, new_tail[tps - 1::tps]


def _ffn_step_kernel(x_ref, g_ref, wa_ref, wv_ref, wc_ref, bc_ref, wd_ref, p0_ref, p1_ref,
                     o_ref, a_ref, h_ref, acc_ref):
    @pl.when(pl.program_id(1) == 0)
    def _():
        h_ref[...] = _rmsnorm(x_ref[...], g_ref[...]).astype(BF16)

    a = _dot(h_ref[...], wa_ref[...])
    v = _dot(h_ref[...], wv_ref[...])
    conv = bc_ref[...] + wc_ref[2:3, :] * a + wc_ref[1:2, :] * p1_ref[...] + wc_ref[0:1, :] * p0_ref[...]
    a_ref[...] = a
    _ffn_gate_down(x_ref, conv, v, wd_ref, o_ref, acc_ref)


def _ffn_step(x, g, w_up, w_conv, b_conv, w_down, cache, tf):
    n, d = x.shape
    dff = w_down.shape[0]
    nc = dff // tf
    nbytes = 4 * n * d * 4 + 4 * d * tf * 2 + 2 * tf * d * 2 + n * d * 6 + 12 * n * tf * 4
    return pl.pallas_call(
        _ffn_step_kernel,
        grid=(1, nc),
        in_specs=[pl.BlockSpec((n, d), lambda i, c: (0, 0)),
                  pl.BlockSpec((1, d), lambda i, c: (0, 0)),
                  pl.BlockSpec((d, tf), lambda i, c: (0, c)),
                  pl.BlockSpec((d, tf), lambda i, c: (0, nc + c)),
                  pl.BlockSpec((CONV_W, tf), lambda i, c: (0, c)),
                  pl.BlockSpec((1, tf), lambda i, c: (0, c)),
                  pl.BlockSpec((tf, d), lambda i, c: (c, 0)),
                  pl.BlockSpec((n, tf), lambda i, c: (0, c)),
                  pl.BlockSpec((n, tf), lambda i, c: (0, nc + c))],
        out_specs=(pl.BlockSpec((n, d), lambda i, c: (0, 0)),
                   pl.BlockSpec((n, tf), lambda i, c: (0, c))),
        out_shape=(jax.ShapeDtypeStruct((n, d), F32),
                   jax.ShapeDtypeStruct((n, dff), F32)),
        scratch_shapes=[pltpu.VMEM((n, d), BF16), pltpu.VMEM((n, d), F32)],
        compiler_params=_params(("arbitrary", "arbitrary"), nbytes),
        name="ffn_step",
    )(x, g.reshape(1, d), w_up, w_up, w_conv, b_conv.reshape(1, dff), w_down, cache, cache)


def _ple_final_kernel(x_ref, p_ref, gp_ref, wg_ref, wp_ref, gf_ref, o_ref):
    x = x_ref[...]
    gate = jax.nn.sigmoid(_dot(_rmsnorm(x, gp_ref[...]).astype(BF16), wg_ref[...]))
    x = x + gate * _dot(p_ref[...].astype(BF16), wp_ref[...])
    o_ref[...] = _rmsnorm(x, gf_ref[...])


def _ple_final(x, p, g_ple, w_gate, w_ple, g_final, tm):
    n, d = x.shape
    dp = p.shape[1]
    nbytes = 2 * (2 * tm * d * 4 + tm * dp * 4 + d * d * 2 + dp * d * 2) + 4 * tm * d * 4
    return pl.pallas_call(
        _ple_final_kernel,
        grid=(n // tm,),
        in_specs=[pl.BlockSpec((tm, d), lambda i: (i, 0)),
                  pl.BlockSpec((tm, dp), lambda i: (i, 0)),
                  pl.BlockSpec((1, d), lambda i: (0, 0)),
                  pl.BlockSpec((d, d), lambda i: (0, 0)),
                  pl.BlockSpec((dp, d), lambda i: (0, 0)),
                  pl.BlockSpec((1, d), lambda i: (0, 0))],
        out_specs=pl.BlockSpec((tm, d), lambda i: (i, 0)),
        out_shape=jax.ShapeDtypeStruct((n, d), F32),
        compiler_params=_params(("arbitrary",), nbytes),
        name="ple_final",
    )(x, p, g_ple.reshape(1, d), w_gate, w_ple, g_final.reshape(1, d))


def _tile(n, pref):
    return pref if n % pref == 0 else n


def kernel(x_prompt, x_sample, cache_pool, state_ssm_re, state_ssm_im, cache_conv, p_prompt, p_sample, g_mix, w_in, w_pool, pool_scale, ssm_lam_re, ssm_lam_im, ssm_log_dt, ssm_b_re, ssm_b_im, ssm_c_re, ssm_c_im, ssm_d, w_glu, w_branch_pool, w_branch_ssm, w_out, g_ffn, w_up, w_conv, b_conv, w_down, g_ple, w_ple_gate, w_ple, g_final):
    depth = g_mix.shape[0]
    nb, t_len, d = x_prompt.shape
    ns = x_sample.shape[0]
    assert x_sample.shape[1] == 1, "the sample group advances one step per call"
    pool_buf, d_pool = cache_pool.shape[2], cache_pool.shape[3]
    conv_buf, d_ff = cache_conv.shape[2], cache_conv.shape[3]
    n_grp, n_state = ssm_lam_re.shape[1], ssm_lam_re.shape[2]
    d_ssm = ssm_d.shape[1]
    assert pool_buf == max(POOL_WINDOWS) - 1 and conv_buf == CONV_W - 1
    assert n_state == SSM_STATE and d_ssm == n_grp * SSM_GROUP

    xp = x_prompt.reshape(nb * t_len, d)
    xs = x_sample.reshape(ns, d)
    outs = [[] for _ in range(8)]
    for i in range(depth):
        w_in_b, w_pool_b, w_glu_b = w_in[i].astype(BF16), w_pool[i].astype(BF16), w_glu[i].astype(BF16)
        w_bp_b, w_bs_b, w_out_b = w_branch_pool[i].astype(BF16), w_branch_ssm[i].astype(BF16), w_out[i].astype(BF16)
        w_up_b, w_down_b = w_up[i].astype(BF16), w_down[i].astype(BF16)
        w_pg_b, w_ple_b = w_ple_gate[i].astype(BF16), w_ple[i].astype(BF16)
        b_cat, c_cat, a_pow, a_tile, a_row = _ssm_params(ssm_lam_re[i], ssm_lam_im[i], ssm_log_dt[i], ssm_b_re[i], ssm_b_im[i],
                                        ssm_c_re[i], ssm_c_im[i])

        def mix_tail(x, z, bp, gy, tm):
            merged = _glu_merge(gy, bp, z, d_pool + d_ssm, w_glu_b, w_bs_b, tm)
            return _matmul_res(merged, w_out_b, x, _tile(x.shape[0], 1024), W_TILE)

        z = _norm_matmul(xp, g_mix[i], w_in_b, 1024, W_TILE)
        bp, pool_new = _pool_seq(z, w_pool_b, pool_scale[i], w_bp_b, nb, t_len, 512, pool_buf)
        gy, st_re, st_im = _ssm_seq(z, d_pool, b_cat, c_cat, a_pow, a_tile, ssm_d[i], nb, t_len)
        xp = mix_tail(xp, z, bp, gy, 256)
        xp, conv_new = _ffn_seq(xp, g_ffn[i], w_up_b, w_conv[i], b_conv[i], w_down_b, nb, t_len, 512, W_TILE, conv_buf)
        xp_out = _ple_final(xp, p_prompt[i].reshape(nb * t_len, -1), g_ple[i], w_pg_b, w_ple_b, g_final, 256)
        for lst, val in zip(outs[:4], (pool_new, st_re.reshape(nb, n_grp, n_state),
                                       st_im.reshape(nb, n_grp, n_state), conv_new)):
            lst.append(val)

        z = _norm_matmul(xs, g_mix[i], w_in_b, ns, W_TILE)
        bp, pool_new = _pool_step(z, cache_pool[i].reshape(ns, pool_buf * d_pool), w_pool_b, pool_scale[i], w_bp_b)
        gy, st_re, st_im = _ssm_step(z, d_pool, state_ssm_re[i].reshape(ns, -1), state_ssm_im[i].reshape(ns, -1),
                                     b_cat, c_cat, a_row, ssm_d[i])
        xs = mix_tail(xs, z, bp, gy, ns)
        xs, a_new = _ffn_step(xs, g_ffn[i], w_up_b, w_conv[i], b_conv[i], w_down_b,
                              cache_conv[i].reshape(ns, conv_buf * d_ff), W_TILE)
        conv_new = jnp.concatenate([cache_conv[i][:, 1:], a_new[:, None, :]], axis=1)
        xs_out = _ple_final(xs, p_sample[i].reshape(ns, -1), g_ple[i], w_pg_b, w_ple_b, g_final, ns)
        for lst, val in zip(outs[4:], (pool_new.reshape(ns, pool_buf, d_pool), st_re.reshape(ns, n_grp, n_state),
                                       st_im.reshape(ns, n_grp, n_state), conv_new)):
            lst.append(val)

    assert depth == 1
    y_prompt = xp_out.reshape(nb, t_len, d)
    y_sample = xs_out.reshape(ns, 1, d)
    return (y_prompt, y_sample) + tuple(jnp.stack(o, axis=0) for o in outs)
```

```python
import functools

import jax
import jax.numpy as jnp
from jax import lax
from jax.experimental import pallas as pl
from jax.experimental.pallas import tpu as pltpu

F32 = jnp.float32
BF16 = jnp.bfloat16

EPS = 1e-6
POOL_WINDOWS = (2, 4, 8, 16)
POOL_HALO = 16
SSM_GROUP = 16
SSM_STATE = 64
SSM_BLOCK_GROUPS = 16
LANES = 128
SUBLANES = 8
SEG_LEN = 64
SEG_PITCH = SEG_LEN + 8
CONV_W = 3
CONV_HALO = 8
W_TILE = 512
V7X_VMEM_BYTES = 64 * 1024 * 1024


def _vmem_limit(nbytes):
    return int(min(nbytes * 1.25 + (8 << 20), V7X_VMEM_BYTES - (6 << 20)))


def _params(sem, nbytes):
    return pltpu.CompilerParams(dimension_semantics=sem, vmem_limit_bytes=_vmem_limit(nbytes))


def _rmsnorm(x, g):
    return x * lax.rsqrt(jnp.mean(x * x, axis=-1, keepdims=True) + EPS) * g


def _dot(a, b):
    return jnp.dot(a, b, preferred_element_type=F32)


def _ssm_params_kernel(lr_ref, li_ref, logdt_ref, br_ref, bi_ref, pwr_ref, pwi_ref, bbr_ref, bbi_ref):
    lr = lr_ref[...]
    li = li_ref[...]
    dt = jnp.exp(logdt_ref[...])
    mag = jnp.exp(lr * dt)
    a_re = mag * jnp.cos(li * dt)
    a_im = mag * jnp.sin(li * dt)
    nr = a_re - 1.0
    ni = a_im
    den = lr * lr + li * li
    coef_re = (nr * lr + ni * li) / den
    coef_im = (ni * lr - nr * li) / den
    br = br_ref[...]
    bi = bi_ref[...]
    bbr_ref[...] = coef_re[None] * br - coef_im[None] * bi
    bbi_ref[...] = coef_re[None] * bi + coef_im[None] * br
    pr, pi = a_re, a_im
    pwr_ref[0] = pr
    pwi_ref[0] = pi
    for n in range(1, SEG_LEN):
        pr, pi = pr * a_re - pi * a_im, pr * a_im + pi * a_re
        pwr_ref[n] = pr
        pwi_ref[n] = pi


def _ssm_params(lam_re, lam_im, log_dt, b_re, b_im, c_re, c_im):
    g, p = lam_re.shape
    h = b_re.shape[-1]
    nkb = g // SSM_BLOCK_GROUPS
    bl = SSM_BLOCK_GROUPS
    pwr, pwi, bbr, bbi = pl.pallas_call(
        _ssm_params_kernel,
        out_shape=(jax.ShapeDtypeStruct((SEG_LEN, g, p), F32),
                   jax.ShapeDtypeStruct((SEG_LEN, g, p), F32),
                   jax.ShapeDtypeStruct((h, g, p), F32),
                   jax.ShapeDtypeStruct((h, g, p), F32)),
        name="ssm_params",
    )(lam_re, lam_im, log_dt.reshape(g, 1), jnp.transpose(b_re, (2, 0, 1)), jnp.transpose(b_im, (2, 0, 1)))

    eye = jnp.eye(bl, dtype=F32)
    bb = jnp.stack([bbr, bbi]).reshape(2, h, nkb, bl, p)
    b_cat = jnp.einsum('shkgp,gj->kghsjp', bb, eye).reshape(nkb, bl * h, 2 * bl * p).astype(BF16)
    cc = jnp.stack([c_re, -c_im]).reshape(2, nkb, bl, h, p)
    c_cat = jnp.einsum('skghp,gj->ksgpjh', cc, eye).reshape(nkb, 2 * bl * p, bl * h).astype(BF16)
    nslab = 2 * bl * p // LANES
    pw = jnp.stack([pwr, pwi], axis=1).reshape(SEG_LEN, 2, nkb, bl * p)
    a_row = jnp.transpose(pw[0], (1, 0, 2)).reshape(nkb, 1, 2 * bl * p)
    pw = jnp.transpose(pw.reshape(SEG_LEN, 2, nkb, nslab // 2, LANES), (2, 1, 3, 0, 4))
    pw = pw.reshape(nkb, nslab, SEG_LEN, LANES)
    a_tile = jnp.broadcast_to(pw[:, :, 0:1, :], (nkb, nslab, SUBLANES, LANES))
    return b_cat, c_cat, pw, a_tile, a_row


def _norm_matmul_kernel(x_ref, g_ref, w_ref, o_ref, h_ref):
    @pl.when(pl.program_id(1) == 0)
    def _():
        h_ref[...] = _rmsnorm(x_ref[...], g_ref[...]).astype(BF16)
    o_ref[...] = _dot(h_ref[...], w_ref[...])


def _norm_matmul(x, g, w, tm, tn):
    n, d = x.shape
    dout = w.shape[1]
    nbytes = 2 * tm * d * 4 + tm * d * 2 + 2 * d * tn * 2 + 2 * tm * tn * 4
    return pl.pallas_call(
        _norm_matmul_kernel,
        grid=(n // tm, dout // tn),
        in_specs=[pl.BlockSpec((tm, d), lambda i, j: (i, 0)),
                  pl.BlockSpec((1, d), lambda i, j: (0, 0)),
                  pl.BlockSpec((d, tn), lambda i, j: (0, j))],
        out_specs=pl.BlockSpec((tm, tn), lambda i, j: (i, j)),
        out_shape=jax.ShapeDtypeStruct((n, dout), F32),
        scratch_shapes=[pltpu.VMEM((tm, d), BF16)],
        compiler_params=_params(("arbitrary", "arbitrary"), nbytes),
        name="norm_matmul",
    )(x, g.reshape(1, d), w)


def _pool_project(diffs, wp_ref, scale_ref, wb_ref, y_ref):
    gw = wp_ref.shape[1]
    for k, diff in enumerate(diffs):
        yk = _dot(diff.astype(BF16), wp_ref[k]) * scale_ref[:, k * gw:(k + 1) * gw]
        y_ref[:, k * gw:(k + 1) * gw] = yk.astype(BF16)
    return _dot(y_ref[...], wb_ref[...])


def _pool_seq_kernel(u_ref, wp_ref, scale_ref, wb_ref, o_ref, new_ref, ext_ref, y_ref):
    tc, dp = u_ref.shape
    gw = wp_ref.shape[1]
    t = pl.program_id(1)

    @pl.when(t == 0)
    def _():
        ext_ref[0:POOL_HALO, :] = jnp.zeros((POOL_HALO, dp), F32)

    ext_ref[POOL_HALO:POOL_HALO + tc, :] = u_ref[...]
    pos = (t * tc + 1 + lax.broadcasted_iota(jnp.int32, (tc, 1), 0)).astype(F32)
    diffs = []
    for k, w in enumerate(POOL_WINDOWS):
        cols = slice(k * gw, (k + 1) * gw)
        u = ext_ref[POOL_HALO:POOL_HALO + tc, cols]
        s = u
        for j in range(1, w):
            s = s + ext_ref[POOL_HALO - j:POOL_HALO - j + tc, cols]
        count = jnp.minimum(pos, float(w))
        diffs.append(s / count - u)
    o_ref[...] = _pool_project(diffs, wp_ref, scale_ref, wb_ref, y_ref)
    nb = new_ref.shape[1]
    new_ref[0] = ext_ref[POOL_HALO + tc - nb:POOL_HALO + tc, :]
    ext_ref[0:POOL_HALO, :] = ext_ref[tc:tc + POOL_HALO, :]


def _pool_seq(z, w_pool, pool_scale, w_branch, nb, t_len, tc, pool_buf):
    n = z.shape[0]
    ng, gw, _ = w_pool.shape
    dp = ng * gw
    dm = w_branch.shape[1]
    nt = t_len // tc
    nbytes = (2 * tc * dp * 4 + 2 * ng * gw * gw * 2 + 2 * dp * dm * 2 + 2 * tc * dm * 4
              + (tc + POOL_HALO) * dp * 4 + tc * dp * 2 + 4 * tc * gw * 4)
    return pl.pallas_call(
        _pool_seq_kernel,
        grid=(nb, nt),
        in_specs=[pl.BlockSpec((tc, dp), lambda b, t: (b * nt + t, 0)),
                  pl.BlockSpec((ng, gw, gw), lambda b, t: (0, 0, 0)),
                  pl.BlockSpec((1, dp), lambda b, t: (0, 0)),
                  pl.BlockSpec((dp, dm), lambda b, t: (0, 0))],
        out_specs=(pl.BlockSpec((tc, dm), lambda b, t: (b * nt + t, 0)),
                   pl.BlockSpec((1, pool_buf, dp), lambda b, t: (b, 0, 0))),
        out_shape=(jax.ShapeDtypeStruct((n, dm), F32),
                   jax.ShapeDtypeStruct((nb, pool_buf, dp), F32)),
        scratch_shapes=[pltpu.VMEM((tc + POOL_HALO, dp), F32), pltpu.VMEM((tc, dp), BF16)],
        compiler_params=_params(("arbitrary", "arbitrary"), nbytes),
        name="pool_seq",
    )(z, w_pool, pool_scale.reshape(1, dp), w_branch)


def _pool_step_kernel(u_ref, cache_ref, wp_ref, scale_ref, wb_ref, o_ref, new_ref, y_ref):
    dp = u_ref.shape[1]
    gw = wp_ref.shape[1]
    lb = cache_ref.shape[1] // dp
    diffs = []
    for k, w in enumerate(POOL_WINDOWS):
        u = u_ref[:, k * gw:(k + 1) * gw]
        s = u
        for j in range(1, w):
            s = s + cache_ref[:, (lb - j) * dp + k * gw:(lb - j) * dp + (k + 1) * gw]
        diffs.append(s / float(w) - u)
    o_ref[...] = _pool_project(diffs, wp_ref, scale_ref, wb_ref, y_ref)
    new_ref[:, 0:(lb - 1) * dp] = cache_ref[:, dp:lb * dp]
    new_ref[:, (lb - 1) * dp:lb * dp] = u_ref[...]


def _pool_step(z, cache, w_pool, pool_scale, w_branch):
    n = z.shape[0]
    ng, gw, _ = w_pool.shape
    dp = ng * gw
    dm = w_branch.shape[1]
    lbdp = cache.shape[1]
    nbytes = 2 * (n * dp * 4 + 2 * n * lbdp * 4 + ng * gw * gw * 2 + dp * dm * 2 + n * dm * 4) + n * dp * 2
    return pl.pallas_call(
        _pool_step_kernel,
        grid=(1,),
        in_specs=[pl.BlockSpec((n, dp), lambda i: (0, 0)),
                  pl.BlockSpec((n, lbdp), lambda i: (0, 0)),
                  pl.BlockSpec((ng, gw, gw), lambda i: (0, 0, 0)),
                  pl.BlockSpec((1, dp), lambda i: (0, 0)),
                  pl.BlockSpec((dp, dm), lambda i: (0, 0))],
        out_specs=(pl.BlockSpec((n, dm), lambda i: (0, 0)),
                   pl.BlockSpec((n, lbdp), lambda i: (0, 0))),
        out_shape=(jax.ShapeDtypeStruct((n, dm), F32),
                   jax.ShapeDtypeStruct((n, lbdp), F32)),
        scratch_shapes=[pltpu.VMEM((n, dp), BF16)],
        compiler_params=_params(("arbitrary",), nbytes),
        name="pool_step",
    )(z, cache, w_pool, pool_scale.reshape(1, dp), w_branch)


def _cmul_add(xr, xi, ar, ai, br, bi):
    return xr + ar * br - ai * bi, xi + ar * bi + ai * br


def _ssm_seq_kernel(u_ref, bcat_ref, ccat_ref, pw_ref, at_ref, d_ref, gy_ref, st_ref,
                    h_ref, hb_ref, seed_ref, carry_ref):
    nslab = h_ref.shape[0]
    npair = nslab // 2
    t = pl.program_id(2)

    @pl.when(t == 0)
    def _():
        carry_ref[...] = jnp.zeros_like(carry_ref)

    u = u_ref[...]
    bu = _dot(u.astype(BF16), bcat_ref[0])
    for j in range(nslab):
        for r in range(SUBLANES):
            h_ref[j, r * SEG_PITCH:r * SEG_PITCH + SEG_LEN, :] = (
                bu[r * SEG_LEN:(r + 1) * SEG_LEN, j * LANES:(j + 1) * LANES])

    def seg_step(n, state):
        rows = pl.ds(n, SUBLANES, stride=SEG_PITCH)
        new = []
        for j in range(npair):
            sr, si = _cmul_add(h_ref[j, rows, :], h_ref[npair + j, rows, :],
                               at_ref[0, j], at_ref[0, npair + j], state[2 * j], state[2 * j + 1])
            h_ref[j, rows, :] = sr
            h_ref[npair + j, rows, :] = si
            new += [sr, si]
        return tuple(new)

    ends = lax.fori_loop(0, SEG_LEN, seg_step, (jnp.zeros((SUBLANES, LANES), F32),) * nslab)

    last = slice(SEG_LEN - 1, SEG_LEN)
    for j in range(npair):
        cr, ci = carry_ref[j], carry_ref[npair + j]
        for r in range(SUBLANES):
            seed_ref[j, r:r + 1, :] = cr
            seed_ref[npair + j, r:r + 1, :] = ci
            cr, ci = _cmul_add(ends[2 * j][r:r + 1, :], ends[2 * j + 1][r:r + 1, :],
                               pw_ref[0, j, last, :], pw_ref[0, npair + j, last, :], cr, ci)
        carry_ref[j] = cr
        carry_ref[npair + j] = ci
    st_ref[0, 0] = carry_ref[...]

    rows_bf16 = 2 * SUBLANES
    for r in range(SUBLANES):
        for i in range(SEG_LEN // rows_bf16):
            src = slice(r * SEG_PITCH + i * rows_bf16, r * SEG_PITCH + (i + 1) * rows_bf16)
            dst = slice(r * SEG_LEN + i * rows_bf16, r * SEG_LEN + (i + 1) * rows_bf16)
            pws = slice(i * rows_bf16, (i + 1) * rows_bf16)
            for j in range(npair):
                hr, hi = _cmul_add(h_ref[j, src, :], h_ref[npair + j, src, :],
                                   pw_ref[0, j, pws, :], pw_ref[0, npair + j, pws, :],
                                   seed_ref[j, r:r + 1, :], seed_ref[npair + j, r:r + 1, :])
                hb_ref[dst, j * LANES:(j + 1) * LANES] = hr.astype(BF16)
                hb_ref[dst, (npair + j) * LANES:(npair + j + 1) * LANES] = hi.astype(BF16)

    y = _dot(hb_ref[...], ccat_ref[0]) + d_ref[...] * u
    gy_ref[...] = jax.nn.gelu(y).astype(BF16)


def _ssm_seq(z, col0, b_cat, c_cat, pw, a_tile, d_skip, nb, t_len):
    n = z.shape[0]
    nkb, kw, sw = b_cat.shape
    nslab = sw // LANES
    tc = SUBLANES * SEG_LEN
    nt = t_len // tc
    cb = col0 // kw
    nbytes = (2 * tc * kw * 4 + 4 * kw * sw * 2 + 2 * (SEG_LEN + SUBLANES) * sw * 4 + 2 * tc * kw * 2
              + SUBLANES * SEG_PITCH * sw * 4 + tc * sw * 2 + tc * sw * 4)
    gy, st = pl.pallas_call(
        _ssm_seq_kernel,
        grid=(nkb, nb, nt),
        in_specs=[pl.BlockSpec((tc, kw), lambda k, b, t: (b * nt + t, cb + k)),
                  pl.BlockSpec((1, kw, sw), lambda k, b, t: (k, 0, 0)),
                  pl.BlockSpec((1, sw, kw), lambda k, b, t: (k, 0, 0)),
                  pl.BlockSpec((1, nslab, SEG_LEN, LANES), lambda k, b, t: (k, 0, 0, 0)),
                  pl.BlockSpec((1, nslab, SUBLANES, LANES), lambda k, b, t: (k, 0, 0, 0)),
                  pl.BlockSpec((1, kw), lambda k, b, t: (0, k))],
        out_specs=(pl.BlockSpec((tc, kw), lambda k, b, t: (b * nt + t, k)),
                   pl.BlockSpec((1, 1, nslab, 1, LANES), lambda k, b, t: (b, k, 0, 0, 0))),
        out_shape=(jax.ShapeDtypeStruct((n, nkb * kw), BF16),
                   jax.ShapeDtypeStruct((nb, nkb, nslab, 1, LANES), F32)),
        scratch_shapes=[pltpu.VMEM((nslab, SUBLANES * SEG_PITCH, LANES), F32), pltpu.VMEM((tc, sw), BF16),
                        pltpu.VMEM((nslab, SUBLANES, LANES), F32), pltpu.VMEM((nslab, 1, LANES), F32)],
        compiler_params=_params(("arbitrary", "arbitrary", "arbitrary"), nbytes),
        name="ssm_seq",
    )(z, b_cat, c_cat, pw, a_tile, d_skip.reshape(1, nkb * kw))
    st = st.reshape(nb, nkb, 2, sw // 2)
    return gy, st[:, :, 0].reshape(nb, -1), st[:, :, 1].reshape(nb, -1)


def _ssm_step_kernel(u_ref, h0r_ref, h0i_ref, bcat_ref, ccat_ref, a_ref, d_ref,
                     gy_ref, h1r_ref, h1i_ref, h_ref):
    half = h0r_ref.shape[1]
    u = u_ref[...]
    bu = _dot(u.astype(BF16), bcat_ref[0])
    hr, hi = _cmul_add(bu[:, 0:half], bu[:, half:2 * half],
                       a_ref[0, :, 0:half], a_ref[0, :, half:2 * half],
                       h0r_ref[...], h0i_ref[...])
    h1r_ref[...] = hr
    h1i_ref[...] = hi
    h_ref[:, 0:half] = hr.astype(BF16)
    h_ref[:, half:2 * half] = hi.astype(BF16)
    y = _dot(h_ref[...], ccat_ref[0]) + d_ref[...] * u
    gy_ref[...] = jax.nn.gelu(y).astype(BF16)


def _ssm_step(z, col0, h0_re, h0_im, b_cat, c_cat, a_row, d_skip):
    n = z.shape[0]
    nkb, kw, sw = b_cat.shape
    half = sw // 2
    cb = col0 // kw
    nbytes = 2 * (n * kw * 4 + 4 * n * half * 4 + 2 * kw * sw * 2 + sw * 4 + n * kw * 2) + n * sw * 6
    return pl.pallas_call(
        _ssm_step_kernel,
        grid=(nkb,),
        in_specs=[pl.BlockSpec((n, kw), lambda k: (0, cb + k)),
                  pl.BlockSpec((n, half), lambda k: (0, k)),
                  pl.BlockSpec((n, half), lambda k: (0, k)),
                  pl.BlockSpec((1, kw, sw), lambda k: (k, 0, 0)),
                  pl.BlockSpec((1, sw, kw), lambda k: (k, 0, 0)),
                  pl.BlockSpec((1, 1, sw), lambda k: (k, 0, 0)),
                  pl.BlockSpec((1, kw), lambda k: (0, k))],
        out_specs=(pl.BlockSpec((n, kw), lambda k: (0, k)),
                   pl.BlockSpec((n, half), lambda k: (0, k)),
                   pl.BlockSpec((n, half), lambda k: (0, k))),
        out_shape=(jax.ShapeDtypeStruct((n, nkb * kw), BF16),
                   jax.ShapeDtypeStruct((n, nkb * half), F32),
                   jax.ShapeDtypeStruct((n, nkb * half), F32)),
        scratch_shapes=[pltpu.VMEM((n, sw), BF16)],
        compiler_params=_params(("arbitrary",), nbytes),
        name="ssm_step",
    )(z, h0_re, h0_im, b_cat, c_cat, a_row, d_skip.reshape(1, nkb * kw))


def _glu_merge_kernel(gy_ref, bp_ref, gp_ref, gs_ref, wg_ref, wb_ref, o_ref):
    ds = wb_ref.shape[0]
    g = _dot(gy_ref[...], wg_ref[...])
    y = g[:, 0:ds] * jax.nn.sigmoid(g[:, ds:2 * ds])
    bs = _dot(y.astype(BF16), wb_ref[...])
    merged = jax.nn.sigmoid(gp_ref[...]) * bp_ref[...] + jax.nn.sigmoid(gs_ref[...]) * bs
    o_ref[...] = merged.astype(BF16)


def _glu_merge(gy, bp, z, gate_col0, w_glu, w_branch, tm):
    n, ds = gy.shape
    dm = w_branch.shape[1]
    gb = gate_col0 // dm
    nbytes = 2 * (tm * ds * 2 + 3 * tm * dm * 4 + ds * 2 * ds * 2 + ds * dm * 2 + tm * dm * 2) + 4 * tm * dm * 4
    return pl.pallas_call(
        _glu_merge_kernel,
        grid=(n // tm,),
        in_specs=[pl.BlockSpec((tm, ds), lambda i: (i, 0)),
                  pl.BlockSpec((tm, dm), lambda i: (i, 0)),
                  pl.BlockSpec((tm, dm), lambda i: (i, gb)),
                  pl.BlockSpec((tm, dm), lambda i: (i, gb + 1)),
                  pl.BlockSpec((ds, 2 * ds), lambda i: (0, 0)),
                  pl.BlockSpec((ds, dm), lambda i: (0, 0))],
        out_specs=pl.BlockSpec((tm, dm), lambda i: (i, 0)),
        out_shape=jax.ShapeDtypeStruct((n, dm), BF16),
        compiler_params=_params(("arbitrary",), nbytes),
        name="glu_merge",
    )(gy, bp, z, z, w_glu, w_branch)


def _matmul_res_kernel(m_ref, w_ref, x_ref, o_ref):
    o_ref[...] = x_ref[...] + _dot(m_ref[...], w_ref[...])


def _matmul_res(m, w, x, tm, tn):
    n, d = m.shape
    dout = w.shape[1]
    nbytes = 2 * (tm * d * 2 + d * tn * 2 + 2 * tm * tn * 4)
    return pl.pallas_call(
        _matmul_res_kernel,
        grid=(n // tm, dout // tn),
        in_specs=[pl.BlockSpec((tm, d), lambda i, j: (i, 0)),
                  pl.BlockSpec((d, tn), lambda i, j: (0, j)),
                  pl.BlockSpec((tm, tn), lambda i, j: (i, j))],
        out_specs=pl.BlockSpec((tm, tn), lambda i, j: (i, j)),
        out_shape=jax.ShapeDtypeStruct((n, dout), F32),
        compiler_params=_params(("arbitrary", "arbitrary"), nbytes),
        name="matmul_res",
    )(m, w, x)


def _ffn_gate_down(x_ref, conv, v, wd_ref, o_ref, acc_ref):
    c = pl.program_id(1)
    part = _dot((jax.nn.gelu(conv) * v).astype(BF16), wd_ref[...])

    @pl.when(c == 0)
    def _():
        acc_ref[...] = part

    @pl.when(c > 0)
    def _():
        acc_ref[...] += part

    @pl.when(c == pl.num_programs(1) - 1)
    def _():
        o_ref[...] = x_ref[...] + acc_ref[...]


def _ffn_seq_kernel(nc, tiles_per_seq, n_steps, x_ref, g_ref, wa_ref, wv_ref, wc_ref, bc_ref, wd_ref,
                    o_ref, new_ref, h_ref, even_ref, odd_ref, ext_ref, carry_ref, gate_ref):
    tm = x_ref.shape[0]
    s = pl.program_id(0)
    sp = jnp.maximum(s - 1, 0)
    cp = sp % nc
    ip = sp // nc

    @pl.when(s == 0)
    def _():
        odd_ref[...] = jnp.zeros(odd_ref.shape, F32)
        o_ref[...] = jnp.zeros(o_ref.shape, F32)
        carry_ref[...] = jnp.zeros(carry_ref.shape, F32)

    @pl.when(jnp.logical_and(s % nc == 0, s < n_steps))
    def _():
        h_ref[...] = _rmsnorm(x_ref[...], g_ref[...]).astype(BF16)

    def step(cur_ref, prev_ref):
        cur_ref[0] = _dot(h_ref[...], wa_ref[...])
        cur_ref[1] = _dot(h_ref[...], wv_ref[...])

        a = prev_ref[0]
        v = prev_ref[1]
        seq_start = ip % tiles_per_seq == 0
        ext_ref[0:CONV_HALO, :] = jnp.where(seq_start, 0.0, carry_ref[cp])
        ext_ref[CONV_HALO:CONV_HALO + tm, :] = a
        conv = bc_ref[...] + wc_ref[CONV_W - 1:CONV_W, :] * a
        for j in range(CONV_W - 1):
            off = CONV_HALO - (CONV_W - 1) + j
            conv = conv + wc_ref[j:j + 1, :] * ext_ref[off:off + tm, :]
        carry_ref[cp] = ext_ref[tm:tm + CONV_HALO, :]
        nb = new_ref.shape[1]
        new_ref[0] = ext_ref[CONV_HALO + tm - nb:CONV_HALO + tm, :]
        gate_ref[...] = (jax.nn.gelu(conv) * v).astype(BF16)
        part = _dot(gate_ref[...], wd_ref[...])
        o_ref[...] = part + jnp.where(cp == 0, x_ref[...], o_ref[...])

    @pl.when(s % 2 == 0)
    def _():
        step(even_ref, odd_ref)

    @pl.when(s % 2 == 1)
    def _():
        step(odd_ref, even_ref)


def _ffn_seq(x, g, w_up, w_conv, b_conv, w_down, nb, t_len, tm, tf, conv_buf):
    n, d = x.shape
    dff = w_down.shape[0]
    nc = dff // tf
    tps = t_len // tm
    n_steps = (n // tm) * nc
    assert nc >= 2, "the residual tile must still be resident when its first chunk is finished"
    nbytes = (4 * tm * d * 4 + 4 * d * tf * 2 + 2 * tf * d * 2 + tm * d * 2 + 4 * tm * tf * 4
              + (tm + CONV_HALO) * tf * 4 + nc * CONV_HALO * tf * 4 + 4 * tm * tf * 4)

    def cur(s):
        return jnp.minimum(s, n_steps - 1)

    def prev(s):
        return jnp.maximum(s - 1, 0)

    out, new_tail = pl.pallas_call(
        functools.partial(_ffn_seq_kernel, nc, tps, n_steps),
        grid=(n_steps + 1,),
        in_specs=[pl.BlockSpec((tm, d), lambda s: (cur(s) // nc, 0)),
                  pl.BlockSpec((1, d), lambda s: (0, 0)),
                  pl.BlockSpec((d, tf), lambda s: (0, cur(s) % nc)),
                  pl.BlockSpec((d, tf), lambda s: (0, nc + cur(s) % nc)),
                  pl.BlockSpec((CONV_W, tf), lambda s: (0, prev(s) % nc)),
                  pl.BlockSpec((1, tf), lambda s: (0, prev(s) % nc)),
                  pl.BlockSpec((tf, d), lambda s: (prev(s) % nc, 0))],
        out_specs=(pl.BlockSpec((tm, d), lambda s: (prev(s) // nc, 0)),
                   pl.BlockSpec((1, conv_buf, tf), lambda s: (prev(s) // nc, 0, prev(s) % nc))),
        out_shape=(jax.ShapeDtypeStruct((n, d), F32),
                   jax.ShapeDtypeStruct((n // tm, conv_buf, dff), F32)),
        scratch_shapes=[pltpu.VMEM((tm, d), BF16), pltpu.VMEM((2, tm, tf), F32), pltpu.VMEM((2, tm, tf), F32),
                        pltpu.VMEM((tm + CONV_HALO, tf), F32), pltpu.VMEM((nc, CONV_HALO, tf), F32),
                        pltpu.VMEM((tm, tf), BF16)],
        compiler_params=_params(("arbitrary",), nbytes),
        name="ffn_seq",
    )(x, g.reshape(1, d), w_up, w_up, w_conv, b_conv.reshape(1, dff), w_down)
    return out, new_tail[tps - 1::tps]


def _ffn_step_kernel(x_ref, g_ref, wa_ref, wv_ref, wc_ref, bc_ref, wd_ref, p0_ref, p1_ref,
                     o_ref, a_ref, h_ref, acc_ref):
    @pl.when(pl.program_id(1) == 0)
    def _():
        h_ref[...] = _rmsnorm(x_ref[...], g_ref[...]).astype(BF16)

    a = _dot(h_ref[...], wa_ref[...])
    v = _dot(h_ref[...], wv_ref[...])
    conv = bc_ref[...] + wc_ref[2:3, :] * a + wc_ref[1:2, :] * p1_ref[...] + wc_ref[0:1, :] * p0_ref[...]
    a_ref[...] = a
    _ffn_gate_down(x_ref, conv, v, wd_ref, o_ref, acc_ref)


def _ffn_step(x, g, w_up, w_conv, b_conv, w_down, cache, tf):
    n, d = x.shape
    dff = w_down.shape[0]
    nc = dff // tf
    nbytes = 4 * n * d * 4 + 4 * d * tf * 2 + 2 * tf * d * 2 + n * d * 6 + 12 * n * tf * 4
    return pl.pallas_call(
        _ffn_step_kernel,
        grid=(1, nc),
        in_specs=[pl.BlockSpec((n, d), lambda i, c: (0, 0)),
                  pl.BlockSpec((1, d), lambda i, c: (0, 0)),
                  pl.BlockSpec((d, tf), lambda i, c: (0, c)),
                  pl.BlockSpec((d, tf), lambda i, c: (0, nc + c)),
                  pl.BlockSpec((CONV_W, tf), lambda i, c: (0, c)),
                  pl.BlockSpec((1, tf), lambda i, c: (0, c)),
                  pl.BlockSpec((tf, d), lambda i, c: (c, 0)),
                  pl.BlockSpec((n, tf), lambda i, c: (0, c)),
                  pl.BlockSpec((n, tf), lambda i, c: (0, nc + c))],
        out_specs=(pl.BlockSpec((n, d), lambda i, c: (0, 0)),
                   pl.BlockSpec((n, tf), lambda i, c: (0, c))),
        out_shape=(jax.ShapeDtypeStruct((n, d), F32),
                   jax.ShapeDtypeStruct((n, dff), F32)),
        scratch_shapes=[pltpu.VMEM((n, d), BF16), pltpu.VMEM((n, d), F32)],
        compiler_params=_params(("arbitrary", "arbitrary"), nbytes),
        name="ffn_step",
    )(x, g.reshape(1, d), w_up, w_up, w_conv, b_conv.reshape(1, dff), w_down, cache, cache)


def _ple_final_kernel(x_ref, p_ref, gp_ref, wg_ref, wp_ref, gf_ref, o_ref):
    x = x_ref[...]
    gate = jax.nn.sigmoid(_dot(_rmsnorm(x, gp_ref[...]).astype(BF16), wg_ref[...]))
    x = x + gate * _dot(p_ref[...].astype(BF16), wp_ref[...])
    o_ref[...] = _rmsnorm(x, gf_ref[...])


def _ple_final(x, p, g_ple, w_gate, w_ple, g_final, tm):
    n, d = x.shape
    dp = p.shape[1]
    nbytes = 2 * (2 * tm * d * 4 + tm * dp * 4 + d * d * 2 + dp * d * 2) + 4 * tm * d * 4
    return pl.pallas_call(
        _ple_final_kernel,
        grid=(n // tm,),
        in_specs=[pl.BlockSpec((tm, d), lambda i: (i, 0)),
                  pl.BlockSpec((tm, dp), lambda i: (i, 0)),
                  pl.BlockSpec((1, d), lambda i: (0, 0)),
                  pl.BlockSpec((d, d), lambda i: (0, 0)),
                  pl.BlockSpec((dp, d), lambda i: (0, 0)),
                  pl.BlockSpec((1, d), lambda i: (0, 0))],
        out_specs=pl.BlockSpec((tm, d), lambda i: (i, 0)),
        out_shape=jax.ShapeDtypeStruct((n, d), F32),
        compiler_params=_params(("arbitrary",), nbytes),
        name="ple_final",
    )(x, p, g_ple.reshape(1, d), w_gate, w_ple, g_final.reshape(1, d))


def _tile(n, pref):
    return pref if n % pref == 0 else n


def kernel(x_prompt, x_sample, cache_pool, state_ssm_re, state_ssm_im, cache_conv, p_prompt, p_sample, g_mix, w_in, w_pool, pool_scale, ssm_lam_re, ssm_lam_im, ssm_log_dt, ssm_b_re, ssm_b_im, ssm_c_re, ssm_c_im, ssm_d, w_glu, w_branch_pool, w_branch_ssm, w_out, g_ffn, w_up, w_conv, b_conv, w_down, g_ple, w_ple_gate, w_ple, g_final):
    depth = g_mix.shape[0]
    nb, t_len, d = x_prompt.shape
    ns = x_sample.shape[0]
    assert x_sample.shape[1] == 1, "the sample group advances one step per call"
    pool_buf, d_pool = cache_pool.shape[2], cache_pool.shape[3]
    conv_buf, d_ff = cache_conv.shape[2], cache_conv.shape[3]
    n_grp, n_state = ssm_lam_re.shape[1], ssm_lam_re.shape[2]
    d_ssm = ssm_d.shape[1]
    assert pool_buf == max(POOL_WINDOWS) - 1 and conv_buf == CONV_W - 1
    assert n_state == SSM_STATE and d_ssm == n_grp * SSM_GROUP

    xp = x_prompt.reshape(nb * t_len, d)
    xs = x_sample.reshape(ns, d)
    outs = [[] for _ in range(8)]
    for i in range(depth):
        w_in_b, w_pool_b, w_glu_b = w_in[i].astype(BF16), w_pool[i].astype(BF16), w_glu[i].astype(BF16)
        w_bp_b, w_bs_b, w_out_b = w_branch_pool[i].astype(BF16), w_branch_ssm[i].astype(BF16), w_out[i].astype(BF16)
        w_up_b, w_down_b = w_up[i].astype(BF16), w_down[i].astype(BF16)
        w_pg_b, w_ple_b = w_ple_gate[i].astype(BF16), w_ple[i].astype(BF16)
        b_cat, c_cat, a_pow, a_tile, a_row = _ssm_params(ssm_lam_re[i], ssm_lam_im[i], ssm_log_dt[i], ssm_b_re[i], ssm_b_im[i],
                                        ssm_c_re[i], ssm_c_im[i])

        def mix_tail(x, z, bp, gy, tm):
            merged = _glu_merge(gy, bp, z, d_pool + d_ssm, w_glu_b, w_bs_b, tm)
            return _matmul_res(merged, w_out_b, x, _tile(x.shape[0], 1024), 2 * W_TILE)

        z = _norm_matmul(xp, g_mix[i], w_in_b, 1024, 3 * W_TILE)
        bp, pool_new = _pool_seq(z, w_pool_b, pool_scale[i], w_bp_b, nb, t_len, 1024, pool_buf)
        gy, st_re, st_im = _ssm_seq(z, d_pool, b_cat, c_cat, a_pow, a_tile, ssm_d[i], nb, t_len)
        xp = mix_tail(xp, z, bp, gy, 512)
        xp, conv_new = _ffn_seq(xp, g_ffn[i], w_up_b, w_conv[i], b_conv[i], w_down_b, nb, t_len, 512, 2 * W_TILE,
                                conv_buf)
        xp_out = _ple_final(xp, p_prompt[i].reshape(nb * t_len, -1), g_ple[i], w_pg_b, w_ple_b, g_final, 512)
        for lst, val in zip(outs[:4], (pool_new, st_re.reshape(nb, n_grp, n_state),
                                       st_im.reshape(nb, n_grp, n_state), conv_new)):
            lst.append(val)

        z = _norm_matmul(xs, g_mix[i], w_in_b, ns, 3 * W_TILE)
        bp, pool_new = _pool_step(z, cache_pool[i].reshape(ns, pool_buf * d_pool), w_pool_b, pool_scale[i], w_bp_b)
        gy, st_re, st_im = _ssm_step(z, d_pool, state_ssm_re[i].reshape(ns, -1), state_ssm_im[i].reshape(ns, -1),
                                     b_cat, c_cat, a_row, ssm_d[i])
        xs = mix_tail(xs, z, bp, gy, ns)
        xs, a_new = _ffn_step(xs, g_ffn[i], w_up_b, w_conv[i], b_conv[i], w_down_b,
                              cache_conv[i].reshape(ns, conv_buf * d_ff), 2 * W_TILE)
        conv_new = jnp.concatenate([cache_conv[i][:, 1:], a_new[:, None, :]], axis=1)
        xs_out = _ple_final(xs, p_sample[i].reshape(ns, -1), g_ple[i], w_pg_b, w_ple_b, g_final, ns)
        for lst, val in zip(outs[4:], (pool_new.reshape(ns, pool_buf, d_pool), st_re.reshape(ns, n_grp, n_state),
                                       st_im.reshape(ns, n_grp, n_state), conv_new)):
            lst.append(val)

    assert depth == 1
    y_prompt = xp_out.reshape(nb, t_len, d)
    y_sample = xs_out.reshape(ns, 1, d)
    return (y_prompt, y_sample) + tuple(jnp.stack(o, axis=0) for o in outs)
```

```python
import functools

import jax
import jax.numpy as jnp
from jax import lax
from jax.experimental import pallas as pl
from jax.experimental.pallas import tpu as pltpu

F32 = jnp.float32
BF16 = jnp.bfloat16

EPS = 1e-6
POOL_WINDOWS = (2, 4, 8, 16)
POOL_HALO = 16
SSM_GROUP = 16
SSM_STATE = 64
SSM_BLOCK_GROUPS = 16
LANES = 128
SUBLANES = 8
SEG_LEN = 64
SEG_PITCH = SEG_LEN + 8
CONV_W = 3
CONV_HALO = 8
W_TILE = 512
V7X_VMEM_BYTES = 64 * 1024 * 1024


def _vmem_limit(nbytes):
    return int(min(nbytes * 1.25 + (8 << 20), V7X_VMEM_BYTES - (6 << 20)))


def _params(sem, nbytes):
    return pltpu.CompilerParams(dimension_semantics=sem, vmem_limit_bytes=_vmem_limit(nbytes))


def _rmsnorm(x, g):
    return x * lax.rsqrt(jnp.mean(x * x, axis=-1, keepdims=True) + EPS) * g


def _dot(a, b):
    return jnp.dot(a, b, preferred_element_type=F32)


def _ssm_params_kernel(lr_ref, li_ref, logdt_ref, br_ref, bi_ref, pwr_ref, pwi_ref, bbr_ref, bbi_ref):
    lr = lr_ref[...]
    li = li_ref[...]
    dt = jnp.exp(logdt_ref[...])
    mag = jnp.exp(lr * dt)
    a_re = mag * jnp.cos(li * dt)
    a_im = mag * jnp.sin(li * dt)
    nr = a_re - 1.0
    ni = a_im
    den = lr * lr + li * li
    coef_re = (nr * lr + ni * li) / den
    coef_im = (ni * lr - nr * li) / den
    br = br_ref[...]
    bi = bi_ref[...]
    bbr_ref[...] = coef_re[None] * br - coef_im[None] * bi
    bbi_ref[...] = coef_re[None] * bi + coef_im[None] * br
    pr, pi = a_re, a_im
    pwr_ref[0] = pr
    pwi_ref[0] = pi
    for n in range(1, SEG_LEN):
        pr, pi = pr * a_re - pi * a_im, pr * a_im + pi * a_re
        pwr_ref[n] = pr
        pwi_ref[n] = pi


def _ssm_params(lam_re, lam_im, log_dt, b_re, b_im, c_re, c_im):
    g, p = lam_re.shape
    h = b_re.shape[-1]
    nkb = g // SSM_BLOCK_GROUPS
    bl = SSM_BLOCK_GROUPS
    pwr, pwi, bbr, bbi = pl.pallas_call(
        _ssm_params_kernel,
        out_shape=(jax.ShapeDtypeStruct((SEG_LEN, g, p), F32),
                   jax.ShapeDtypeStruct((SEG_LEN, g, p), F32),
                   jax.ShapeDtypeStruct((h, g, p), F32),
                   jax.ShapeDtypeStruct((h, g, p), F32)),
        name="ssm_params",
    )(lam_re, lam_im, log_dt.reshape(g, 1), jnp.transpose(b_re, (2, 0, 1)), jnp.transpose(b_im, (2, 0, 1)))

    eye = jnp.eye(bl, dtype=F32)
    bb = jnp.stack([bbr, bbi]).reshape(2, h, nkb, bl, p)
    b_cat = jnp.einsum('shkgp,gj->kghsjp', bb, eye).reshape(nkb, bl * h, 2 * bl * p).astype(BF16)
    cc = jnp.stack([c_re, -c_im]).reshape(2, nkb, bl, h, p)
    c_cat = jnp.einsum('skghp,gj->ksgpjh', cc, eye).reshape(nkb, 2 * bl * p, bl * h).astype(BF16)
    nslab = 2 * bl * p // LANES
    pw = jnp.stack([pwr, pwi], axis=1).reshape(SEG_LEN, 2, nkb, bl * p)
    a_row = jnp.transpose(pw[0], (1, 0, 2)).reshape(nkb, 1, 2 * bl * p)
    pw = jnp.transpose(pw.reshape(SEG_LEN, 2, nkb, nslab // 2, LANES), (2, 1, 3, 0, 4))
    pw = pw.reshape(nkb, nslab, SEG_LEN, LANES)
    a_tile = jnp.broadcast_to(pw[:, :, 0:1, :], (nkb, nslab, SUBLANES, LANES))
    return b_cat, c_cat, pw, a_tile, a_row


def _norm_matmul_kernel(x_ref, g_ref, w_ref, o_ref, h_ref):
    @pl.when(pl.program_id(1) == 0)
    def _():
        h_ref[...] = _rmsnorm(x_ref[...], g_ref[...]).astype(BF16)
    o_ref[...] = _dot(h_ref[...], w_ref[...])


def _norm_matmul(x, g, w, tm, tn):
    n, d = x.shape
    dout = w.shape[1]
    nbytes = 2 * tm * d * 4 + tm * d * 2 + 2 * d * tn * 2 + 2 * tm * tn * 4
    return pl.pallas_call(
        _norm_matmul_kernel,
        grid=(n // tm, dout // tn),
        in_specs=[pl.BlockSpec((tm, d), lambda i, j: (i, 0)),
                  pl.BlockSpec((1, d), lambda i, j: (0, 0)),
                  pl.BlockSpec((d, tn), lambda i, j: (0, j))],
        out_specs=pl.BlockSpec((tm, tn), lambda i, j: (i, j)),
        out_shape=jax.ShapeDtypeStruct((n, dout), F32),
        scratch_shapes=[pltpu.VMEM((tm, d), BF16)],
        compiler_params=_params(("arbitrary", "arbitrary"), nbytes),
        name="norm_matmul",
    )(x, g.reshape(1, d), w)


def _pool_project(diffs, wp_ref, scale_ref, wb_ref, y_ref):
    gw = wp_ref.shape[1]
    for k, diff in enumerate(diffs):
        yk = _dot(diff.astype(BF16), wp_ref[k]) * scale_ref[:, k * gw:(k + 1) * gw]
        y_ref[:, k * gw:(k + 1) * gw] = yk.astype(BF16)
    return _dot(y_ref[...], wb_ref[...])


def _pool_seq_kernel(u_ref, wp_ref, scale_ref, wb_ref, o_ref, new_ref, ext_ref, y_ref):
    tc, dp = u_ref.shape
    gw = wp_ref.shape[1]
    t = pl.program_id(1)

    @pl.when(t == 0)
    def _():
        ext_ref[0:POOL_HALO, :] = jnp.zeros((POOL_HALO, dp), F32)

    ext_ref[POOL_HALO:POOL_HALO + tc, :] = u_ref[...]
    pos = (t * tc + 1 + lax.broadcasted_iota(jnp.int32, (tc, 1), 0)).astype(F32)
    diffs = []
    for k, w in enumerate(POOL_WINDOWS):
        cols = slice(k * gw, (k + 1) * gw)
        u = ext_ref[POOL_HALO:POOL_HALO + tc, cols]
        s = u
        for j in range(1, w):
            s = s + ext_ref[POOL_HALO - j:POOL_HALO - j + tc, cols]
        count = jnp.minimum(pos, float(w))
        diffs.append(s / count - u)
    o_ref[...] = _pool_project(diffs, wp_ref, scale_ref, wb_ref, y_ref)
    nb = new_ref.shape[1]
    new_ref[0] = ext_ref[POOL_HALO + tc - nb:POOL_HALO + tc, :]
    ext_ref[0:POOL_HALO, :] = ext_ref[tc:tc + POOL_HALO, :]


def _pool_seq(z, w_pool, pool_scale, w_branch, nb, t_len, tc, pool_buf):
    n = z.shape[0]
    ng, gw, _ = w_pool.shape
    dp = ng * gw
    dm = w_branch.shape[1]
    nt = t_len // tc
    nbytes = (2 * tc * dp * 4 + 2 * ng * gw * gw * 2 + 2 * dp * dm * 2 + 2 * tc * dm * 4
              + (tc + POOL_HALO) * dp * 4 + tc * dp * 2 + 4 * tc * gw * 4)
    return pl.pallas_call(
        _pool_seq_kernel,
        grid=(nb, nt),
        in_specs=[pl.BlockSpec((tc, dp), lambda b, t: (b * nt + t, 0)),
                  pl.BlockSpec((ng, gw, gw), lambda b, t: (0, 0, 0)),
                  pl.BlockSpec((1, dp), lambda b, t: (0, 0)),
                  pl.BlockSpec((dp, dm), lambda b, t: (0, 0))],
        out_specs=(pl.BlockSpec((tc, dm), lambda b, t: (b * nt + t, 0)),
                   pl.BlockSpec((1, pool_buf, dp), lambda b, t: (b, 0, 0))),
        out_shape=(jax.ShapeDtypeStruct((n, dm), F32),
                   jax.ShapeDtypeStruct((nb, pool_buf, dp), F32)),
        scratch_shapes=[pltpu.VMEM((tc + POOL_HALO, dp), F32), pltpu.VMEM((tc, dp), BF16)],
        compiler_params=_params(("arbitrary", "arbitrary"), nbytes),
        name="pool_seq",
    )(z, w_pool, pool_scale.reshape(1, dp), w_branch)


def _pool_step_kernel(u_ref, cache_ref, wp_ref, scale_ref, wb_ref, o_ref, new_ref, y_ref):
    dp = u_ref.shape[1]
    gw = wp_ref.shape[1]
    lb = cache_ref.shape[1] // dp
    diffs = []
    for k, w in enumerate(POOL_WINDOWS):
        u = u_ref[:, k * gw:(k + 1) * gw]
        s = u
        for j in range(1, w):
            s = s + cache_ref[:, (lb - j) * dp + k * gw:(lb - j) * dp + (k + 1) * gw]
        diffs.append(s / float(w) - u)
    o_ref[...] = _pool_project(diffs, wp_ref, scale_ref, wb_ref, y_ref)
    new_ref[:, 0:(lb - 1) * dp] = cache_ref[:, dp:lb * dp]
    new_ref[:, (lb - 1) * dp:lb * dp] = u_ref[...]


def _pool_step(z, cache, w_pool, pool_scale, w_branch):
    n = z.shape[0]
    ng, gw, _ = w_pool.shape
    dp = ng * gw
    dm = w_branch.shape[1]
    lbdp = cache.shape[1]
    nbytes = 2 * (n * dp * 4 + 2 * n * lbdp * 4 + ng * gw * gw * 2 + dp * dm * 2 + n * dm * 4) + n * dp * 2
    return pl.pallas_call(
        _pool_step_kernel,
        grid=(1,),
        in_specs=[pl.BlockSpec((n, dp), lambda i: (0, 0)),
                  pl.BlockSpec((n, lbdp), lambda i: (0, 0)),
                  pl.BlockSpec((ng, gw, gw), lambda i: (0, 0, 0)),
                  pl.BlockSpec((1, dp), lambda i: (0, 0)),
                  pl.BlockSpec((dp, dm), lambda i: (0, 0))],
        out_specs=(pl.BlockSpec((n, dm), lambda i: (0, 0)),
                   pl.BlockSpec((n, lbdp), lambda i: (0, 0))),
        out_shape=(jax.ShapeDtypeStruct((n, dm), F32),
                   jax.ShapeDtypeStruct((n, lbdp), F32)),
        scratch_shapes=[pltpu.VMEM((n, dp), BF16)],
        compiler_params=_params(("arbitrary",), nbytes),
        name="pool_step",
    )(z, cache, w_pool, pool_scale.reshape(1, dp), w_branch)


def _cmul_add(xr, xi, ar, ai, br, bi):
    return xr + ar * br - ai * bi, xi + ar * bi + ai * br


def _ssm_seq_kernel(n_cast, u_ref, bcat_ref, ccat_ref, pw_ref, at_ref, d_ref, *refs):
    cast_in, (gy_ref, st_ref), refs = refs[:n_cast], refs[n_cast:n_cast + 2], refs[n_cast + 2:]
    cast_out, (h_ref, hb_ref, seed_ref, carry_ref) = refs[:n_cast], refs[n_cast:]
    for src_ref, dst_ref in zip(cast_in, cast_out):
        dst_ref[...] = src_ref[...].astype(BF16)

    nslab = h_ref.shape[0]
    npair = nslab // 2
    t = pl.program_id(2)

    @pl.when(t == 0)
    def _():
        carry_ref[...] = jnp.zeros_like(carry_ref)

    u = u_ref[...]
    bu = _dot(u.astype(BF16), bcat_ref[0])
    for j in range(nslab):
        for r in range(SUBLANES):
            h_ref[j, r * SEG_PITCH:r * SEG_PITCH + SEG_LEN, :] = (
                bu[r * SEG_LEN:(r + 1) * SEG_LEN, j * LANES:(j + 1) * LANES])

    def seg_step(n, state):
        rows = pl.ds(n, SUBLANES, stride=SEG_PITCH)
        new = []
        for j in range(npair):
            sr, si = _cmul_add(h_ref[j, rows, :], h_ref[npair + j, rows, :],
                               at_ref[0, j], at_ref[0, npair + j], state[2 * j], state[2 * j + 1])
            h_ref[j, rows, :] = sr
            h_ref[npair + j, rows, :] = si
            new += [sr, si]
        return tuple(new)

    ends = lax.fori_loop(0, SEG_LEN, seg_step, (jnp.zeros((SUBLANES, LANES), F32),) * nslab)

    last = slice(SEG_LEN - 1, SEG_LEN)
    for j in range(npair):
        cr, ci = carry_ref[j], carry_ref[npair + j]
        for r in range(SUBLANES):
            seed_ref[j, r:r + 1, :] = cr
            seed_ref[npair + j, r:r + 1, :] = ci
            cr, ci = _cmul_add(ends[2 * j][r:r + 1, :], ends[2 * j + 1][r:r + 1, :],
                               pw_ref[0, j, last, :], pw_ref[0, npair + j, last, :], cr, ci)
        carry_ref[j] = cr
        carry_ref[npair + j] = ci
    st_ref[0, 0] = carry_ref[...]

    rows_bf16 = 2 * SUBLANES
    for r in range(SUBLANES):
        for i in range(SEG_LEN // rows_bf16):
            src = slice(r * SEG_PITCH + i * rows_bf16, r * SEG_PITCH + (i + 1) * rows_bf16)
            dst = slice(r * SEG_LEN + i * rows_bf16, r * SEG_LEN + (i + 1) * rows_bf16)
            pws = slice(i * rows_bf16, (i + 1) * rows_bf16)
            for j in range(npair):
                hr, hi = _cmul_add(h_ref[j, src, :], h_ref[npair + j, src, :],
                                   pw_ref[0, j, pws, :], pw_ref[0, npair + j, pws, :],
                                   seed_ref[j, r:r + 1, :], seed_ref[npair + j, r:r + 1, :])
                hb_ref[dst, j * LANES:(j + 1) * LANES] = hr.astype(BF16)
                hb_ref[dst, (npair + j) * LANES:(npair + j + 1) * LANES] = hi.astype(BF16)

    y = _dot(hb_ref[...], ccat_ref[0]) + d_ref[...] * u
    gy_ref[...] = jax.nn.gelu(y).astype(BF16)


def _ssm_seq(z, col0, b_cat, c_cat, pw, a_tile, d_skip, nb, t_len, side_casts):
    n = z.shape[0]
    nkb, kw, sw = b_cat.shape
    nslab = sw // LANES
    tc = SUBLANES * SEG_LEN
    nt = t_len // tc
    cb = col0 // kw
    n_steps = nkb * nb * nt
    cast_rows = [w.shape[0] // n_steps for w in side_casts]
    assert all(r % (2 * SUBLANES) == 0 and r * n_steps == w.shape[0] for r, w in zip(cast_rows, side_casts))
    nbytes = (2 * tc * kw * 4 + 4 * kw * sw * 2 + 2 * (SEG_LEN + SUBLANES) * sw * 4 + 2 * tc * kw * 2
              + SUBLANES * SEG_PITCH * sw * 4 + tc * sw * 2 + tc * sw * 4
              + sum(2 * r * w.shape[1] * 6 for r, w in zip(cast_rows, side_casts)))

    def cast_spec(r, w):
        return pl.BlockSpec((r, w.shape[1]), lambda k, b, t: ((k * nb + b) * nt + t, 0))

    cast_specs = [cast_spec(r, w) for r, w in zip(cast_rows, side_casts)]
    gy, st, *cast = pl.pallas_call(
        functools.partial(_ssm_seq_kernel, len(side_casts)),
        grid=(nkb, nb, nt),
        in_specs=[pl.BlockSpec((tc, kw), lambda k, b, t: (b * nt + t, cb + k)),
                  pl.BlockSpec((1, kw, sw), lambda k, b, t: (k, 0, 0)),
                  pl.BlockSpec((1, sw, kw), lambda k, b, t: (k, 0, 0)),
                  pl.BlockSpec((1, nslab, SEG_LEN, LANES), lambda k, b, t: (k, 0, 0, 0)),
                  pl.BlockSpec((1, nslab, SUBLANES, LANES), lambda k, b, t: (k, 0, 0, 0)),
                  pl.BlockSpec((1, kw), lambda k, b, t: (0, k))] + cast_specs,
        out_specs=[pl.BlockSpec((tc, kw), lambda k, b, t: (b * nt + t, k)),
                   pl.BlockSpec((1, 1, nslab, 1, LANES), lambda k, b, t: (b, k, 0, 0, 0))] + cast_specs,
        out_shape=[jax.ShapeDtypeStruct((n, nkb * kw), BF16),
                   jax.ShapeDtypeStruct((nb, nkb, nslab, 1, LANES), F32)]
                  + [jax.ShapeDtypeStruct(w.shape, BF16) for w in side_casts],
        scratch_shapes=[pltpu.VMEM((nslab, SUBLANES * SEG_PITCH, LANES), F32), pltpu.VMEM((tc, sw), BF16),
                        pltpu.VMEM((nslab, SUBLANES, LANES), F32), pltpu.VMEM((nslab, 1, LANES), F32)],
        compiler_params=_params(("arbitrary", "arbitrary", "arbitrary"), nbytes),
        name="ssm_seq",
    )(z, b_cat, c_cat, pw, a_tile, d_skip.reshape(1, nkb * kw), *side_casts)
    st = st.reshape(nb, nkb, 2, sw // 2)
    return gy, st[:, :, 0].reshape(nb, -1), st[:, :, 1].reshape(nb, -1), cast


def _ssm_step_kernel(u_ref, h0r_ref, h0i_ref, bcat_ref, ccat_ref, a_ref, d_ref,
                     gy_ref, h1r_ref, h1i_ref, h_ref):
    half = h0r_ref.shape[1]
    u = u_ref[...]
    bu = _dot(u.astype(BF16), bcat_ref[0])
    hr, hi = _cmul_add(bu[:, 0:half], bu[:, half:2 * half],
                       a_ref[0, :, 0:half], a_ref[0, :, half:2 * half],
                       h0r_ref[...], h0i_ref[...])
    h1r_ref[...] = hr
    h1i_ref[...] = hi
    h_ref[:, 0:half] = hr.astype(BF16)
    h_ref[:, half:2 * half] = hi.astype(BF16)
    y = _dot(h_ref[...], ccat_ref[0]) + d_ref[...] * u
    gy_ref[...] = jax.nn.gelu(y).astype(BF16)


def _ssm_step(z, col0, h0_re, h0_im, b_cat, c_cat, a_row, d_skip):
    n = z.shape[0]
    nkb, kw, sw = b_cat.shape
    half = sw // 2
    cb = col0 // kw
    nbytes = 2 * (n * kw * 4 + 4 * n * half * 4 + 2 * kw * sw * 2 + sw * 4 + n * kw * 2) + n * sw * 6
    return pl.pallas_call(
        _ssm_step_kernel,
        grid=(nkb,),
        in_specs=[pl.BlockSpec((n, kw), lambda k: (0, cb + k)),
                  pl.BlockSpec((n, half), lambda k: (0, k)),
                  pl.BlockSpec((n, half), lambda k: (0, k)),
                  pl.BlockSpec((1, kw, sw), lambda k: (k, 0, 0)),
                  pl.BlockSpec((1, sw, kw), lambda k: (k, 0, 0)),
                  pl.BlockSpec((1, 1, sw), lambda k: (k, 0, 0)),
                  pl.BlockSpec((1, kw), lambda k: (0, k))],
        out_specs=(pl.BlockSpec((n, kw), lambda k: (0, k)),
                   pl.BlockSpec((n, half), lambda k: (0, k)),
                   pl.BlockSpec((n, half), lambda k: (0, k))),
        out_shape=(jax.ShapeDtypeStruct((n, nkb * kw), BF16),
                   jax.ShapeDtypeStruct((n, nkb * half), F32),
                   jax.ShapeDtypeStruct((n, nkb * half), F32)),
        scratch_shapes=[pltpu.VMEM((n, sw), BF16)],
        compiler_params=_params(("arbitrary",), nbytes),
        name="ssm_step",
    )(z, h0_re, h0_im, b_cat, c_cat, a_row, d_skip.reshape(1, nkb * kw))


def _glu_merge_kernel(gy_ref, bp_ref, gp_ref, gs_ref, wg_ref, wb_ref, o_ref):
    ds = wb_ref.shape[0]
    g = _dot(gy_ref[...], wg_ref[...])
    y = g[:, 0:ds] * jax.nn.sigmoid(g[:, ds:2 * ds])
    bs = _dot(y.astype(BF16), wb_ref[...])
    merged = jax.nn.sigmoid(gp_ref[...]) * bp_ref[...] + jax.nn.sigmoid(gs_ref[...]) * bs
    o_ref[...] = merged.astype(BF16)


def _glu_merge(gy, bp, z, gate_col0, w_glu, w_branch, tm):
    n, ds = gy.shape
    dm = w_branch.shape[1]
    gb = gate_col0 // dm
    nbytes = 2 * (tm * ds * 2 + 3 * tm * dm * 4 + ds * 2 * ds * 2 + ds * dm * 2 + tm * dm * 2) + 4 * tm * dm * 4
    return pl.pallas_call(
        _glu_merge_kernel,
        grid=(n // tm,),
        in_specs=[pl.BlockSpec((tm, ds), lambda i: (i, 0)),
                  pl.BlockSpec((tm, dm), lambda i: (i, 0)),
                  pl.BlockSpec((tm, dm), lambda i: (i, gb)),
                  pl.BlockSpec((tm, dm), lambda i: (i, gb + 1)),
                  pl.BlockSpec((ds, 2 * ds), lambda i: (0, 0)),
                  pl.BlockSpec((ds, dm), lambda i: (0, 0))],
        out_specs=pl.BlockSpec((tm, dm), lambda i: (i, 0)),
        out_shape=jax.ShapeDtypeStruct((n, dm), BF16),
        compiler_params=_params(("arbitrary",), nbytes),
        name="glu_merge",
    )(gy, bp, z, z, w_glu, w_branch)


def _matmul_res_kernel(m_ref, w_ref, x_ref, o_ref):
    o_ref[...] = x_ref[...] + _dot(m_ref[...], w_ref[...])


def _matmul_res(m, w, x, tm, tn):
    n, d = m.shape
    dout = w.shape[1]
    nbytes = 2 * (tm * d * 2 + d * tn * 2 + 2 * tm * tn * 4)
    return pl.pallas_call(
        _matmul_res_kernel,
        grid=(n // tm, dout // tn),
        in_specs=[pl.BlockSpec((tm, d), lambda i, j: (i, 0)),
                  pl.BlockSpec((d, tn), lambda i, j: (0, j)),
                  pl.BlockSpec((tm, tn), lambda i, j: (i, j))],
        out_specs=pl.BlockSpec((tm, tn), lambda i, j: (i, j)),
        out_shape=jax.ShapeDtypeStruct((n, dout), F32),
        compiler_params=_params(("arbitrary", "arbitrary"), nbytes),
        name="matmul_res",
    )(m, w, x)


def _ffn_gate_down(x_ref, conv, v, wd_ref, o_ref, acc_ref):
    c = pl.program_id(1)
    part = _dot((jax.nn.gelu(conv) * v).astype(BF16), wd_ref[...])

    @pl.when(c == 0)
    def _():
        acc_ref[...] = part

    @pl.when(c > 0)
    def _():
        acc_ref[...] += part

    @pl.when(c == pl.num_programs(1) - 1)
    def _():
        o_ref[...] = x_ref[...] + acc_ref[...]


def _ffn_seq_kernel(nc, tiles_per_seq, n_steps, x_ref, g_ref, wa_ref, wv_ref, wc_ref, bc_ref, wd_ref,
                    o_ref, new_ref, h_ref, even_ref, odd_ref, ext_ref, carry_ref, gate_ref):
    tm = x_ref.shape[0]
    s = pl.program_id(0)
    sp = jnp.maximum(s - 1, 0)
    cp = sp % nc
    ip = sp // nc

    @pl.when(s == 0)
    def _():
        odd_ref[...] = jnp.zeros(odd_ref.shape, F32)
        o_ref[...] = jnp.zeros(o_ref.shape, F32)
        carry_ref[...] = jnp.zeros(carry_ref.shape, F32)

    @pl.when(jnp.logical_and(s % nc == 0, s < n_steps))
    def _():
        h_ref[...] = _rmsnorm(x_ref[...], g_ref[...]).astype(BF16)

    def step(cur_ref, prev_ref):
        cur_ref[0] = _dot(h_ref[...], wa_ref[...])
        cur_ref[1] = _dot(h_ref[...], wv_ref[...])

        a = prev_ref[0]
        v = prev_ref[1]
        seq_start = ip % tiles_per_seq == 0
        ext_ref[0:CONV_HALO, :] = jnp.where(seq_start, 0.0, carry_ref[cp])
        ext_ref[CONV_HALO:CONV_HALO + tm, :] = a
        conv = bc_ref[...] + wc_ref[CONV_W - 1:CONV_W, :] * a
        for j in range(CONV_W - 1):
            off = CONV_HALO - (CONV_W - 1) + j
            conv = conv + wc_ref[j:j + 1, :] * ext_ref[off:off + tm, :]
        carry_ref[cp] = ext_ref[tm:tm + CONV_HALO, :]
        nb = new_ref.shape[1]
        new_ref[0] = ext_ref[CONV_HALO + tm - nb:CONV_HALO + tm, :]
        gate_ref[...] = (jax.nn.gelu(conv) * v).astype(BF16)
        part = _dot(gate_ref[...], wd_ref[...])
        o_ref[...] = part + jnp.where(cp == 0, x_ref[...], o_ref[...])

    @pl.when(s % 2 == 0)
    def _():
        step(even_ref, odd_ref)

    @pl.when(s % 2 == 1)
    def _():
        step(odd_ref, even_ref)


def _ffn_seq(x, g, w_up, w_conv, b_conv, w_down, nb, t_len, tm, tf, conv_buf):
    n, d = x.shape
    dff = w_down.shape[0]
    nc = dff // tf
    tps = t_len // tm
    n_steps = (n // tm) * nc
    assert nc >= 2, "the residual tile must still be resident when its first chunk is finished"
    nbytes = (4 * tm * d * 4 + 4 * d * tf * 2 + 2 * tf * d * 2 + tm * d * 2 + 4 * tm * tf * 4
              + (tm + CONV_HALO) * tf * 4 + nc * CONV_HALO * tf * 4 + 4 * tm * tf * 4)

    def cur(s):
        return jnp.minimum(s, n_steps - 1)

    def prev(s):
        return jnp.maximum(s - 1, 0)

    out, new_tail = pl.pallas_call(
        functools.partial(_ffn_seq_kernel, nc, tps, n_steps),
        grid=(n_steps + 1,),
        in_specs=[pl.BlockSpec((tm, d), lambda s: (cur(s) // nc, 0)),
                  pl.BlockSpec((1, d), lambda s: (0, 0)),
                  pl.BlockSpec((d, tf), lambda s: (0, cur(s) % nc)),
                  pl.BlockSpec((d, tf), lambda s: (0, nc + cur(s) % nc)),
                  pl.BlockSpec((CONV_W, tf), lambda s: (0, prev(s) % nc)),
                  pl.BlockSpec((1, tf), lambda s: (0, prev(s) % nc)),
                  pl.BlockSpec((tf, d), lambda s: (prev(s) % nc, 0))],
        out_specs=(pl.BlockSpec((tm, d), lambda s: (prev(s) // nc, 0)),
                   pl.BlockSpec((1, conv_buf, tf), lambda s: (prev(s) // nc, 0, prev(s) % nc))),
        out_shape=(jax.ShapeDtypeStruct((n, d), F32),
                   jax.ShapeDtypeStruct((n // tm, conv_buf, dff), F32)),
        scratch_shapes=[pltpu.VMEM((tm, d), BF16), pltpu.VMEM((2, tm, tf), F32), pltpu.VMEM((2, tm, tf), F32),
                        pltpu.VMEM((tm + CONV_HALO, tf), F32), pltpu.VMEM((nc, CONV_HALO, tf), F32),
                        pltpu.VMEM((tm, tf), BF16)],
        compiler_params=_params(("arbitrary",), nbytes),
        name="ffn_seq",
    )(x, g.reshape(1, d), w_up, w_up, w_conv, b_conv.reshape(1, dff), w_down)
    return out, new_tail[tps - 1::tps]


def _ffn_step_kernel(x_ref, g_ref, wa_ref, wv_ref, wc_ref, bc_ref, wd_ref, p0_ref, p1_ref,
                     o_ref, a_ref, h_ref, acc_ref):
    @pl.when(pl.program_id(1) == 0)
    def _():
        h_ref[...] = _rmsnorm(x_ref[...], g_ref[...]).astype(BF16)

    a = _dot(h_ref[...], wa_ref[...])
    v = _dot(h_ref[...], wv_ref[...])
    conv = bc_ref[...] + wc_ref[2:3, :] * a + wc_ref[1:2, :] * p1_ref[...] + wc_ref[0:1, :] * p0_ref[...]
    a_ref[...] = a
    _ffn_gate_down(x_ref, conv, v, wd_ref, o_ref, acc_ref)


def _ffn_step(x, g, w_up, w_conv, b_conv, w_down, cache, tf):
    n, d = x.shape
    dff = w_down.shape[0]
    nc = dff // tf
    nbytes = 4 * n * d * 4 + 4 * d * tf * 2 + 2 * tf * d * 2 + n * d * 6 + 12 * n * tf * 4
    return pl.pallas_call(
        _ffn_step_kernel,
        grid=(1, nc),
        in_specs=[pl.BlockSpec((n, d), lambda i, c: (0, 0)),
                  pl.BlockSpec((1, d), lambda i, c: (0, 0)),
                  pl.BlockSpec((d, tf), lambda i, c: (0, c)),
                  pl.BlockSpec((d, tf), lambda i, c: (0, nc + c)),
                  pl.BlockSpec((CONV_W, tf), lambda i, c: (0, c)),
                  pl.BlockSpec((1, tf), lambda i, c: (0, c)),
                  pl.BlockSpec((tf, d), lambda i, c: (c, 0)),
                  pl.BlockSpec((n, tf), lambda i, c: (0, c)),
                  pl.BlockSpec((n, tf), lambda i, c: (0, nc + c))],
        out_specs=(pl.BlockSpec((n, d), lambda i, c: (0, 0)),
                   pl.BlockSpec((n, tf), lambda i, c: (0, c))),
        out_shape=(jax.ShapeDtypeStruct((n, d), F32),
                   jax.ShapeDtypeStruct((n, dff), F32)),
        scratch_shapes=[pltpu.VMEM((n, d), BF16), pltpu.VMEM((n, d), F32)],
        compiler_params=_params(("arbitrary", "arbitrary"), nbytes),
        name="ffn_step",
    )(x, g.reshape(1, d), w_up, w_up, w_conv, b_conv.reshape(1, dff), w_down, cache, cache)


def _ple_final_kernel(x_ref, p_ref, gp_ref, wg_ref, wp_ref, gf_ref, o_ref):
    x = x_ref[...]
    gate = jax.nn.sigmoid(_dot(_rmsnorm(x, gp_ref[...]).astype(BF16), wg_ref[...]))
    x = x + gate * _dot(p_ref[...].astype(BF16), wp_ref[...])
    o_ref[...] = _rmsnorm(x, gf_ref[...])


def _ple_final(x, p, g_ple, w_gate, w_ple, g_final, tm):
    n, d = x.shape
    dp = p.shape[1]
    nbytes = 2 * (2 * tm * d * 4 + tm * dp * 4 + d * d * 2 + dp * d * 2) + 4 * tm * d * 4
    return pl.pallas_call(
        _ple_final_kernel,
        grid=(n // tm,),
        in_specs=[pl.BlockSpec((tm, d), lambda i: (i, 0)),
                  pl.BlockSpec((tm, dp), lambda i: (i, 0)),
                  pl.BlockSpec((1, d), lambda i: (0, 0)),
                  pl.BlockSpec((d, d), lambda i: (0, 0)),
                  pl.BlockSpec((dp, d), lambda i: (0, 0)),
                  pl.BlockSpec((1, d), lambda i: (0, 0))],
        out_specs=pl.BlockSpec((tm, d), lambda i: (i, 0)),
        out_shape=jax.ShapeDtypeStruct((n, d), F32),
        compiler_params=_params(("arbitrary",), nbytes),
        name="ple_final",
    )(x, p, g_ple.reshape(1, d), w_gate, w_ple, g_final.reshape(1, d))


def _tile(n, pref):
    return pref if n % pref == 0 else n


def kernel(x_prompt, x_sample, cache_pool, state_ssm_re, state_ssm_im, cache_conv, p_prompt, p_sample, g_mix, w_in, w_pool, pool_scale, ssm_lam_re, ssm_lam_im, ssm_log_dt, ssm_b_re, ssm_b_im, ssm_c_re, ssm_c_im, ssm_d, w_glu, w_branch_pool, w_branch_ssm, w_out, g_ffn, w_up, w_conv, b_conv, w_down, g_ple, w_ple_gate, w_ple, g_final):
    depth = g_mix.shape[0]
    nb, t_len, d = x_prompt.shape
    ns = x_sample.shape[0]
    assert x_sample.shape[1] == 1, "the sample group advances one step per call"
    pool_buf, d_pool = cache_pool.shape[2], cache_pool.shape[3]
    conv_buf, d_ff = cache_conv.shape[2], cache_conv.shape[3]
    n_grp, n_state = ssm_lam_re.shape[1], ssm_lam_re.shape[2]
    d_ssm = ssm_d.shape[1]
    assert pool_buf == max(POOL_WINDOWS) - 1 and conv_buf == CONV_W - 1
    assert n_state == SSM_STATE and d_ssm == n_grp * SSM_GROUP

    xp = x_prompt.reshape(nb * t_len, d)
    xs = x_sample.reshape(ns, d)
    outs = [[] for _ in range(8)]
    for i in range(depth):
        w_in_b, w_pool_b, w_ple_b = w_in[i].astype(BF16), w_pool[i].astype(BF16), w_ple[i].astype(BF16)
        b_cat, c_cat, a_pow, a_tile, a_row = _ssm_params(ssm_lam_re[i], ssm_lam_im[i], ssm_log_dt[i],
                                                         ssm_b_re[i], ssm_b_im[i], ssm_c_re[i], ssm_c_im[i])

        z = _norm_matmul(xp, g_mix[i], w_in_b, 1024, 3 * W_TILE)
        gy, st_re, st_im, (w_up_b, w_down_b, w_out_b, w_pg_b, w_glu_b, w_bs_b, w_bp_b) = _ssm_seq(
            z, d_pool, b_cat, c_cat, a_pow, a_tile, ssm_d[i], nb, t_len,
            [w_up[i], w_down[i], w_out[i], w_ple_gate[i], w_glu[i], w_branch_ssm[i], w_branch_pool[i]])

        def mix_tail(x, z, bp, gy, tm):
            merged = _glu_merge(gy, bp, z, d_pool + d_ssm, w_glu_b, w_bs_b, tm)
            return _matmul_res(merged, w_out_b, x, _tile(x.shape[0], 1024), 2 * W_TILE)

        bp, pool_new = _pool_seq(z, w_pool_b, pool_scale[i], w_bp_b, nb, t_len, 1024, pool_buf)
        xp = mix_tail(xp, z, bp, gy, 512)
        xp, conv_new = _ffn_seq(xp, g_ffn[i], w_up_b, w_conv[i], b_conv[i], w_down_b, nb, t_len, 512, 2 * W_TILE,
                                conv_buf)
        xp_out = _ple_final(xp, p_prompt[i].reshape(nb * t_len, -1), g_ple[i], w_pg_b, w_ple_b, g_final, 512)
        for lst, val in zip(outs[:4], (pool_new, st_re.reshape(nb, n_grp, n_state),
                                       st_im.reshape(nb, n_grp, n_state), conv_new)):
            lst.append(val)

        z = _norm_matmul(xs, g_mix[i], w_in_b, ns, 3 * W_TILE)
        bp, pool_new = _pool_step(z, cache_pool[i].reshape(ns, pool_buf * d_pool), w_pool_b, pool_scale[i], w_bp_b)
        gy, st_re, st_im = _ssm_step(z, d_pool, state_ssm_re[i].reshape(ns, -1), state_ssm_im[i].reshape(ns, -1),
                                     b_cat, c_cat, a_row, ssm_d[i])
        xs = mix_tail(xs, z, bp, gy, ns)
        xs, a_new = _ffn_step(xs, g_ffn[i], w_up_b, w_conv[i], b_conv[i], w_down_b,
                              cache_conv[i].reshape(ns, conv_buf * d_ff), 2 * W_TILE)
        conv_new = jnp.concatenate([cache_conv[i][:, 1:], a_new[:, None, :]], axis=1)
        xs_out = _ple_final(xs, p_sample[i].reshape(ns, -1), g_ple[i], w_pg_b, w_ple_b, g_final, ns)
        for lst, val in zip(outs[4:], (pool_new.reshape(ns, pool_buf, d_pool), st_re.reshape(ns, n_grp, n_state),
                                       st_im.reshape(ns, n_grp, n_state), conv_new)):
            lst.append(val)

    assert depth == 1
    y_prompt = xp_out.reshape(nb, t_len, d)
    y_sample = xs_out.reshape(ns, 1, d)
    return (y_prompt, y_sample) + tuple(jnp.stack(o, axis=0) for o in outs)
```

```python
import functools

import jax
import jax.numpy as jnp
from jax import lax
from jax.experimental import pallas as pl
from jax.experimental.pallas import tpu as pltpu

F32 = jnp.float32
BF16 = jnp.bfloat16

EPS = 1e-6
POOL_WINDOWS = (2, 4, 8, 16)
POOL_HALO = 16
SSM_GROUP = 16
SSM_STATE = 64
SSM_BLOCK_GROUPS = 16
LANES = 128
SUBLANES = 8
SEG_LEN = 64
SEG_PITCH = SEG_LEN + 8
CONV_W = 3
CONV_HALO = 8
FFN_SLICE = 256
W_TILE = 512
V7X_VMEM_BYTES = 64 * 1024 * 1024


def _vmem_limit(nbytes):
    return int(min(nbytes * 1.25 + (8 << 20), V7X_VMEM_BYTES - (6 << 20)))


def _params(sem, nbytes):
    return pltpu.CompilerParams(dimension_semantics=sem, vmem_limit_bytes=_vmem_limit(nbytes))


def _rmsnorm(x, g):
    return x * lax.rsqrt(jnp.mean(x * x, axis=-1, keepdims=True) + EPS) * g


def _dot(a, b):
    return jnp.dot(a, b, preferred_element_type=F32)


def _ssm_params_kernel(lr_ref, li_ref, logdt_ref, br_ref, bi_ref, pwr_ref, pwi_ref, bbr_ref, bbi_ref):
    lr = lr_ref[...]
    li = li_ref[...]
    dt = jnp.exp(logdt_ref[...])
    mag = jnp.exp(lr * dt)
    a_re = mag * jnp.cos(li * dt)
    a_im = mag * jnp.sin(li * dt)
    nr = a_re - 1.0
    ni = a_im
    den = lr * lr + li * li
    coef_re = (nr * lr + ni * li) / den
    coef_im = (ni * lr - nr * li) / den
    br = br_ref[...]
    bi = bi_ref[...]
    bbr_ref[...] = coef_re[None] * br - coef_im[None] * bi
    bbi_ref[...] = coef_re[None] * bi + coef_im[None] * br
    pr, pi = a_re, a_im
    pwr_ref[0] = pr
    pwi_ref[0] = pi
    for n in range(1, SEG_LEN):
        pr, pi = pr * a_re - pi * a_im, pr * a_im + pi * a_re
        pwr_ref[n] = pr
        pwi_ref[n] = pi


def _ssm_params(lam_re, lam_im, log_dt, b_re, b_im, c_re, c_im):
    g, p = lam_re.shape
    h = b_re.shape[-1]
    nkb = g // SSM_BLOCK_GROUPS
    bl = SSM_BLOCK_GROUPS
    pwr, pwi, bbr, bbi = pl.pallas_call(
        _ssm_params_kernel,
        out_shape=(jax.ShapeDtypeStruct((SEG_LEN, g, p), F32),
                   jax.ShapeDtypeStruct((SEG_LEN, g, p), F32),
                   jax.ShapeDtypeStruct((h, g, p), F32),
                   jax.ShapeDtypeStruct((h, g, p), F32)),
        name="ssm_params",
    )(lam_re, lam_im, log_dt.reshape(g, 1), jnp.transpose(b_re, (2, 0, 1)), jnp.transpose(b_im, (2, 0, 1)))

    eye = jnp.eye(bl, dtype=F32)
    bb = jnp.stack([bbr, bbi]).reshape(2, h, nkb, bl, p)
    b_cat = jnp.einsum('shkgp,gj->kghsjp', bb, eye).reshape(nkb, bl * h, 2 * bl * p).astype(BF16)
    cc = jnp.stack([c_re, -c_im]).reshape(2, nkb, bl, h, p)
    c_cat = jnp.einsum('skghp,gj->ksgpjh', cc, eye).reshape(nkb, 2 * bl * p, bl * h).astype(BF16)
    nslab = 2 * bl * p // LANES
    pw = jnp.stack([pwr, pwi], axis=1).reshape(SEG_LEN, 2, nkb, bl * p)
    a_row = jnp.transpose(pw[0], (1, 0, 2)).reshape(nkb, 1, 2 * bl * p)
    pw = jnp.transpose(pw.reshape(SEG_LEN, 2, nkb, nslab // 2, LANES), (2, 1, 3, 0, 4))
    pw = pw.reshape(nkb, nslab, SEG_LEN, LANES)
    a_tile = jnp.broadcast_to(pw[:, :, 0:1, :], (nkb, nslab, SUBLANES, LANES))
    return b_cat, c_cat, pw, a_tile, a_row


def _norm_matmul_kernel(x_ref, g_ref, w_ref, o_ref, h_ref):
    @pl.when(pl.program_id(1) == 0)
    def _():
        h_ref[...] = _rmsnorm(x_ref[...], g_ref[...]).astype(BF16)
    o_ref[...] = _dot(h_ref[...], w_ref[...])


def _norm_matmul(x, g, w, tm, tn):
    n, d = x.shape
    dout = w.shape[1]
    nbytes = 2 * tm * d * 4 + tm * d * 2 + 2 * d * tn * 2 + 2 * tm * tn * 4
    return pl.pallas_call(
        _norm_matmul_kernel,
        grid=(n // tm, dout // tn),
        in_specs=[pl.BlockSpec((tm, d), lambda i, j: (i, 0)),
                  pl.BlockSpec((1, d), lambda i, j: (0, 0)),
                  pl.BlockSpec((d, tn), lambda i, j: (0, j))],
        out_specs=pl.BlockSpec((tm, tn), lambda i, j: (i, j)),
        out_shape=jax.ShapeDtypeStruct((n, dout), F32),
        scratch_shapes=[pltpu.VMEM((tm, d), BF16)],
        compiler_params=_params(("arbitrary", "arbitrary"), nbytes),
        name="norm_matmul",
    )(x, g.reshape(1, d), w)


def _pool_project(diffs, wp_ref, scale_ref, wb_ref, y_ref):
    gw = wp_ref.shape[1]
    for k, diff in enumerate(diffs):
        yk = _dot(diff.astype(BF16), wp_ref[k]) * scale_ref[:, k * gw:(k + 1) * gw]
        y_ref[:, k * gw:(k + 1) * gw] = yk.astype(BF16)
    return _dot(y_ref[...], wb_ref[...])


def _pool_seq_kernel(u_ref, wp_ref, scale_ref, wb_ref, o_ref, new_ref, ext_ref, y_ref):
    tc, dp = u_ref.shape
    gw = wp_ref.shape[1]
    t = pl.program_id(1)

    @pl.when(t == 0)
    def _():
        ext_ref[0:POOL_HALO, :] = jnp.zeros((POOL_HALO, dp), F32)

    ext_ref[POOL_HALO:POOL_HALO + tc, :] = u_ref[...]
    pos = (t * tc + 1 + lax.broadcasted_iota(jnp.int32, (tc, 1), 0)).astype(F32)
    diffs = []
    for k, w in enumerate(POOL_WINDOWS):
        cols = slice(k * gw, (k + 1) * gw)
        u = ext_ref[POOL_HALO:POOL_HALO + tc, cols]
        s = u
        for j in range(1, w):
            s = s + ext_ref[POOL_HALO - j:POOL_HALO - j + tc, cols]
        count = jnp.minimum(pos, float(w))
        diffs.append(s / count - u)
    o_ref[...] = _pool_project(diffs, wp_ref, scale_ref, wb_ref, y_ref)
    nb = new_ref.shape[1]
    new_ref[0] = ext_ref[POOL_HALO + tc - nb:POOL_HALO + tc, :]
    ext_ref[0:POOL_HALO, :] = ext_ref[tc:tc + POOL_HALO, :]


def _pool_seq(z, w_pool, pool_scale, w_branch, nb, t_len, tc, pool_buf):
    n = z.shape[0]
    ng, gw, _ = w_pool.shape
    dp = ng * gw
    dm = w_branch.shape[1]
    nt = t_len // tc
    nbytes = (2 * tc * dp * 4 + 2 * ng * gw * gw * 2 + 2 * dp * dm * 2 + 2 * tc * dm * 4
              + (tc + POOL_HALO) * dp * 4 + tc * dp * 2 + 4 * tc * gw * 4)
    return pl.pallas_call(
        _pool_seq_kernel,
        grid=(nb, nt),
        in_specs=[pl.BlockSpec((tc, dp), lambda b, t: (b * nt + t, 0)),
                  pl.BlockSpec((ng, gw, gw), lambda b, t: (0, 0, 0)),
                  pl.BlockSpec((1, dp), lambda b, t: (0, 0)),
                  pl.BlockSpec((dp, dm), lambda b, t: (0, 0))],
        out_specs=(pl.BlockSpec((tc, dm), lambda b, t: (b * nt + t, 0)),
                   pl.BlockSpec((1, pool_buf, dp), lambda b, t: (b, 0, 0))),
        out_shape=(jax.ShapeDtypeStruct((n, dm), F32),
                   jax.ShapeDtypeStruct((nb, pool_buf, dp), F32)),
        scratch_shapes=[pltpu.VMEM((tc + POOL_HALO, dp), F32), pltpu.VMEM((tc, dp), BF16)],
        compiler_params=_params(("arbitrary", "arbitrary"), nbytes),
        name="pool_seq",
    )(z, w_pool, pool_scale.reshape(1, dp), w_branch)


def _pool_step_kernel(u_ref, cache_ref, wp_ref, scale_ref, wb_ref, o_ref, new_ref, y_ref):
    dp = u_ref.shape[1]
    gw = wp_ref.shape[1]
    lb = cache_ref.shape[1] // dp
    diffs = []
    for k, w in enumerate(POOL_WINDOWS):
        u = u_ref[:, k * gw:(k + 1) * gw]
        s = u
        for j in range(1, w):
            s = s + cache_ref[:, (lb - j) * dp + k * gw:(lb - j) * dp + (k + 1) * gw]
        diffs.append(s / float(w) - u)
    o_ref[...] = _pool_project(diffs, wp_ref, scale_ref, wb_ref, y_ref)
    new_ref[:, 0:(lb - 1) * dp] = cache_ref[:, dp:lb * dp]
    new_ref[:, (lb - 1) * dp:lb * dp] = u_ref[...]


def _pool_step(z, cache, w_pool, pool_scale, w_branch):
    n = z.shape[0]
    ng, gw, _ = w_pool.shape
    dp = ng * gw
    dm = w_branch.shape[1]
    lbdp = cache.shape[1]
    nbytes = 2 * (n * dp * 4 + 2 * n * lbdp * 4 + ng * gw * gw * 2 + dp * dm * 2 + n * dm * 4) + n * dp * 2
    return pl.pallas_call(
        _pool_step_kernel,
        grid=(1,),
        in_specs=[pl.BlockSpec((n, dp), lambda i: (0, 0)),
                  pl.BlockSpec((n, lbdp), lambda i: (0, 0)),
                  pl.BlockSpec((ng, gw, gw), lambda i: (0, 0, 0)),
                  pl.BlockSpec((1, dp), lambda i: (0, 0)),
                  pl.BlockSpec((dp, dm), lambda i: (0, 0))],
        out_specs=(pl.BlockSpec((n, dm), lambda i: (0, 0)),
                   pl.BlockSpec((n, lbdp), lambda i: (0, 0))),
        out_shape=(jax.ShapeDtypeStruct((n, dm), F32),
                   jax.ShapeDtypeStruct((n, lbdp), F32)),
        scratch_shapes=[pltpu.VMEM((n, dp), BF16)],
        compiler_params=_params(("arbitrary",), nbytes),
        name="pool_step",
    )(z, cache, w_pool, pool_scale.reshape(1, dp), w_branch)


def _cmul_add(xr, xi, ar, ai, br, bi):
    return xr + ar * br - ai * bi, xi + ar * bi + ai * br


def _ssm_seq_kernel(n_cast, u_ref, bcat_ref, ccat_ref, pw_ref, at_ref, d_ref, *refs):
    cast_in, (gy_ref, st_ref), refs = refs[:n_cast], refs[n_cast:n_cast + 2], refs[n_cast + 2:]
    cast_out, (h_ref, loc_ref, hb_ref, seed_ref, carry_ref) = refs[:n_cast], refs[n_cast:]
    for src_ref, dst_ref in zip(cast_in, cast_out):
        dst_ref[...] = src_ref[...].astype(BF16)

    nslab = h_ref.shape[0]
    npair = nslab // 2
    t = pl.program_id(2)

    @pl.when(t == 0)
    def _():
        carry_ref[...] = jnp.zeros_like(carry_ref)

    u = u_ref[...]
    bu = _dot(u.astype(BF16), bcat_ref[0])
    for j in range(nslab):
        for r in range(SUBLANES):
            h_ref[j, r * SEG_PITCH:r * SEG_PITCH + SEG_LEN, :] = (
                bu[r * SEG_LEN:(r + 1) * SEG_LEN, j * LANES:(j + 1) * LANES])

    def seg_step(n, state):
        rows = pl.ds(n, SUBLANES, stride=SEG_PITCH)
        new = []
        for j in range(npair):
            sr, si = _cmul_add(h_ref[j, rows, :], h_ref[npair + j, rows, :],
                               at_ref[0, j], at_ref[0, npair + j], state[2 * j], state[2 * j + 1])
            loc_ref[j, rows, :] = sr
            loc_ref[npair + j, rows, :] = si
            new += [sr, si]
        return tuple(new)

    ends = (jnp.zeros((SUBLANES, LANES), F32),) * nslab
    for n in range(SEG_LEN):
        ends = seg_step(n, ends)

    last = slice(SEG_LEN - 1, SEG_LEN)
    for j in range(npair):
        cr, ci = carry_ref[j], carry_ref[npair + j]
        for r in range(SUBLANES):
            seed_ref[j, r:r + 1, :] = cr
            seed_ref[npair + j, r:r + 1, :] = ci
            cr, ci = _cmul_add(ends[2 * j][r:r + 1, :], ends[2 * j + 1][r:r + 1, :],
                               pw_ref[0, j, last, :], pw_ref[0, npair + j, last, :], cr, ci)
        carry_ref[j] = cr
        carry_ref[npair + j] = ci
    st_ref[0, 0] = carry_ref[...]

    rows_bf16 = 2 * SUBLANES
    for r in range(SUBLANES):
        for i in range(SEG_LEN // rows_bf16):
            src = slice(r * SEG_PITCH + i * rows_bf16, r * SEG_PITCH + (i + 1) * rows_bf16)
            dst = slice(r * SEG_LEN + i * rows_bf16, r * SEG_LEN + (i + 1) * rows_bf16)
            pws = slice(i * rows_bf16, (i + 1) * rows_bf16)
            for j in range(npair):
                hr, hi = _cmul_add(loc_ref[j, src, :], loc_ref[npair + j, src, :],
                                   pw_ref[0, j, pws, :], pw_ref[0, npair + j, pws, :],
                                   seed_ref[j, r:r + 1, :], seed_ref[npair + j, r:r + 1, :])
                hb_ref[dst, j * LANES:(j + 1) * LANES] = hr.astype(BF16)
                hb_ref[dst, (npair + j) * LANES:(npair + j + 1) * LANES] = hi.astype(BF16)

    y = _dot(hb_ref[...], ccat_ref[0]) + d_ref[...] * u
    gy_ref[...] = jax.nn.gelu(y).astype(BF16)


def _ssm_seq(z, col0, b_cat, c_cat, pw, a_tile, d_skip, nb, t_len, side_casts):
    n = z.shape[0]
    nkb, kw, sw = b_cat.shape
    nslab = sw // LANES
    tc = SUBLANES * SEG_LEN
    nt = t_len // tc
    cb = col0 // kw
    n_steps = nkb * nb * nt
    cast_rows = [w.shape[0] // n_steps for w in side_casts]
    assert all(r % (2 * SUBLANES) == 0 and r * n_steps == w.shape[0] for r, w in zip(cast_rows, side_casts))
    nbytes = (2 * tc * kw * 4 + 4 * kw * sw * 2 + 2 * (SEG_LEN + SUBLANES) * sw * 4 + 2 * tc * kw * 2
              + 2 * SUBLANES * SEG_PITCH * sw * 4 + tc * sw * 2 + tc * sw * 4
              + sum(2 * r * w.shape[1] * 6 for r, w in zip(cast_rows, side_casts)))

    def cast_spec(r, w):
        return pl.BlockSpec((r, w.shape[1]), lambda k, b, t: ((k * nb + b) * nt + t, 0))

    cast_specs = [cast_spec(r, w) for r, w in zip(cast_rows, side_casts)]
    gy, st, *cast = pl.pallas_call(
        functools.partial(_ssm_seq_kernel, len(side_casts)),
        grid=(nkb, nb, nt),
        in_specs=[pl.BlockSpec((tc, kw), lambda k, b, t: (b * nt + t, cb + k)),
                  pl.BlockSpec((1, kw, sw), lambda k, b, t: (k, 0, 0)),
                  pl.BlockSpec((1, sw, kw), lambda k, b, t: (k, 0, 0)),
                  pl.BlockSpec((1, nslab, SEG_LEN, LANES), lambda k, b, t: (k, 0, 0, 0)),
                  pl.BlockSpec((1, nslab, SUBLANES, LANES), lambda k, b, t: (k, 0, 0, 0)),
                  pl.BlockSpec((1, kw), lambda k, b, t: (0, k))] + cast_specs,
        out_specs=[pl.BlockSpec((tc, kw), lambda k, b, t: (b * nt + t, k)),
                   pl.BlockSpec((1, 1, nslab, 1, LANES), lambda k, b, t: (b, k, 0, 0, 0))] + cast_specs,
        out_shape=[jax.ShapeDtypeStruct((n, nkb * kw), BF16),
                   jax.ShapeDtypeStruct((nb, nkb, nslab, 1, LANES), F32)]
                  + [jax.ShapeDtypeStruct(w.shape, BF16) for w in side_casts],
        scratch_shapes=[pltpu.VMEM((nslab, SUBLANES * SEG_PITCH, LANES), F32),
                        pltpu.VMEM((nslab, SUBLANES * SEG_PITCH, LANES), F32), pltpu.VMEM((tc, sw), BF16),
                        pltpu.VMEM((nslab, SUBLANES, LANES), F32), pltpu.VMEM((nslab, 1, LANES), F32)],
        compiler_params=_params(("arbitrary", "arbitrary", "arbitrary"), nbytes),
        name="ssm_seq",
    )(z, b_cat, c_cat, pw, a_tile, d_skip.reshape(1, nkb * kw), *side_casts)
    st = st.reshape(nb, nkb, 2, sw // 2)
    return gy, st[:, :, 0].reshape(nb, -1), st[:, :, 1].reshape(nb, -1), cast


def _ssm_step_kernel(u_ref, h0r_ref, h0i_ref, bcat_ref, ccat_ref, a_ref, d_ref,
                     gy_ref, h1r_ref, h1i_ref, h_ref):
    half = h0r_ref.shape[1]
    u = u_ref[...]
    bu = _dot(u.astype(BF16), bcat_ref[0])
    hr, hi = _cmul_add(bu[:, 0:half], bu[:, half:2 * half],
                       a_ref[0, :, 0:half], a_ref[0, :, half:2 * half],
                       h0r_ref[...], h0i_ref[...])
    h1r_ref[...] = hr
    h1i_ref[...] = hi
    h_ref[:, 0:half] = hr.astype(BF16)
    h_ref[:, half:2 * half] = hi.astype(BF16)
    y = _dot(h_ref[...], ccat_ref[0]) + d_ref[...] * u
    gy_ref[...] = jax.nn.gelu(y).astype(BF16)


def _ssm_step(z, col0, h0_re, h0_im, b_cat, c_cat, a_row, d_skip):
    n = z.shape[0]
    nkb, kw, sw = b_cat.shape
    half = sw // 2
    cb = col0 // kw
    nbytes = 2 * (n * kw * 4 + 4 * n * half * 4 + 2 * kw * sw * 2 + sw * 4 + n * kw * 2) + n * sw * 6
    return pl.pallas_call(
        _ssm_step_kernel,
        grid=(nkb,),
        in_specs=[pl.BlockSpec((n, kw), lambda k: (0, cb + k)),
                  pl.BlockSpec((n, half), lambda k: (0, k)),
                  pl.BlockSpec((n, half), lambda k: (0, k)),
                  pl.BlockSpec((1, kw, sw), lambda k: (k, 0, 0)),
                  pl.BlockSpec((1, sw, kw), lambda k: (k, 0, 0)),
                  pl.BlockSpec((1, 1, sw), lambda k: (k, 0, 0)),
                  pl.BlockSpec((1, kw), lambda k: (0, k))],
        out_specs=(pl.BlockSpec((n, kw), lambda k: (0, k)),
                   pl.BlockSpec((n, half), lambda k: (0, k)),
                   pl.BlockSpec((n, half), lambda k: (0, k))),
        out_shape=(jax.ShapeDtypeStruct((n, nkb * kw), BF16),
                   jax.ShapeDtypeStruct((n, nkb * half), F32),
                   jax.ShapeDtypeStruct((n, nkb * half), F32)),
        scratch_shapes=[pltpu.VMEM((n, sw), BF16)],
        compiler_params=_params(("arbitrary",), nbytes),
        name="ssm_step",
    )(z, h0_re, h0_im, b_cat, c_cat, a_row, d_skip.reshape(1, nkb * kw))


def _glu_merge_kernel(gy_ref, bp_ref, gp_ref, gs_ref, wg_ref, wb_ref, o_ref):
    ds = wb_ref.shape[0]
    g = _dot(gy_ref[...], wg_ref[...])
    y = g[:, 0:ds] * jax.nn.sigmoid(g[:, ds:2 * ds])
    bs = _dot(y.astype(BF16), wb_ref[...])
    merged = jax.nn.sigmoid(gp_ref[...]) * bp_ref[...] + jax.nn.sigmoid(gs_ref[...]) * bs
    o_ref[...] = merged.astype(BF16)


def _glu_merge(gy, bp, z, gate_col0, w_glu, w_branch, tm):
    n, ds = gy.shape
    dm = w_branch.shape[1]
    gb = gate_col0 // dm
    nbytes = 2 * (tm * ds * 2 + 3 * tm * dm * 4 + ds * 2 * ds * 2 + ds * dm * 2 + tm * dm * 2) + 4 * tm * dm * 4
    return pl.pallas_call(
        _glu_merge_kernel,
        grid=(n // tm,),
        in_specs=[pl.BlockSpec((tm, ds), lambda i: (i, 0)),
                  pl.BlockSpec((tm, dm), lambda i: (i, 0)),
                  pl.BlockSpec((tm, dm), lambda i: (i, gb)),
                  pl.BlockSpec((tm, dm), lambda i: (i, gb + 1)),
                  pl.BlockSpec((ds, 2 * ds), lambda i: (0, 0)),
                  pl.BlockSpec((ds, dm), lambda i: (0, 0))],
        out_specs=pl.BlockSpec((tm, dm), lambda i: (i, 0)),
        out_shape=jax.ShapeDtypeStruct((n, dm), BF16),
        compiler_params=_params(("arbitrary",), nbytes),
        name="glu_merge",
    )(gy, bp, z, z, w_glu, w_branch)


def _matmul_res_kernel(m_ref, w_ref, x_ref, o_ref):
    o_ref[...] = x_ref[...] + _dot(m_ref[...], w_ref[...])


def _matmul_res(m, w, x, tm, tn):
    n, d = m.shape
    dout = w.shape[1]
    nbytes = 2 * (tm * d * 2 + d * tn * 2 + 2 * tm * tn * 4)
    return pl.pallas_call(
        _matmul_res_kernel,
        grid=(n // tm, dout // tn),
        in_specs=[pl.BlockSpec((tm, d), lambda i, j: (i, 0)),
                  pl.BlockSpec((d, tn), lambda i, j: (0, j)),
                  pl.BlockSpec((tm, tn), lambda i, j: (i, j))],
        out_specs=pl.BlockSpec((tm, tn), lambda i, j: (i, j)),
        out_shape=jax.ShapeDtypeStruct((n, dout), F32),
        compiler_params=_params(("arbitrary", "arbitrary"), nbytes),
        name="matmul_res",
    )(m, w, x)


def _ffn_gate_down(x_ref, conv, v, wd_ref, o_ref, acc_ref):
    c = pl.program_id(1)
    part = _dot((jax.nn.gelu(conv) * v).astype(BF16), wd_ref[...])

    @pl.when(c == 0)
    def _():
        acc_ref[...] = part

    @pl.when(c > 0)
    def _():
        acc_ref[...] += part

    @pl.when(c == pl.num_programs(1) - 1)
    def _():
        o_ref[...] = x_ref[...] + acc_ref[...]


def _ffn_seq_kernel(tiles_per_seq, x_ref, g_ref, wa_ref, wv_ref, wc_ref, bc_ref, wd_ref,
                    o_ref, new_ref, h_ref, ext_ref, carry_ref, gate_ref):
    tm = x_ref.shape[0]
    tf = gate_ref.shape[1]
    i = pl.program_id(0)
    c = pl.program_id(1)

    @pl.when(jnp.logical_and(i == 0, c == 0))
    def _():
        o_ref[...] = jnp.zeros(o_ref.shape, F32)
        carry_ref[...] = jnp.zeros(carry_ref.shape, F32)

    @pl.when(c == 0)
    def _():
        h_ref[...] = _rmsnorm(x_ref[...], g_ref[...]).astype(BF16)

    seq_start = i % tiles_per_seq == 0
    nb = new_ref.shape[1]
    for q in range(tf // FFN_SLICE):
        cols = slice(q * FFN_SLICE, (q + 1) * FFN_SLICE)
        a = _dot(h_ref[...], wa_ref[:, cols])
        v = _dot(h_ref[...], wv_ref[:, cols])
        ext_ref[0:CONV_HALO, cols] = jnp.where(seq_start, 0.0, carry_ref[c, :, cols])
        ext_ref[CONV_HALO:CONV_HALO + tm, cols] = a
        conv = bc_ref[:, cols] + wc_ref[CONV_W - 1:CONV_W, cols] * a
        for j in range(CONV_W - 1):
            off = CONV_HALO - (CONV_W - 1) + j
            conv = conv + wc_ref[j:j + 1, cols] * ext_ref[off:off + tm, cols]
        carry_ref[c, :, cols] = ext_ref[tm:tm + CONV_HALO, cols]
        new_ref[0, :, cols] = ext_ref[CONV_HALO + tm - nb:CONV_HALO + tm, cols]
        gate_ref[:, cols] = (jax.nn.gelu(conv) * v).astype(BF16)
    o_ref[...] = _dot(gate_ref[...], wd_ref[...]) + jnp.where(c == 0, x_ref[...], o_ref[...])


def _ffn_seq(x, g, w_up, w_conv, b_conv, w_down, nb, t_len, tm, tf, conv_buf):
    n, d = x.shape
    dff = w_down.shape[0]
    nc = dff // tf
    tps = t_len // tm
    nbytes = (4 * tm * d * 4 + 4 * d * tf * 2 + 2 * tf * d * 2 + tm * d * 2 + tm * tf * 2
              + (tm + CONV_HALO) * tf * 4 + nc * CONV_HALO * tf * 4 + 6 * tm * FFN_SLICE * 4)
    out, new_tail = pl.pallas_call(
        functools.partial(_ffn_seq_kernel, tps),
        grid=(n // tm, nc),
        in_specs=[pl.BlockSpec((tm, d), lambda i, c: (i, 0)),
                  pl.BlockSpec((1, d), lambda i, c: (0, 0)),
                  pl.BlockSpec((d, tf), lambda i, c: (0, c)),
                  pl.BlockSpec((d, tf), lambda i, c: (0, nc + c)),
                  pl.BlockSpec((CONV_W, tf), lambda i, c: (0, c)),
                  pl.BlockSpec((1, tf), lambda i, c: (0, c)),
                  pl.BlockSpec((tf, d), lambda i, c: (c, 0))],
        out_specs=(pl.BlockSpec((tm, d), lambda i, c: (i, 0)),
                   pl.BlockSpec((1, conv_buf, tf), lambda i, c: (i, 0, c))),
        out_shape=(jax.ShapeDtypeStruct((n, d), F32),
                   jax.ShapeDtypeStruct((n // tm, conv_buf, dff), F32)),
        scratch_shapes=[pltpu.VMEM((tm, d), BF16), pltpu.VMEM((tm + CONV_HALO, tf), F32),
                        pltpu.VMEM((nc, CONV_HALO, tf), F32), pltpu.VMEM((tm, tf), BF16)],
        compiler_params=_params(("arbitrary", "arbitrary"), nbytes),
        name="ffn_seq",
    )(x, g.reshape(1, d), w_up, w_up, w_conv, b_conv.reshape(1, dff), w_down)
    return out, new_tail[tps - 1::tps]


def _ffn_step_kernel(x_ref, g_ref, wa_ref, wv_ref, wc_ref, bc_ref, wd_ref, p0_ref, p1_ref,
                     o_ref, a_ref, h_ref, acc_ref):
    @pl.when(pl.program_id(1) == 0)
    def _():
        h_ref[...] = _rmsnorm(x_ref[...], g_ref[...]).astype(BF16)

    a = _dot(h_ref[...], wa_ref[...])
    v = _dot(h_ref[...], wv_ref[...])
    conv = bc_ref[...] + wc_ref[2:3, :] * a + wc_ref[1:2, :] * p1_ref[...] + wc_ref[0:1, :] * p0_ref[...]
    a_ref[...] = a
    _ffn_gate_down(x_ref, conv, v, wd_ref, o_ref, acc_ref)


def _ffn_step(x, g, w_up, w_conv, b_conv, w_down, cache, tf):
    n, d = x.shape
    dff = w_down.shape[0]
    nc = dff // tf
    nbytes = 4 * n * d * 4 + 4 * d * tf * 2 + 2 * tf * d * 2 + n * d * 6 + 12 * n * tf * 4
    return pl.pallas_call(
        _ffn_step_kernel,
        grid=(1, nc),
        in_specs=[pl.BlockSpec((n, d), lambda i, c: (0, 0)),
                  pl.BlockSpec((1, d), lambda i, c: (0, 0)),
                  pl.BlockSpec((d, tf), lambda i, c: (0, c)),
                  pl.BlockSpec((d, tf), lambda i, c: (0, nc + c)),
                  pl.BlockSpec((CONV_W, tf), lambda i, c: (0, c)),
                  pl.BlockSpec((1, tf), lambda i, c: (0, c)),
                  pl.BlockSpec((tf, d), lambda i, c: (c, 0)),
                  pl.BlockSpec((n, tf), lambda i, c: (0, c)),
                  pl.BlockSpec((n, tf), lambda i, c: (0, nc + c))],
        out_specs=(pl.BlockSpec((n, d), lambda i, c: (0, 0)),
                   pl.BlockSpec((n, tf), lambda i, c: (0, c))),
        out_shape=(jax.ShapeDtypeStruct((n, d), F32),
                   jax.ShapeDtypeStruct((n, dff), F32)),
        scratch_shapes=[pltpu.VMEM((n, d), BF16), pltpu.VMEM((n, d), F32)],
        compiler_params=_params(("arbitrary", "arbitrary"), nbytes),
        name="ffn_step",
    )(x, g.reshape(1, d), w_up, w_up, w_conv, b_conv.reshape(1, dff), w_down, cache, cache)


def _ple_final_kernel(x_ref, p_ref, gp_ref, wg_ref, wp_ref, gf_ref, o_ref):
    x = x_ref[...]
    gate = jax.nn.sigmoid(_dot(_rmsnorm(x, gp_ref[...]).astype(BF16), wg_ref[...]))
    x = x + gate * _dot(p_ref[...].astype(BF16), wp_ref[...])
    o_ref[...] = _rmsnorm(x, gf_ref[...])


def _ple_final(x, p, g_ple, w_gate, w_ple, g_final, tm):
    n, d = x.shape
    dp = p.shape[1]
    nbytes = 2 * (2 * tm * d * 4 + tm * dp * 4 + d * d * 2 + dp * d * 2) + 4 * tm * d * 4
    return pl.pallas_call(
        _ple_final_kernel,
        grid=(n // tm,),
        in_specs=[pl.BlockSpec((tm, d), lambda i: (i, 0)),
                  pl.BlockSpec((tm, dp), lambda i: (i, 0)),
                  pl.BlockSpec((1, d), lambda i: (0, 0)),
                  pl.BlockSpec((d, d), lambda i: (0, 0)),
                  pl.BlockSpec((dp, d), lambda i: (0, 0)),
                  pl.BlockSpec((1, d), lambda i: (0, 0))],
        out_specs=pl.BlockSpec((tm, d), lambda i: (i, 0)),
        out_shape=jax.ShapeDtypeStruct((n, d), F32),
        compiler_params=_params(("arbitrary",), nbytes),
        name="ple_final",
    )(x, p, g_ple.reshape(1, d), w_gate, w_ple, g_final.reshape(1, d))


def _tile(n, pref):
    return pref if n % pref == 0 else n


def kernel(x_prompt, x_sample, cache_pool, state_ssm_re, state_ssm_im, cache_conv, p_prompt, p_sample, g_mix, w_in, w_pool, pool_scale, ssm_lam_re, ssm_lam_im, ssm_log_dt, ssm_b_re, ssm_b_im, ssm_c_re, ssm_c_im, ssm_d, w_glu, w_branch_pool, w_branch_ssm, w_out, g_ffn, w_up, w_conv, b_conv, w_down, g_ple, w_ple_gate, w_ple, g_final):
    depth = g_mix.shape[0]
    nb, t_len, d = x_prompt.shape
    ns = x_sample.shape[0]
    assert x_sample.shape[1] == 1, "the sample group advances one step per call"
    pool_buf, d_pool = cache_pool.shape[2], cache_pool.shape[3]
    conv_buf, d_ff = cache_conv.shape[2], cache_conv.shape[3]
    n_grp, n_state = ssm_lam_re.shape[1], ssm_lam_re.shape[2]
    d_ssm = ssm_d.shape[1]
    assert pool_buf == max(POOL_WINDOWS) - 1 and conv_buf == CONV_W - 1
    assert n_state == SSM_STATE and d_ssm == n_grp * SSM_GROUP

    xp = x_prompt.reshape(nb * t_len, d)
    xs = x_sample.reshape(ns, d)
    outs = [[] for _ in range(8)]
    for i in range(depth):
        w_in_b, w_pool_b, w_ple_b = w_in[i].astype(BF16), w_pool[i].astype(BF16), w_ple[i].astype(BF16)
        b_cat, c_cat, a_pow, a_tile, a_row = _ssm_params(ssm_lam_re[i], ssm_lam_im[i], ssm_log_dt[i],
                                                         ssm_b_re[i], ssm_b_im[i], ssm_c_re[i], ssm_c_im[i])

        z = _norm_matmul(xp, g_mix[i], w_in_b, 1024, 3 * W_TILE)
        gy, st_re, st_im, (w_up_b, w_down_b, w_out_b, w_pg_b, w_glu_b, w_bs_b, w_bp_b) = _ssm_seq(
            z, d_pool, b_cat, c_cat, a_pow, a_tile, ssm_d[i], nb, t_len,
            [w_up[i], w_down[i], w_out[i], w_ple_gate[i], w_glu[i], w_branch_ssm[i], w_branch_pool[i]])

        def mix_tail(x, z, bp, gy, tm):
            merged = _glu_merge(gy, bp, z, d_pool + d_ssm, w_glu_b, w_bs_b, tm)
            return _matmul_res(merged, w_out_b, x, _tile(x.shape[0], 1024), 2 * W_TILE)

        bp, pool_new = _pool_seq(z, w_pool_b, pool_scale[i], w_bp_b, nb, t_len, 1024, pool_buf)
        xp = mix_tail(xp, z, bp, gy, 512)
        xp, conv_new = _ffn_seq(xp, g_ffn[i], w_up_b, w_conv[i], b_conv[i], w_down_b, nb, t_len, 512, 2 * W_TILE,
                                conv_buf)
        xp_out = _ple_final(xp, p_prompt[i].reshape(nb * t_len, -1), g_ple[i], w_pg_b, w_ple_b, g_final, 512)
        for lst, val in zip(outs[:4], (pool_new, st_re.reshape(nb, n_grp, n_state),
                                       st_im.reshape(nb, n_grp, n_state), conv_new)):
            lst.append(val)

        z = _norm_matmul(xs, g_mix[i], w_in_b, ns, 3 * W_TILE)
        bp, pool_new = _pool_step(z, cache_pool[i].reshape(ns, pool_buf * d_pool), w_pool_b, pool_scale[i], w_bp_b)
        gy, st_re, st_im = _ssm_step(z, d_pool, state_ssm_re[i].reshape(ns, -1), state_ssm_im[i].reshape(ns, -1),
                                     b_cat, c_cat, a_row, ssm_d[i])
        xs = mix_tail(xs, z, bp, gy, ns)
        xs, a_new = _ffn_step(xs, g_ffn[i], w_up_b, w_conv[i], b_conv[i], w_down_b,
                              cache_conv[i].reshape(ns, conv_buf * d_ff), 2 * W_TILE)
        conv_new = jnp.concatenate([cache_conv[i][:, 1:], a_new[:, None, :]], axis=1)
        xs_out = _ple_final(xs, p_sample[i].reshape(ns, -1), g_ple[i], w_pg_b, w_ple_b, g_final, ns)
        for lst, val in zip(outs[4:], (pool_new.reshape(ns, pool_buf, d_pool), st_re.reshape(ns, n_grp, n_state),
                                       st_im.reshape(ns, n_grp, n_state), conv_new)):
            lst.append(val)

    assert depth == 1
    y_prompt = xp_out.reshape(nb, t_len, d)
    y_sample = xs_out.reshape(ns, 1, d)
    return (y_prompt, y_sample) + tuple(jnp.stack(o, axis=0) for o in outs)
```

```python
import functools

import jax
import jax.numpy as jnp
from jax import lax
from jax.experimental import pallas as pl
from jax.experimental.pallas import tpu as pltpu

F32 = jnp.float32
BF16 = jnp.bfloat16

EPS = 1e-6
POOL_WINDOWS = (2, 4, 8, 16)
POOL_HALO = 16
SSM_GROUP = 16
SSM_STATE = 64
SSM_BLOCK_GROUPS = 16
LANES = 128
SUBLANES = 8
SEG_LEN = 64
SEG_PITCH = SEG_LEN + 8
CONV_W = 3
CONV_HALO = 8
FFN_SLICE = 256
W_TILE = 512
V7X_VMEM_BYTES = 64 * 1024 * 1024


def _vmem_limit(nbytes):
    return int(min(nbytes * 1.25 + (8 << 20), V7X_VMEM_BYTES - (6 << 20)))


def _params(sem, nbytes):
    return pltpu.CompilerParams(dimension_semantics=sem, vmem_limit_bytes=_vmem_limit(nbytes))


def _rmsnorm(x, g):
    return x * lax.rsqrt(jnp.mean(x * x, axis=-1, keepdims=True) + EPS) * g


def _dot(a, b):
    return jnp.dot(a, b, preferred_element_type=F32)


def _ssm_params_kernel(lr_ref, li_ref, logdt_ref, br_ref, bi_ref, pwr_ref, pwi_ref, bbr_ref, bbi_ref):
    lr = lr_ref[...]
    li = li_ref[...]
    dt = jnp.exp(logdt_ref[...])
    mag = jnp.exp(lr * dt)
    a_re = mag * jnp.cos(li * dt)
    a_im = mag * jnp.sin(li * dt)
    nr = a_re - 1.0
    ni = a_im
    den = lr * lr + li * li
    coef_re = (nr * lr + ni * li) / den
    coef_im = (ni * lr - nr * li) / den
    br = br_ref[...]
    bi = bi_ref[...]
    bbr_ref[...] = coef_re[None] * br - coef_im[None] * bi
    bbi_ref[...] = coef_re[None] * bi + coef_im[None] * br
    pr, pi = a_re, a_im
    pwr_ref[0] = pr
    pwi_ref[0] = pi
    for n in range(1, SEG_LEN):
        pr, pi = pr * a_re - pi * a_im, pr * a_im + pi * a_re
        pwr_ref[n] = pr
        pwi_ref[n] = pi


def _ssm_params(lam_re, lam_im, log_dt, b_re, b_im, c_re, c_im):
    g, p = lam_re.shape
    h = b_re.shape[-1]
    nkb = g // SSM_BLOCK_GROUPS
    bl = SSM_BLOCK_GROUPS
    pwr, pwi, bbr, bbi = pl.pallas_call(
        _ssm_params_kernel,
        out_shape=(jax.ShapeDtypeStruct((SEG_LEN, g, p), F32),
                   jax.ShapeDtypeStruct((SEG_LEN, g, p), F32),
                   jax.ShapeDtypeStruct((h, g, p), F32),
                   jax.ShapeDtypeStruct((h, g, p), F32)),
        name="ssm_params",
    )(lam_re, lam_im, log_dt.reshape(g, 1), jnp.transpose(b_re, (2, 0, 1)), jnp.transpose(b_im, (2, 0, 1)))

    eye = jnp.eye(bl, dtype=F32)
    bb = jnp.stack([bbr, bbi]).reshape(2, h, nkb, bl, p)
    b_cat = jnp.einsum('shkgp,gj->kghsjp', bb, eye).reshape(nkb, bl * h, 2 * bl * p).astype(BF16)
    cc = jnp.stack([c_re, -c_im]).reshape(2, nkb, bl, h, p)
    c_cat = jnp.einsum('skghp,gj->ksgpjh', cc, eye).reshape(nkb, 2 * bl * p, bl * h).astype(BF16)
    nslab = 2 * bl * p // LANES
    pw = jnp.stack([pwr, pwi], axis=1).reshape(SEG_LEN, 2, nkb, bl * p)
    a_row = jnp.transpose(pw[0], (1, 0, 2)).reshape(nkb, 1, 2 * bl * p)
    pw = jnp.transpose(pw.reshape(SEG_LEN, 2, nkb, nslab // 2, LANES), (2, 1, 3, 0, 4))
    pw = pw.reshape(nkb, nslab, SEG_LEN, LANES)
    a_tile = jnp.broadcast_to(pw[:, :, 0:1, :], (nkb, nslab, SUBLANES, LANES))
    return b_cat, c_cat, pw, a_tile, a_row


def _norm_matmul_kernel(x_ref, g_ref, w_ref, o_ref, h_ref):
    @pl.when(pl.program_id(1) == 0)
    def _():
        h_ref[...] = _rmsnorm(x_ref[...], g_ref[...]).astype(BF16)
    o_ref[...] = _dot(h_ref[...], w_ref[...])


def _norm_matmul(x, g, w, tm, tn):
    n, d = x.shape
    dout = w.shape[1]
    nbytes = 2 * tm * d * 4 + tm * d * 2 + 2 * d * tn * 2 + 2 * tm * tn * 4
    return pl.pallas_call(
        _norm_matmul_kernel,
        grid=(n // tm, dout // tn),
        in_specs=[pl.BlockSpec((tm, d), lambda i, j: (i, 0)),
                  pl.BlockSpec((1, d), lambda i, j: (0, 0)),
                  pl.BlockSpec((d, tn), lambda i, j: (0, j))],
        out_specs=pl.BlockSpec((tm, tn), lambda i, j: (i, j)),
        out_shape=jax.ShapeDtypeStruct((n, dout), F32),
        scratch_shapes=[pltpu.VMEM((tm, d), BF16)],
        compiler_params=_params(("arbitrary", "arbitrary"), nbytes),
        name="norm_matmul",
    )(x, g.reshape(1, d), w)


def _pool_project(diffs, wp_ref, scale_ref, wb_ref, y_ref):
    gw = wp_ref.shape[1]
    for k, diff in enumerate(diffs):
        yk = _dot(diff.astype(BF16), wp_ref[k]) * scale_ref[:, k * gw:(k + 1) * gw]
        y_ref[:, k * gw:(k + 1) * gw] = yk.astype(BF16)
    return _dot(y_ref[...], wb_ref[...])


def _pool_seq_kernel(u_ref, wp_ref, scale_ref, wb_ref, o_ref, new_ref, ext_ref, y_ref):
    tc, dp = u_ref.shape
    gw = wp_ref.shape[1]
    t = pl.program_id(1)

    @pl.when(t == 0)
    def _():
        ext_ref[0:POOL_HALO, :] = jnp.zeros((POOL_HALO, dp), F32)

    ext_ref[POOL_HALO:POOL_HALO + tc, :] = u_ref[...]
    pos = (t * tc + 1 + lax.broadcasted_iota(jnp.int32, (tc, 1), 0)).astype(F32)
    diffs = []
    for k, w in enumerate(POOL_WINDOWS):
        cols = slice(k * gw, (k + 1) * gw)
        u = ext_ref[POOL_HALO:POOL_HALO + tc, cols]
        s = u
        for j in range(1, w):
            s = s + ext_ref[POOL_HALO - j:POOL_HALO - j + tc, cols]
        count = jnp.minimum(pos, float(w))
        diffs.append(s / count - u)
    o_ref[...] = _pool_project(diffs, wp_ref, scale_ref, wb_ref, y_ref)
    nb = new_ref.shape[1]
    new_ref[0] = ext_ref[POOL_HALO + tc - nb:POOL_HALO + tc, :]
    ext_ref[0:POOL_HALO, :] = ext_ref[tc:tc + POOL_HALO, :]


def _pool_seq(z, w_pool, pool_scale, w_branch, nb, t_len, tc, pool_buf):
    n = z.shape[0]
    ng, gw, _ = w_pool.shape
    dp = ng * gw
    dm = w_branch.shape[1]
    nt = t_len // tc
    nbytes = (2 * tc * dp * 4 + 2 * ng * gw * gw * 2 + 2 * dp * dm * 2 + 2 * tc * dm * 4
              + (tc + POOL_HALO) * dp * 4 + tc * dp * 2 + 4 * tc * gw * 4)
    return pl.pallas_call(
        _pool_seq_kernel,
        grid=(nb, nt),
        in_specs=[pl.BlockSpec((tc, dp), lambda b, t: (b * nt + t, 0)),
                  pl.BlockSpec((ng, gw, gw), lambda b, t: (0, 0, 0)),
                  pl.BlockSpec((1, dp), lambda b, t: (0, 0)),
                  pl.BlockSpec((dp, dm), lambda b, t: (0, 0))],
        out_specs=(pl.BlockSpec((tc, dm), lambda b, t: (b * nt + t, 0)),
                   pl.BlockSpec((1, pool_buf, dp), lambda b, t: (b, 0, 0))),
        out_shape=(jax.ShapeDtypeStruct((n, dm), F32),
                   jax.ShapeDtypeStruct((nb, pool_buf, dp), F32)),
        scratch_shapes=[pltpu.VMEM((tc + POOL_HALO, dp), F32), pltpu.VMEM((tc, dp), BF16)],
        compiler_params=_params(("arbitrary", "arbitrary"), nbytes),
        name="pool_seq",
    )(z, w_pool, pool_scale.reshape(1, dp), w_branch)


def _pool_step_kernel(u_ref, cache_ref, wp_ref, scale_ref, wb_ref, o_ref, new_ref, y_ref):
    dp = u_ref.shape[1]
    gw = wp_ref.shape[1]
    lb = cache_ref.shape[1] // dp
    diffs = []
    for k, w in enumerate(POOL_WINDOWS):
        u = u_ref[:, k * gw:(k + 1) * gw]
        s = u
        for j in range(1, w):
            s = s + cache_ref[:, (lb - j) * dp + k * gw:(lb - j) * dp + (k + 1) * gw]
        diffs.append(s / float(w) - u)
    o_ref[...] = _pool_project(diffs, wp_ref, scale_ref, wb_ref, y_ref)
    new_ref[:, 0:(lb - 1) * dp] = cache_ref[:, dp:lb * dp]
    new_ref[:, (lb - 1) * dp:lb * dp] = u_ref[...]


def _pool_step(z, cache, w_pool, pool_scale, w_branch):
    n = z.shape[0]
    ng, gw, _ = w_pool.shape
    dp = ng * gw
    dm = w_branch.shape[1]
    lbdp = cache.shape[1]
    nbytes = 2 * (n * dp * 4 + 2 * n * lbdp * 4 + ng * gw * gw * 2 + dp * dm * 2 + n * dm * 4) + n * dp * 2
    return pl.pallas_call(
        _pool_step_kernel,
        grid=(1,),
        in_specs=[pl.BlockSpec((n, dp), lambda i: (0, 0)),
                  pl.BlockSpec((n, lbdp), lambda i: (0, 0)),
                  pl.BlockSpec((ng, gw, gw), lambda i: (0, 0, 0)),
                  pl.BlockSpec((1, dp), lambda i: (0, 0)),
                  pl.BlockSpec((dp, dm), lambda i: (0, 0))],
        out_specs=(pl.BlockSpec((n, dm), lambda i: (0, 0)),
                   pl.BlockSpec((n, lbdp), lambda i: (0, 0))),
        out_shape=(jax.ShapeDtypeStruct((n, dm), F32),
                   jax.ShapeDtypeStruct((n, lbdp), F32)),
        scratch_shapes=[pltpu.VMEM((n, dp), BF16)],
        compiler_params=_params(("arbitrary",), nbytes),
        name="pool_step",
    )(z, cache, w_pool, pool_scale.reshape(1, dp), w_branch)


def _cmul_add(xr, xi, ar, ai, br, bi):
    return xr + ar * br - ai * bi, xi + ar * bi + ai * br


def _ssm_seq_kernel(n_cast, u_ref, bcat_ref, ccat_ref, pw_ref, at_ref, d_ref, *refs):
    cast_in, (gy_ref, st_ref), refs = refs[:n_cast], refs[n_cast:n_cast + 2], refs[n_cast + 2:]
    cast_out, (h_ref, loc_ref, hb_ref, seed_ref, carry_ref) = refs[:n_cast], refs[n_cast:]
    for src_ref, dst_ref in zip(cast_in, cast_out):
        dst_ref[...] = src_ref[...].astype(BF16)

    nslab = h_ref.shape[0]
    npair = nslab // 2
    t = pl.program_id(2)

    @pl.when(t == 0)
    def _():
        carry_ref[...] = jnp.zeros_like(carry_ref)

    u = u_ref[...]
    bu = _dot(u.astype(BF16), bcat_ref[0])
    for j in range(nslab):
        for r in range(SUBLANES):
            h_ref[j, r * SEG_PITCH:r * SEG_PITCH + SEG_LEN, :] = (
                bu[r * SEG_LEN:(r + 1) * SEG_LEN, j * LANES:(j + 1) * LANES])

    def seg_step(n, state):
        rows = pl.ds(n, SUBLANES, stride=SEG_PITCH)
        new = []
        for j in range(npair):
            sr, si = _cmul_add(h_ref[j, rows, :], h_ref[npair + j, rows, :],
                               at_ref[0, j], at_ref[0, npair + j], state[2 * j], state[2 * j + 1])
            loc_ref[j, rows, :] = sr
            loc_ref[npair + j, rows, :] = si
            new += [sr, si]
        return tuple(new)

    ends = (jnp.zeros((SUBLANES, LANES), F32),) * nslab
    for n in range(SEG_LEN):
        ends = seg_step(n, ends)

    last = slice(SEG_LEN - 1, SEG_LEN)
    for j in range(npair):
        cr, ci = carry_ref[j], carry_ref[npair + j]
        for r in range(SUBLANES):
            seed_ref[j, r:r + 1, :] = cr
            seed_ref[npair + j, r:r + 1, :] = ci
            cr, ci = _cmul_add(ends[2 * j][r:r + 1, :], ends[2 * j + 1][r:r + 1, :],
                               pw_ref[0, j, last, :], pw_ref[0, npair + j, last, :], cr, ci)
        carry_ref[j] = cr
        carry_ref[npair + j] = ci
    st_ref[0, 0] = carry_ref[...]

    rows_bf16 = 2 * SUBLANES
    for r in range(SUBLANES):
        for i in range(SEG_LEN // rows_bf16):
            src = slice(r * SEG_PITCH + i * rows_bf16, r * SEG_PITCH + (i + 1) * rows_bf16)
            dst = slice(r * SEG_LEN + i * rows_bf16, r * SEG_LEN + (i + 1) * rows_bf16)
            pws = slice(i * rows_bf16, (i + 1) * rows_bf16)
            for j in range(npair):
                hr, hi = _cmul_add(loc_ref[j, src, :], loc_ref[npair + j, src, :],
                                   pw_ref[0, j, pws, :], pw_ref[0, npair + j, pws, :],
                                   seed_ref[j, r:r + 1, :], seed_ref[npair + j, r:r + 1, :])
                hb_ref[dst, j * LANES:(j + 1) * LANES] = hr.astype(BF16)
                hb_ref[dst, (npair + j) * LANES:(npair + j + 1) * LANES] = hi.astype(BF16)

    y = _dot(hb_ref[...], ccat_ref[0]) + d_ref[...] * u
    gy_ref[...] = jax.nn.gelu(y).astype(BF16)


def _ssm_seq(z, col0, b_cat, c_cat, pw, a_tile, d_skip, nb, t_len, side_casts):
    n = z.shape[0]
    nkb, kw, sw = b_cat.shape
    nslab = sw // LANES
    tc = SUBLANES * SEG_LEN
    nt = t_len // tc
    cb = col0 // kw
    n_steps = nkb * nb * nt
    cast_rows = [w.shape[0] // n_steps for w in side_casts]
    assert all(r % (2 * SUBLANES) == 0 and r * n_steps == w.shape[0] for r, w in zip(cast_rows, side_casts))
    nbytes = (2 * tc * kw * 4 + 4 * kw * sw * 2 + 2 * (SEG_LEN + SUBLANES) * sw * 4 + 2 * tc * kw * 2
              + 2 * SUBLANES * SEG_PITCH * sw * 4 + tc * sw * 2 + tc * sw * 4
              + sum(2 * r * w.shape[1] * 6 for r, w in zip(cast_rows, side_casts)))

    def cast_spec(r, w):
        return pl.BlockSpec((r, w.shape[1]), lambda k, b, t: ((k * nb + b) * nt + t, 0))

    cast_specs = [cast_spec(r, w) for r, w in zip(cast_rows, side_casts)]
    gy, st, *cast = pl.pallas_call(
        functools.partial(_ssm_seq_kernel, len(side_casts)),
        grid=(nkb, nb, nt),
        in_specs=[pl.BlockSpec((tc, kw), lambda k, b, t: (b * nt + t, cb + k)),
                  pl.BlockSpec((1, kw, sw), lambda k, b, t: (k, 0, 0)),
                  pl.BlockSpec((1, sw, kw), lambda k, b, t: (k, 0, 0)),
                  pl.BlockSpec((1, nslab, SEG_LEN, LANES), lambda k, b, t: (k, 0, 0, 0)),
                  pl.BlockSpec((1, nslab, SUBLANES, LANES), lambda k, b, t: (k, 0, 0, 0)),
                  pl.BlockSpec((1, kw), lambda k, b, t: (0, k))] + cast_specs,
        out_specs=[pl.BlockSpec((tc, kw), lambda k, b, t: (b * nt + t, k)),
                   pl.BlockSpec((1, 1, nslab, 1, LANES), lambda k, b, t: (b, k, 0, 0, 0))] + cast_specs,
        out_shape=[jax.ShapeDtypeStruct((n, nkb * kw), BF16),
                   jax.ShapeDtypeStruct((nb, nkb, nslab, 1, LANES), F32)]
                  + [jax.ShapeDtypeStruct(w.shape, BF16) for w in side_casts],
        scratch_shapes=[pltpu.VMEM((nslab, SUBLANES * SEG_PITCH, LANES), F32),
                        pltpu.VMEM((nslab, SUBLANES * SEG_PITCH, LANES), F32), pltpu.VMEM((tc, sw), BF16),
                        pltpu.VMEM((nslab, SUBLANES, LANES), F32), pltpu.VMEM((nslab, 1, LANES), F32)],
        compiler_params=_params(("arbitrary", "arbitrary", "arbitrary"), nbytes),
        name="ssm_seq",
    )(z, b_cat, c_cat, pw, a_tile, d_skip.reshape(1, nkb * kw), *side_casts)
    st = st.reshape(nb, nkb, 2, sw // 2)
    return gy, st[:, :, 0].reshape(nb, -1), st[:, :, 1].reshape(nb, -1), cast


def _ssm_step_kernel(u_ref, h0r_ref, h0i_ref, bcat_ref, ccat_ref, a_ref, d_ref,
                     gy_ref, h1r_ref, h1i_ref, h_ref):
    half = h0r_ref.shape[1]
    u = u_ref[...]
    bu = _dot(u.astype(BF16), bcat_ref[0])
    hr, hi = _cmul_add(bu[:, 0:half], bu[:, half:2 * half],
                       a_ref[0, :, 0:half], a_ref[0, :, half:2 * half],
                       h0r_ref[...], h0i_ref[...])
    h1r_ref[...] = hr
    h1i_ref[...] = hi
    h_ref[:, 0:half] = hr.astype(BF16)
    h_ref[:, half:2 * half] = hi.astype(BF16)
    y = _dot(h_ref[...], ccat_ref[0]) + d_ref[...] * u
    gy_ref[...] = jax.nn.gelu(y).astype(BF16)


def _ssm_step(z, col0, h0_re, h0_im, b_cat, c_cat, a_row, d_skip):
    n = z.shape[0]
    nkb, kw, sw = b_cat.shape
    half = sw // 2
    cb = col0 // kw
    nbytes = 2 * (n * kw * 4 + 4 * n * half * 4 + 2 * kw * sw * 2 + sw * 4 + n * kw * 2) + n * sw * 6
    return pl.pallas_call(
        _ssm_step_kernel,
        grid=(nkb,),
        in_specs=[pl.BlockSpec((n, kw), lambda k: (0, cb + k)),
                  pl.BlockSpec((n, half), lambda k: (0, k)),
                  pl.BlockSpec((n, half), lambda k: (0, k)),
                  pl.BlockSpec((1, kw, sw), lambda k: (k, 0, 0)),
                  pl.BlockSpec((1, sw, kw), lambda k: (k, 0, 0)),
                  pl.BlockSpec((1, 1, sw), lambda k: (k, 0, 0)),
                  pl.BlockSpec((1, kw), lambda k: (0, k))],
        out_specs=(pl.BlockSpec((n, kw), lambda k: (0, k)),
                   pl.BlockSpec((n, half), lambda k: (0, k)),
                   pl.BlockSpec((n, half), lambda k: (0, k))),
        out_shape=(jax.ShapeDtypeStruct((n, nkb * kw), BF16),
                   jax.ShapeDtypeStruct((n, nkb * half), F32),
                   jax.ShapeDtypeStruct((n, nkb * half), F32)),
        scratch_shapes=[pltpu.VMEM((n, sw), BF16)],
        compiler_params=_params(("arbitrary",), nbytes),
        name="ssm_step",
    )(z, h0_re, h0_im, b_cat, c_cat, a_row, d_skip.reshape(1, nkb * kw))


def _mix_out_kernel(gy_ref, bp_ref, gp_ref, gs_ref, x_ref, wg_ref, wb_ref, wo_ref, o_ref):
    ds = wb_ref.shape[0]
    g = _dot(gy_ref[...], wg_ref[...])
    y = g[:, 0:ds] * jax.nn.sigmoid(g[:, ds:2 * ds])
    bs = _dot(y.astype(BF16), wb_ref[...])
    merged = jax.nn.sigmoid(gp_ref[...]) * bp_ref[...] + jax.nn.sigmoid(gs_ref[...]) * bs
    o_ref[...] = x_ref[...] + _dot(merged.astype(BF16), wo_ref[...])


def _mix_out(gy, bp, z, gate_col0, x, w_glu, w_branch, w_out, tm):
    n, ds = gy.shape
    dm = w_branch.shape[1]
    gb = gate_col0 // dm
    w_bytes = (ds * 2 * ds + ds * dm + dm * dm) * 2
    nbytes = 2 * (tm * ds * 2 + 5 * tm * dm * 4) + w_bytes + 6 * tm * dm * 4

    def resident(shape):
        return pl.BlockSpec(shape, lambda i: (0, 0), pipeline_mode=pl.Buffered(1))

    return pl.pallas_call(
        _mix_out_kernel,
        grid=(n // tm,),
        in_specs=[pl.BlockSpec((tm, ds), lambda i: (i, 0)),
                  pl.BlockSpec((tm, dm), lambda i: (i, 0)),
                  pl.BlockSpec((tm, dm), lambda i: (i, gb)),
                  pl.BlockSpec((tm, dm), lambda i: (i, gb + 1)),
                  pl.BlockSpec((tm, dm), lambda i: (i, 0)),
                  resident((ds, 2 * ds)), resident((ds, dm)), resident((dm, dm))],
        out_specs=pl.BlockSpec((tm, dm), lambda i: (i, 0)),
        out_shape=jax.ShapeDtypeStruct((n, dm), F32),
        compiler_params=_params(("arbitrary",), nbytes),
        name="mix_out",
    )(gy, bp, z, z, x, w_glu, w_branch, w_out)


def _ffn_gate_down(x_ref, conv, v, wd_ref, o_ref, acc_ref):
    c = pl.program_id(1)
    part = _dot((jax.nn.gelu(conv) * v).astype(BF16), wd_ref[...])

    @pl.when(c == 0)
    def _():
        acc_ref[...] = part

    @pl.when(c > 0)
    def _():
        acc_ref[...] += part

    @pl.when(c == pl.num_programs(1) - 1)
    def _():
        o_ref[...] = x_ref[...] + acc_ref[...]


def _ffn_seq_kernel(tiles_per_seq, x_ref, g_ref, wa_ref, wv_ref, wc_ref, bc_ref, wd_ref,
                    o_ref, new_ref, h_ref, ext_ref, carry_ref, gate_ref):
    tm = x_ref.shape[0]
    tf = gate_ref.shape[1]
    i = pl.program_id(0)
    c = pl.program_id(1)

    @pl.when(jnp.logical_and(i == 0, c == 0))
    def _():
        o_ref[...] = jnp.zeros(o_ref.shape, F32)
        carry_ref[...] = jnp.zeros(carry_ref.shape, F32)

    @pl.when(c == 0)
    def _():
        h_ref[...] = _rmsnorm(x_ref[...], g_ref[...]).astype(BF16)

    seq_start = i % tiles_per_seq == 0
    nb = new_ref.shape[1]
    for q in range(tf // FFN_SLICE):
        cols = slice(q * FFN_SLICE, (q + 1) * FFN_SLICE)
        a = _dot(h_ref[...], wa_ref[:, cols])
        v = _dot(h_ref[...], wv_ref[:, cols])
        ext_ref[0:CONV_HALO, cols] = jnp.where(seq_start, 0.0, carry_ref[c, :, cols])
        ext_ref[CONV_HALO:CONV_HALO + tm, cols] = a
        conv = bc_ref[:, cols] + wc_ref[CONV_W - 1:CONV_W, cols] * a
        for j in range(CONV_W - 1):
            off = CONV_HALO - (CONV_W - 1) + j
            conv = conv + wc_ref[j:j + 1, cols] * ext_ref[off:off + tm, cols]
        carry_ref[c, :, cols] = ext_ref[tm:tm + CONV_HALO, cols]
        new_ref[0, :, cols] = ext_ref[CONV_HALO + tm - nb:CONV_HALO + tm, cols]
        gate_ref[:, cols] = (jax.nn.gelu(conv) * v).astype(BF16)
    o_ref[...] = _dot(gate_ref[...], wd_ref[...]) + jnp.where(c == 0, x_ref[...], o_ref[...])


def _ffn_seq(x, g, w_up, w_conv, b_conv, w_down, nb, t_len, tm, tf, conv_buf):
    n, d = x.shape
    dff = w_down.shape[0]
    nc = dff // tf
    tps = t_len // tm
    nbytes = (4 * tm * d * 4 + 4 * d * tf * 2 + 2 * tf * d * 2 + tm * d * 2 + tm * tf * 2
              + (tm + CONV_HALO) * tf * 4 + nc * CONV_HALO * tf * 4 + 6 * tm * FFN_SLICE * 4)
    out, new_tail = pl.pallas_call(
        functools.partial(_ffn_seq_kernel, tps),
        grid=(n // tm, nc),
        in_specs=[pl.BlockSpec((tm, d), lambda i, c: (i, 0)),
                  pl.BlockSpec((1, d), lambda i, c: (0, 0)),
                  pl.BlockSpec((d, tf), lambda i, c: (0, c)),
                  pl.BlockSpec((d, tf), lambda i, c: (0, nc + c)),
                  pl.BlockSpec((CONV_W, tf), lambda i, c: (0, c)),
                  pl.BlockSpec((1, tf), lambda i, c: (0, c)),
                  pl.BlockSpec((tf, d), lambda i, c: (c, 0))],
        out_specs=(pl.BlockSpec((tm, d), lambda i, c: (i, 0)),
                   pl.BlockSpec((1, conv_buf, tf), lambda i, c: (i, 0, c))),
        out_shape=(jax.ShapeDtypeStruct((n, d), F32),
                   jax.ShapeDtypeStruct((n // tm, conv_buf, dff), F32)),
        scratch_shapes=[pltpu.VMEM((tm, d), BF16), pltpu.VMEM((tm + CONV_HALO, tf), F32),
                        pltpu.VMEM((nc, CONV_HALO, tf), F32), pltpu.VMEM((tm, tf), BF16)],
        compiler_params=_params(("arbitrary", "arbitrary"), nbytes),
        name="ffn_seq",
    )(x, g.reshape(1, d), w_up, w_up, w_conv, b_conv.reshape(1, dff), w_down)
    return out, new_tail[tps - 1::tps]


def _ffn_step_kernel(x_ref, g_ref, wa_ref, wv_ref, wc_ref, bc_ref, wd_ref, p0_ref, p1_ref,
                     o_ref, a_ref, h_ref, acc_ref):
    @pl.when(pl.program_id(1) == 0)
    def _():
        h_ref[...] = _rmsnorm(x_ref[...], g_ref[...]).astype(BF16)

    a = _dot(h_ref[...], wa_ref[...])
    v = _dot(h_ref[...], wv_ref[...])
    conv = bc_ref[...] + wc_ref[2:3, :] * a + wc_ref[1:2, :] * p1_ref[...] + wc_ref[0:1, :] * p0_ref[...]
    a_ref[...] = a
    _ffn_gate_down(x_ref, conv, v, wd_ref, o_ref, acc_ref)


def _ffn_step(x, g, w_up, w_conv, b_conv, w_down, cache, tf):
    n, d = x.shape
    dff = w_down.shape[0]
    nc = dff // tf
    nbytes = 4 * n * d * 4 + 4 * d * tf * 2 + 2 * tf * d * 2 + n * d * 6 + 12 * n * tf * 4
    return pl.pallas_call(
        _ffn_step_kernel,
        grid=(1, nc),
        in_specs=[pl.BlockSpec((n, d), lambda i, c: (0, 0)),
                  pl.BlockSpec((1, d), lambda i, c: (0, 0)),
                  pl.BlockSpec((d, tf), lambda i, c: (0, c)),
                  pl.BlockSpec((d, tf), lambda i, c: (0, nc + c)),
                  pl.BlockSpec((CONV_W, tf), lambda i, c: (0, c)),
                  pl.BlockSpec((1, tf), lambda i, c: (0, c)),
                  pl.BlockSpec((tf, d), lambda i, c: (c, 0)),
                  pl.BlockSpec((n, tf), lambda i, c: (0, c)),
                  pl.BlockSpec((n, tf), lambda i, c: (0, nc + c))],
        out_specs=(pl.BlockSpec((n, d), lambda i, c: (0, 0)),
                   pl.BlockSpec((n, tf), lambda i, c: (0, c))),
        out_shape=(jax.ShapeDtypeStruct((n, d), F32),
                   jax.ShapeDtypeStruct((n, dff), F32)),
        scratch_shapes=[pltpu.VMEM((n, d), BF16), pltpu.VMEM((n, d), F32)],
        compiler_params=_params(("arbitrary", "arbitrary"), nbytes),
        name="ffn_step",
    )(x, g.reshape(1, d), w_up, w_up, w_conv, b_conv.reshape(1, dff), w_down, cache, cache)


def _ple_final_kernel(x_ref, p_ref, gp_ref, wg_ref, wp_ref, gf_ref, o_ref):
    x = x_ref[...]
    gate = jax.nn.sigmoid(_dot(_rmsnorm(x, gp_ref[...]).astype(BF16), wg_ref[...]))
    x = x + gate * _dot(p_ref[...].astype(BF16), wp_ref[...])
    o_ref[...] = _rmsnorm(x, gf_ref[...])


def _ple_final(x, p, g_ple, w_gate, w_ple, g_final, tm):
    n, d = x.shape
    dp = p.shape[1]
    nbytes = 2 * (2 * tm * d * 4 + tm * dp * 4 + d * d * 2 + dp * d * 2) + 4 * tm * d * 4
    return pl.pallas_call(
        _ple_final_kernel,
        grid=(n // tm,),
        in_specs=[pl.BlockSpec((tm, d), lambda i: (i, 0)),
                  pl.BlockSpec((tm, dp), lambda i: (i, 0)),
                  pl.BlockSpec((1, d), lambda i: (0, 0)),
                  pl.BlockSpec((d, d), lambda i: (0, 0)),
                  pl.BlockSpec((dp, d), lambda i: (0, 0)),
                  pl.BlockSpec((1, d), lambda i: (0, 0))],
        out_specs=pl.BlockSpec((tm, d), lambda i: (i, 0)),
        out_shape=jax.ShapeDtypeStruct((n, d), F32),
        compiler_params=_params(("arbitrary",), nbytes),
        name="ple_final",
    )(x, p, g_ple.reshape(1, d), w_gate, w_ple, g_final.reshape(1, d))


def kernel(x_prompt, x_sample, cache_pool, state_ssm_re, state_ssm_im, cache_conv, p_prompt, p_sample, g_mix, w_in, w_pool, pool_scale, ssm_lam_re, ssm_lam_im, ssm_log_dt, ssm_b_re, ssm_b_im, ssm_c_re, ssm_c_im, ssm_d, w_glu, w_branch_pool, w_branch_ssm, w_out, g_ffn, w_up, w_conv, b_conv, w_down, g_ple, w_ple_gate, w_ple, g_final):
    depth = g_mix.shape[0]
    nb, t_len, d = x_prompt.shape
    ns = x_sample.shape[0]
    assert x_sample.shape[1] == 1, "the sample group advances one step per call"
    pool_buf, d_pool = cache_pool.shape[2], cache_pool.shape[3]
    conv_buf, d_ff = cache_conv.shape[2], cache_conv.shape[3]
    n_grp, n_state = ssm_lam_re.shape[1], ssm_lam_re.shape[2]
    d_ssm = ssm_d.shape[1]
    assert pool_buf == max(POOL_WINDOWS) - 1 and conv_buf == CONV_W - 1
    assert n_state == SSM_STATE and d_ssm == n_grp * SSM_GROUP

    xp = x_prompt.reshape(nb * t_len, d)
    xs = x_sample.reshape(ns, d)
    outs = [[] for _ in range(8)]
    for i in range(depth):
        w_in_b, w_pool_b, w_ple_b = w_in[i].astype(BF16), w_pool[i].astype(BF16), w_ple[i].astype(BF16)
        b_cat, c_cat, a_pow, a_tile, a_row = _ssm_params(ssm_lam_re[i], ssm_lam_im[i], ssm_log_dt[i],
                                                         ssm_b_re[i], ssm_b_im[i], ssm_c_re[i], ssm_c_im[i])

        z = _norm_matmul(xp, g_mix[i], w_in_b, 1024, 3 * W_TILE)
        gy, st_re, st_im, (w_up_b, w_down_b, w_out_b, w_pg_b, w_glu_b, w_bs_b, w_bp_b) = _ssm_seq(
            z, d_pool, b_cat, c_cat, a_pow, a_tile, ssm_d[i], nb, t_len,
            [w_up[i], w_down[i], w_out[i], w_ple_gate[i], w_glu[i], w_branch_ssm[i], w_branch_pool[i]])

        def mix_tail(x, z, bp, gy, tm):
            return _mix_out(gy, bp, z, d_pool + d_ssm, x, w_glu_b, w_bs_b, w_out_b, tm)

        bp, pool_new = _pool_seq(z, w_pool_b, pool_scale[i], w_bp_b, nb, t_len, 1024, pool_buf)
        xp = mix_tail(xp, z, bp, gy, 256)
        xp, conv_new = _ffn_seq(xp, g_ffn[i], w_up_b, w_conv[i], b_conv[i], w_down_b, nb, t_len, 512, 2 * W_TILE,
                                conv_buf)
        xp_out = _ple_final(xp, p_prompt[i].reshape(nb * t_len, -1), g_ple[i], w_pg_b, w_ple_b, g_final, 512)
        for lst, val in zip(outs[:4], (pool_new, st_re.reshape(nb, n_grp, n_state),
                                       st_im.reshape(nb, n_grp, n_state), conv_new)):
            lst.append(val)

        z = _norm_matmul(xs, g_mix[i], w_in_b, ns, 3 * W_TILE)
        bp, pool_new = _pool_step(z, cache_pool[i].reshape(ns, pool_buf * d_pool), w_pool_b, pool_scale[i], w_bp_b)
        gy, st_re, st_im = _ssm_step(z, d_pool, state_ssm_re[i].reshape(ns, -1), state_ssm_im[i].reshape(ns, -1),
                                     b_cat, c_cat, a_row, ssm_d[i])
        xs = mix_tail(xs, z, bp, gy, ns)
        xs, a_new = _ffn_step(xs, g_ffn[i], w_up_b, w_conv[i], b_conv[i], w_down_b,
                              cache_conv[i].reshape(ns, conv_buf * d_ff), 2 * W_TILE)
        conv_new = jnp.concatenate([cache_conv[i][:, 1:], a_new[:, None, :]], axis=1)
        xs_out = _ple_final(xs, p_sample[i].reshape(ns, -1), g_ple[i], w_pg_b, w_ple_b, g_final, ns)
        for lst, val in zip(outs[4:], (pool_new.reshape(ns, pool_buf, d_pool), st_re.reshape(ns, n_grp, n_state),
                                       st_im.reshape(ns, n_grp, n_state), conv_new)):
            lst.append(val)

    assert depth == 1
    y_prompt = xp_out.reshape(nb, t_len, d)
    y_sample = xs_out.reshape(ns, 1, d)
    return (y_prompt, y_sample) + tuple(jnp.stack(o, axis=0) for o in outs)
```

```python
import functools

import jax
import jax.numpy as jnp
from jax import lax
from jax.experimental import pallas as pl
from jax.experimental.pallas import tpu as pltpu

F32 = jnp.float32
BF16 = jnp.bfloat16

EPS = 1e-6
POOL_WINDOWS = (2, 4, 8, 16)
POOL_HALO = 16
SSM_GROUP = 16
SSM_STATE = 64
SSM_BLOCK_GROUPS = 16
LANES = 128
SUBLANES = 8
SEG_LEN = 64
SEG_PITCH = SEG_LEN + 8
CONV_W = 3
CONV_HALO = 8
FFN_SLICE = 256
W_TILE = 512
V7X_VMEM_BYTES = 64 * 1024 * 1024


def _vmem_limit(nbytes):
    return int(min(nbytes * 1.25 + (8 << 20), V7X_VMEM_BYTES - (6 << 20)))


def _params(sem, nbytes):
    return pltpu.CompilerParams(dimension_semantics=sem, vmem_limit_bytes=_vmem_limit(nbytes))


def _rmsnorm(x, g):
    return x * lax.rsqrt(jnp.mean(x * x, axis=-1, keepdims=True) + EPS) * g


def _dot(a, b):
    return jnp.dot(a, b, preferred_element_type=F32)


def _ssm_params_kernel(lr_ref, li_ref, logdt_ref, br_ref, bi_ref, pwr_ref, pwi_ref, bbr_ref, bbi_ref):
    lr = lr_ref[...]
    li = li_ref[...]
    dt = jnp.exp(logdt_ref[...])
    mag = jnp.exp(lr * dt)
    a_re = mag * jnp.cos(li * dt)
    a_im = mag * jnp.sin(li * dt)
    nr = a_re - 1.0
    ni = a_im
    den = lr * lr + li * li
    coef_re = (nr * lr + ni * li) / den
    coef_im = (ni * lr - nr * li) / den
    br = br_ref[...]
    bi = bi_ref[...]
    bbr_ref[...] = coef_re[None] * br - coef_im[None] * bi
    bbi_ref[...] = coef_re[None] * bi + coef_im[None] * br
    pr, pi = a_re, a_im
    pwr_ref[0] = pr
    pwi_ref[0] = pi
    for n in range(1, SEG_LEN):
        pr, pi = pr * a_re - pi * a_im, pr * a_im + pi * a_re
        pwr_ref[n] = pr
        pwi_ref[n] = pi


def _ssm_params(lam_re, lam_im, log_dt, b_re, b_im, c_re, c_im):
    g, p = lam_re.shape
    h = b_re.shape[-1]
    nkb = g // SSM_BLOCK_GROUPS
    bl = SSM_BLOCK_GROUPS
    pwr, pwi, bbr, bbi = pl.pallas_call(
        _ssm_params_kernel,
        out_shape=(jax.ShapeDtypeStruct((SEG_LEN, g, p), F32),
                   jax.ShapeDtypeStruct((SEG_LEN, g, p), F32),
                   jax.ShapeDtypeStruct((h, g, p), F32),
                   jax.ShapeDtypeStruct((h, g, p), F32)),
        name="ssm_params",
    )(lam_re, lam_im, log_dt.reshape(g, 1), jnp.transpose(b_re, (2, 0, 1)), jnp.transpose(b_im, (2, 0, 1)))

    eye = jnp.eye(bl, dtype=F32)
    bb = jnp.stack([bbr, bbi]).reshape(2, h, nkb, bl, p)
    b_cat = jnp.einsum('shkgp,gj->kghsjp', bb, eye).reshape(nkb, bl * h, 2 * bl * p).astype(BF16)
    cc = jnp.stack([c_re, -c_im]).reshape(2, nkb, bl, h, p)
    c_cat = jnp.einsum('skghp,gj->ksgpjh', cc, eye).reshape(nkb, 2 * bl * p, bl * h).astype(BF16)
    nslab = 2 * bl * p // LANES
    pw = jnp.stack([pwr, pwi], axis=1).reshape(SEG_LEN, 2, nkb, bl * p)
    a_row = jnp.transpose(pw[0], (1, 0, 2)).reshape(nkb, 1, 2 * bl * p)
    pw = jnp.transpose(pw.reshape(SEG_LEN, 2, nkb, nslab // 2, LANES), (2, 1, 3, 0, 4))
    pw = pw.reshape(nkb, nslab, SEG_LEN, LANES)
    a_tile = jnp.broadcast_to(pw[:, :, 0:1, :], (nkb, nslab, SUBLANES, LANES))
    return b_cat, c_cat, pw, a_tile, a_row


def _norm_matmul_kernel(x_ref, g_ref, w_ref, o_ref, h_ref):
    @pl.when(pl.program_id(1) == 0)
    def _():
        h_ref[...] = _rmsnorm(x_ref[...], g_ref[...]).astype(BF16)
    o_ref[...] = _dot(h_ref[...], w_ref[...])


def _norm_matmul(x, g, w, tm, tn):
    n, d = x.shape
    dout = w.shape[1]
    nbytes = 2 * tm * d * 4 + tm * d * 2 + 2 * d * tn * 2 + 2 * tm * tn * 4
    return pl.pallas_call(
        _norm_matmul_kernel,
        grid=(n // tm, dout // tn),
        in_specs=[pl.BlockSpec((tm, d), lambda i, j: (i, 0)),
                  pl.BlockSpec((1, d), lambda i, j: (0, 0)),
                  pl.BlockSpec((d, tn), lambda i, j: (0, j))],
        out_specs=pl.BlockSpec((tm, tn), lambda i, j: (i, j)),
        out_shape=jax.ShapeDtypeStruct((n, dout), F32),
        scratch_shapes=[pltpu.VMEM((tm, d), BF16)],
        compiler_params=_params(("arbitrary", "arbitrary"), nbytes),
        name="norm_matmul",
    )(x, g.reshape(1, d), w)


def _pool_project(diffs, wp_ref, scale_ref, wb_ref, y_ref):
    gw = wp_ref.shape[1]
    for k, diff in enumerate(diffs):
        yk = _dot(diff.astype(BF16), wp_ref[k]) * scale_ref[:, k * gw:(k + 1) * gw]
        y_ref[:, k * gw:(k + 1) * gw] = yk.astype(BF16)
    return _dot(y_ref[...], wb_ref[...])


def _pool_seq_kernel(u_ref, wp_ref, scale_ref, wb_ref, o_ref, new_ref, ext_ref, y_ref):
    tc, dp = u_ref.shape
    gw = wp_ref.shape[1]
    t = pl.program_id(1)

    @pl.when(t == 0)
    def _():
        ext_ref[0:POOL_HALO, :] = jnp.zeros((POOL_HALO, dp), F32)

    ext_ref[POOL_HALO:POOL_HALO + tc, :] = u_ref[...]
    pos = (t * tc + 1 + lax.broadcasted_iota(jnp.int32, (tc, 1), 0)).astype(F32)
    diffs = []
    for k, w in enumerate(POOL_WINDOWS):
        cols = slice(k * gw, (k + 1) * gw)
        u = ext_ref[POOL_HALO:POOL_HALO + tc, cols]
        s = u
        for j in range(1, w):
            s = s + ext_ref[POOL_HALO - j:POOL_HALO - j + tc, cols]
        count = jnp.minimum(pos, float(w))
        diffs.append(s / count - u)
    o_ref[...] = _pool_project(diffs, wp_ref, scale_ref, wb_ref, y_ref)
    nb = new_ref.shape[1]
    new_ref[0] = ext_ref[POOL_HALO + tc - nb:POOL_HALO + tc, :]
    ext_ref[0:POOL_HALO, :] = ext_ref[tc:tc + POOL_HALO, :]


def _pool_seq(z, w_pool, pool_scale, w_branch, nb, t_len, tc, pool_buf):
    n = z.shape[0]
    ng, gw, _ = w_pool.shape
    dp = ng * gw
    dm = w_branch.shape[1]
    nt = t_len // tc
    nbytes = (2 * tc * dp * 4 + 2 * ng * gw * gw * 2 + 2 * dp * dm * 2 + 2 * tc * dm * 4
              + (tc + POOL_HALO) * dp * 4 + tc * dp * 2 + 4 * tc * gw * 4)
    return pl.pallas_call(
        _pool_seq_kernel,
        grid=(nb, nt),
        in_specs=[pl.BlockSpec((tc, dp), lambda b, t: (b * nt + t, 0)),
                  pl.BlockSpec((ng, gw, gw), lambda b, t: (0, 0, 0)),
                  pl.BlockSpec((1, dp), lambda b, t: (0, 0)),
                  pl.BlockSpec((dp, dm), lambda b, t: (0, 0))],
        out_specs=(pl.BlockSpec((tc, dm), lambda b, t: (b * nt + t, 0)),
                   pl.BlockSpec((1, pool_buf, dp), lambda b, t: (b, 0, 0))),
        out_shape=(jax.ShapeDtypeStruct((n, dm), F32),
                   jax.ShapeDtypeStruct((nb, pool_buf, dp), F32)),
        scratch_shapes=[pltpu.VMEM((tc + POOL_HALO, dp), F32), pltpu.VMEM((tc, dp), BF16)],
        compiler_params=_params(("arbitrary", "arbitrary"), nbytes),
        name="pool_seq",
    )(z, w_pool, pool_scale.reshape(1, dp), w_branch)


def _pool_step_kernel(u_ref, cache_ref, wp_ref, scale_ref, wb_ref, o_ref, new_ref, y_ref):
    dp = u_ref.shape[1]
    gw = wp_ref.shape[1]
    lb = cache_ref.shape[1] // dp
    diffs = []
    for k, w in enumerate(POOL_WINDOWS):
        u = u_ref[:, k * gw:(k + 1) * gw]
        s = u
        for j in range(1, w):
            s = s + cache_ref[:, (lb - j) * dp + k * gw:(lb - j) * dp + (k + 1) * gw]
        diffs.append(s / float(w) - u)
    o_ref[...] = _pool_project(diffs, wp_ref, scale_ref, wb_ref, y_ref)
    new_ref[:, 0:(lb - 1) * dp] = cache_ref[:, dp:lb * dp]
    new_ref[:, (lb - 1) * dp:lb * dp] = u_ref[...]


def _pool_step(z, cache, w_pool, pool_scale, w_branch):
    n = z.shape[0]
    ng, gw, _ = w_pool.shape
    dp = ng * gw
    dm = w_branch.shape[1]
    lbdp = cache.shape[1]
    nbytes = 2 * (n * dp * 4 + 2 * n * lbdp * 4 + ng * gw * gw * 2 + dp * dm * 2 + n * dm * 4) + n * dp * 2
    return pl.pallas_call(
        _pool_step_kernel,
        grid=(1,),
        in_specs=[pl.BlockSpec((n, dp), lambda i: (0, 0)),
                  pl.BlockSpec((n, lbdp), lambda i: (0, 0)),
                  pl.BlockSpec((ng, gw, gw), lambda i: (0, 0, 0)),
                  pl.BlockSpec((1, dp), lambda i: (0, 0)),
                  pl.BlockSpec((dp, dm), lambda i: (0, 0))],
        out_specs=(pl.BlockSpec((n, dm), lambda i: (0, 0)),
                   pl.BlockSpec((n, lbdp), lambda i: (0, 0))),
        out_shape=(jax.ShapeDtypeStruct((n, dm), F32),
                   jax.ShapeDtypeStruct((n, lbdp), F32)),
        scratch_shapes=[pltpu.VMEM((n, dp), BF16)],
        compiler_params=_params(("arbitrary",), nbytes),
        name="pool_step",
    )(z, cache, w_pool, pool_scale.reshape(1, dp), w_branch)


def _cmul_add(xr, xi, ar, ai, br, bi):
    return xr + ar * br - ai * bi, xi + ar * bi + ai * br


def _ssm_seq_kernel(n_cast, u_ref, bcat_ref, ccat_ref, pw_ref, at_ref, d_ref, *refs):
    cast_in, (gy_ref, st_ref), refs = refs[:n_cast], refs[n_cast:n_cast + 2], refs[n_cast + 2:]
    cast_out, (h_ref, loc_ref, hb_ref, seed_ref, carry_ref) = refs[:n_cast], refs[n_cast:]
    for src_ref, dst_ref in zip(cast_in, cast_out):
        dst_ref[...] = src_ref[...].astype(BF16)

    nslab = h_ref.shape[0]
    npair = nslab // 2
    t = pl.program_id(2)

    @pl.when(t == 0)
    def _():
        carry_ref[...] = jnp.zeros_like(carry_ref)

    u = u_ref[...]
    bu = _dot(u.astype(BF16), bcat_ref[0])
    for j in range(nslab):
        for r in range(SUBLANES):
            h_ref[j, r * SEG_PITCH:r * SEG_PITCH + SEG_LEN, :] = (
                bu[r * SEG_LEN:(r + 1) * SEG_LEN, j * LANES:(j + 1) * LANES])

    def seg_step(n, state):
        rows = pl.ds(n, SUBLANES, stride=SEG_PITCH)
        new = []
        for j in range(npair):
            sr, si = _cmul_add(h_ref[j, rows, :], h_ref[npair + j, rows, :],
                               at_ref[0, j], at_ref[0, npair + j], state[2 * j], state[2 * j + 1])
            loc_ref[j, rows, :] = sr
            loc_ref[npair + j, rows, :] = si
            new += [sr, si]
        return tuple(new)

    ends = (jnp.zeros((SUBLANES, LANES), F32),) * nslab
    for n in range(SEG_LEN):
        ends = seg_step(n, ends)

    last = slice(SEG_LEN - 1, SEG_LEN)
    for j in range(npair):
        cr, ci = carry_ref[j], carry_ref[npair + j]
        for r in range(SUBLANES):
            seed_ref[j, r:r + 1, :] = cr
            seed_ref[npair + j, r:r + 1, :] = ci
            cr, ci = _cmul_add(ends[2 * j][r:r + 1, :], ends[2 * j + 1][r:r + 1, :],
                               pw_ref[0, j, last, :], pw_ref[0, npair + j, last, :], cr, ci)
        carry_ref[j] = cr
        carry_ref[npair + j] = ci
    st_ref[0, 0] = carry_ref[...]

    rows_bf16 = 2 * SUBLANES
    for r in range(SUBLANES):
        for i in range(SEG_LEN // rows_bf16):
            src = slice(r * SEG_PITCH + i * rows_bf16, r * SEG_PITCH + (i + 1) * rows_bf16)
            dst = slice(r * SEG_LEN + i * rows_bf16, r * SEG_LEN + (i + 1) * rows_bf16)
            pws = slice(i * rows_bf16, (i + 1) * rows_bf16)
            for j in range(npair):
                hr, hi = _cmul_add(loc_ref[j, src, :], loc_ref[npair + j, src, :],
                                   pw_ref[0, j, pws, :], pw_ref[0, npair + j, pws, :],
                                   seed_ref[j, r:r + 1, :], seed_ref[npair + j, r:r + 1, :])
                hb_ref[dst, j * LANES:(j + 1) * LANES] = hr.astype(BF16)
                hb_ref[dst, (npair + j) * LANES:(npair + j + 1) * LANES] = hi.astype(BF16)

    y = _dot(hb_ref[...], ccat_ref[0]) + d_ref[...] * u
    gy_ref[...] = jax.nn.gelu(y).astype(BF16)


def _ssm_seq(z, col0, b_cat, c_cat, pw, a_tile, d_skip, nb, t_len, side_casts):
    n = z.shape[0]
    nkb, kw, sw = b_cat.shape
    nslab = sw // LANES
    tc = SUBLANES * SEG_LEN
    nt = t_len // tc
    cb = col0 // kw
    n_steps = nkb * nb * nt
    cast_rows = [w.shape[0] // n_steps for w in side_casts]
    assert all(r % (2 * SUBLANES) == 0 and r * n_steps == w.shape[0] for r, w in zip(cast_rows, side_casts))
    nbytes = (2 * tc * kw * 4 + 4 * kw * sw * 2 + 2 * (SEG_LEN + SUBLANES) * sw * 4 + 2 * tc * kw * 2
              + 2 * SUBLANES * SEG_PITCH * sw * 4 + tc * sw * 2 + tc * sw * 4
              + sum(2 * r * w.shape[1] * 6 for r, w in zip(cast_rows, side_casts)))

    def cast_spec(r, w):
        return pl.BlockSpec((r, w.shape[1]), lambda k, b, t: ((k * nb + b) * nt + t, 0))

    cast_specs = [cast_spec(r, w) for r, w in zip(cast_rows, side_casts)]
    gy, st, *cast = pl.pallas_call(
        functools.partial(_ssm_seq_kernel, len(side_casts)),
        grid=(nkb, nb, nt),
        in_specs=[pl.BlockSpec((tc, kw), lambda k, b, t: (b * nt + t, cb + k)),
                  pl.BlockSpec((1, kw, sw), lambda k, b, t: (k, 0, 0)),
                  pl.BlockSpec((1, sw, kw), lambda k, b, t: (k, 0, 0)),
                  pl.BlockSpec((1, nslab, SEG_LEN, LANES), lambda k, b, t: (k, 0, 0, 0)),
                  pl.BlockSpec((1, nslab, SUBLANES, LANES), lambda k, b, t: (k, 0, 0, 0)),
                  pl.BlockSpec((1, kw), lambda k, b, t: (0, k))] + cast_specs,
        out_specs=[pl.BlockSpec((tc, kw), lambda k, b, t: (b * nt + t, k)),
                   pl.BlockSpec((1, 1, nslab, 1, LANES), lambda k, b, t: (b, k, 0, 0, 0))] + cast_specs,
        out_shape=[jax.ShapeDtypeStruct((n, nkb * kw), BF16),
                   jax.ShapeDtypeStruct((nb, nkb, nslab, 1, LANES), F32)]
                  + [jax.ShapeDtypeStruct(w.shape, BF16) for w in side_casts],
        scratch_shapes=[pltpu.VMEM((nslab, SUBLANES * SEG_PITCH, LANES), F32),
                        pltpu.VMEM((nslab, SUBLANES * SEG_PITCH, LANES), F32), pltpu.VMEM((tc, sw), BF16),
                        pltpu.VMEM((nslab, SUBLANES, LANES), F32), pltpu.VMEM((nslab, 1, LANES), F32)],
        compiler_params=_params(("arbitrary", "arbitrary", "arbitrary"), nbytes),
        name="ssm_seq",
    )(z, b_cat, c_cat, pw, a_tile, d_skip.reshape(1, nkb * kw), *side_casts)
    st = st.reshape(nb, nkb, 2, sw // 2)
    return gy, st[:, :, 0].reshape(nb, -1), st[:, :, 1].reshape(nb, -1), cast


def _ssm_step_kernel(u_ref, h0r_ref, h0i_ref, bcat_ref, ccat_ref, a_ref, d_ref,
                     gy_ref, h1r_ref, h1i_ref, h_ref):
    half = h0r_ref.shape[1]
    u = u_ref[...]
    bu = _dot(u.astype(BF16), bcat_ref[0])
    hr, hi = _cmul_add(bu[:, 0:half], bu[:, half:2 * half],
                       a_ref[0, :, 0:half], a_ref[0, :, half:2 * half],
                       h0r_ref[...], h0i_ref[...])
    h1r_ref[...] = hr
    h1i_ref[...] = hi
    h_ref[:, 0:half] = hr.astype(BF16)
    h_ref[:, half:2 * half] = hi.astype(BF16)
    y = _dot(h_ref[...], ccat_ref[0]) + d_ref[...] * u
    gy_ref[...] = jax.nn.gelu(y).astype(BF16)


def _ssm_step(z, col0, h0_re, h0_im, b_cat, c_cat, a_row, d_skip):
    n = z.shape[0]
    nkb, kw, sw = b_cat.shape
    half = sw // 2
    cb = col0 // kw
    nbytes = 2 * (n * kw * 4 + 4 * n * half * 4 + 2 * kw * sw * 2 + sw * 4 + n * kw * 2) + n * sw * 6
    return pl.pallas_call(
        _ssm_step_kernel,
        grid=(nkb,),
        in_specs=[pl.BlockSpec((n, kw), lambda k: (0, cb + k)),
                  pl.BlockSpec((n, half), lambda k: (0, k)),
                  pl.BlockSpec((n, half), lambda k: (0, k)),
                  pl.BlockSpec((1, kw, sw), lambda k: (k, 0, 0)),
                  pl.BlockSpec((1, sw, kw), lambda k: (k, 0, 0)),
                  pl.BlockSpec((1, 1, sw), lambda k: (k, 0, 0)),
                  pl.BlockSpec((1, kw), lambda k: (0, k))],
        out_specs=(pl.BlockSpec((n, kw), lambda k: (0, k)),
                   pl.BlockSpec((n, half), lambda k: (0, k)),
                   pl.BlockSpec((n, half), lambda k: (0, k))),
        out_shape=(jax.ShapeDtypeStruct((n, nkb * kw), BF16),
                   jax.ShapeDtypeStruct((n, nkb * half), F32),
                   jax.ShapeDtypeStruct((n, nkb * half), F32)),
        scratch_shapes=[pltpu.VMEM((n, sw), BF16)],
        compiler_params=_params(("arbitrary",), nbytes),
        name="ssm_step",
    )(z, h0_re, h0_im, b_cat, c_cat, a_row, d_skip.reshape(1, nkb * kw))


def _mix_out_kernel(gy_ref, bp_ref, gp_ref, gs_ref, x_ref, wg_ref, wb_ref, wo_ref, o_ref):
    ds = wb_ref.shape[0]
    g = _dot(gy_ref[...], wg_ref[...])
    y = g[:, 0:ds] * jax.nn.sigmoid(g[:, ds:2 * ds])
    bs = _dot(y.astype(BF16), wb_ref[...])
    merged = jax.nn.sigmoid(gp_ref[...]) * bp_ref[...] + jax.nn.sigmoid(gs_ref[...]) * bs
    o_ref[...] = x_ref[...] + _dot(merged.astype(BF16), wo_ref[...])


def _mix_out(gy, bp, z, gate_col0, x, w_glu, w_branch, w_out, tm):
    n, ds = gy.shape
    dm = w_branch.shape[1]
    gb = gate_col0 // dm
    w_bytes = (ds * 2 * ds + ds * dm + dm * dm) * 2
    nbytes = 2 * (tm * ds * 2 + 5 * tm * dm * 4) + w_bytes + 6 * tm * dm * 4

    def resident(shape):
        return pl.BlockSpec(shape, lambda i: (0, 0), pipeline_mode=pl.Buffered(1))

    return pl.pallas_call(
        _mix_out_kernel,
        grid=(n // tm,),
        in_specs=[pl.BlockSpec((tm, ds), lambda i: (i, 0)),
                  pl.BlockSpec((tm, dm), lambda i: (i, 0)),
                  pl.BlockSpec((tm, dm), lambda i: (i, gb)),
                  pl.BlockSpec((tm, dm), lambda i: (i, gb + 1)),
                  pl.BlockSpec((tm, dm), lambda i: (i, 0)),
                  resident((ds, 2 * ds)), resident((ds, dm)), resident((dm, dm))],
        out_specs=pl.BlockSpec((tm, dm), lambda i: (i, 0)),
        out_shape=jax.ShapeDtypeStruct((n, dm), F32),
        compiler_params=_params(("arbitrary",), nbytes),
        name="mix_out",
    )(gy, bp, z, z, x, w_glu, w_branch, w_out)


def _ffn_gate_down(x_ref, conv, v, wd_ref, o_ref, acc_ref):
    c = pl.program_id(1)
    part = _dot((jax.nn.gelu(conv) * v).astype(BF16), wd_ref[...])

    @pl.when(c == 0)
    def _():
        acc_ref[...] = part

    @pl.when(c > 0)
    def _():
        acc_ref[...] += part

    @pl.when(c == pl.num_programs(1) - 1)
    def _():
        o_ref[...] = x_ref[...] + acc_ref[...]


def _ffn_seq_kernel(tiles_per_seq, x_ref, g_ref, wa_ref, wv_ref, wc_ref, bc_ref, wd_ref,
                    o_ref, new_ref, h_ref, ext_ref, carry_ref, gate_ref):
    tm = x_ref.shape[0]
    tf = gate_ref.shape[1]
    i = pl.program_id(0)
    c = pl.program_id(1)

    @pl.when(jnp.logical_and(i == 0, c == 0))
    def _():
        o_ref[...] = jnp.zeros(o_ref.shape, F32)
        carry_ref[...] = jnp.zeros(carry_ref.shape, F32)

    @pl.when(c == 0)
    def _():
        h_ref[...] = _rmsnorm(x_ref[...], g_ref[...]).astype(BF16)

    seq_start = i % tiles_per_seq == 0
    nb = new_ref.shape[1]
    for q in range(tf // FFN_SLICE):
        cols = slice(q * FFN_SLICE, (q + 1) * FFN_SLICE)
        a = _dot(h_ref[...], wa_ref[:, cols])
        v = _dot(h_ref[...], wv_ref[:, cols])
        ext_ref[0:CONV_HALO, cols] = jnp.where(seq_start, 0.0, carry_ref[c, :, cols])
        ext_ref[CONV_HALO:CONV_HALO + tm, cols] = a
        conv = bc_ref[:, cols] + wc_ref[CONV_W - 1:CONV_W, cols] * a
        for j in range(CONV_W - 1):
            off = CONV_HALO - (CONV_W - 1) + j
            conv = conv + wc_ref[j:j + 1, cols] * ext_ref[off:off + tm, cols]
        carry_ref[c, :, cols] = ext_ref[tm:tm + CONV_HALO, cols]
        new_ref[0, :, cols] = ext_ref[CONV_HALO + tm - nb:CONV_HALO + tm, cols]
        gate_ref[:, cols] = (jax.nn.gelu(conv) * v).astype(BF16)
    o_ref[...] = _dot(gate_ref[...], wd_ref[...]) + jnp.where(c == 0, x_ref[...], o_ref[...])


def _ffn_seq(x, g, w_up, w_conv, b_conv, w_down, nb, t_len, tm, tf, conv_buf):
    n, d = x.shape
    dff = w_down.shape[0]
    nc = dff // tf
    tps = t_len // tm
    nbytes = (4 * tm * d * 4 + 4 * d * tf * 2 + 2 * tf * d * 2 + tm * d * 2 + tm * tf * 2
              + (tm + CONV_HALO) * tf * 4 + nc * CONV_HALO * tf * 4 + 6 * tm * FFN_SLICE * 4)
    out, new_tail = pl.pallas_call(
        functools.partial(_ffn_seq_kernel, tps),
        grid=(n // tm, nc),
        in_specs=[pl.BlockSpec((tm, d), lambda i, c: (i, 0)),
                  pl.BlockSpec((1, d), lambda i, c: (0, 0)),
                  pl.BlockSpec((d, tf), lambda i, c: (0, c)),
                  pl.BlockSpec((d, tf), lambda i, c: (0, nc + c)),
                  pl.BlockSpec((CONV_W, tf), lambda i, c: (0, c)),
                  pl.BlockSpec((1, tf), lambda i, c: (0, c)),
                  pl.BlockSpec((tf, d), lambda i, c: (c, 0))],
        out_specs=(pl.BlockSpec((tm, d), lambda i, c: (i, 0)),
                   pl.BlockSpec((1, conv_buf, tf), lambda i, c: (i, 0, c))),
        out_shape=(jax.ShapeDtypeStruct((n, d), F32),
                   jax.ShapeDtypeStruct((n // tm, conv_buf, dff), F32)),
        scratch_shapes=[pltpu.VMEM((tm, d), BF16), pltpu.VMEM((tm + CONV_HALO, tf), F32),
                        pltpu.VMEM((nc, CONV_HALO, tf), F32), pltpu.VMEM((tm, tf), BF16)],
        compiler_params=_params(("arbitrary", "arbitrary"), nbytes),
        name="ffn_seq",
    )(x, g.reshape(1, d), w_up, w_up, w_conv, b_conv.reshape(1, dff), w_down)
    return out, new_tail[tps - 1::tps]


def _ffn_step_kernel(x_ref, g_ref, wa_ref, wv_ref, wc_ref, bc_ref, wd_ref, p0_ref, p1_ref,
                     o_ref, a_ref, h_ref, acc_ref):
    @pl.when(pl.program_id(1) == 0)
    def _():
        h_ref[...] = _rmsnorm(x_ref[...], g_ref[...]).astype(BF16)

    a = _dot(h_ref[...], wa_ref[...])
    v = _dot(h_ref[...], wv_ref[...])
    conv = bc_ref[...] + wc_ref[2:3, :] * a + wc_ref[1:2, :] * p1_ref[...] + wc_ref[0:1, :] * p0_ref[...]
    a_ref[...] = a
    _ffn_gate_down(x_ref, conv, v, wd_ref, o_ref, acc_ref)


def _ffn_step(x, g, w_up, w_conv, b_conv, w_down, cache, tf):
    n, d = x.shape
    dff = w_down.shape[0]
    nc = dff // tf
    nbytes = 4 * n * d * 4 + 4 * d * tf * 2 + 2 * tf * d * 2 + n * d * 6 + 12 * n * tf * 4
    return pl.pallas_call(
        _ffn_step_kernel,
        grid=(1, nc),
        in_specs=[pl.BlockSpec((n, d), lambda i, c: (0, 0)),
                  pl.BlockSpec((1, d), lambda i, c: (0, 0)),
                  pl.BlockSpec((d, tf), lambda i, c: (0, c)),
                  pl.BlockSpec((d, tf), lambda i, c: (0, nc + c)),
                  pl.BlockSpec((CONV_W, tf), lambda i, c: (0, c)),
                  pl.BlockSpec((1, tf), lambda i, c: (0, c)),
                  pl.BlockSpec((tf, d), lambda i, c: (c, 0)),
                  pl.BlockSpec((n, tf), lambda i, c: (0, c)),
                  pl.BlockSpec((n, tf), lambda i, c: (0, nc + c))],
        out_specs=(pl.BlockSpec((n, d), lambda i, c: (0, 0)),
                   pl.BlockSpec((n, tf), lambda i, c: (0, c))),
        out_shape=(jax.ShapeDtypeStruct((n, d), F32),
                   jax.ShapeDtypeStruct((n, dff), F32)),
        scratch_shapes=[pltpu.VMEM((n, d), BF16), pltpu.VMEM((n, d), F32)],
        compiler_params=_params(("arbitrary", "arbitrary"), nbytes),
        name="ffn_step",
    )(x, g.reshape(1, d), w_up, w_up, w_conv, b_conv.reshape(1, dff), w_down, cache, cache)


def _ple_final_kernel(x_ref, p_ref, gp_ref, wg_ref, wp_ref, gf_ref, o_ref):
    x = x_ref[...]
    gate = jax.nn.sigmoid(_dot(_rmsnorm(x, gp_ref[...]).astype(BF16), wg_ref[...]))
    x = x + gate * _dot(p_ref[...].astype(BF16), wp_ref[...])
    o_ref[...] = _rmsnorm(x, gf_ref[...])


def _ple_final(x, p, g_ple, w_gate, w_ple, g_final, tm):
    n, d = x.shape
    dp = p.shape[1]
    nbytes = 2 * (2 * tm * d * 4 + tm * dp * 4 + d * d * 2 + dp * d * 2) + 4 * tm * d * 4
    return pl.pallas_call(
        _ple_final_kernel,
        grid=(n // tm,),
        in_specs=[pl.BlockSpec((tm, d), lambda i: (i, 0)),
                  pl.BlockSpec((tm, dp), lambda i: (i, 0)),
                  pl.BlockSpec((1, d), lambda i: (0, 0)),
                  pl.BlockSpec((d, d), lambda i: (0, 0)),
                  pl.BlockSpec((dp, d), lambda i: (0, 0)),
                  pl.BlockSpec((1, d), lambda i: (0, 0))],
        out_specs=pl.BlockSpec((tm, d), lambda i: (i, 0)),
        out_shape=jax.ShapeDtypeStruct((n, d), F32),
        compiler_params=_params(("arbitrary",), nbytes),
        name="ple_final",
    )(x, p, g_ple.reshape(1, d), w_gate, w_ple, g_final.reshape(1, d))


def kernel(x_prompt, x_sample, cache_pool, state_ssm_re, state_ssm_im, cache_conv, p_prompt, p_sample, g_mix, w_in, w_pool, pool_scale, ssm_lam_re, ssm_lam_im, ssm_log_dt, ssm_b_re, ssm_b_im, ssm_c_re, ssm_c_im, ssm_d, w_glu, w_branch_pool, w_branch_ssm, w_out, g_ffn, w_up, w_conv, b_conv, w_down, g_ple, w_ple_gate, w_ple, g_final):
    depth = g_mix.shape[0]
    nb, t_len, d = x_prompt.shape
    ns = x_sample.shape[0]
    assert x_sample.shape[1] == 1, "the sample group advances one step per call"
    pool_buf, d_pool = cache_pool.shape[2], cache_pool.shape[3]
    conv_buf, d_ff = cache_conv.shape[2], cache_conv.shape[3]
    n_grp, n_state = ssm_lam_re.shape[1], ssm_lam_re.shape[2]
    d_ssm = ssm_d.shape[1]
    assert pool_buf == max(POOL_WINDOWS) - 1 and conv_buf == CONV_W - 1
    assert n_state == SSM_STATE and d_ssm == n_grp * SSM_GROUP

    xp = x_prompt.reshape(nb * t_len, d)
    xs = x_sample.reshape(ns, d)
    outs = [[] for _ in range(8)]
    for i in range(depth):
        w_in_b, w_pool_b, w_ple_b = w_in[i].astype(BF16), w_pool[i].astype(BF16), w_ple[i].astype(BF16)
        b_cat, c_cat, a_pow, a_tile, a_row = _ssm_params(ssm_lam_re[i], ssm_lam_im[i], ssm_log_dt[i],
                                                         ssm_b_re[i], ssm_b_im[i], ssm_c_re[i], ssm_c_im[i])

        z = _norm_matmul(xp, g_mix[i], w_in_b, 1024, 3 * W_TILE)
        gy, st_re, st_im, (w_up_b, w_down_b, w_out_b, w_pg_b, w_glu_b, w_bs_b, w_bp_b) = _ssm_seq(
            z, d_pool, b_cat, c_cat, a_pow, a_tile, ssm_d[i], nb, t_len,
            [w_up[i], w_down[i], w_out[i], w_ple_gate[i], w_glu[i], w_branch_ssm[i], w_branch_pool[i]])

        def mix_tail(x, z, bp, gy, tm):
            return _mix_out(gy, bp, z, d_pool + d_ssm, x, w_glu_b, w_bs_b, w_out_b, tm)

        bp, pool_new = _pool_seq(z, w_pool_b, pool_scale[i], w_bp_b, nb, t_len, 1024, pool_buf)
        xp = mix_tail(xp, z, bp, gy, 256)
        xp, conv_new = _ffn_seq(xp, g_ffn[i], w_up_b, w_conv[i], b_conv[i], w_down_b, nb, t_len, 1024, W_TILE,
                                conv_buf)
        xp_out = _ple_final(xp, p_prompt[i].reshape(nb * t_len, -1), g_ple[i], w_pg_b, w_ple_b, g_final, 512)
        for lst, val in zip(outs[:4], (pool_new, st_re.reshape(nb, n_grp, n_state),
                                       st_im.reshape(nb, n_grp, n_state), conv_new)):
            lst.append(val)

        z = _norm_matmul(xs, g_mix[i], w_in_b, ns, 3 * W_TILE)
        bp, pool_new = _pool_step(z, cache_pool[i].reshape(ns, pool_buf * d_pool), w_pool_b, pool_scale[i], w_bp_b)
        gy, st_re, st_im = _ssm_step(z, d_pool, state_ssm_re[i].reshape(ns, -1), state_ssm_im[i].reshape(ns, -1),
                                     b_cat, c_cat, a_row, ssm_d[i])
        xs = mix_tail(xs, z, bp, gy, ns)
        xs, a_new = _ffn_step(xs, g_ffn[i], w_up_b, w_conv[i], b_conv[i], w_down_b,
                              cache_conv[i].reshape(ns, conv_buf * d_ff), 2 * W_TILE)
        conv_new = jnp.concatenate([cache_conv[i][:, 1:], a_new[:, None, :]], axis=1)
        xs_out = _ple_final(xs, p_sample[i].reshape(ns, -1), g_ple[i], w_pg_b, w_ple_b, g_final, ns)
        for lst, val in zip(outs[4:], (pool_new.reshape(ns, pool_buf, d_pool), st_re.reshape(ns, n_grp, n_state),
                                       st_im.reshape(ns, n_grp, n_state), conv_new)):
            lst.append(val)

    assert depth == 1
    y_prompt = xp_out.reshape(nb, t_len, d)
    y_sample = xs_out.reshape(ns, 1, d)
    return (y_prompt, y_sample) + tuple(jnp.stack(o, axis=0) for o in outs)
```

```python
import functools

import jax
import jax.numpy as jnp
from jax import lax
from jax.experimental import pallas as pl
from jax.experimental.pallas import tpu as pltpu

F32 = jnp.float32
BF16 = jnp.bfloat16

EPS = 1e-6
POOL_WINDOWS = (2, 4, 8, 16)
POOL_HALO = 16
SSM_GROUP = 16
SSM_STATE = 64
SSM_BLOCK_GROUPS = 16
LANES = 128
SUBLANES = 8
SEG_LEN = 128
SEG_PITCH = SEG_LEN + 8
CONV_W = 3
CONV_HALO = 8
FFN_SLICE = 256
W_TILE = 512
V7X_VMEM_BYTES = 64 * 1024 * 1024


def _vmem_limit(nbytes):
    return int(min(nbytes * 1.25 + (8 << 20), V7X_VMEM_BYTES - (6 << 20)))


def _params(sem, nbytes):
    return pltpu.CompilerParams(dimension_semantics=sem, vmem_limit_bytes=_vmem_limit(nbytes))


def _rmsnorm(x, g):
    return x * lax.rsqrt(jnp.mean(x * x, axis=-1, keepdims=True) + EPS) * g


def _dot(a, b):
    return jnp.dot(a, b, preferred_element_type=F32)


def _ssm_params_kernel(lr_ref, li_ref, logdt_ref, br_ref, bi_ref, pwr_ref, pwi_ref, bbr_ref, bbi_ref):
    lr = lr_ref[...]
    li = li_ref[...]
    dt = jnp.exp(logdt_ref[...])
    mag = jnp.exp(lr * dt)
    a_re = mag * jnp.cos(li * dt)
    a_im = mag * jnp.sin(li * dt)
    nr = a_re - 1.0
    ni = a_im
    den = lr * lr + li * li
    coef_re = (nr * lr + ni * li) / den
    coef_im = (ni * lr - nr * li) / den
    br = br_ref[...]
    bi = bi_ref[...]
    bbr_ref[...] = coef_re[None] * br - coef_im[None] * bi
    bbi_ref[...] = coef_re[None] * bi + coef_im[None] * br
    pr, pi = a_re, a_im
    pwr_ref[0] = pr
    pwi_ref[0] = pi
    for n in range(1, SEG_LEN):
        pr, pi = pr * a_re - pi * a_im, pr * a_im + pi * a_re
        pwr_ref[n] = pr
        pwi_ref[n] = pi


def _ssm_params(lam_re, lam_im, log_dt, b_re, b_im, c_re, c_im):
    g, p = lam_re.shape
    h = b_re.shape[-1]
    nkb = g // SSM_BLOCK_GROUPS
    bl = SSM_BLOCK_GROUPS
    pwr, pwi, bbr, bbi = pl.pallas_call(
        _ssm_params_kernel,
        out_shape=(jax.ShapeDtypeStruct((SEG_LEN, g, p), F32),
                   jax.ShapeDtypeStruct((SEG_LEN, g, p), F32),
                   jax.ShapeDtypeStruct((h, g, p), F32),
                   jax.ShapeDtypeStruct((h, g, p), F32)),
        name="ssm_params",
    )(lam_re, lam_im, log_dt.reshape(g, 1), jnp.transpose(b_re, (2, 0, 1)), jnp.transpose(b_im, (2, 0, 1)))

    eye = jnp.eye(bl, dtype=F32)
    bb = jnp.stack([bbr, bbi]).reshape(2, h, nkb, bl, p)
    b_cat = jnp.einsum('shkgp,gj->kghsjp', bb, eye).reshape(nkb, bl * h, 2 * bl * p).astype(BF16)
    cc = jnp.stack([c_re, -c_im]).reshape(2, nkb, bl, h, p)
    c_cat = jnp.einsum('skghp,gj->ksgpjh', cc, eye).reshape(nkb, 2 * bl * p, bl * h).astype(BF16)
    nslab = 2 * bl * p // LANES
    pw = jnp.stack([pwr, pwi], axis=1).reshape(SEG_LEN, 2, nkb, bl * p)
    a_row = jnp.transpose(pw[0], (1, 0, 2)).reshape(nkb, 1, 2 * bl * p)
    pw = jnp.transpose(pw.reshape(SEG_LEN, 2, nkb, nslab // 2, LANES), (2, 1, 3, 0, 4))
    pw = pw.reshape(nkb, nslab, SEG_LEN, LANES)
    a_tile = jnp.broadcast_to(pw[:, :, 0:1, :], (nkb, nslab, SUBLANES, LANES))
    return b_cat, c_cat, pw, a_tile, a_row


def _norm_matmul_kernel(x_ref, g_ref, w_ref, o_ref, h_ref):
    @pl.when(pl.program_id(1) == 0)
    def _():
        h_ref[...] = _rmsnorm(x_ref[...], g_ref[...]).astype(BF16)
    o_ref[...] = _dot(h_ref[...], w_ref[...])


def _norm_matmul(x, g, w, tm, tn):
    n, d = x.shape
    dout = w.shape[1]
    nbytes = 2 * tm * d * 4 + tm * d * 2 + 2 * d * tn * 2 + 2 * tm * tn * 4
    return pl.pallas_call(
        _norm_matmul_kernel,
        grid=(n // tm, dout // tn),
        in_specs=[pl.BlockSpec((tm, d), lambda i, j: (i, 0)),
                  pl.BlockSpec((1, d), lambda i, j: (0, 0)),
                  pl.BlockSpec((d, tn), lambda i, j: (0, j))],
        out_specs=pl.BlockSpec((tm, tn), lambda i, j: (i, j)),
        out_shape=jax.ShapeDtypeStruct((n, dout), F32),
        scratch_shapes=[pltpu.VMEM((tm, d), BF16)],
        compiler_params=_params(("arbitrary", "arbitrary"), nbytes),
        name="norm_matmul",
    )(x, g.reshape(1, d), w)


def _pool_project(diffs, wp_ref, scale_ref, wb_ref, y_ref):
    gw = wp_ref.shape[1]
    for k, diff in enumerate(diffs):
        yk = _dot(diff.astype(BF16), wp_ref[k]) * scale_ref[:, k * gw:(k + 1) * gw]
        y_ref[:, k * gw:(k + 1) * gw] = yk.astype(BF16)
    return _dot(y_ref[...], wb_ref[...])


def _pool_seq_kernel(u_ref, wp_ref, scale_ref, wb_ref, o_ref, new_ref, ext_ref, y_ref):
    tc, dp = u_ref.shape
    gw = wp_ref.shape[1]
    t = pl.program_id(1)

    @pl.when(t == 0)
    def _():
        ext_ref[0:POOL_HALO, :] = jnp.zeros((POOL_HALO, dp), F32)

    ext_ref[POOL_HALO:POOL_HALO + tc, :] = u_ref[...]
    pos = (t * tc + 1 + lax.broadcasted_iota(jnp.int32, (tc, 1), 0)).astype(F32)
    diffs = []
    for k, w in enumerate(POOL_WINDOWS):
        cols = slice(k * gw, (k + 1) * gw)
        u = ext_ref[POOL_HALO:POOL_HALO + tc, cols]
        s = u
        for j in range(1, w):
            s = s + ext_ref[POOL_HALO - j:POOL_HALO - j + tc, cols]
        count = jnp.minimum(pos, float(w))
        diffs.append(s / count - u)
    o_ref[...] = _pool_project(diffs, wp_ref, scale_ref, wb_ref, y_ref)
    nb = new_ref.shape[1]
    new_ref[0] = ext_ref[POOL_HALO + tc - nb:POOL_HALO + tc, :]
    ext_ref[0:POOL_HALO, :] = ext_ref[tc:tc + POOL_HALO, :]


def _pool_seq(z, w_pool, pool_scale, w_branch, nb, t_len, tc, pool_buf):
    n = z.shape[0]
    ng, gw, _ = w_pool.shape
    dp = ng * gw
    dm = w_branch.shape[1]
    nt = t_len // tc
    nbytes = (2 * tc * dp * 4 + 2 * ng * gw * gw * 2 + 2 * dp * dm * 2 + 2 * tc * dm * 4
              + (tc + POOL_HALO) * dp * 4 + tc * dp * 2 + 4 * tc * gw * 4)
    return pl.pallas_call(
        _pool_seq_kernel,
        grid=(nb, nt),
        in_specs=[pl.BlockSpec((tc, dp), lambda b, t: (b * nt + t, 0)),
                  pl.BlockSpec((ng, gw, gw), lambda b, t: (0, 0, 0)),
                  pl.BlockSpec((1, dp), lambda b, t: (0, 0)),
                  pl.BlockSpec((dp, dm), lambda b, t: (0, 0))],
        out_specs=(pl.BlockSpec((tc, dm), lambda b, t: (b * nt + t, 0)),
                   pl.BlockSpec((1, pool_buf, dp), lambda b, t: (b, 0, 0))),
        out_shape=(jax.ShapeDtypeStruct((n, dm), F32),
                   jax.ShapeDtypeStruct((nb, pool_buf, dp), F32)),
        scratch_shapes=[pltpu.VMEM((tc + POOL_HALO, dp), F32), pltpu.VMEM((tc, dp), BF16)],
        compiler_params=_params(("arbitrary", "arbitrary"), nbytes),
        name="pool_seq",
    )(z, w_pool, pool_scale.reshape(1, dp), w_branch)


def _pool_step_kernel(u_ref, cache_ref, wp_ref, scale_ref, wb_ref, o_ref, new_ref, y_ref):
    dp = u_ref.shape[1]
    gw = wp_ref.shape[1]
    lb = cache_ref.shape[1] // dp
    diffs = []
    for k, w in enumerate(POOL_WINDOWS):
        u = u_ref[:, k * gw:(k + 1) * gw]
        s = u
        for j in range(1, w):
            s = s + cache_ref[:, (lb - j) * dp + k * gw:(lb - j) * dp + (k + 1) * gw]
        diffs.append(s / float(w) - u)
    o_ref[...] = _pool_project(diffs, wp_ref, scale_ref, wb_ref, y_ref)
    new_ref[:, 0:(lb - 1) * dp] = cache_ref[:, dp:lb * dp]
    new_ref[:, (lb - 1) * dp:lb * dp] = u_ref[...]


def _pool_step(z, cache, w_pool, pool_scale, w_branch):
    n = z.shape[0]
    ng, gw, _ = w_pool.shape
    dp = ng * gw
    dm = w_branch.shape[1]
    lbdp = cache.shape[1]
    nbytes = 2 * (n * dp * 4 + 2 * n * lbdp * 4 + ng * gw * gw * 2 + dp * dm * 2 + n * dm * 4) + n * dp * 2
    return pl.pallas_call(
        _pool_step_kernel,
        grid=(1,),
        in_specs=[pl.BlockSpec((n, dp), lambda i: (0, 0)),
                  pl.BlockSpec((n, lbdp), lambda i: (0, 0)),
                  pl.BlockSpec((ng, gw, gw), lambda i: (0, 0, 0)),
                  pl.BlockSpec((1, dp), lambda i: (0, 0)),
                  pl.BlockSpec((dp, dm), lambda i: (0, 0))],
        out_specs=(pl.BlockSpec((n, dm), lambda i: (0, 0)),
                   pl.BlockSpec((n, lbdp), lambda i: (0, 0))),
        out_shape=(jax.ShapeDtypeStruct((n, dm), F32),
                   jax.ShapeDtypeStruct((n, lbdp), F32)),
        scratch_shapes=[pltpu.VMEM((n, dp), BF16)],
        compiler_params=_params(("arbitrary",), nbytes),
        name="pool_step",
    )(z, cache, w_pool, pool_scale.reshape(1, dp), w_branch)


def _cmul_add(xr, xi, ar, ai, br, bi):
    return xr + ar * br - ai * bi, xi + ar * bi + ai * br


def _ssm_seq_kernel(n_cast, u_ref, bcat_ref, ccat_ref, pw_ref, at_ref, d_ref, *refs):
    cast_in, (gy_ref, st_ref), refs = refs[:n_cast], refs[n_cast:n_cast + 2], refs[n_cast + 2:]
    cast_out, (h_ref, loc_ref, hb_ref, seed_ref, carry_ref) = refs[:n_cast], refs[n_cast:]
    for src_ref, dst_ref in zip(cast_in, cast_out):
        dst_ref[...] = src_ref[...].astype(BF16)

    nslab = h_ref.shape[0]
    npair = nslab // 2
    t = pl.program_id(2)

    @pl.when(t == 0)
    def _():
        carry_ref[...] = jnp.zeros_like(carry_ref)

    u = u_ref[...]
    bu = _dot(u.astype(BF16), bcat_ref[0])
    for j in range(nslab):
        for r in range(SUBLANES):
            h_ref[j, r * SEG_PITCH:r * SEG_PITCH + SEG_LEN, :] = (
                bu[r * SEG_LEN:(r + 1) * SEG_LEN, j * LANES:(j + 1) * LANES])

    def seg_step(n, state):
        rows = pl.ds(n, SUBLANES, stride=SEG_PITCH)
        new = []
        for j in range(npair):
            sr, si = _cmul_add(h_ref[j, rows, :], h_ref[npair + j, rows, :],
                               at_ref[0, j], at_ref[0, npair + j], state[2 * j], state[2 * j + 1])
            loc_ref[j, rows, :] = sr
            loc_ref[npair + j, rows, :] = si
            new += [sr, si]
        return tuple(new)

    ends = (jnp.zeros((SUBLANES, LANES), F32),) * nslab
    for n in range(SEG_LEN):
        ends = seg_step(n, ends)

    last = slice(SEG_LEN - 1, SEG_LEN)
    for j in range(npair):
        cr, ci = carry_ref[j], carry_ref[npair + j]
        for r in range(SUBLANES):
            seed_ref[j, r:r + 1, :] = cr
            seed_ref[npair + j, r:r + 1, :] = ci
            cr, ci = _cmul_add(ends[2 * j][r:r + 1, :], ends[2 * j + 1][r:r + 1, :],
                               pw_ref[0, j, last, :], pw_ref[0, npair + j, last, :], cr, ci)
        carry_ref[j] = cr
        carry_ref[npair + j] = ci
    st_ref[0, 0] = carry_ref[...]

    rows_bf16 = 2 * SUBLANES
    for r in range(SUBLANES):
        for i in range(SEG_LEN // rows_bf16):
            src = slice(r * SEG_PITCH + i * rows_bf16, r * SEG_PITCH + (i + 1) * rows_bf16)
            dst = slice(r * SEG_LEN + i * rows_bf16, r * SEG_LEN + (i + 1) * rows_bf16)
            pws = slice(i * rows_bf16, (i + 1) * rows_bf16)
            for j in range(npair):
                hr, hi = _cmul_add(loc_ref[j, src, :], loc_ref[npair + j, src, :],
                                   pw_ref[0, j, pws, :], pw_ref[0, npair + j, pws, :],
                                   seed_ref[j, r:r + 1, :], seed_ref[npair + j, r:r + 1, :])
                hb_ref[dst, j * LANES:(j + 1) * LANES] = hr.astype(BF16)
                hb_ref[dst, (npair + j) * LANES:(npair + j + 1) * LANES] = hi.astype(BF16)

    y = _dot(hb_ref[...], ccat_ref[0]) + d_ref[...] * u
    gy_ref[...] = jax.nn.gelu(y).astype(BF16)


def _ssm_seq(z, col0, b_cat, c_cat, pw, a_tile, d_skip, nb, t_len, side_casts):
    n = z.shape[0]
    nkb, kw, sw = b_cat.shape
    nslab = sw // LANES
    tc = SUBLANES * SEG_LEN
    nt = t_len // tc
    cb = col0 // kw
    n_steps = nkb * nb * nt
    cast_rows = [w.shape[0] // n_steps for w in side_casts]
    assert all(r % (2 * SUBLANES) == 0 and r * n_steps == w.shape[0] for r, w in zip(cast_rows, side_casts))
    nbytes = (2 * tc * kw * 4 + 4 * kw * sw * 2 + 2 * (SEG_LEN + SUBLANES) * sw * 4 + 2 * tc * kw * 2
              + 2 * SUBLANES * SEG_PITCH * sw * 4 + tc * sw * 2 + tc * sw * 4
              + sum(2 * r * w.shape[1] * 6 for r, w in zip(cast_rows, side_casts)))

    def cast_spec(r, w):
        return pl.BlockSpec((r, w.shape[1]), lambda k, b, t: ((k * nb + b) * nt + t, 0))

    cast_specs = [cast_spec(r, w) for r, w in zip(cast_rows, side_casts)]
    gy, st, *cast = pl.pallas_call(
        functools.partial(_ssm_seq_kernel, len(side_casts)),
        grid=(nkb, nb, nt),
        in_specs=[pl.BlockSpec((tc, kw), lambda k, b, t: (b * nt + t, cb + k)),
                  pl.BlockSpec((1, kw, sw), lambda k, b, t: (k, 0, 0)),
                  pl.BlockSpec((1, sw, kw), lambda k, b, t: (k, 0, 0)),
                  pl.BlockSpec((1, nslab, SEG_LEN, LANES), lambda k, b, t: (k, 0, 0, 0)),
                  pl.BlockSpec((1, nslab, SUBLANES, LANES), lambda k, b, t: (k, 0, 0, 0)),
                  pl.BlockSpec((1, kw), lambda k, b, t: (0, k))] + cast_specs,
        out_specs=[pl.BlockSpec((tc, kw), lambda k, b, t: (b * nt + t, k)),
                   pl.BlockSpec((1, 1, nslab, 1, LANES), lambda k, b, t: (b, k, 0, 0, 0))] + cast_specs,
        out_shape=[jax.ShapeDtypeStruct((n, nkb * kw), BF16),
                   jax.ShapeDtypeStruct((nb, nkb, nslab, 1, LANES), F32)]
                  + [jax.ShapeDtypeStruct(w.shape, BF16) for w in side_casts],
        scratch_shapes=[pltpu.VMEM((nslab, SUBLANES * SEG_PITCH, LANES), F32),
                        pltpu.VMEM((nslab, SUBLANES * SEG_PITCH, LANES), F32), pltpu.VMEM((tc, sw), BF16),
                        pltpu.VMEM((nslab, SUBLANES, LANES), F32), pltpu.VMEM((nslab, 1, LANES), F32)],
        compiler_params=_params(("arbitrary", "arbitrary", "arbitrary"), nbytes),
        name="ssm_seq",
    )(z, b_cat, c_cat, pw, a_tile, d_skip.reshape(1, nkb * kw), *side_casts)
    st = st.reshape(nb, nkb, 2, sw // 2)
    return gy, st[:, :, 0].reshape(nb, -1), st[:, :, 1].reshape(nb, -1), cast


def _ssm_step_kernel(u_ref, h0r_ref, h0i_ref, bcat_ref, ccat_ref, a_ref, d_ref,
                     gy_ref, h1r_ref, h1i_ref, h_ref):
    half = h0r_ref.shape[1]
    u = u_ref[...]
    bu = _dot(u.astype(BF16), bcat_ref[0])
    hr, hi = _cmul_add(bu[:, 0:half], bu[:, half:2 * half],
                       a_ref[0, :, 0:half], a_ref[0, :, half:2 * half],
                       h0r_ref[...], h0i_ref[...])
    h1r_ref[...] = hr
    h1i_ref[...] = hi
    h_ref[:, 0:half] = hr.astype(BF16)
    h_ref[:, half:2 * half] = hi.astype(BF16)
    y = _dot(h_ref[...], ccat_ref[0]) + d_ref[...] * u
    gy_ref[...] = jax.nn.gelu(y).astype(BF16)


def _ssm_step(z, col0, h0_re, h0_im, b_cat, c_cat, a_row, d_skip):
    n = z.shape[0]
    nkb, kw, sw = b_cat.shape
    half = sw // 2
    cb = col0 // kw
    nbytes = 2 * (n * kw * 4 + 4 * n * half * 4 + 2 * kw * sw * 2 + sw * 4 + n * kw * 2) + n * sw * 6
    return pl.pallas_call(
        _ssm_step_kernel,
        grid=(nkb,),
        in_specs=[pl.BlockSpec((n, kw), lambda k: (0, cb + k)),
                  pl.BlockSpec((n, half), lambda k: (0, k)),
                  pl.BlockSpec((n, half), lambda k: (0, k)),
                  pl.BlockSpec((1, kw, sw), lambda k: (k, 0, 0)),
                  pl.BlockSpec((1, sw, kw), lambda k: (k, 0, 0)),
                  pl.BlockSpec((1, 1, sw), lambda k: (k, 0, 0)),
                  pl.BlockSpec((1, kw), lambda k: (0, k))],
        out_specs=(pl.BlockSpec((n, kw), lambda k: (0, k)),
                   pl.BlockSpec((n, half), lambda k: (0, k)),
                   pl.BlockSpec((n, half), lambda k: (0, k))),
        out_shape=(jax.ShapeDtypeStruct((n, nkb * kw), BF16),
                   jax.ShapeDtypeStruct((n, nkb * half), F32),
                   jax.ShapeDtypeStruct((n, nkb * half), F32)),
        scratch_shapes=[pltpu.VMEM((n, sw), BF16)],
        compiler_params=_params(("arbitrary",), nbytes),
        name="ssm_step",
    )(z, h0_re, h0_im, b_cat, c_cat, a_row, d_skip.reshape(1, nkb * kw))


def _mix_out_kernel(gy_ref, bp_ref, gp_ref, gs_ref, x_ref, wg_ref, wb_ref, wo_ref, o_ref):
    ds = wb_ref.shape[0]
    g = _dot(gy_ref[...], wg_ref[...])
    y = g[:, 0:ds] * jax.nn.sigmoid(g[:, ds:2 * ds])
    bs = _dot(y.astype(BF16), wb_ref[...])
    merged = jax.nn.sigmoid(gp_ref[...]) * bp_ref[...] + jax.nn.sigmoid(gs_ref[...]) * bs
    o_ref[...] = x_ref[...] + _dot(merged.astype(BF16), wo_ref[...])


def _mix_out(gy, bp, z, gate_col0, x, w_glu, w_branch, w_out, tm):
    n, ds = gy.shape
    dm = w_branch.shape[1]
    gb = gate_col0 // dm
    w_bytes = (ds * 2 * ds + ds * dm + dm * dm) * 2
    nbytes = 2 * (tm * ds * 2 + 5 * tm * dm * 4) + w_bytes + 6 * tm * dm * 4

    def resident(shape):
        return pl.BlockSpec(shape, lambda i: (0, 0), pipeline_mode=pl.Buffered(1))

    return pl.pallas_call(
        _mix_out_kernel,
        grid=(n // tm,),
        in_specs=[pl.BlockSpec((tm, ds), lambda i: (i, 0)),
                  pl.BlockSpec((tm, dm), lambda i: (i, 0)),
                  pl.BlockSpec((tm, dm), lambda i: (i, gb)),
                  pl.BlockSpec((tm, dm), lambda i: (i, gb + 1)),
                  pl.BlockSpec((tm, dm), lambda i: (i, 0)),
                  resident((ds, 2 * ds)), resident((ds, dm)), resident((dm, dm))],
        out_specs=pl.BlockSpec((tm, dm), lambda i: (i, 0)),
        out_shape=jax.ShapeDtypeStruct((n, dm), F32),
        compiler_params=_params(("arbitrary",), nbytes),
        name="mix_out",
    )(gy, bp, z, z, x, w_glu, w_branch, w_out)


def _ffn_gate_down(x_ref, conv, v, wd_ref, o_ref, acc_ref):
    c = pl.program_id(1)
    part = _dot((jax.nn.gelu(conv) * v).astype(BF16), wd_ref[...])

    @pl.when(c == 0)
    def _():
        acc_ref[...] = part

    @pl.when(c > 0)
    def _():
        acc_ref[...] += part

    @pl.when(c == pl.num_programs(1) - 1)
    def _():
        o_ref[...] = x_ref[...] + acc_ref[...]


def _ffn_seq_kernel(tiles_per_seq, x_ref, g_ref, wa_ref, wv_ref, wc_ref, bc_ref, wd_ref,
                    o_ref, new_ref, h_ref, ext_ref, carry_ref, gate_ref):
    tm = x_ref.shape[0]
    tf = gate_ref.shape[1]
    i = pl.program_id(0)
    c = pl.program_id(1)

    @pl.when(jnp.logical_and(i == 0, c == 0))
    def _():
        o_ref[...] = jnp.zeros(o_ref.shape, F32)
        carry_ref[...] = jnp.zeros(carry_ref.shape, F32)

    @pl.when(c == 0)
    def _():
        h_ref[...] = _rmsnorm(x_ref[...], g_ref[...]).astype(BF16)

    seq_start = i % tiles_per_seq == 0
    nb = new_ref.shape[1]
    for q in range(tf // FFN_SLICE):
        cols = slice(q * FFN_SLICE, (q + 1) * FFN_SLICE)
        a = _dot(h_ref[...], wa_ref[:, cols])
        v = _dot(h_ref[...], wv_ref[:, cols])
        ext_ref[0:CONV_HALO, cols] = jnp.where(seq_start, 0.0, carry_ref[c, :, cols])
        ext_ref[CONV_HALO:CONV_HALO + tm, cols] = a
        conv = bc_ref[:, cols] + wc_ref[CONV_W - 1:CONV_W, cols] * a
        for j in range(CONV_W - 1):
            off = CONV_HALO - (CONV_W - 1) + j
            conv = conv + wc_ref[j:j + 1, cols] * ext_ref[off:off + tm, cols]
        carry_ref[c, :, cols] = ext_ref[tm:tm + CONV_HALO, cols]
        new_ref[0, :, cols] = ext_ref[CONV_HALO + tm - nb:CONV_HALO + tm, cols]
        gate_ref[:, cols] = (jax.nn.gelu(conv) * v).astype(BF16)
    o_ref[...] = _dot(gate_ref[...], wd_ref[...]) + jnp.where(c == 0, x_ref[...], o_ref[...])


def _ffn_seq(x, g, w_up, w_conv, b_conv, w_down, nb, t_len, tm, tf, conv_buf):
    n, d = x.shape
    dff = w_down.shape[0]
    nc = dff // tf
    tps = t_len // tm
    nbytes = (4 * tm * d * 4 + 4 * d * tf * 2 + 2 * tf * d * 2 + tm * d * 2 + tm * tf * 2
              + (tm + CONV_HALO) * tf * 4 + nc * CONV_HALO * tf * 4 + 6 * tm * FFN_SLICE * 4)
    out, new_tail = pl.pallas_call(
        functools.partial(_ffn_seq_kernel, tps),
        grid=(n // tm, nc),
        in_specs=[pl.BlockSpec((tm, d), lambda i, c: (i, 0)),
                  pl.BlockSpec((1, d), lambda i, c: (0, 0)),
                  pl.BlockSpec((d, tf), lambda i, c: (0, c)),
                  pl.BlockSpec((d, tf), lambda i, c: (0, nc + c)),
                  pl.BlockSpec((CONV_W, tf), lambda i, c: (0, c)),
                  pl.BlockSpec((1, tf), lambda i, c: (0, c)),
                  pl.BlockSpec((tf, d), lambda i, c: (c, 0))],
        out_specs=(pl.BlockSpec((tm, d), lambda i, c: (i, 0)),
                   pl.BlockSpec((1, conv_buf, tf), lambda i, c: (i, 0, c))),
        out_shape=(jax.ShapeDtypeStruct((n, d), F32),
                   jax.ShapeDtypeStruct((n // tm, conv_buf, dff), F32)),
        scratch_shapes=[pltpu.VMEM((tm, d), BF16), pltpu.VMEM((tm + CONV_HALO, tf), F32),
                        pltpu.VMEM((nc, CONV_HALO, tf), F32), pltpu.VMEM((tm, tf), BF16)],
        compiler_params=_params(("arbitrary", "arbitrary"), nbytes),
        name="ffn_seq",
    )(x, g.reshape(1, d), w_up, w_up, w_conv, b_conv.reshape(1, dff), w_down)
    return out, new_tail[tps - 1::tps]


def _ffn_step_kernel(x_ref, g_ref, wa_ref, wv_ref, wc_ref, bc_ref, wd_ref, p0_ref, p1_ref,
                     o_ref, a_ref, h_ref, acc_ref):
    @pl.when(pl.program_id(1) == 0)
    def _():
        h_ref[...] = _rmsnorm(x_ref[...], g_ref[...]).astype(BF16)

    a = _dot(h_ref[...], wa_ref[...])
    v = _dot(h_ref[...], wv_ref[...])
    conv = bc_ref[...] + wc_ref[2:3, :] * a + wc_ref[1:2, :] * p1_ref[...] + wc_ref[0:1, :] * p0_ref[...]
    a_ref[...] = a
    _ffn_gate_down(x_ref, conv, v, wd_ref, o_ref, acc_ref)


def _ffn_step(x, g, w_up, w_conv, b_conv, w_down, cache, tf):
    n, d = x.shape
    dff = w_down.shape[0]
    nc = dff // tf
    nbytes = 4 * n * d * 4 + 4 * d * tf * 2 + 2 * tf * d * 2 + n * d * 6 + 12 * n * tf * 4
    return pl.pallas_call(
        _ffn_step_kernel,
        grid=(1, nc),
        in_specs=[pl.BlockSpec((n, d), lambda i, c: (0, 0)),
                  pl.BlockSpec((1, d), lambda i, c: (0, 0)),
                  pl.BlockSpec((d, tf), lambda i, c: (0, c)),
                  pl.BlockSpec((d, tf), lambda i, c: (0, nc + c)),
                  pl.BlockSpec((CONV_W, tf), lambda i, c: (0, c)),
                  pl.BlockSpec((1, tf), lambda i, c: (0, c)),
                  pl.BlockSpec((tf, d), lambda i, c: (c, 0)),
                  pl.BlockSpec((n, tf), lambda i, c: (0, c)),
                  pl.BlockSpec((n, tf), lambda i, c: (0, nc + c))],
        out_specs=(pl.BlockSpec((n, d), lambda i, c: (0, 0)),
                   pl.BlockSpec((n, tf), lambda i, c: (0, c))),
        out_shape=(jax.ShapeDtypeStruct((n, d), F32),
                   jax.ShapeDtypeStruct((n, dff), F32)),
        scratch_shapes=[pltpu.VMEM((n, d), BF16), pltpu.VMEM((n, d), F32)],
        compiler_params=_params(("arbitrary", "arbitrary"), nbytes),
        name="ffn_step",
    )(x, g.reshape(1, d), w_up, w_up, w_conv, b_conv.reshape(1, dff), w_down, cache, cache)


def _ple_final_kernel(x_ref, p_ref, gp_ref, wg_ref, wp_ref, gf_ref, o_ref):
    x = x_ref[...]
    gate = jax.nn.sigmoid(_dot(_rmsnorm(x, gp_ref[...]).astype(BF16), wg_ref[...]))
    x = x + gate * _dot(p_ref[...].astype(BF16), wp_ref[...])
    o_ref[...] = _rmsnorm(x, gf_ref[...])


def _ple_final(x, p, g_ple, w_gate, w_ple, g_final, tm):
    n, d = x.shape
    dp = p.shape[1]
    nbytes = 2 * (2 * tm * d * 4 + tm * dp * 4 + d * d * 2 + dp * d * 2) + 4 * tm * d * 4
    return pl.pallas_call(
        _ple_final_kernel,
        grid=(n // tm,),
        in_specs=[pl.BlockSpec((tm, d), lambda i: (i, 0)),
                  pl.BlockSpec((tm, dp), lambda i: (i, 0)),
                  pl.BlockSpec((1, d), lambda i: (0, 0)),
                  pl.BlockSpec((d, d), lambda i: (0, 0)),
                  pl.BlockSpec((dp, d), lambda i: (0, 0)),
                  pl.BlockSpec((1, d), lambda i: (0, 0))],
        out_specs=pl.BlockSpec((tm, d), lambda i: (i, 0)),
        out_shape=jax.ShapeDtypeStruct((n, d), F32),
        compiler_params=_params(("arbitrary",), nbytes),
        name="ple_final",
    )(x, p, g_ple.reshape(1, d), w_gate, w_ple, g_final.reshape(1, d))


def kernel(x_prompt, x_sample, cache_pool, state_ssm_re, state_ssm_im, cache_conv, p_prompt, p_sample, g_mix, w_in, w_pool, pool_scale, ssm_lam_re, ssm_lam_im, ssm_log_dt, ssm_b_re, ssm_b_im, ssm_c_re, ssm_c_im, ssm_d, w_glu, w_branch_pool, w_branch_ssm, w_out, g_ffn, w_up, w_conv, b_conv, w_down, g_ple, w_ple_gate, w_ple, g_final):
    depth = g_mix.shape[0]
    nb, t_len, d = x_prompt.shape
    ns = x_sample.shape[0]
    assert x_sample.shape[1] == 1, "the sample group advances one step per call"
    pool_buf, d_pool = cache_pool.shape[2], cache_pool.shape[3]
    conv_buf, d_ff = cache_conv.shape[2], cache_conv.shape[3]
    n_grp, n_state = ssm_lam_re.shape[1], ssm_lam_re.shape[2]
    d_ssm = ssm_d.shape[1]
    assert pool_buf == max(POOL_WINDOWS) - 1 and conv_buf == CONV_W - 1
    assert n_state == SSM_STATE and d_ssm == n_grp * SSM_GROUP

    xp = x_prompt.reshape(nb * t_len, d)
    xs = x_sample.reshape(ns, d)
    outs = [[] for _ in range(8)]
    for i in range(depth):
        w_in_b, w_pool_b, w_ple_b = w_in[i].astype(BF16), w_pool[i].astype(BF16), w_ple[i].astype(BF16)
        b_cat, c_cat, a_pow, a_tile, a_row = _ssm_params(ssm_lam_re[i], ssm_lam_im[i], ssm_log_dt[i],
                                                         ssm_b_re[i], ssm_b_im[i], ssm_c_re[i], ssm_c_im[i])

        z = _norm_matmul(xp, g_mix[i], w_in_b, 1024, 3 * W_TILE)
        gy, st_re, st_im, (w_up_b, w_down_b, w_out_b, w_pg_b, w_glu_b, w_bs_b, w_bp_b) = _ssm_seq(
            z, d_pool, b_cat, c_cat, a_pow, a_tile, ssm_d[i], nb, t_len,
            [w_up[i], w_down[i], w_out[i], w_ple_gate[i], w_glu[i], w_branch_ssm[i], w_branch_pool[i]])

        def mix_tail(x, z, bp, gy, tm):
            return _mix_out(gy, bp, z, d_pool + d_ssm, x, w_glu_b, w_bs_b, w_out_b, tm)

        bp, pool_new = _pool_seq(z, w_pool_b, pool_scale[i], w_bp_b, nb, t_len, 1024, pool_buf)
        xp = mix_tail(xp, z, bp, gy, 256)
        xp, conv_new = _ffn_seq(xp, g_ffn[i], w_up_b, w_conv[i], b_conv[i], w_down_b, nb, t_len, 1024, W_TILE,
                                conv_buf)
        xp_out = _ple_final(xp, p_prompt[i].reshape(nb * t_len, -1), g_ple[i], w_pg_b, w_ple_b, g_final, 512)
        for lst, val in zip(outs[:4], (pool_new, st_re.reshape(nb, n_grp, n_state),
                                       st_im.reshape(nb, n_grp, n_state), conv_new)):
            lst.append(val)

        z = _norm_matmul(xs, g_mix[i], w_in_b, ns, 3 * W_TILE)
        bp, pool_new = _pool_step(z, cache_pool[i].reshape(ns, pool_buf * d_pool), w_pool_b, pool_scale[i], w_bp_b)
        gy, st_re, st_im = _ssm_step(z, d_pool, state_ssm_re[i].reshape(ns, -1), state_ssm_im[i].reshape(ns, -1),
                                     b_cat, c_cat, a_row, ssm_d[i])
        xs = mix_tail(xs, z, bp, gy, ns)
        xs, a_new = _ffn_step(xs, g_ffn[i], w_up_b, w_conv[i], b_conv[i], w_down_b,
                              cache_conv[i].reshape(ns, conv_buf * d_ff), 2 * W_TILE)
        conv_new = jnp.concatenate([cache_conv[i][:, 1:], a_new[:, None, :]], axis=1)
        xs_out = _ple_final(xs, p_sample[i].reshape(ns, -1), g_ple[i], w_pg_b, w_ple_b, g_final, ns)
        for lst, val in zip(outs[4:], (pool_new.reshape(ns, pool_buf, d_pool), st_re.reshape(ns, n_grp, n_state),
                                       st_im.reshape(ns, n_grp, n_state), conv_new)):
            lst.append(val)

    assert depth == 1
    y_prompt = xp_out.reshape(nb, t_len, d)
    y_sample = xs_out.reshape(ns, 1, d)
    return (y_prompt, y_sample) + tuple(jnp.stack(o, axis=0) for o in outs)
```

```python
import functools

import jax
import jax.numpy as jnp
from jax import lax
from jax.experimental import pallas as pl
from jax.experimental.pallas import tpu as pltpu

F32 = jnp.float32
BF16 = jnp.bfloat16

EPS = 1e-6
POOL_WINDOWS = (2, 4, 8, 16)
POOL_HALO = 16
SSM_GROUP = 16
SSM_STATE = 64
SSM_BLOCK_GROUPS = 16
LANES = 128
SUBLANES = 8
SEG_LEN = 128
SEG_PITCH = SEG_LEN + 8
CONV_W = 3
CONV_HALO = 8
FFN_SLICE = 256
W_TILE = 512
V7X_VMEM_BYTES = 64 * 1024 * 1024


def _vmem_limit(nbytes):
    return int(min(nbytes * 1.25 + (8 << 20), V7X_VMEM_BYTES - (6 << 20)))


def _params(sem, nbytes):
    return pltpu.CompilerParams(dimension_semantics=sem, vmem_limit_bytes=_vmem_limit(nbytes))


def _rmsnorm(x, g):
    return x * lax.rsqrt(jnp.mean(x * x, axis=-1, keepdims=True) + EPS) * g


def _dot(a, b):
    return jnp.dot(a, b, preferred_element_type=F32)


def _ssm_params_kernel(lr_ref, li_ref, logdt_ref, br_ref, bi_ref, pwr_ref, pwi_ref, bbr_ref, bbi_ref):
    lr = lr_ref[...]
    li = li_ref[...]
    dt = jnp.exp(logdt_ref[...])
    mag = jnp.exp(lr * dt)
    a_re = mag * jnp.cos(li * dt)
    a_im = mag * jnp.sin(li * dt)
    nr = a_re - 1.0
    ni = a_im
    den = lr * lr + li * li
    coef_re = (nr * lr + ni * li) / den
    coef_im = (ni * lr - nr * li) / den
    br = br_ref[...]
    bi = bi_ref[...]
    bbr_ref[...] = coef_re[None] * br - coef_im[None] * bi
    bbi_ref[...] = coef_re[None] * bi + coef_im[None] * br
    pr, pi = a_re, a_im
    pwr_ref[0] = pr
    pwi_ref[0] = pi
    for n in range(1, SEG_LEN):
        pr, pi = pr * a_re - pi * a_im, pr * a_im + pi * a_re
        pwr_ref[n] = pr
        pwi_ref[n] = pi


def _ssm_params(lam_re, lam_im, log_dt, b_re, b_im, c_re, c_im):
    g, p = lam_re.shape
    h = b_re.shape[-1]
    nkb = g // SSM_BLOCK_GROUPS
    bl = SSM_BLOCK_GROUPS
    pwr, pwi, bbr, bbi = pl.pallas_call(
        _ssm_params_kernel,
        out_shape=(jax.ShapeDtypeStruct((SEG_LEN, g, p), F32),
                   jax.ShapeDtypeStruct((SEG_LEN, g, p), F32),
                   jax.ShapeDtypeStruct((h, g, p), F32),
                   jax.ShapeDtypeStruct((h, g, p), F32)),
        name="ssm_params",
    )(lam_re, lam_im, log_dt.reshape(g, 1), jnp.transpose(b_re, (2, 0, 1)), jnp.transpose(b_im, (2, 0, 1)))

    eye = jnp.eye(bl, dtype=F32)
    bb = jnp.stack([bbr, bbi]).reshape(2, h, nkb, bl, p)
    b_cat = jnp.einsum('shkgp,gj->kghsjp', bb, eye).reshape(nkb, bl * h, 2 * bl * p).astype(BF16)
    cc = jnp.stack([c_re, -c_im]).reshape(2, nkb, bl, h, p)
    c_cat = jnp.einsum('skghp,gj->ksgpjh', cc, eye).reshape(nkb, 2 * bl * p, bl * h).astype(BF16)
    nslab = 2 * bl * p // LANES
    pw = jnp.stack([pwr, pwi], axis=1).reshape(SEG_LEN, 2, nkb, bl * p)
    a_row = jnp.transpose(pw[0], (1, 0, 2)).reshape(nkb, 1, 2 * bl * p)
    pw = jnp.transpose(pw.reshape(SEG_LEN, 2, nkb, nslab // 2, LANES), (2, 1, 3, 0, 4))
    pw = pw.reshape(nkb, nslab, SEG_LEN, LANES)
    a_tile = jnp.broadcast_to(pw[:, :, 0:1, :], (nkb, nslab, SUBLANES, LANES))
    return b_cat, c_cat, pw, a_tile, a_row


def _norm_matmul_kernel(x_ref, g_ref, w_ref, o_ref, h_ref):
    @pl.when(pl.program_id(1) == 0)
    def _():
        h_ref[...] = _rmsnorm(x_ref[...], g_ref[...]).astype(BF16)
    o_ref[...] = _dot(h_ref[...], w_ref[...])


def _norm_matmul(x, g, w, tm, tn):
    n, d = x.shape
    dout = w.shape[1]
    nbytes = 2 * tm * d * 4 + tm * d * 2 + 2 * d * tn * 2 + 2 * tm * tn * 4
    return pl.pallas_call(
        _norm_matmul_kernel,
        grid=(n // tm, dout // tn),
        in_specs=[pl.BlockSpec((tm, d), lambda i, j: (i, 0)),
                  pl.BlockSpec((1, d), lambda i, j: (0, 0)),
                  pl.BlockSpec((d, tn), lambda i, j: (0, j))],
        out_specs=pl.BlockSpec((tm, tn), lambda i, j: (i, j)),
        out_shape=jax.ShapeDtypeStruct((n, dout), F32),
        scratch_shapes=[pltpu.VMEM((tm, d), BF16)],
        compiler_params=_params(("arbitrary", "arbitrary"), nbytes),
        name="norm_matmul",
    )(x, g.reshape(1, d), w)


def _pool_project(diffs, wp_ref, scale_ref, wb_ref, y_ref):
    gw = wp_ref.shape[1]
    for k, diff in enumerate(diffs):
        yk = _dot(diff.astype(BF16), wp_ref[k]) * scale_ref[:, k * gw:(k + 1) * gw]
        y_ref[:, k * gw:(k + 1) * gw] = yk.astype(BF16)
    return _dot(y_ref[...], wb_ref[...])


def _pool_seq_kernel(u_ref, wp_ref, scale_ref, wb_ref, o_ref, new_ref, ext_ref, y_ref):
    tc, dp = u_ref.shape
    gw = wp_ref.shape[1]
    t = pl.program_id(1)

    @pl.when(t == 0)
    def _():
        ext_ref[0:POOL_HALO, :] = jnp.zeros((POOL_HALO, dp), F32)

    ext_ref[POOL_HALO:POOL_HALO + tc, :] = u_ref[...]
    pos = (t * tc + 1 + lax.broadcasted_iota(jnp.int32, (tc, 1), 0)).astype(F32)
    diffs = []
    for k, w in enumerate(POOL_WINDOWS):
        cols = slice(k * gw, (k + 1) * gw)
        u = ext_ref[POOL_HALO:POOL_HALO + tc, cols]
        s = u
        for j in range(1, w):
            s = s + ext_ref[POOL_HALO - j:POOL_HALO - j + tc, cols]
        count = jnp.minimum(pos, float(w))
        diffs.append(s / count - u)
    o_ref[...] = _pool_project(diffs, wp_ref, scale_ref, wb_ref, y_ref)
    nb = new_ref.shape[1]
    new_ref[0] = ext_ref[POOL_HALO + tc - nb:POOL_HALO + tc, :]
    ext_ref[0:POOL_HALO, :] = ext_ref[tc:tc + POOL_HALO, :]


def _pool_seq(z, w_pool, pool_scale, w_branch, nb, t_len, tc, pool_buf):
    n = z.shape[0]
    ng, gw, _ = w_pool.shape
    dp = ng * gw
    dm = w_branch.shape[1]
    nt = t_len // tc
    nbytes = (2 * tc * dp * 4 + 2 * ng * gw * gw * 2 + 2 * dp * dm * 2 + 2 * tc * dm * 4
              + (tc + POOL_HALO) * dp * 4 + tc * dp * 2 + 4 * tc * gw * 4)
    return pl.pallas_call(
        _pool_seq_kernel,
        grid=(nb, nt),
        in_specs=[pl.BlockSpec((tc, dp), lambda b, t: (b * nt + t, 0)),
                  pl.BlockSpec((ng, gw, gw), lambda b, t: (0, 0, 0)),
                  pl.BlockSpec((1, dp), lambda b, t: (0, 0)),
                  pl.BlockSpec((dp, dm), lambda b, t: (0, 0))],
        out_specs=(pl.BlockSpec((tc, dm), lambda b, t: (b * nt + t, 0)),
                   pl.BlockSpec((1, pool_buf, dp), lambda b, t: (b, 0, 0))),
        out_shape=(jax.ShapeDtypeStruct((n, dm), F32),
                   jax.ShapeDtypeStruct((nb, pool_buf, dp), F32)),
        scratch_shapes=[pltpu.VMEM((tc + POOL_HALO, dp), F32), pltpu.VMEM((tc, dp), BF16)],
        compiler_params=_params(("arbitrary", "arbitrary"), nbytes),
        name="pool_seq",
    )(z, w_pool, pool_scale.reshape(1, dp), w_branch)


def _pool_step_kernel(u_ref, cache_ref, wp_ref, scale_ref, wb_ref, o_ref, new_ref, y_ref):
    dp = u_ref.shape[1]
    gw = wp_ref.shape[1]
    lb = cache_ref.shape[1] // dp
    diffs = []
    for k, w in enumerate(POOL_WINDOWS):
        u = u_ref[:, k * gw:(k + 1) * gw]
        s = u
        for j in range(1, w):
            s = s + cache_ref[:, (lb - j) * dp + k * gw:(lb - j) * dp + (k + 1) * gw]
        diffs.append(s / float(w) - u)
    o_ref[...] = _pool_project(diffs, wp_ref, scale_ref, wb_ref, y_ref)
    new_ref[:, 0:(lb - 1) * dp] = cache_ref[:, dp:lb * dp]
    new_ref[:, (lb - 1) * dp:lb * dp] = u_ref[...]


def _pool_step(z, cache, w_pool, pool_scale, w_branch):
    n = z.shape[0]
    ng, gw, _ = w_pool.shape
    dp = ng * gw
    dm = w_branch.shape[1]
    lbdp = cache.shape[1]
    nbytes = 2 * (n * dp * 4 + 2 * n * lbdp * 4 + ng * gw * gw * 2 + dp * dm * 2 + n * dm * 4) + n * dp * 2
    return pl.pallas_call(
        _pool_step_kernel,
        grid=(1,),
        in_specs=[pl.BlockSpec((n, dp), lambda i: (0, 0)),
                  pl.BlockSpec((n, lbdp), lambda i: (0, 0)),
                  pl.BlockSpec((ng, gw, gw), lambda i: (0, 0, 0)),
                  pl.BlockSpec((1, dp), lambda i: (0, 0)),
                  pl.BlockSpec((dp, dm), lambda i: (0, 0))],
        out_specs=(pl.BlockSpec((n, dm), lambda i: (0, 0)),
                   pl.BlockSpec((n, lbdp), lambda i: (0, 0))),
        out_shape=(jax.ShapeDtypeStruct((n, dm), F32),
                   jax.ShapeDtypeStruct((n, lbdp), F32)),
        scratch_shapes=[pltpu.VMEM((n, dp), BF16)],
        compiler_params=_params(("arbitrary",), nbytes),
        name="pool_step",
    )(z, cache, w_pool, pool_scale.reshape(1, dp), w_branch)


def _cmul_add(xr, xi, ar, ai, br, bi):
    return xr + ar * br - ai * bi, xi + ar * bi + ai * br


def _ssm_seq_kernel(n_cast, u_ref, bcat_ref, ccat_ref, pw_ref, at_ref, d_ref, *refs):
    cast_in, (gy_ref, st_ref), refs = refs[:n_cast], refs[n_cast:n_cast + 2], refs[n_cast + 2:]
    cast_out, (us_ref, up_ref, h_ref, hb_ref, ys_ref, seed_ref, carry_ref) = refs[:n_cast], refs[n_cast:]
    for src_ref, dst_ref in zip(cast_in, cast_out):
        dst_ref[...] = src_ref[...].astype(BF16)

    kw = u_ref.shape[1]
    nslab = h_ref.shape[1] // LANES
    npair = nslab // 2
    t = pl.program_id(2)

    def lanes(j):
        return slice(j * LANES, (j + 1) * LANES)

    def step_rows(n, count=1):
        return slice(n * SUBLANES, (n + count) * SUBLANES)

    @pl.when(t == 0)
    def _():
        carry_ref[...] = jnp.zeros_like(carry_ref)

    for k in range(kw // LANES):
        for r in range(SUBLANES):
            us_ref[k, r * SEG_PITCH:r * SEG_PITCH + SEG_LEN, :] = u_ref[r * SEG_LEN:(r + 1) * SEG_LEN, lanes(k)]
    for n in range(SEG_LEN):
        for k in range(kw // LANES):
            up_ref[step_rows(n), lanes(k)] = us_ref[k, pl.ds(n, SUBLANES, stride=SEG_PITCH), :]
    u = up_ref[...]
    h_ref[...] = _dot(u.astype(BF16), bcat_ref[0])

    state = [jnp.zeros((SUBLANES, LANES), F32)] * nslab
    for n in range(SEG_LEN):
        for j in range(npair):
            sr, si = _cmul_add(h_ref[step_rows(n), lanes(j)], h_ref[step_rows(n), lanes(npair + j)],
                               at_ref[0, j], at_ref[0, npair + j], state[j], state[npair + j])
            h_ref[step_rows(n), lanes(j)] = sr
            h_ref[step_rows(n), lanes(npair + j)] = si
            state[j], state[npair + j] = sr, si

    last = slice(SEG_LEN - 1, SEG_LEN)
    for j in range(npair):
        cr, ci = carry_ref[j], carry_ref[npair + j]
        for r in range(SUBLANES):
            seed_ref[j, r:r + 1, :] = cr
            seed_ref[npair + j, r:r + 1, :] = ci
            cr, ci = _cmul_add(state[j][r:r + 1, :], state[npair + j][r:r + 1, :],
                               pw_ref[0, j, last, :], pw_ref[0, npair + j, last, :], cr, ci)
        carry_ref[j] = cr
        carry_ref[npair + j] = ci
    st_ref[0, 0] = carry_ref[...]

    for i in range(SEG_LEN // 2):
        for j in range(npair):
            parts = []
            for n in (2 * i, 2 * i + 1):
                parts.append(_cmul_add(h_ref[step_rows(n), lanes(j)], h_ref[step_rows(n), lanes(npair + j)],
                                       pw_ref[0, j, n:n + 1, :], pw_ref[0, npair + j, n:n + 1, :],
                                       seed_ref[j], seed_ref[npair + j]))
            hb_ref[step_rows(2 * i, 2), lanes(j)] = jnp.concatenate([p[0] for p in parts], axis=0).astype(BF16)
            hb_ref[step_rows(2 * i, 2), lanes(npair + j)] = jnp.concatenate([p[1] for p in parts],
                                                                            axis=0).astype(BF16)

    y = _dot(hb_ref[...], ccat_ref[0]) + d_ref[...] * u
    g = jax.nn.gelu(y)
    for n in range(SEG_LEN):
        for k in range(kw // LANES):
            ys_ref[k, pl.ds(n, SUBLANES, stride=SEG_PITCH), :] = g[step_rows(n), lanes(k)]
    for k in range(kw // LANES):
        for r in range(SUBLANES):
            gy_ref[r * SEG_LEN:(r + 1) * SEG_LEN, lanes(k)] = (
                ys_ref[k, r * SEG_PITCH:r * SEG_PITCH + SEG_LEN, :].astype(BF16))


def _ssm_seq(z, col0, b_cat, c_cat, pw, a_tile, d_skip, nb, t_len, side_casts):
    n = z.shape[0]
    nkb, kw, sw = b_cat.shape
    nslab = sw // LANES
    tc = SUBLANES * SEG_LEN
    nt = t_len // tc
    cb = col0 // kw
    n_steps = nkb * nb * nt
    cast_rows = [w.shape[0] // n_steps for w in side_casts]
    assert all(r % (2 * SUBLANES) == 0 and r * n_steps == w.shape[0] for r, w in zip(cast_rows, side_casts))
    nbytes = (2 * tc * kw * 4 + 4 * kw * sw * 2 + 2 * (SEG_LEN + SUBLANES) * sw * 4 + 2 * tc * kw * 2
              + 2 * SUBLANES * SEG_PITCH * kw * 4 + tc * kw * 4 + tc * sw * 4 + tc * sw * 2 + tc * sw * 4
              + sum(2 * r * w.shape[1] * 6 for r, w in zip(cast_rows, side_casts)))

    def cast_spec(r, w):
        return pl.BlockSpec((r, w.shape[1]), lambda k, b, t: ((k * nb + b) * nt + t, 0))

    cast_specs = [cast_spec(r, w) for r, w in zip(cast_rows, side_casts)]
    gy, st, *cast = pl.pallas_call(
        functools.partial(_ssm_seq_kernel, len(side_casts)),
        grid=(nkb, nb, nt),
        in_specs=[pl.BlockSpec((tc, kw), lambda k, b, t: (b * nt + t, cb + k)),
                  pl.BlockSpec((1, kw, sw), lambda k, b, t: (k, 0, 0)),
                  pl.BlockSpec((1, sw, kw), lambda k, b, t: (k, 0, 0)),
                  pl.BlockSpec((1, nslab, SEG_LEN, LANES), lambda k, b, t: (k, 0, 0, 0)),
                  pl.BlockSpec((1, nslab, SUBLANES, LANES), lambda k, b, t: (k, 0, 0, 0)),
                  pl.BlockSpec((1, kw), lambda k, b, t: (0, k))] + cast_specs,
        out_specs=[pl.BlockSpec((tc, kw), lambda k, b, t: (b * nt + t, k)),
                   pl.BlockSpec((1, 1, nslab, 1, LANES), lambda k, b, t: (b, k, 0, 0, 0))] + cast_specs,
        out_shape=[jax.ShapeDtypeStruct((n, nkb * kw), BF16),
                   jax.ShapeDtypeStruct((nb, nkb, nslab, 1, LANES), F32)]
                  + [jax.ShapeDtypeStruct(w.shape, BF16) for w in side_casts],
        scratch_shapes=[pltpu.VMEM((kw // LANES, SUBLANES * SEG_PITCH, LANES), F32), pltpu.VMEM((tc, kw), F32),
                        pltpu.VMEM((tc, sw), F32), pltpu.VMEM((tc, sw), BF16),
                        pltpu.VMEM((kw // LANES, SUBLANES * SEG_PITCH, LANES), F32),
                        pltpu.VMEM((nslab, SUBLANES, LANES), F32), pltpu.VMEM((nslab, 1, LANES), F32)],
        compiler_params=_params(("arbitrary", "arbitrary", "arbitrary"), nbytes),
        name="ssm_seq",
    )(z, b_cat, c_cat, pw, a_tile, d_skip.reshape(1, nkb * kw), *side_casts)
    st = st.reshape(nb, nkb, 2, sw // 2)
    return gy, st[:, :, 0].reshape(nb, -1), st[:, :, 1].reshape(nb, -1), cast


def _ssm_step_kernel(u_ref, h0r_ref, h0i_ref, bcat_ref, ccat_ref, a_ref, d_ref,
                     gy_ref, h1r_ref, h1i_ref, h_ref):
    half = h0r_ref.shape[1]
    u = u_ref[...]
    bu = _dot(u.astype(BF16), bcat_ref[0])
    hr, hi = _cmul_add(bu[:, 0:half], bu[:, half:2 * half],
                       a_ref[0, :, 0:half], a_ref[0, :, half:2 * half],
                       h0r_ref[...], h0i_ref[...])
    h1r_ref[...] = hr
    h1i_ref[...] = hi
    h_ref[:, 0:half] = hr.astype(BF16)
    h_ref[:, half:2 * half] = hi.astype(BF16)
    y = _dot(h_ref[...], ccat_ref[0]) + d_ref[...] * u
    gy_ref[...] = jax.nn.gelu(y).astype(BF16)


def _ssm_step(z, col0, h0_re, h0_im, b_cat, c_cat, a_row, d_skip):
    n = z.shape[0]
    nkb, kw, sw = b_cat.shape
    half = sw // 2
    cb = col0 // kw
    nbytes = 2 * (n * kw * 4 + 4 * n * half * 4 + 2 * kw * sw * 2 + sw * 4 + n * kw * 2) + n * sw * 6
    return pl.pallas_call(
        _ssm_step_kernel,
        grid=(nkb,),
        in_specs=[pl.BlockSpec((n, kw), lambda k: (0, cb + k)),
                  pl.BlockSpec((n, half), lambda k: (0, k)),
                  pl.BlockSpec((n, half), lambda k: (0, k)),
                  pl.BlockSpec((1, kw, sw), lambda k: (k, 0, 0)),
                  pl.BlockSpec((1, sw, kw), lambda k: (k, 0, 0)),
                  pl.BlockSpec((1, 1, sw), lambda k: (k, 0, 0)),
                  pl.BlockSpec((1, kw), lambda k: (0, k))],
        out_specs=(pl.BlockSpec((n, kw), lambda k: (0, k)),
                   pl.BlockSpec((n, half), lambda k: (0, k)),
                   pl.BlockSpec((n, half), lambda k: (0, k))),
        out_shape=(jax.ShapeDtypeStruct((n, nkb * kw), BF16),
                   jax.ShapeDtypeStruct((n, nkb * half), F32),
                   jax.ShapeDtypeStruct((n, nkb * half), F32)),
        scratch_shapes=[pltpu.VMEM((n, sw), BF16)],
        compiler_params=_params(("arbitrary",), nbytes),
        name="ssm_step",
    )(z, h0_re, h0_im, b_cat, c_cat, a_row, d_skip.reshape(1, nkb * kw))


def _mix_out_kernel(gy_ref, bp_ref, gp_ref, gs_ref, x_ref, wg_ref, wb_ref, wo_ref, o_ref):
    ds = wb_ref.shape[0]
    g = _dot(gy_ref[...], wg_ref[...])
    y = g[:, 0:ds] * jax.nn.sigmoid(g[:, ds:2 * ds])
    bs = _dot(y.astype(BF16), wb_ref[...])
    merged = jax.nn.sigmoid(gp_ref[...]) * bp_ref[...] + jax.nn.sigmoid(gs_ref[...]) * bs
    o_ref[...] = x_ref[...] + _dot(merged.astype(BF16), wo_ref[...])


def _mix_out(gy, bp, z, gate_col0, x, w_glu, w_branch, w_out, tm):
    n, ds = gy.shape
    dm = w_branch.shape[1]
    gb = gate_col0 // dm
    w_bytes = (ds * 2 * ds + ds * dm + dm * dm) * 2
    nbytes = 2 * (tm * ds * 2 + 5 * tm * dm * 4) + w_bytes + 6 * tm * dm * 4

    def resident(shape):
        return pl.BlockSpec(shape, lambda i: (0, 0), pipeline_mode=pl.Buffered(1))

    return pl.pallas_call(
        _mix_out_kernel,
        grid=(n // tm,),
        in_specs=[pl.BlockSpec((tm, ds), lambda i: (i, 0)),
                  pl.BlockSpec((tm, dm), lambda i: (i, 0)),
                  pl.BlockSpec((tm, dm), lambda i: (i, gb)),
                  pl.BlockSpec((tm, dm), lambda i: (i, gb + 1)),
                  pl.BlockSpec((tm, dm), lambda i: (i, 0)),
                  resident((ds, 2 * ds)), resident((ds, dm)), resident((dm, dm))],
        out_specs=pl.BlockSpec((tm, dm), lambda i: (i, 0)),
        out_shape=jax.ShapeDtypeStruct((n, dm), F32),
        compiler_params=_params(("arbitrary",), nbytes),
        name="mix_out",
    )(gy, bp, z, z, x, w_glu, w_branch, w_out)


def _ffn_gate_down(x_ref, conv, v, wd_ref, o_ref, acc_ref):
    c = pl.program_id(1)
    part = _dot((jax.nn.gelu(conv) * v).astype(BF16), wd_ref[...])

    @pl.when(c == 0)
    def _():
        acc_ref[...] = part

    @pl.when(c > 0)
    def _():
        acc_ref[...] += part

    @pl.when(c == pl.num_programs(1) - 1)
    def _():
        o_ref[...] = x_ref[...] + acc_ref[...]


def _ffn_seq_kernel(tiles_per_seq, x_ref, g_ref, wa_ref, wv_ref, wc_ref, bc_ref, wd_ref,
                    o_ref, new_ref, h_ref, ext_ref, carry_ref, gate_ref):
    tm = x_ref.shape[0]
    tf = gate_ref.shape[1]
    i = pl.program_id(0)
    c = pl.program_id(1)

    @pl.when(jnp.logical_and(i == 0, c == 0))
    def _():
        o_ref[...] = jnp.zeros(o_ref.shape, F32)
        carry_ref[...] = jnp.zeros(carry_ref.shape, F32)

    @pl.when(c == 0)
    def _():
        h_ref[...] = _rmsnorm(x_ref[...], g_ref[...]).astype(BF16)

    seq_start = i % tiles_per_seq == 0
    nb = new_ref.shape[1]
    for q in range(tf // FFN_SLICE):
        cols = slice(q * FFN_SLICE, (q + 1) * FFN_SLICE)
        a = _dot(h_ref[...], wa_ref[:, cols])
        v = _dot(h_ref[...], wv_ref[:, cols])
        ext_ref[0:CONV_HALO, cols] = jnp.where(seq_start, 0.0, carry_ref[c, :, cols])
        ext_ref[CONV_HALO:CONV_HALO + tm, cols] = a
        conv = bc_ref[:, cols] + wc_ref[CONV_W - 1:CONV_W, cols] * a
        for j in range(CONV_W - 1):
            off = CONV_HALO - (CONV_W - 1) + j
            conv = conv + wc_ref[j:j + 1, cols] * ext_ref[off:off + tm, cols]
        carry_ref[c, :, cols] = ext_ref[tm:tm + CONV_HALO, cols]
        new_ref[0, :, cols] = ext_ref[CONV_HALO + tm - nb:CONV_HALO + tm, cols]
        gate_ref[:, cols] = (jax.nn.gelu(conv) * v).astype(BF16)
    o_ref[...] = _dot(gate_ref[...], wd_ref[...]) + jnp.where(c == 0, x_ref[...], o_ref[...])


def _ffn_seq(x, g, w_up, w_conv, b_conv, w_down, nb, t_len, tm, tf, conv_buf):
    n, d = x.shape
    dff = w_down.shape[0]
    nc = dff // tf
    tps = t_len // tm
    nbytes = (4 * tm * d * 4 + 4 * d * tf * 2 + 2 * tf * d * 2 + tm * d * 2 + tm * tf * 2
              + (tm + CONV_HALO) * tf * 4 + nc * CONV_HALO * tf * 4 + 6 * tm * FFN_SLICE * 4)
    out, new_tail = pl.pallas_call(
        functools.partial(_ffn_seq_kernel, tps),
        grid=(n // tm, nc),
        in_specs=[pl.BlockSpec((tm, d), lambda i, c: (i, 0)),
                  pl.BlockSpec((1, d), lambda i, c: (0, 0)),
                  pl.BlockSpec((d, tf), lambda i, c: (0, c)),
                  pl.BlockSpec((d, tf), lambda i, c: (0, nc + c)),
                  pl.BlockSpec((CONV_W, tf), lambda i, c: (0, c)),
                  pl.BlockSpec((1, tf), lambda i, c: (0, c)),
                  pl.BlockSpec((tf, d), lambda i, c: (c, 0))],
        out_specs=(pl.BlockSpec((tm, d), lambda i, c: (i, 0)),
                   pl.BlockSpec((1, conv_buf, tf), lambda i, c: (i, 0, c))),
        out_shape=(jax.ShapeDtypeStruct((n, d), F32),
                   jax.ShapeDtypeStruct((n // tm, conv_buf, dff), F32)),
        scratch_shapes=[pltpu.VMEM((tm, d), BF16), pltpu.VMEM((tm + CONV_HALO, tf), F32),
                        pltpu.VMEM((nc, CONV_HALO, tf), F32), pltpu.VMEM((tm, tf), BF16)],
        compiler_params=_params(("arbitrary", "arbitrary"), nbytes),
        name="ffn_seq",
    )(x, g.reshape(1, d), w_up, w_up, w_conv, b_conv.reshape(1, dff), w_down)
    return out, new_tail[tps - 1::tps]


def _ffn_step_kernel(x_ref, g_ref, wa_ref, wv_ref, wc_ref, bc_ref, wd_ref, p0_ref, p1_ref,
                     o_ref, a_ref, h_ref, acc_ref):
    @pl.when(pl.program_id(1) == 0)
    def _():
        h_ref[...] = _rmsnorm(x_ref[...], g_ref[...]).astype(BF16)

    a = _dot(h_ref[...], wa_ref[...])
    v = _dot(h_ref[...], wv_ref[...])
    conv = bc_ref[...] + wc_ref[2:3, :] * a + wc_ref[1:2, :] * p1_ref[...] + wc_ref[0:1, :] * p0_ref[...]
    a_ref[...] = a
    _ffn_gate_down(x_ref, conv, v, wd_ref, o_ref, acc_ref)


def _ffn_step(x, g, w_up, w_conv, b_conv, w_down, cache, tf):
    n, d = x.shape
    dff = w_down.shape[0]
    nc = dff // tf
    nbytes = 4 * n * d * 4 + 4 * d * tf * 2 + 2 * tf * d * 2 + n * d * 6 + 12 * n * tf * 4
    return pl.pallas_call(
        _ffn_step_kernel,
        grid=(1, nc),
        in_specs=[pl.BlockSpec((n, d), lambda i, c: (0, 0)),
                  pl.BlockSpec((1, d), lambda i, c: (0, 0)),
                  pl.BlockSpec((d, tf), lambda i, c: (0, c)),
                  pl.BlockSpec((d, tf), lambda i, c: (0, nc + c)),
                  pl.BlockSpec((CONV_W, tf), lambda i, c: (0, c)),
                  pl.BlockSpec((1, tf), lambda i, c: (0, c)),
                  pl.BlockSpec((tf, d), lambda i, c: (c, 0)),
                  pl.BlockSpec((n, tf), lambda i, c: (0, c)),
                  pl.BlockSpec((n, tf), lambda i, c: (0, nc + c))],
        out_specs=(pl.BlockSpec((n, d), lambda i, c: (0, 0)),
                   pl.BlockSpec((n, tf), lambda i, c: (0, c))),
        out_shape=(jax.ShapeDtypeStruct((n, d), F32),
                   jax.ShapeDtypeStruct((n, dff), F32)),
        scratch_shapes=[pltpu.VMEM((n, d), BF16), pltpu.VMEM((n, d), F32)],
        compiler_params=_params(("arbitrary", "arbitrary"), nbytes),
        name="ffn_step",
    )(x, g.reshape(1, d), w_up, w_up, w_conv, b_conv.reshape(1, dff), w_down, cache, cache)


def _ple_final_kernel(x_ref, p_ref, gp_ref, wg_ref, wp_ref, gf_ref, o_ref):
    x = x_ref[...]
    gate = jax.nn.sigmoid(_dot(_rmsnorm(x, gp_ref[...]).astype(BF16), wg_ref[...]))
    x = x + gate * _dot(p_ref[...].astype(BF16), wp_ref[...])
    o_ref[...] = _rmsnorm(x, gf_ref[...])


def _ple_final(x, p, g_ple, w_gate, w_ple, g_final, tm):
    n, d = x.shape
    dp = p.shape[1]
    nbytes = 2 * (2 * tm * d * 4 + tm * dp * 4 + d * d * 2 + dp * d * 2) + 4 * tm * d * 4
    return pl.pallas_call(
        _ple_final_kernel,
        grid=(n // tm,),
        in_specs=[pl.BlockSpec((tm, d), lambda i: (i, 0)),
                  pl.BlockSpec((tm, dp), lambda i: (i, 0)),
                  pl.BlockSpec((1, d), lambda i: (0, 0)),
                  pl.BlockSpec((d, d), lambda i: (0, 0)),
                  pl.BlockSpec((dp, d), lambda i: (0, 0)),
                  pl.BlockSpec((1, d), lambda i: (0, 0))],
        out_specs=pl.BlockSpec((tm, d), lambda i: (i, 0)),
        out_shape=jax.ShapeDtypeStruct((n, d), F32),
        compiler_params=_params(("arbitrary",), nbytes),
        name="ple_final",
    )(x, p, g_ple.reshape(1, d), w_gate, w_ple, g_final.reshape(1, d))


def kernel(x_prompt, x_sample, cache_pool, state_ssm_re, state_ssm_im, cache_conv, p_prompt, p_sample, g_mix, w_in, w_pool, pool_scale, ssm_lam_re, ssm_lam_im, ssm_log_dt, ssm_b_re, ssm_b_im, ssm_c_re, ssm_c_im, ssm_d, w_glu, w_branch_pool, w_branch_ssm, w_out, g_ffn, w_up, w_conv, b_conv, w_down, g_ple, w_ple_gate, w_ple, g_final):
    depth = g_mix.shape[0]
    nb, t_len, d = x_prompt.shape
    ns = x_sample.shape[0]
    assert x_sample.shape[1] == 1, "the sample group advances one step per call"
    pool_buf, d_pool = cache_pool.shape[2], cache_pool.shape[3]
    conv_buf, d_ff = cache_conv.shape[2], cache_conv.shape[3]
    n_grp, n_state = ssm_lam_re.shape[1], ssm_lam_re.shape[2]
    d_ssm = ssm_d.shape[1]
    assert pool_buf == max(POOL_WINDOWS) - 1 and conv_buf == CONV_W - 1
    assert n_state == SSM_STATE and d_ssm == n_grp * SSM_GROUP

    xp = x_prompt.reshape(nb * t_len, d)
    xs = x_sample.reshape(ns, d)
    outs = [[] for _ in range(8)]
    for i in range(depth):
        w_in_b, w_pool_b, w_ple_b = w_in[i].astype(BF16), w_pool[i].astype(BF16), w_ple[i].astype(BF16)
        b_cat, c_cat, a_pow, a_tile, a_row = _ssm_params(ssm_lam_re[i], ssm_lam_im[i], ssm_log_dt[i],
                                                         ssm_b_re[i], ssm_b_im[i], ssm_c_re[i], ssm_c_im[i])

        z = _norm_matmul(xp, g_mix[i], w_in_b, 1024, 3 * W_TILE)
        gy, st_re, st_im, (w_up_b, w_down_b, w_out_b, w_pg_b, w_glu_b, w_bs_b, w_bp_b) = _ssm_seq(
            z, d_pool, b_cat, c_cat, a_pow, a_tile, ssm_d[i], nb, t_len,
            [w_up[i], w_down[i], w_out[i], w_ple_gate[i], w_glu[i], w_branch_ssm[i], w_branch_pool[i]])

        def mix_tail(x, z, bp, gy, tm):
            return _mix_out(gy, bp, z, d_pool + d_ssm, x, w_glu_b, w_bs_b, w_out_b, tm)

        bp, pool_new = _pool_seq(z, w_pool_b, pool_scale[i], w_bp_b, nb, t_len, 1024, pool_buf)
        xp = mix_tail(xp, z, bp, gy, 256)
        xp, conv_new = _ffn_seq(xp, g_ffn[i], w_up_b, w_conv[i], b_conv[i], w_down_b, nb, t_len, 1024, W_TILE,
                                conv_buf)
        xp_out = _ple_final(xp, p_prompt[i].reshape(nb * t_len, -1), g_ple[i], w_pg_b, w_ple_b, g_final, 512)
        for lst, val in zip(outs[:4], (pool_new, st_re.reshape(nb, n_grp, n_state),
                                       st_im.reshape(nb, n_grp, n_state), conv_new)):
            lst.append(val)

        z = _norm_matmul(xs, g_mix[i], w_in_b, ns, 3 * W_TILE)
        bp, pool_new = _pool_step(z, cache_pool[i].reshape(ns, pool_buf * d_pool), w_pool_b, pool_scale[i], w_bp_b)
        gy, st_re, st_im = _ssm_step(z, d_pool, state_ssm_re[i].reshape(ns, -1), state_ssm_im[i].reshape(ns, -1),
                                     b_cat, c_cat, a_row, ssm_d[i])
        xs = mix_tail(xs, z, bp, gy, ns)
        xs, a_new = _ffn_step(xs, g_ffn[i], w_up_b, w_conv[i], b_conv[i], w_down_b,
                              cache_conv[i].reshape(ns, conv_buf * d_ff), 2 * W_TILE)
        conv_new = jnp.concatenate([cache_conv[i][:, 1:], a_new[:, None, :]], axis=1)
        xs_out = _ple_final(xs, p_sample[i].reshape(ns, -1), g_ple[i], w_pg_b, w_ple_b, g_final, ns)
        for lst, val in zip(outs[4:], (pool_new.reshape(ns, pool_buf, d_pool), st_re.reshape(ns, n_grp, n_state),
                                       st_im.reshape(ns, n_grp, n_state), conv_new)):
            lst.append(val)

    assert depth == 1
    y_prompt = xp_out.reshape(nb, t_len, d)
    y_sample = xs_out.reshape(ns, 1, d)
    return (y_prompt, y_sample) + tuple(jnp.stack(o, axis=0) for o in outs)
```

```python
import functools

import jax
import jax.numpy as jnp
from jax import lax
from jax.experimental import pallas as pl
from jax.experimental.pallas import tpu as pltpu

F32 = jnp.float32
BF16 = jnp.bfloat16

EPS = 1e-6
POOL_WINDOWS = (2, 4, 8, 16)
POOL_HALO = 16
SSM_GROUP = 16
SSM_STATE = 64
SSM_BLOCK_GROUPS = 16
LANES = 128
SUBLANES = 8
SEG_LEN = 128
SEG_PITCH = SEG_LEN + 8
CONV_W = 3
CONV_HALO = 8
FFN_SLICE = 256
W_TILE = 512
V7X_VMEM_BYTES = 64 * 1024 * 1024


def _vmem_limit(nbytes):
    return int(min(nbytes * 1.25 + (8 << 20), V7X_VMEM_BYTES - (6 << 20)))


def _params(sem, nbytes):
    return pltpu.CompilerParams(dimension_semantics=sem, vmem_limit_bytes=_vmem_limit(nbytes))


def _rmsnorm(x, g):
    return x * lax.rsqrt(jnp.mean(x * x, axis=-1, keepdims=True) + EPS) * g


def _dot(a, b):
    return jnp.dot(a, b, preferred_element_type=F32)


def _ssm_params_kernel(lr_ref, li_ref, logdt_ref, br_ref, bi_ref, pwr_ref, pwi_ref, bbr_ref, bbi_ref):
    lr = lr_ref[...]
    li = li_ref[...]
    dt = jnp.exp(logdt_ref[...])
    mag = jnp.exp(lr * dt)
    a_re = mag * jnp.cos(li * dt)
    a_im = mag * jnp.sin(li * dt)
    nr = a_re - 1.0
    ni = a_im
    den = lr * lr + li * li
    coef_re = (nr * lr + ni * li) / den
    coef_im = (ni * lr - nr * li) / den
    br = br_ref[...]
    bi = bi_ref[...]
    bbr_ref[...] = coef_re[None] * br - coef_im[None] * bi
    bbi_ref[...] = coef_re[None] * bi + coef_im[None] * br
    pr, pi = a_re, a_im
    pwr_ref[0] = pr
    pwi_ref[0] = pi
    for n in range(1, SEG_LEN):
        pr, pi = pr * a_re - pi * a_im, pr * a_im + pi * a_re
        pwr_ref[n] = pr
        pwi_ref[n] = pi


def _ssm_params(lam_re, lam_im, log_dt, b_re, b_im, c_re, c_im):
    g, p = lam_re.shape
    h = b_re.shape[-1]
    nkb = g // SSM_BLOCK_GROUPS
    bl = SSM_BLOCK_GROUPS
    pwr, pwi, bbr, bbi = pl.pallas_call(
        _ssm_params_kernel,
        out_shape=(jax.ShapeDtypeStruct((SEG_LEN, g, p), F32),
                   jax.ShapeDtypeStruct((SEG_LEN, g, p), F32),
                   jax.ShapeDtypeStruct((h, g, p), F32),
                   jax.ShapeDtypeStruct((h, g, p), F32)),
        name="ssm_params",
    )(lam_re, lam_im, log_dt.reshape(g, 1), jnp.transpose(b_re, (2, 0, 1)), jnp.transpose(b_im, (2, 0, 1)))

    eye = jnp.eye(bl, dtype=F32)
    bb = jnp.stack([bbr, bbi]).reshape(2, h, nkb, bl, p)
    b_cat = jnp.einsum('shkgp,gj->kghsjp', bb, eye).reshape(nkb, bl * h, 2 * bl * p).astype(BF16)
    cc = jnp.stack([c_re, -c_im]).reshape(2, nkb, bl, h, p)
    c_cat = jnp.einsum('skghp,gj->ksgpjh', cc, eye).reshape(nkb, 2 * bl * p, bl * h).astype(BF16)
    nslab = 2 * bl * p // LANES
    pw = jnp.stack([pwr, pwi], axis=1).reshape(SEG_LEN, 2, nkb, bl * p)
    a_row = jnp.transpose(pw[0], (1, 0, 2)).reshape(nkb, 1, 2 * bl * p)
    pw = jnp.transpose(pw.reshape(SEG_LEN, 2, nkb, nslab // 2, LANES), (2, 1, 3, 0, 4))
    pw = pw.reshape(nkb, nslab, SEG_LEN, LANES)
    a_tile = jnp.broadcast_to(pw[:, :, 0:1, :], (nkb, nslab, SUBLANES, LANES))
    return b_cat, c_cat, pw, a_tile, a_row


def _norm_matmul_kernel(x_ref, g_ref, w_ref, o_ref, h_ref):
    @pl.when(pl.program_id(1) == 0)
    def _():
        h_ref[...] = _rmsnorm(x_ref[...], g_ref[...]).astype(BF16)
    o_ref[...] = _dot(h_ref[...], w_ref[...])


def _norm_matmul(x, g, w, tm, tn):
    n, d = x.shape
    dout = w.shape[1]
    nbytes = 2 * tm * d * 4 + tm * d * 2 + 2 * d * tn * 2 + 2 * tm * tn * 4
    return pl.pallas_call(
        _norm_matmul_kernel,
        grid=(n // tm, dout // tn),
        in_specs=[pl.BlockSpec((tm, d), lambda i, j: (i, 0)),
                  pl.BlockSpec((1, d), lambda i, j: (0, 0)),
                  pl.BlockSpec((d, tn), lambda i, j: (0, j))],
        out_specs=pl.BlockSpec((tm, tn), lambda i, j: (i, j)),
        out_shape=jax.ShapeDtypeStruct((n, dout), F32),
        scratch_shapes=[pltpu.VMEM((tm, d), BF16)],
        compiler_params=_params(("arbitrary", "arbitrary"), nbytes),
        name="norm_matmul",
    )(x, g.reshape(1, d), w)


def _pool_project(diffs, wp_ref, scale_ref, wb_ref, y_ref):
    gw = wp_ref.shape[1]
    for k, diff in enumerate(diffs):
        yk = _dot(diff.astype(BF16), wp_ref[k]) * scale_ref[:, k * gw:(k + 1) * gw]
        y_ref[:, k * gw:(k + 1) * gw] = yk.astype(BF16)
    return _dot(y_ref[...], wb_ref[...])


def _pool_seq_kernel(u_ref, wp_ref, scale_ref, wb_ref, o_ref, new_ref, ext_ref, y_ref):
    tc, dp = u_ref.shape
    gw = wp_ref.shape[1]
    t = pl.program_id(1)

    @pl.when(t == 0)
    def _():
        ext_ref[0:POOL_HALO, :] = jnp.zeros((POOL_HALO, dp), F32)

    ext_ref[POOL_HALO:POOL_HALO + tc, :] = u_ref[...]
    pos = (t * tc + 1 + lax.broadcasted_iota(jnp.int32, (tc, 1), 0)).astype(F32)
    diffs = []
    for k, w in enumerate(POOL_WINDOWS):
        cols = slice(k * gw, (k + 1) * gw)
        u = ext_ref[POOL_HALO:POOL_HALO + tc, cols]
        s = u
        for j in range(1, w):
            s = s + ext_ref[POOL_HALO - j:POOL_HALO - j + tc, cols]
        count = jnp.minimum(pos, float(w))
        diffs.append(s / count - u)
    o_ref[...] = _pool_project(diffs, wp_ref, scale_ref, wb_ref, y_ref)
    nb = new_ref.shape[1]
    new_ref[0] = ext_ref[POOL_HALO + tc - nb:POOL_HALO + tc, :]
    ext_ref[0:POOL_HALO, :] = ext_ref[tc:tc + POOL_HALO, :]


def _pool_seq(z, w_pool, pool_scale, w_branch, nb, t_len, tc, pool_buf):
    n = z.shape[0]
    ng, gw, _ = w_pool.shape
    dp = ng * gw
    dm = w_branch.shape[1]
    nt = t_len // tc
    nbytes = (2 * tc * dp * 4 + 2 * ng * gw * gw * 2 + 2 * dp * dm * 2 + 2 * tc * dm * 4
              + (tc + POOL_HALO) * dp * 4 + tc * dp * 2 + 4 * tc * gw * 4)
    return pl.pallas_call(
        _pool_seq_kernel,
        grid=(nb, nt),
        in_specs=[pl.BlockSpec((tc, dp), lambda b, t: (b * nt + t, 0)),
                  pl.BlockSpec((ng, gw, gw), lambda b, t: (0, 0, 0)),
                  pl.BlockSpec((1, dp), lambda b, t: (0, 0)),
                  pl.BlockSpec((dp, dm), lambda b, t: (0, 0))],
        out_specs=(pl.BlockSpec((tc, dm), lambda b, t: (b * nt + t, 0)),
                   pl.BlockSpec((1, pool_buf, dp), lambda b, t: (b, 0, 0))),
        out_shape=(jax.ShapeDtypeStruct((n, dm), F32),
                   jax.ShapeDtypeStruct((nb, pool_buf, dp), F32)),
        scratch_shapes=[pltpu.VMEM((tc + POOL_HALO, dp), F32), pltpu.VMEM((tc, dp), BF16)],
        compiler_params=_params(("arbitrary", "arbitrary"), nbytes),
        name="pool_seq",
    )(z, w_pool, pool_scale.reshape(1, dp), w_branch)


def _pool_step_kernel(u_ref, cache_ref, wp_ref, scale_ref, wb_ref, o_ref, new_ref, y_ref):
    dp = u_ref.shape[1]
    gw = wp_ref.shape[1]
    lb = cache_ref.shape[1] // dp
    diffs = []
    for k, w in enumerate(POOL_WINDOWS):
        u = u_ref[:, k * gw:(k + 1) * gw]
        s = u
        for j in range(1, w):
            s = s + cache_ref[:, (lb - j) * dp + k * gw:(lb - j) * dp + (k + 1) * gw]
        diffs.append(s / float(w) - u)
    o_ref[...] = _pool_project(diffs, wp_ref, scale_ref, wb_ref, y_ref)
    new_ref[:, 0:(lb - 1) * dp] = cache_ref[:, dp:lb * dp]
    new_ref[:, (lb - 1) * dp:lb * dp] = u_ref[...]


def _pool_step(z, cache, w_pool, pool_scale, w_branch):
    n = z.shape[0]
    ng, gw, _ = w_pool.shape
    dp = ng * gw
    dm = w_branch.shape[1]
    lbdp = cache.shape[1]
    nbytes = 2 * (n * dp * 4 + 2 * n * lbdp * 4 + ng * gw * gw * 2 + dp * dm * 2 + n * dm * 4) + n * dp * 2
    return pl.pallas_call(
        _pool_step_kernel,
        grid=(1,),
        in_specs=[pl.BlockSpec((n, dp), lambda i: (0, 0)),
                  pl.BlockSpec((n, lbdp), lambda i: (0, 0)),
                  pl.BlockSpec((ng, gw, gw), lambda i: (0, 0, 0)),
                  pl.BlockSpec((1, dp), lambda i: (0, 0)),
                  pl.BlockSpec((dp, dm), lambda i: (0, 0))],
        out_specs=(pl.BlockSpec((n, dm), lambda i: (0, 0)),
                   pl.BlockSpec((n, lbdp), lambda i: (0, 0))),
        out_shape=(jax.ShapeDtypeStruct((n, dm), F32),
                   jax.ShapeDtypeStruct((n, lbdp), F32)),
        scratch_shapes=[pltpu.VMEM((n, dp), BF16)],
        compiler_params=_params(("arbitrary",), nbytes),
        name="pool_step",
    )(z, cache, w_pool, pool_scale.reshape(1, dp), w_branch)


def _cmul_add(xr, xi, ar, ai, br, bi):
    return xr + ar * br - ai * bi, xi + ar * bi + ai * br


def _ssm_seq_kernel(n_cast, u_ref, bcat_ref, ccat_ref, pw_ref, at_ref, d_ref, *refs):
    cast_in, (gy_ref, st_ref), refs = refs[:n_cast], refs[n_cast:n_cast + 2], refs[n_cast + 2:]
    cast_out, (us_ref, up_ref, h_ref, hb_ref, ys_ref, seed_ref, carry_ref) = refs[:n_cast], refs[n_cast:]
    for src_ref, dst_ref in zip(cast_in, cast_out):
        dst_ref[...] = src_ref[...].astype(BF16)

    kw = u_ref.shape[1]
    nslab = h_ref.shape[1] // LANES
    npair = nslab // 2
    t = pl.program_id(2)

    def lanes(j):
        return slice(j * LANES, (j + 1) * LANES)

    def step_rows(n, count=1):
        return slice(n * SUBLANES, (n + count) * SUBLANES)

    @pl.when(t == 0)
    def _():
        carry_ref[...] = jnp.zeros_like(carry_ref)

    for k in range(kw // LANES):
        for r in range(SUBLANES):
            us_ref[k, r * SEG_PITCH:r * SEG_PITCH + SEG_LEN, :] = u_ref[r * SEG_LEN:(r + 1) * SEG_LEN, lanes(k)]
    for n in range(SEG_LEN):
        for k in range(kw // LANES):
            up_ref[step_rows(n), lanes(k)] = us_ref[k, pl.ds(n, SUBLANES, stride=SEG_PITCH), :]
    u = up_ref[...]
    h_ref[...] = _dot(u.astype(BF16), bcat_ref[0])

    state = [jnp.zeros((SUBLANES, LANES), F32)] * nslab
    for n in range(SEG_LEN):
        for j in range(npair):
            sr, si = _cmul_add(h_ref[step_rows(n), lanes(j)], h_ref[step_rows(n), lanes(npair + j)],
                               at_ref[0, j], at_ref[0, npair + j], state[j], state[npair + j])
            h_ref[step_rows(n), lanes(j)] = sr
            h_ref[step_rows(n), lanes(npair + j)] = si
            state[j], state[npair + j] = sr, si

    last = slice(SEG_LEN - 1, SEG_LEN)
    for j in range(npair):
        cr, ci = carry_ref[j], carry_ref[npair + j]
        for r in range(SUBLANES):
            seed_ref[j, r:r + 1, :] = cr
            seed_ref[npair + j, r:r + 1, :] = ci
            cr, ci = _cmul_add(state[j][r:r + 1, :], state[npair + j][r:r + 1, :],
                               pw_ref[0, j, last, :], pw_ref[0, npair + j, last, :], cr, ci)
        carry_ref[j] = cr
        carry_ref[npair + j] = ci
    st_ref[0, 0] = carry_ref[...]

    @pl.when(t >= 0)
    def _():
        for i in range(SEG_LEN // 2):
            for j in range(npair):
                parts = []
                for n in (2 * i, 2 * i + 1):
                    parts.append(_cmul_add(h_ref[step_rows(n), lanes(j)], h_ref[step_rows(n), lanes(npair + j)],
                                           pw_ref[0, j, n:n + 1, :], pw_ref[0, npair + j, n:n + 1, :],
                                           seed_ref[j], seed_ref[npair + j]))
                hb_ref[step_rows(2 * i, 2), lanes(j)] = (
                    jnp.concatenate([p[0] for p in parts], axis=0).astype(BF16))
                hb_ref[step_rows(2 * i, 2), lanes(npair + j)] = (
                    jnp.concatenate([p[1] for p in parts], axis=0).astype(BF16))

        y = _dot(hb_ref[...], ccat_ref[0]) + d_ref[...] * up_ref[...]
        g = jax.nn.gelu(y)
        for n in range(SEG_LEN):
            for k in range(kw // LANES):
                ys_ref[k, pl.ds(n, SUBLANES, stride=SEG_PITCH), :] = g[step_rows(n), lanes(k)]
        for k in range(kw // LANES):
            for r in range(SUBLANES):
                gy_ref[r * SEG_LEN:(r + 1) * SEG_LEN, lanes(k)] = (
                    ys_ref[k, r * SEG_PITCH:r * SEG_PITCH + SEG_LEN, :].astype(BF16))


def _ssm_seq(z, col0, b_cat, c_cat, pw, a_tile, d_skip, nb, t_len, side_casts):
    n = z.shape[0]
    nkb, kw, sw = b_cat.shape
    nslab = sw // LANES
    tc = SUBLANES * SEG_LEN
    nt = t_len // tc
    cb = col0 // kw
    n_steps = nkb * nb * nt
    cast_rows = [w.shape[0] // n_steps for w in side_casts]
    assert all(r % (2 * SUBLANES) == 0 and r * n_steps == w.shape[0] for r, w in zip(cast_rows, side_casts))
    nbytes = (2 * tc * kw * 4 + 4 * kw * sw * 2 + 2 * (SEG_LEN + SUBLANES) * sw * 4 + 2 * tc * kw * 2
              + 2 * SUBLANES * SEG_PITCH * kw * 4 + tc * kw * 4 + tc * sw * 4 + tc * sw * 2 + tc * sw * 4
              + sum(2 * r * w.shape[1] * 6 for r, w in zip(cast_rows, side_casts)))

    def cast_spec(r, w):
        return pl.BlockSpec((r, w.shape[1]), lambda k, b, t: ((k * nb + b) * nt + t, 0))

    cast_specs = [cast_spec(r, w) for r, w in zip(cast_rows, side_casts)]
    gy, st, *cast = pl.pallas_call(
        functools.partial(_ssm_seq_kernel, len(side_casts)),
        grid=(nkb, nb, nt),
        in_specs=[pl.BlockSpec((tc, kw), lambda k, b, t: (b * nt + t, cb + k)),
                  pl.BlockSpec((1, kw, sw), lambda k, b, t: (k, 0, 0)),
                  pl.BlockSpec((1, sw, kw), lambda k, b, t: (k, 0, 0)),
                  pl.BlockSpec((1, nslab, SEG_LEN, LANES), lambda k, b, t: (k, 0, 0, 0)),
                  pl.BlockSpec((1, nslab, SUBLANES, LANES), lambda k, b, t: (k, 0, 0, 0)),
                  pl.BlockSpec((1, kw), lambda k, b, t: (0, k))] + cast_specs,
        out_specs=[pl.BlockSpec((tc, kw), lambda k, b, t: (b * nt + t, k)),
                   pl.BlockSpec((1, 1, nslab, 1, LANES), lambda k, b, t: (b, k, 0, 0, 0))] + cast_specs,
        out_shape=[jax.ShapeDtypeStruct((n, nkb * kw), BF16),
                   jax.ShapeDtypeStruct((nb, nkb, nslab, 1, LANES), F32)]
                  + [jax.ShapeDtypeStruct(w.shape, BF16) for w in side_casts],
        scratch_shapes=[pltpu.VMEM((kw // LANES, SUBLANES * SEG_PITCH, LANES), F32), pltpu.VMEM((tc, kw), F32),
                        pltpu.VMEM((tc, sw), F32), pltpu.VMEM((tc, sw), BF16),
                        pltpu.VMEM((kw // LANES, SUBLANES * SEG_PITCH, LANES), F32),
                        pltpu.VMEM((nslab, SUBLANES, LANES), F32), pltpu.VMEM((nslab, 1, LANES), F32)],
        compiler_params=_params(("arbitrary", "arbitrary", "arbitrary"), nbytes),
        name="ssm_seq",
    )(z, b_cat, c_cat, pw, a_tile, d_skip.reshape(1, nkb * kw), *side_casts)
    st = st.reshape(nb, nkb, 2, sw // 2)
    return gy, st[:, :, 0].reshape(nb, -1), st[:, :, 1].reshape(nb, -1), cast


def _ssm_step_kernel(u_ref, h0r_ref, h0i_ref, bcat_ref, ccat_ref, a_ref, d_ref,
                     gy_ref, h1r_ref, h1i_ref, h_ref):
    half = h0r_ref.shape[1]
    u = u_ref[...]
    bu = _dot(u.astype(BF16), bcat_ref[0])
    hr, hi = _cmul_add(bu[:, 0:half], bu[:, half:2 * half],
                       a_ref[0, :, 0:half], a_ref[0, :, half:2 * half],
                       h0r_ref[...], h0i_ref[...])
    h1r_ref[...] = hr
    h1i_ref[...] = hi
    h_ref[:, 0:half] = hr.astype(BF16)
    h_ref[:, half:2 * half] = hi.astype(BF16)
    y = _dot(h_ref[...], ccat_ref[0]) + d_ref[...] * u
    gy_ref[...] = jax.nn.gelu(y).astype(BF16)


def _ssm_step(z, col0, h0_re, h0_im, b_cat, c_cat, a_row, d_skip):
    n = z.shape[0]
    nkb, kw, sw = b_cat.shape
    half = sw // 2
    cb = col0 // kw
    nbytes = 2 * (n * kw * 4 + 4 * n * half * 4 + 2 * kw * sw * 2 + sw * 4 + n * kw * 2) + n * sw * 6
    return pl.pallas_call(
        _ssm_step_kernel,
        grid=(nkb,),
        in_specs=[pl.BlockSpec((n, kw), lambda k: (0, cb + k)),
                  pl.BlockSpec((n, half), lambda k: (0, k)),
                  pl.BlockSpec((n, half), lambda k: (0, k)),
                  pl.BlockSpec((1, kw, sw), lambda k: (k, 0, 0)),
                  pl.BlockSpec((1, sw, kw), lambda k: (k, 0, 0)),
                  pl.BlockSpec((1, 1, sw), lambda k: (k, 0, 0)),
                  pl.BlockSpec((1, kw), lambda k: (0, k))],
        out_specs=(pl.BlockSpec((n, kw), lambda k: (0, k)),
                   pl.BlockSpec((n, half), lambda k: (0, k)),
                   pl.BlockSpec((n, half), lambda k: (0, k))),
        out_shape=(jax.ShapeDtypeStruct((n, nkb * kw), BF16),
                   jax.ShapeDtypeStruct((n, nkb * half), F32),
                   jax.ShapeDtypeStruct((n, nkb * half), F32)),
        scratch_shapes=[pltpu.VMEM((n, sw), BF16)],
        compiler_params=_params(("arbitrary",), nbytes),
        name="ssm_step",
    )(z, h0_re, h0_im, b_cat, c_cat, a_row, d_skip.reshape(1, nkb * kw))


def _mix_out_kernel(gy_ref, bp_ref, gp_ref, gs_ref, x_ref, wg_ref, wb_ref, wo_ref, o_ref):
    ds = wb_ref.shape[0]
    g = _dot(gy_ref[...], wg_ref[...])
    y = g[:, 0:ds] * jax.nn.sigmoid(g[:, ds:2 * ds])
    bs = _dot(y.astype(BF16), wb_ref[...])
    merged = jax.nn.sigmoid(gp_ref[...]) * bp_ref[...] + jax.nn.sigmoid(gs_ref[...]) * bs
    o_ref[...] = x_ref[...] + _dot(merged.astype(BF16), wo_ref[...])


def _mix_out(gy, bp, z, gate_col0, x, w_glu, w_branch, w_out, tm):
    n, ds = gy.shape
    dm = w_branch.shape[1]
    gb = gate_col0 // dm
    w_bytes = (ds * 2 * ds + ds * dm + dm * dm) * 2
    nbytes = 2 * (tm * ds * 2 + 5 * tm * dm * 4) + w_bytes + 6 * tm * dm * 4

    def resident(shape):
        return pl.BlockSpec(shape, lambda i: (0, 0), pipeline_mode=pl.Buffered(1))

    return pl.pallas_call(
        _mix_out_kernel,
        grid=(n // tm,),
        in_specs=[pl.BlockSpec((tm, ds), lambda i: (i, 0)),
                  pl.BlockSpec((tm, dm), lambda i: (i, 0)),
                  pl.BlockSpec((tm, dm), lambda i: (i, gb)),
                  pl.BlockSpec((tm, dm), lambda i: (i, gb + 1)),
                  pl.BlockSpec((tm, dm), lambda i: (i, 0)),
                  resident((ds, 2 * ds)), resident((ds, dm)), resident((dm, dm))],
        out_specs=pl.BlockSpec((tm, dm), lambda i: (i, 0)),
        out_shape=jax.ShapeDtypeStruct((n, dm), F32),
        compiler_params=_params(("arbitrary",), nbytes),
        name="mix_out",
    )(gy, bp, z, z, x, w_glu, w_branch, w_out)


def _ffn_gate_down(x_ref, conv, v, wd_ref, o_ref, acc_ref):
    c = pl.program_id(1)
    part = _dot((jax.nn.gelu(conv) * v).astype(BF16), wd_ref[...])

    @pl.when(c == 0)
    def _():
        acc_ref[...] = part

    @pl.when(c > 0)
    def _():
        acc_ref[...] += part

    @pl.when(c == pl.num_programs(1) - 1)
    def _():
        o_ref[...] = x_ref[...] + acc_ref[...]


def _ffn_seq_kernel(tiles_per_seq, x_ref, g_ref, wa_ref, wv_ref, wc_ref, bc_ref, wd_ref,
                    o_ref, new_ref, h_ref, ext_ref, carry_ref, gate_ref):
    tm = x_ref.shape[0]
    tf = gate_ref.shape[1]
    i = pl.program_id(0)
    c = pl.program_id(1)

    @pl.when(jnp.logical_and(i == 0, c == 0))
    def _():
        o_ref[...] = jnp.zeros(o_ref.shape, F32)
        carry_ref[...] = jnp.zeros(carry_ref.shape, F32)

    @pl.when(c == 0)
    def _():
        h_ref[...] = _rmsnorm(x_ref[...], g_ref[...]).astype(BF16)

    seq_start = i % tiles_per_seq == 0
    nb = new_ref.shape[1]
    for q in range(tf // FFN_SLICE):
        cols = slice(q * FFN_SLICE, (q + 1) * FFN_SLICE)
        a = _dot(h_ref[...], wa_ref[:, cols])
        v = _dot(h_ref[...], wv_ref[:, cols])
        ext_ref[0:CONV_HALO, cols] = jnp.where(seq_start, 0.0, carry_ref[c, :, cols])
        ext_ref[CONV_HALO:CONV_HALO + tm, cols] = a
        conv = bc_ref[:, cols] + wc_ref[CONV_W - 1:CONV_W, cols] * a
        for j in range(CONV_W - 1):
            off = CONV_HALO - (CONV_W - 1) + j
            conv = conv + wc_ref[j:j + 1, cols] * ext_ref[off:off + tm, cols]
        carry_ref[c, :, cols] = ext_ref[tm:tm + CONV_HALO, cols]
        new_ref[0, :, cols] = ext_ref[CONV_HALO + tm - nb:CONV_HALO + tm, cols]
        gate_ref[:, cols] = (jax.nn.gelu(conv) * v).astype(BF16)
    o_ref[...] = _dot(gate_ref[...], wd_ref[...]) + jnp.where(c == 0, x_ref[...], o_ref[...])


def _ffn_seq(x, g, w_up, w_conv, b_conv, w_down, nb, t_len, tm, tf, conv_buf):
    n, d = x.shape
    dff = w_down.shape[0]
    nc = dff // tf
    tps = t_len // tm
    nbytes = (4 * tm * d * 4 + 4 * d * tf * 2 + 2 * tf * d * 2 + tm * d * 2 + tm * tf * 2
              + (tm + CONV_HALO) * tf * 4 + nc * CONV_HALO * tf * 4 + 6 * tm * FFN_SLICE * 4)
    out, new_tail = pl.pallas_call(
        functools.partial(_ffn_seq_kernel, tps),
        grid=(n // tm, nc),
        in_specs=[pl.BlockSpec((tm, d), lambda i, c: (i, 0)),
                  pl.BlockSpec((1, d), lambda i, c: (0, 0)),
                  pl.BlockSpec((d, tf), lambda i, c: (0, c)),
                  pl.BlockSpec((d, tf), lambda i, c: (0, nc + c)),
                  pl.BlockSpec((CONV_W, tf), lambda i, c: (0, c)),
                  pl.BlockSpec((1, tf), lambda i, c: (0, c)),
                  pl.BlockSpec((tf, d), lambda i, c: (c, 0))],
        out_specs=(pl.BlockSpec((tm, d), lambda i, c: (i, 0)),
                   pl.BlockSpec((1, conv_buf, tf), lambda i, c: (i, 0, c))),
        out_shape=(jax.ShapeDtypeStruct((n, d), F32),
                   jax.ShapeDtypeStruct((n // tm, conv_buf, dff), F32)),
        scratch_shapes=[pltpu.VMEM((tm, d), BF16), pltpu.VMEM((tm + CONV_HALO, tf), F32),
                        pltpu.VMEM((nc, CONV_HALO, tf), F32), pltpu.VMEM((tm, tf), BF16)],
        compiler_params=_params(("arbitrary", "arbitrary"), nbytes),
        name="ffn_seq",
    )(x, g.reshape(1, d), w_up, w_up, w_conv, b_conv.reshape(1, dff), w_down)
    return out, new_tail[tps - 1::tps]


def _ffn_step_kernel(x_ref, g_ref, wa_ref, wv_ref, wc_ref, bc_ref, wd_ref, p0_ref, p1_ref,
                     o_ref, a_ref, h_ref, acc_ref):
    @pl.when(pl.program_id(1) == 0)
    def _():
        h_ref[...] = _rmsnorm(x_ref[...], g_ref[...]).astype(BF16)

    a = _dot(h_ref[...], wa_ref[...])
    v = _dot(h_ref[...], wv_ref[...])
    conv = bc_ref[...] + wc_ref[2:3, :] * a + wc_ref[1:2, :] * p1_ref[...] + wc_ref[0:1, :] * p0_ref[...]
    a_ref[...] = a
    _ffn_gate_down(x_ref, conv, v, wd_ref, o_ref, acc_ref)


def _ffn_step(x, g, w_up, w_conv, b_conv, w_down, cache, tf):
    n, d = x.shape
    dff = w_down.shape[0]
    nc = dff // tf
    nbytes = 4 * n * d * 4 + 4 * d * tf * 2 + 2 * tf * d * 2 + n * d * 6 + 12 * n * tf * 4
    return pl.pallas_call(
        _ffn_step_kernel,
        grid=(1, nc),
        in_specs=[pl.BlockSpec((n, d), lambda i, c: (0, 0)),
                  pl.BlockSpec((1, d), lambda i, c: (0, 0)),
                  pl.BlockSpec((d, tf), lambda i, c: (0, c)),
                  pl.BlockSpec((d, tf), lambda i, c: (0, nc + c)),
                  pl.BlockSpec((CONV_W, tf), lambda i, c: (0, c)),
                  pl.BlockSpec((1, tf), lambda i, c: (0, c)),
                  pl.BlockSpec((tf, d), lambda i, c: (c, 0)),
                  pl.BlockSpec((n, tf), lambda i, c: (0, c)),
                  pl.BlockSpec((n, tf), lambda i, c: (0, nc + c))],
        out_specs=(pl.BlockSpec((n, d), lambda i, c: (0, 0)),
                   pl.BlockSpec((n, tf), lambda i, c: (0, c))),
        out_shape=(jax.ShapeDtypeStruct((n, d), F32),
                   jax.ShapeDtypeStruct((n, dff), F32)),
        scratch_shapes=[pltpu.VMEM((n, d), BF16), pltpu.VMEM((n, d), F32)],
        compiler_params=_params(("arbitrary", "arbitrary"), nbytes),
        name="ffn_step",
    )(x, g.reshape(1, d), w_up, w_up, w_conv, b_conv.reshape(1, dff), w_down, cache, cache)


def _ple_final_kernel(x_ref, p_ref, gp_ref, wg_ref, wp_ref, gf_ref, o_ref):
    x = x_ref[...]
    gate = jax.nn.sigmoid(_dot(_rmsnorm(x, gp_ref[...]).astype(BF16), wg_ref[...]))
    x = x + gate * _dot(p_ref[...].astype(BF16), wp_ref[...])
    o_ref[...] = _rmsnorm(x, gf_ref[...])


def _ple_final(x, p, g_ple, w_gate, w_ple, g_final, tm):
    n, d = x.shape
    dp = p.shape[1]
    nbytes = 2 * (2 * tm * d * 4 + tm * dp * 4 + d * d * 2 + dp * d * 2) + 4 * tm * d * 4
    return pl.pallas_call(
        _ple_final_kernel,
        grid=(n // tm,),
        in_specs=[pl.BlockSpec((tm, d), lambda i: (i, 0)),
                  pl.BlockSpec((tm, dp), lambda i: (i, 0)),
                  pl.BlockSpec((1, d), lambda i: (0, 0)),
                  pl.BlockSpec((d, d), lambda i: (0, 0)),
                  pl.BlockSpec((dp, d), lambda i: (0, 0)),
                  pl.BlockSpec((1, d), lambda i: (0, 0))],
        out_specs=pl.BlockSpec((tm, d), lambda i: (i, 0)),
        out_shape=jax.ShapeDtypeStruct((n, d), F32),
        compiler_params=_params(("arbitrary",), nbytes),
        name="ple_final",
    )(x, p, g_ple.reshape(1, d), w_gate, w_ple, g_final.reshape(1, d))


def kernel(x_prompt, x_sample, cache_pool, state_ssm_re, state_ssm_im, cache_conv, p_prompt, p_sample, g_mix, w_in, w_pool, pool_scale, ssm_lam_re, ssm_lam_im, ssm_log_dt, ssm_b_re, ssm_b_im, ssm_c_re, ssm_c_im, ssm_d, w_glu, w_branch_pool, w_branch_ssm, w_out, g_ffn, w_up, w_conv, b_conv, w_down, g_ple, w_ple_gate, w_ple, g_final):
    depth = g_mix.shape[0]
    nb, t_len, d = x_prompt.shape
    ns = x_sample.shape[0]
    assert x_sample.shape[1] == 1, "the sample group advances one step per call"
    pool_buf, d_pool = cache_pool.shape[2], cache_pool.shape[3]
    conv_buf, d_ff = cache_conv.shape[2], cache_conv.shape[3]
    n_grp, n_state = ssm_lam_re.shape[1], ssm_lam_re.shape[2]
    d_ssm = ssm_d.shape[1]
    assert pool_buf == max(POOL_WINDOWS) - 1 and conv_buf == CONV_W - 1
    assert n_state == SSM_STATE and d_ssm == n_grp * SSM_GROUP

    xp = x_prompt.reshape(nb * t_len, d)
    xs = x_sample.reshape(ns, d)
    outs = [[] for _ in range(8)]
    for i in range(depth):
        w_in_b, w_pool_b, w_ple_b = w_in[i].astype(BF16), w_pool[i].astype(BF16), w_ple[i].astype(BF16)
        b_cat, c_cat, a_pow, a_tile, a_row = _ssm_params(ssm_lam_re[i], ssm_lam_im[i], ssm_log_dt[i],
                                                         ssm_b_re[i], ssm_b_im[i], ssm_c_re[i], ssm_c_im[i])

        z = _norm_matmul(xp, g_mix[i], w_in_b, 1024, 3 * W_TILE)
        gy, st_re, st_im, (w_up_b, w_down_b, w_out_b, w_pg_b, w_glu_b, w_bs_b, w_bp_b) = _ssm_seq(
            z, d_pool, b_cat, c_cat, a_pow, a_tile, ssm_d[i], nb, t_len,
            [w_up[i], w_down[i], w_out[i], w_ple_gate[i], w_glu[i], w_branch_ssm[i], w_branch_pool[i]])

        def mix_tail(x, z, bp, gy, tm):
            return _mix_out(gy, bp, z, d_pool + d_ssm, x, w_glu_b, w_bs_b, w_out_b, tm)

        bp, pool_new = _pool_seq(z, w_pool_b, pool_scale[i], w_bp_b, nb, t_len, 1024, pool_buf)
        xp = mix_tail(xp, z, bp, gy, 256)
        xp, conv_new = _ffn_seq(xp, g_ffn[i], w_up_b, w_conv[i], b_conv[i], w_down_b, nb, t_len, 1024, W_TILE,
                                conv_buf)
        xp_out = _ple_final(xp, p_prompt[i].reshape(nb * t_len, -1), g_ple[i], w_pg_b, w_ple_b, g_final, 512)
        for lst, val in zip(outs[:4], (pool_new, st_re.reshape(nb, n_grp, n_state),
                                       st_im.reshape(nb, n_grp, n_state), conv_new)):
            lst.append(val)

        z = _norm_matmul(xs, g_mix[i], w_in_b, ns, 3 * W_TILE)
        bp, pool_new = _pool_step(z, cache_pool[i].reshape(ns, pool_buf * d_pool), w_pool_b, pool_scale[i], w_bp_b)
        gy, st_re, st_im = _ssm_step(z, d_pool, state_ssm_re[i].reshape(ns, -1), state_ssm_im[i].reshape(ns, -1),
                                     b_cat, c_cat, a_row, ssm_d[i])
        xs = mix_tail(xs, z, bp, gy, ns)
        xs, a_new = _ffn_step(xs, g_ffn[i], w_up_b, w_conv[i], b_conv[i], w_down_b,
                              cache_conv[i].reshape(ns, conv_buf * d_ff), 2 * W_TILE)
        conv_new = jnp.concatenate([cache_conv[i][:, 1:], a_new[:, None, :]], axis=1)
        xs_out = _ple_final(xs, p_sample[i].reshape(ns, -1), g_ple[i], w_pg_b, w_ple_b, g_final, ns)
        for lst, val in zip(outs[4:], (pool_new.reshape(ns, pool_buf, d_pool), st_re.reshape(ns, n_grp, n_state),
                                       st_im.reshape(ns, n_grp, n_state), conv_new)):
            lst.append(val)

    assert depth == 1
    y_prompt = xp_out.reshape(nb, t_len, d)
    y_sample = xs_out.reshape(ns, 1, d)
    return (y_prompt, y_sample) + tuple(jnp.stack(o, axis=0) for o in outs)
```

```python
import functools

import jax
import jax.numpy as jnp
from jax import lax
from jax.experimental import pallas as pl
from jax.experimental.pallas import tpu as pltpu

F32 = jnp.float32
BF16 = jnp.bfloat16

EPS = 1e-6
POOL_WINDOWS = (2, 4, 8, 16)
POOL_HALO = 16
SSM_GROUP = 16
SSM_STATE = 64
SSM_BLOCK_GROUPS = 16
LANES = 128
SUBLANES = 8
SEG_LEN = 128
SEG_PITCH = SEG_LEN + 8
CONV_W = 3
CONV_HALO = 8
FFN_SLICE = 256
W_TILE = 512
V7X_VMEM_BYTES = 64 * 1024 * 1024


def _vmem_limit(nbytes):
    return int(min(nbytes * 1.25 + (8 << 20), V7X_VMEM_BYTES - (6 << 20)))


def _params(sem, nbytes):
    return pltpu.CompilerParams(dimension_semantics=sem, vmem_limit_bytes=_vmem_limit(nbytes))


def _rmsnorm(x, g):
    return x * lax.rsqrt(jnp.mean(x * x, axis=-1, keepdims=True) + EPS) * g


def _dot(a, b):
    return jnp.dot(a, b, preferred_element_type=F32)


def _abar(lr, li, logdt):
    dt = jnp.exp(logdt)
    mag = jnp.exp(lr * dt)
    return mag * jnp.cos(li * dt), mag * jnp.sin(li * dt)


def _ssm_params_kernel(lr_ref, li_ref, logdt_ref, lrs_ref, lis_ref, logdts_ref, br_ref, bi_ref,
                       pwr_ref, pwi_ref, bbr_ref, bbi_ref):
    lr = lr_ref[...]
    li = li_ref[...]
    a_re, a_im = _abar(lr, li, logdt_ref[...])
    nr = a_re - 1.0
    ni = a_im
    den = lr * lr + li * li
    coef_re = ((nr * lr + ni * li) / den)[:, None, :]
    coef_im = ((ni * lr - nr * li) / den)[:, None, :]
    br = br_ref[...]
    bi = bi_ref[...]
    bbr_ref[...] = coef_re * br - coef_im * bi
    bbi_ref[...] = coef_re * bi + coef_im * br
    a_re, a_im = _abar(lrs_ref[...], lis_ref[...], logdts_ref[...])
    nkb, _, per_blk, _ = pwr_ref.shape
    pr, pi = a_re, a_im
    for n in range(SEG_LEN):
        if n:
            pr, pi = pr * a_re - pi * a_im, pr * a_im + pi * a_re
        for k in range(nkb):
            pwr_ref[k, n] = pr[k * per_blk:(k + 1) * per_blk, :]
            pwi_ref[k, n] = pi[k * per_blk:(k + 1) * per_blk, :]


def _ssm_params(lam_re, lam_im, log_dt, b_re, b_im, c_re, c_im):
    g, p = lam_re.shape
    h = b_re.shape[-1]
    nkb = g // SSM_BLOCK_GROUPS
    bl = SSM_BLOCK_GROUPS
    per_blk = bl * p // LANES
    slab = (g * p // LANES, LANES)
    pwr, pwi, bbr, bbi = pl.pallas_call(
        _ssm_params_kernel,
        out_shape=(jax.ShapeDtypeStruct((nkb, SEG_LEN, per_blk, LANES), F32),
                   jax.ShapeDtypeStruct((nkb, SEG_LEN, per_blk, LANES), F32),
                   jax.ShapeDtypeStruct((g, h, p), F32),
                   jax.ShapeDtypeStruct((g, h, p), F32)),
        name="ssm_params",
    )(lam_re, lam_im, log_dt.reshape(g, 1),
      lam_re.reshape(slab), lam_im.reshape(slab), jnp.broadcast_to(log_dt[:, None], (g, p)).reshape(slab),
      jnp.transpose(b_re, (0, 2, 1)), jnp.transpose(b_im, (0, 2, 1)))

    def c_rows(c):
        return jnp.transpose(c.reshape(nkb, bl * h, p), (0, 2, 1))

    bq = jnp.stack([bbr.reshape(nkb, bl * h, p), bbi.reshape(nkb, bl * h, p)], axis=1)
    cq = jnp.stack([c_rows(c_re), -c_rows(c_im)], axis=1)
    a_row = jnp.concatenate([pwr[:, 0].reshape(nkb, 1, bl * p), pwi[:, 0].reshape(nkb, 1, bl * p)], axis=-1)
    return bq, cq, pwr, pwi, a_row


def _expand_blockdiag(bq_ref, cq_ref, bcat_ref, ccat_ref):
    kw, p = bq_ref.shape[2], bq_ref.shape[3]
    half = bcat_ref.shape[1] // 2
    hch = kw * p // half
    log_p, log_h = p.bit_length() - 1, hch.bit_length() - 1
    assert (1 << log_p) == p and (1 << log_h) == hch

    def iota(shape, axis):
        return lax.broadcasted_iota(jnp.int32, shape, axis)

    def same(a, b):
        return a == b

    tile_b = same(iota((p, half), 1) & (p - 1), iota((p, half), 0)).astype(BF16)
    mask_b = same(iota((kw, half), 0) >> log_h, iota((kw, half), 1) >> log_p)
    tile_c = same(iota((half, p), 0) & (p - 1), iota((half, p), 1)).astype(BF16)
    mask_c = same(iota((half, kw), 0) >> log_p, iota((half, kw), 1) >> log_h)
    for s in range(2):
        full = _dot(bq_ref[0, s].astype(BF16), tile_b)
        bcat_ref[:, s * half:(s + 1) * half] = jnp.where(mask_b, full, 0.0).astype(BF16)
        full = _dot(tile_c, cq_ref[0, s].astype(BF16))
        ccat_ref[s * half:(s + 1) * half, :] = jnp.where(mask_c, full, 0.0).astype(BF16)


def _norm_matmul_kernel(x_ref, g_ref, w_ref, o_ref, h_ref):
    @pl.when(pl.program_id(1) == 0)
    def _():
        h_ref[...] = _rmsnorm(x_ref[...], g_ref[...]).astype(BF16)
    o_ref[...] = _dot(h_ref[...], w_ref[...])


def _norm_matmul(x, g, w, tm, tn):
    n, d = x.shape
    dout = w.shape[1]
    nbytes = 2 * tm * d * 4 + tm * d * 2 + 2 * d * tn * 2 + 2 * tm * tn * 4
    return pl.pallas_call(
        _norm_matmul_kernel,
        grid=(n // tm, dout // tn),
        in_specs=[pl.BlockSpec((tm, d), lambda i, j: (i, 0)),
                  pl.BlockSpec((1, d), lambda i, j: (0, 0)),
                  pl.BlockSpec((d, tn), lambda i, j: (0, j))],
        out_specs=pl.BlockSpec((tm, tn), lambda i, j: (i, j)),
        out_shape=jax.ShapeDtypeStruct((n, dout), F32),
        scratch_shapes=[pltpu.VMEM((tm, d), BF16)],
        compiler_params=_params(("arbitrary", "arbitrary"), nbytes),
        name="norm_matmul",
    )(x, g.reshape(1, d), w)


def _pool_project(diffs, wp_ref, scale_ref, wb_ref, y_ref):
    gw = wp_ref.shape[1]
    for k, diff in enumerate(diffs):
        yk = _dot(diff.astype(BF16), wp_ref[k]) * scale_ref[:, k * gw:(k + 1) * gw]
        y_ref[:, k * gw:(k + 1) * gw] = yk.astype(BF16)
    return _dot(y_ref[...], wb_ref[...])


def _pool_seq_kernel(u_ref, wp_ref, scale_ref, wb_ref, o_ref, new_ref, ext_ref, y_ref):
    tc, dp = u_ref.shape
    gw = wp_ref.shape[1]
    t = pl.program_id(1)

    @pl.when(t == 0)
    def _():
        ext_ref[0:POOL_HALO, :] = jnp.zeros((POOL_HALO, dp), F32)

    ext_ref[POOL_HALO:POOL_HALO + tc, :] = u_ref[...]
    pos = (t * tc + 1 + lax.broadcasted_iota(jnp.int32, (tc, 1), 0)).astype(F32)
    diffs = []
    for k, w in enumerate(POOL_WINDOWS):
        cols = slice(k * gw, (k + 1) * gw)
        u = ext_ref[POOL_HALO:POOL_HALO + tc, cols]
        s = u
        for j in range(1, w):
            s = s + ext_ref[POOL_HALO - j:POOL_HALO - j + tc, cols]
        count = jnp.minimum(pos, float(w))
        diffs.append(s / count - u)
    o_ref[...] = _pool_project(diffs, wp_ref, scale_ref, wb_ref, y_ref)
    nb = new_ref.shape[1]
    new_ref[0] = ext_ref[POOL_HALO + tc - nb:POOL_HALO + tc, :]
    ext_ref[0:POOL_HALO, :] = ext_ref[tc:tc + POOL_HALO, :]


def _pool_seq(z, w_pool, pool_scale, w_branch, nb, t_len, tc, pool_buf):
    n = z.shape[0]
    ng, gw, _ = w_pool.shape
    dp = ng * gw
    dm = w_branch.shape[1]
    nt = t_len // tc
    nbytes = (2 * tc * dp * 4 + 2 * ng * gw * gw * 2 + 2 * dp * dm * 2 + 2 * tc * dm * 4
              + (tc + POOL_HALO) * dp * 4 + tc * dp * 2 + 4 * tc * gw * 4)
    return pl.pallas_call(
        _pool_seq_kernel,
        grid=(nb, nt),
        in_specs=[pl.BlockSpec((tc, dp), lambda b, t: (b * nt + t, 0)),
                  pl.BlockSpec((ng, gw, gw), lambda b, t: (0, 0, 0)),
                  pl.BlockSpec((1, dp), lambda b, t: (0, 0)),
                  pl.BlockSpec((dp, dm), lambda b, t: (0, 0))],
        out_specs=(pl.BlockSpec((tc, dm), lambda b, t: (b * nt + t, 0)),
                   pl.BlockSpec((1, pool_buf, dp), lambda b, t: (b, 0, 0))),
        out_shape=(jax.ShapeDtypeStruct((n, dm), F32),
                   jax.ShapeDtypeStruct((nb, pool_buf, dp), F32)),
        scratch_shapes=[pltpu.VMEM((tc + POOL_HALO, dp), F32), pltpu.VMEM((tc, dp), BF16)],
        compiler_params=_params(("arbitrary", "arbitrary"), nbytes),
        name="pool_seq",
    )(z, w_pool, pool_scale.reshape(1, dp), w_branch)


def _pool_step_kernel(u_ref, cache_ref, wp_ref, scale_ref, wb_ref, o_ref, new_ref, y_ref):
    dp = u_ref.shape[1]
    gw = wp_ref.shape[1]
    lb = cache_ref.shape[1] // dp
    diffs = []
    for k, w in enumerate(POOL_WINDOWS):
        u = u_ref[:, k * gw:(k + 1) * gw]
        s = u
        for j in range(1, w):
            s = s + cache_ref[:, (lb - j) * dp + k * gw:(lb - j) * dp + (k + 1) * gw]
        diffs.append(s / float(w) - u)
    o_ref[...] = _pool_project(diffs, wp_ref, scale_ref, wb_ref, y_ref)
    new_ref[:, 0:(lb - 1) * dp] = cache_ref[:, dp:lb * dp]
    new_ref[:, (lb - 1) * dp:lb * dp] = u_ref[...]


def _pool_step(z, cache, w_pool, pool_scale, w_branch):
    n = z.shape[0]
    ng, gw, _ = w_pool.shape
    dp = ng * gw
    dm = w_branch.shape[1]
    lbdp = cache.shape[1]
    nbytes = 2 * (n * dp * 4 + 2 * n * lbdp * 4 + ng * gw * gw * 2 + dp * dm * 2 + n * dm * 4) + n * dp * 2
    return pl.pallas_call(
        _pool_step_kernel,
        grid=(1,),
        in_specs=[pl.BlockSpec((n, dp), lambda i: (0, 0)),
                  pl.BlockSpec((n, lbdp), lambda i: (0, 0)),
                  pl.BlockSpec((ng, gw, gw), lambda i: (0, 0, 0)),
                  pl.BlockSpec((1, dp), lambda i: (0, 0)),
                  pl.BlockSpec((dp, dm), lambda i: (0, 0))],
        out_specs=(pl.BlockSpec((n, dm), lambda i: (0, 0)),
                   pl.BlockSpec((n, lbdp), lambda i: (0, 0))),
        out_shape=(jax.ShapeDtypeStruct((n, dm), F32),
                   jax.ShapeDtypeStruct((n, lbdp), F32)),
        scratch_shapes=[pltpu.VMEM((n, dp), BF16)],
        compiler_params=_params(("arbitrary",), nbytes),
        name="pool_step",
    )(z, cache, w_pool, pool_scale.reshape(1, dp), w_branch)


def _cmul_add(xr, xi, ar, ai, br, bi):
    return xr + ar * br - ai * bi, xi + ar * bi + ai * br


def _ssm_seq_kernel(n_cast, u_ref, bq_ref, cq_ref, pwr_ref, pwi_ref, d_ref, *refs):
    cast_in, (gy_ref, st_ref), refs = refs[:n_cast], refs[n_cast:n_cast + 2], refs[n_cast + 2:]
    cast_out, refs = refs[:n_cast], refs[n_cast:]
    bcat_ref, ccat_ref, us_ref, up_ref, h_ref, hb_ref, ys_ref, seed_ref, carry_ref = refs
    for src_ref, dst_ref in zip(cast_in, cast_out):
        dst_ref[...] = src_ref[...].astype(BF16)

    kw = u_ref.shape[1]
    nslab = h_ref.shape[1] // LANES
    npair = nslab // 2
    t = pl.program_id(2)

    def lanes(j):
        return slice(j * LANES, (j + 1) * LANES)

    def step_rows(n, count=1):
        return slice(n * SUBLANES, (n + count) * SUBLANES)

    def apow(n, j):
        return pwr_ref[0, n, j:j + 1, :], pwi_ref[0, n, j:j + 1, :]

    @pl.when(jnp.logical_and(pl.program_id(1) == 0, t == 0))
    def _():
        _expand_blockdiag(bq_ref, cq_ref, bcat_ref, ccat_ref)

    @pl.when(t == 0)
    def _():
        carry_ref[...] = jnp.zeros_like(carry_ref)

    for k in range(kw // LANES):
        for r in range(SUBLANES):
            us_ref[k, r * SEG_PITCH:r * SEG_PITCH + SEG_LEN, :] = u_ref[r * SEG_LEN:(r + 1) * SEG_LEN, lanes(k)]
    for n in range(SEG_LEN):
        for k in range(kw // LANES):
            up_ref[step_rows(n), lanes(k)] = us_ref[k, pl.ds(n, SUBLANES, stride=SEG_PITCH), :]
    u = up_ref[...]
    h_ref[...] = _dot(u.astype(BF16), bcat_ref[...])

    state = [jnp.zeros((SUBLANES, LANES), F32)] * nslab
    for n in range(SEG_LEN):
        for j in range(npair):
            sr, si = _cmul_add(h_ref[step_rows(n), lanes(j)], h_ref[step_rows(n), lanes(npair + j)],
                               *apow(0, j), state[j], state[npair + j])
            h_ref[step_rows(n), lanes(j)] = sr
            h_ref[step_rows(n), lanes(npair + j)] = si
            state[j], state[npair + j] = sr, si

    for j in range(npair):
        cr, ci = carry_ref[j], carry_ref[npair + j]
        for r in range(SUBLANES):
            seed_ref[j, r:r + 1, :] = cr
            seed_ref[npair + j, r:r + 1, :] = ci
            cr, ci = _cmul_add(state[j][r:r + 1, :], state[npair + j][r:r + 1, :],
                               *apow(SEG_LEN - 1, j), cr, ci)
        carry_ref[j] = cr
        carry_ref[npair + j] = ci
    st_ref[0, 0] = carry_ref[...]

    @pl.when(t >= 0)
    def _():
        for i in range(SEG_LEN // 2):
            for j in range(npair):
                parts = []
                for n in (2 * i, 2 * i + 1):
                    parts.append(_cmul_add(h_ref[step_rows(n), lanes(j)], h_ref[step_rows(n), lanes(npair + j)],
                                           *apow(n, j), seed_ref[j], seed_ref[npair + j]))
                hb_ref[step_rows(2 * i, 2), lanes(j)] = (
                    jnp.concatenate([p[0] for p in parts], axis=0).astype(BF16))
                hb_ref[step_rows(2 * i, 2), lanes(npair + j)] = (
                    jnp.concatenate([p[1] for p in parts], axis=0).astype(BF16))

        y = _dot(hb_ref[...], ccat_ref[...]) + d_ref[...] * up_ref[...]
        g = jax.nn.gelu(y)
        for n in range(SEG_LEN):
            for k in range(kw // LANES):
                ys_ref[k, pl.ds(n, SUBLANES, stride=SEG_PITCH), :] = g[step_rows(n), lanes(k)]
        for k in range(kw // LANES):
            for r in range(SUBLANES):
                gy_ref[r * SEG_LEN:(r + 1) * SEG_LEN, lanes(k)] = (
                    ys_ref[k, r * SEG_PITCH:r * SEG_PITCH + SEG_LEN, :].astype(BF16))


def _ssm_seq(z, col0, bq, cq, pwr, pwi, d_skip, nb, t_len, side_casts):
    n = z.shape[0]
    nkb, _, kw, p = bq.shape
    sw = 2 * SSM_BLOCK_GROUPS * p
    nslab = sw // LANES
    tc = SUBLANES * SEG_LEN
    nt = t_len // tc
    cb = col0 // kw
    n_steps = nkb * nb * nt
    cast_rows = [w.shape[0] // n_steps for w in side_casts]
    assert all(r % (2 * SUBLANES) == 0 and r * n_steps == w.shape[0] for r, w in zip(cast_rows, side_casts))
    nbytes = (2 * tc * kw * 4 + 4 * kw * sw * 2 + 2 * (SEG_LEN + SUBLANES) * sw * 4 + 2 * tc * kw * 2
              + 2 * SUBLANES * SEG_PITCH * kw * 4 + tc * kw * 4 + tc * sw * 4 + tc * sw * 2 + tc * sw * 4
              + sum(2 * r * w.shape[1] * 6 for r, w in zip(cast_rows, side_casts)))

    def cast_spec(r, w):
        return pl.BlockSpec((r, w.shape[1]), lambda k, b, t: ((k * nb + b) * nt + t, 0))

    cast_specs = [cast_spec(r, w) for r, w in zip(cast_rows, side_casts)]
    gy, st, *cast = pl.pallas_call(
        functools.partial(_ssm_seq_kernel, len(side_casts)),
        grid=(nkb, nb, nt),
        in_specs=[pl.BlockSpec((tc, kw), lambda k, b, t: (b * nt + t, cb + k)),
                  pl.BlockSpec((1, 2, kw, p), lambda k, b, t: (k, 0, 0, 0)),
                  pl.BlockSpec((1, 2, p, kw), lambda k, b, t: (k, 0, 0, 0)),
                  pl.BlockSpec((1, SEG_LEN, nslab // 2, LANES), lambda k, b, t: (k, 0, 0, 0)),
                  pl.BlockSpec((1, SEG_LEN, nslab // 2, LANES), lambda k, b, t: (k, 0, 0, 0)),
                  pl.BlockSpec((1, kw), lambda k, b, t: (0, k))] + cast_specs,
        out_specs=[pl.BlockSpec((tc, kw), lambda k, b, t: (b * nt + t, k)),
                   pl.BlockSpec((1, 1, nslab, 1, LANES), lambda k, b, t: (b, k, 0, 0, 0))] + cast_specs,
        out_shape=[jax.ShapeDtypeStruct((n, nkb * kw), BF16),
                   jax.ShapeDtypeStruct((nb, nkb, nslab, 1, LANES), F32)]
                  + [jax.ShapeDtypeStruct(w.shape, BF16) for w in side_casts],
        scratch_shapes=[pltpu.VMEM((kw, sw), BF16), pltpu.VMEM((sw, kw), BF16),
                        pltpu.VMEM((kw // LANES, SUBLANES * SEG_PITCH, LANES), F32), pltpu.VMEM((tc, kw), F32),
                        pltpu.VMEM((tc, sw), F32), pltpu.VMEM((tc, sw), BF16),
                        pltpu.VMEM((kw // LANES, SUBLANES * SEG_PITCH, LANES), F32),
                        pltpu.VMEM((nslab, SUBLANES, LANES), F32), pltpu.VMEM((nslab, 1, LANES), F32)],
        compiler_params=_params(("arbitrary", "arbitrary", "arbitrary"), nbytes),
        name="ssm_seq",
    )(z, bq, cq, pwr, pwi, d_skip.reshape(1, nkb * kw), *side_casts)
    st = st.reshape(nb, nkb, 2, sw // 2)
    return gy, st[:, :, 0].reshape(nb, -1), st[:, :, 1].reshape(nb, -1), cast


def _ssm_step_kernel(u_ref, h0r_ref, h0i_ref, bq_ref, cq_ref, a_ref, d_ref,
                     gy_ref, h1r_ref, h1i_ref, h_ref, bcat_ref, ccat_ref):
    half = h0r_ref.shape[1]
    _expand_blockdiag(bq_ref, cq_ref, bcat_ref, ccat_ref)
    u = u_ref[...]
    bu = _dot(u.astype(BF16), bcat_ref[...])
    hr, hi = _cmul_add(bu[:, 0:half], bu[:, half:2 * half],
                       a_ref[0, :, 0:half], a_ref[0, :, half:2 * half],
                       h0r_ref[...], h0i_ref[...])
    h1r_ref[...] = hr
    h1i_ref[...] = hi
    h_ref[:, 0:half] = hr.astype(BF16)
    h_ref[:, half:2 * half] = hi.astype(BF16)
    y = _dot(h_ref[...], ccat_ref[...]) + d_ref[...] * u
    gy_ref[...] = jax.nn.gelu(y).astype(BF16)


def _ssm_step(z, col0, h0_re, h0_im, bq, cq, a_row, d_skip):
    n = z.shape[0]
    nkb, _, kw, p = bq.shape
    half = SSM_BLOCK_GROUPS * p
    sw = 2 * half
    cb = col0 // kw
    nbytes = 2 * (n * kw * 4 + 4 * n * half * 4 + 2 * kw * sw * 2 + sw * 4 + n * kw * 2) + n * sw * 6
    return pl.pallas_call(
        _ssm_step_kernel,
        grid=(nkb,),
        in_specs=[pl.BlockSpec((n, kw), lambda k: (0, cb + k)),
                  pl.BlockSpec((n, half), lambda k: (0, k)),
                  pl.BlockSpec((n, half), lambda k: (0, k)),
                  pl.BlockSpec((1, 2, kw, p), lambda k: (k, 0, 0, 0)),
                  pl.BlockSpec((1, 2, p, kw), lambda k: (k, 0, 0, 0)),
                  pl.BlockSpec((1, 1, sw), lambda k: (k, 0, 0)),
                  pl.BlockSpec((1, kw), lambda k: (0, k))],
        out_specs=(pl.BlockSpec((n, kw), lambda k: (0, k)),
                   pl.BlockSpec((n, half), lambda k: (0, k)),
                   pl.BlockSpec((n, half), lambda k: (0, k))),
        out_shape=(jax.ShapeDtypeStruct((n, nkb * kw), BF16),
                   jax.ShapeDtypeStruct((n, nkb * half), F32),
                   jax.ShapeDtypeStruct((n, nkb * half), F32)),
        scratch_shapes=[pltpu.VMEM((n, sw), BF16), pltpu.VMEM((kw, sw), BF16), pltpu.VMEM((sw, kw), BF16)],
        compiler_params=_params(("arbitrary",), nbytes),
        name="ssm_step",
    )(z, h0_re, h0_im, bq, cq, a_row, d_skip.reshape(1, nkb * kw))


def _mix_out_kernel(gy_ref, bp_ref, gp_ref, gs_ref, x_ref, wg_ref, wb_ref, wo_ref, o_ref):
    ds = wb_ref.shape[0]
    g = _dot(gy_ref[...], wg_ref[...])
    y = g[:, 0:ds] * jax.nn.sigmoid(g[:, ds:2 * ds])
    bs = _dot(y.astype(BF16), wb_ref[...])
    merged = jax.nn.sigmoid(gp_ref[...]) * bp_ref[...] + jax.nn.sigmoid(gs_ref[...]) * bs
    o_ref[...] = x_ref[...] + _dot(merged.astype(BF16), wo_ref[...])


def _mix_out(gy, bp, z, gate_col0, x, w_glu, w_branch, w_out, tm):
    n, ds = gy.shape
    dm = w_branch.shape[1]
    gb = gate_col0 // dm
    w_bytes = (ds * 2 * ds + ds * dm + dm * dm) * 2
    nbytes = 2 * (tm * ds * 2 + 5 * tm * dm * 4) + w_bytes + 6 * tm * dm * 4

    def resident(shape):
        return pl.BlockSpec(shape, lambda i: (0, 0), pipeline_mode=pl.Buffered(1))

    return pl.pallas_call(
        _mix_out_kernel,
        grid=(n // tm,),
        in_specs=[pl.BlockSpec((tm, ds), lambda i: (i, 0)),
                  pl.BlockSpec((tm, dm), lambda i: (i, 0)),
                  pl.BlockSpec((tm, dm), lambda i: (i, gb)),
                  pl.BlockSpec((tm, dm), lambda i: (i, gb + 1)),
                  pl.BlockSpec((tm, dm), lambda i: (i, 0)),
                  resident((ds, 2 * ds)), resident((ds, dm)), resident((dm, dm))],
        out_specs=pl.BlockSpec((tm, dm), lambda i: (i, 0)),
        out_shape=jax.ShapeDtypeStruct((n, dm), F32),
        compiler_params=_params(("arbitrary",), nbytes),
        name="mix_out",
    )(gy, bp, z, z, x, w_glu, w_branch, w_out)


def _ffn_gate_down(x_ref, conv, v, wd_ref, o_ref, acc_ref):
    c = pl.program_id(1)
    part = _dot((jax.nn.gelu(conv) * v).astype(BF16), wd_ref[...])

    @pl.when(c == 0)
    def _():
        acc_ref[...] = part

    @pl.when(c > 0)
    def _():
        acc_ref[...] += part

    @pl.when(c == pl.num_programs(1) - 1)
    def _():
        o_ref[...] = x_ref[...] + acc_ref[...]


def _ffn_seq_kernel(tiles_per_seq, x_ref, g_ref, wa_ref, wv_ref, wc_ref, bc_ref, wd_ref,
                    o_ref, new_ref, h_ref, ext_ref, carry_ref, gate_ref):
    tm = x_ref.shape[0]
    tf = gate_ref.shape[1]
    i = pl.program_id(0)
    c = pl.program_id(1)

    @pl.when(jnp.logical_and(i == 0, c == 0))
    def _():
        o_ref[...] = jnp.zeros(o_ref.shape, F32)
        carry_ref[...] = jnp.zeros(carry_ref.shape, F32)

    @pl.when(c == 0)
    def _():
        h_ref[...] = _rmsnorm(x_ref[...], g_ref[...]).astype(BF16)

    seq_start = i % tiles_per_seq == 0
    nb = new_ref.shape[1]
    for q in range(tf // FFN_SLICE):
        cols = slice(q * FFN_SLICE, (q + 1) * FFN_SLICE)
        a = _dot(h_ref[...], wa_ref[:, cols])
        v = _dot(h_ref[...], wv_ref[:, cols])
        ext_ref[0:CONV_HALO, cols] = jnp.where(seq_start, 0.0, carry_ref[c, :, cols])
        ext_ref[CONV_HALO:CONV_HALO + tm, cols] = a
        conv = bc_ref[:, cols] + wc_ref[CONV_W - 1:CONV_W, cols] * a
        for j in range(CONV_W - 1):
            off = CONV_HALO - (CONV_W - 1) + j
            conv = conv + wc_ref[j:j + 1, cols] * ext_ref[off:off + tm, cols]
        carry_ref[c, :, cols] = ext_ref[tm:tm + CONV_HALO, cols]
        new_ref[0, :, cols] = ext_ref[CONV_HALO + tm - nb:CONV_HALO + tm, cols]
        gate_ref[:, cols] = (jax.nn.gelu(conv) * v).astype(BF16)
    o_ref[...] = _dot(gate_ref[...], wd_ref[...]) + jnp.where(c == 0, x_ref[...], o_ref[...])


def _ffn_seq(x, g, w_up, w_conv, b_conv, w_down, nb, t_len, tm, tf, conv_buf):
    n, d = x.shape
    dff = w_down.shape[0]
    nc = dff // tf
    tps = t_len // tm
    nbytes = (4 * tm * d * 4 + 4 * d * tf * 2 + 2 * tf * d * 2 + tm * d * 2 + tm * tf * 2
              + (tm + CONV_HALO) * tf * 4 + nc * CONV_HALO * tf * 4 + 6 * tm * FFN_SLICE * 4)
    out, new_tail = pl.pallas_call(
        functools.partial(_ffn_seq_kernel, tps),
        grid=(n // tm, nc),
        in_specs=[pl.BlockSpec((tm, d), lambda i, c: (i, 0)),
                  pl.BlockSpec((1, d), lambda i, c: (0, 0)),
                  pl.BlockSpec((d, tf), lambda i, c: (0, c)),
                  pl.BlockSpec((d, tf), lambda i, c: (0, nc + c)),
                  pl.BlockSpec((CONV_W, tf), lambda i, c: (0, c)),
                  pl.BlockSpec((1, tf), lambda i, c: (0, c)),
                  pl.BlockSpec((tf, d), lambda i, c: (c, 0))],
        out_specs=(pl.BlockSpec((tm, d), lambda i, c: (i, 0)),
                   pl.BlockSpec((1, conv_buf, tf), lambda i, c: (i, 0, c))),
        out_shape=(jax.ShapeDtypeStruct((n, d), F32),
                   jax.ShapeDtypeStruct((n // tm, conv_buf, dff), F32)),
        scratch_shapes=[pltpu.VMEM((tm, d), BF16), pltpu.VMEM((tm + CONV_HALO, tf), F32),
                        pltpu.VMEM((nc, CONV_HALO, tf), F32), pltpu.VMEM((tm, tf), BF16)],
        compiler_params=_params(("arbitrary", "arbitrary"), nbytes),
        name="ffn_seq",
    )(x, g.reshape(1, d), w_up, w_up, w_conv, b_conv.reshape(1, dff), w_down)
    return out, new_tail[tps - 1::tps]


def _ffn_step_kernel(x_ref, g_ref, wa_ref, wv_ref, wc_ref, bc_ref, wd_ref, p0_ref, p1_ref,
                     o_ref, a_ref, h_ref, acc_ref):
    @pl.when(pl.program_id(1) == 0)
    def _():
        h_ref[...] = _rmsnorm(x_ref[...], g_ref[...]).astype(BF16)

    a = _dot(h_ref[...], wa_ref[...])
    v = _dot(h_ref[...], wv_ref[...])
    conv = bc_ref[...] + wc_ref[2:3, :] * a + wc_ref[1:2, :] * p1_ref[...] + wc_ref[0:1, :] * p0_ref[...]
    a_ref[...] = a
    _ffn_gate_down(x_ref, conv, v, wd_ref, o_ref, acc_ref)


def _ffn_step(x, g, w_up, w_conv, b_conv, w_down, cache, tf):
    n, d = x.shape
    dff = w_down.shape[0]
    nc = dff // tf
    nbytes = 4 * n * d * 4 + 4 * d * tf * 2 + 2 * tf * d * 2 + n * d * 6 + 12 * n * tf * 4
    return pl.pallas_call(
        _ffn_step_kernel,
        grid=(1, nc),
        in_specs=[pl.BlockSpec((n, d), lambda i, c: (0, 0)),
                  pl.BlockSpec((1, d), lambda i, c: (0, 0)),
                  pl.BlockSpec((d, tf), lambda i, c: (0, c)),
                  pl.BlockSpec((d, tf), lambda i, c: (0, nc + c)),
                  pl.BlockSpec((CONV_W, tf), lambda i, c: (0, c)),
                  pl.BlockSpec((1, tf), lambda i, c: (0, c)),
                  pl.BlockSpec((tf, d), lambda i, c: (c, 0)),
                  pl.BlockSpec((n, tf), lambda i, c: (0, c)),
                  pl.BlockSpec((n, tf), lambda i, c: (0, nc + c))],
        out_specs=(pl.BlockSpec((n, d), lambda i, c: (0, 0)),
                   pl.BlockSpec((n, tf), lambda i, c: (0, c))),
        out_shape=(jax.ShapeDtypeStruct((n, d), F32),
                   jax.ShapeDtypeStruct((n, dff), F32)),
        scratch_shapes=[pltpu.VMEM((n, d), BF16), pltpu.VMEM((n, d), F32)],
        compiler_params=_params(("arbitrary", "arbitrary"), nbytes),
        name="ffn_step",
    )(x, g.reshape(1, d), w_up, w_up, w_conv, b_conv.reshape(1, dff), w_down, cache, cache)


def _ple_final_kernel(x_ref, p_ref, gp_ref, wg_ref, wp_ref, gf_ref, o_ref):
    x = x_ref[...]
    gate = jax.nn.sigmoid(_dot(_rmsnorm(x, gp_ref[...]).astype(BF16), wg_ref[...]))
    x = x + gate * _dot(p_ref[...].astype(BF16), wp_ref[...])
    o_ref[...] = _rmsnorm(x, gf_ref[...])


def _ple_final(x, p, g_ple, w_gate, w_ple, g_final, tm):
    n, d = x.shape
    dp = p.shape[1]
    nbytes = 2 * (2 * tm * d * 4 + tm * dp * 4 + d * d * 2 + dp * d * 2) + 4 * tm * d * 4
    return pl.pallas_call(
        _ple_final_kernel,
        grid=(n // tm,),
        in_specs=[pl.BlockSpec((tm, d), lambda i: (i, 0)),
                  pl.BlockSpec((tm, dp), lambda i: (i, 0)),
                  pl.BlockSpec((1, d), lambda i: (0, 0)),
                  pl.BlockSpec((d, d), lambda i: (0, 0)),
                  pl.BlockSpec((dp, d), lambda i: (0, 0)),
                  pl.BlockSpec((1, d), lambda i: (0, 0))],
        out_specs=pl.BlockSpec((tm, d), lambda i: (i, 0)),
        out_shape=jax.ShapeDtypeStruct((n, d), F32),
        compiler_params=_params(("arbitrary",), nbytes),
        name="ple_final",
    )(x, p, g_ple.reshape(1, d), w_gate, w_ple, g_final.reshape(1, d))


def kernel(x_prompt, x_sample, cache_pool, state_ssm_re, state_ssm_im, cache_conv, p_prompt, p_sample, g_mix, w_in, w_pool, pool_scale, ssm_lam_re, ssm_lam_im, ssm_log_dt, ssm_b_re, ssm_b_im, ssm_c_re, ssm_c_im, ssm_d, w_glu, w_branch_pool, w_branch_ssm, w_out, g_ffn, w_up, w_conv, b_conv, w_down, g_ple, w_ple_gate, w_ple, g_final):
    depth = g_mix.shape[0]
    nb, t_len, d = x_prompt.shape
    ns = x_sample.shape[0]
    assert x_sample.shape[1] == 1, "the sample group advances one step per call"
    pool_buf, d_pool = cache_pool.shape[2], cache_pool.shape[3]
    conv_buf, d_ff = cache_conv.shape[2], cache_conv.shape[3]
    n_grp, n_state = ssm_lam_re.shape[1], ssm_lam_re.shape[2]
    d_ssm = ssm_d.shape[1]
    assert pool_buf == max(POOL_WINDOWS) - 1 and conv_buf == CONV_W - 1
    assert n_state == SSM_STATE and d_ssm == n_grp * SSM_GROUP

    xp = x_prompt.reshape(nb * t_len, d)
    xs = x_sample.reshape(ns, d)
    outs = [[] for _ in range(8)]
    for i in range(depth):
        w_in_b, w_pool_b, w_ple_b = w_in[i].astype(BF16), w_pool[i].astype(BF16), w_ple[i].astype(BF16)
        bq, cq, a_pow_re, a_pow_im, a_row = _ssm_params(ssm_lam_re[i], ssm_lam_im[i], ssm_log_dt[i],
                                                        ssm_b_re[i], ssm_b_im[i], ssm_c_re[i], ssm_c_im[i])

        z = _norm_matmul(xp, g_mix[i], w_in_b, 1024, 3 * W_TILE)
        gy, st_re, st_im, (w_up_b, w_down_b, w_out_b, w_pg_b, w_glu_b, w_bs_b, w_bp_b) = _ssm_seq(
            z, d_pool, bq, cq, a_pow_re, a_pow_im, ssm_d[i], nb, t_len,
            [w_up[i], w_down[i], w_out[i], w_ple_gate[i], w_glu[i], w_branch_ssm[i], w_branch_pool[i]])

        def mix_tail(x, z, bp, gy, tm):
            return _mix_out(gy, bp, z, d_pool + d_ssm, x, w_glu_b, w_bs_b, w_out_b, tm)

        bp, pool_new = _pool_seq(z, w_pool_b, pool_scale[i], w_bp_b, nb, t_len, 1024, pool_buf)
        xp = mix_tail(xp, z, bp, gy, 256)
        xp, conv_new = _ffn_seq(xp, g_ffn[i], w_up_b, w_conv[i], b_conv[i], w_down_b, nb, t_len, 1024, W_TILE,
                                conv_buf)
        xp_out = _ple_final(xp, p_prompt[i].reshape(nb * t_len, -1), g_ple[i], w_pg_b, w_ple_b, g_final, 512)
        for lst, val in zip(outs[:4], (pool_new, st_re.reshape(nb, n_grp, n_state),
                                       st_im.reshape(nb, n_grp, n_state), conv_new)):
            lst.append(val)

        z = _norm_matmul(xs, g_mix[i], w_in_b, ns, 3 * W_TILE)
        bp, pool_new = _pool_step(z, cache_pool[i].reshape(ns, pool_buf * d_pool), w_pool_b, pool_scale[i], w_bp_b)
        gy, st_re, st_im = _ssm_step(z, d_pool, state_ssm_re[i].reshape(ns, -1), state_ssm_im[i].reshape(ns, -1),
                                     bq, cq, a_row, ssm_d[i])
        xs = mix_tail(xs, z, bp, gy, ns)
        xs, a_new = _ffn_step(xs, g_ffn[i], w_up_b, w_conv[i], b_conv[i], w_down_b,
                              cache_conv[i].reshape(ns, conv_buf * d_ff), 2 * W_TILE)
        conv_new = jnp.concatenate([cache_conv[i][:, 1:], a_new[:, None, :]], axis=1)
        xs_out = _ple_final(xs, p_sample[i].reshape(ns, -1), g_ple[i], w_pg_b, w_ple_b, g_final, ns)
        for lst, val in zip(outs[4:], (pool_new.reshape(ns, pool_buf, d_pool), st_re.reshape(ns, n_grp, n_state),
                                       st_im.reshape(ns, n_grp, n_state), conv_new)):
            lst.append(val)

    assert depth == 1
    y_prompt = xp_out.reshape(nb, t_len, d)
    y_sample = xs_out.reshape(ns, 1, d)
    return (y_prompt, y_sample) + tuple(jnp.stack(o, axis=0) for o in outs)
```

```python
import functools

import jax
import jax.numpy as jnp
from jax import lax
from jax.experimental import pallas as pl
from jax.experimental.pallas import tpu as pltpu

F32 = jnp.float32
BF16 = jnp.bfloat16

EPS = 1e-6
POOL_WINDOWS = (2, 4, 8, 16)
POOL_HALO = 16
SSM_GROUP = 16
SSM_STATE = 64
SSM_BLOCK_GROUPS = 16
LANES = 128
SUBLANES = 8
SEG_LEN = 128
SEG_PITCH = SEG_LEN + 8
CONV_W = 3
CONV_HALO = 8
FFN_SLICE = 256
W_TILE = 512
V7X_VMEM_BYTES = 64 * 1024 * 1024


def _vmem_limit(nbytes):
    return int(min(nbytes * 1.25 + (8 << 20), V7X_VMEM_BYTES - (6 << 20)))


def _params(sem, nbytes):
    return pltpu.CompilerParams(dimension_semantics=sem, vmem_limit_bytes=_vmem_limit(nbytes))


def _rmsnorm(x, g):
    return x * lax.rsqrt(jnp.mean(x * x, axis=-1, keepdims=True) + EPS) * g


def _dot(a, b):
    return jnp.dot(a, b, preferred_element_type=F32)


def _abar(lr, li, logdt):
    dt = jnp.exp(logdt)
    mag = jnp.exp(lr * dt)
    return mag * jnp.cos(li * dt), mag * jnp.sin(li * dt)


def _ssm_params_kernel(lr_ref, li_ref, logdt_ref, lrs_ref, lis_ref, logdts_ref, br_ref, bi_ref,
                       pwr_ref, pwi_ref, bbr_ref, bbi_ref):
    lr = lr_ref[...]
    li = li_ref[...]
    a_re, a_im = _abar(lr, li, logdt_ref[...])
    nr = a_re - 1.0
    ni = a_im
    den = lr * lr + li * li
    coef_re = ((nr * lr + ni * li) / den)[:, None, :]
    coef_im = ((ni * lr - nr * li) / den)[:, None, :]
    br = br_ref[...]
    bi = bi_ref[...]
    bbr_ref[...] = coef_re * br - coef_im * bi
    bbi_ref[...] = coef_re * bi + coef_im * br
    a_re, a_im = _abar(lrs_ref[...], lis_ref[...], logdts_ref[...])
    nkb, _, per_blk, _ = pwr_ref.shape
    pr, pi = a_re, a_im
    for n in range(SEG_LEN):
        if n:
            pr, pi = pr * a_re - pi * a_im, pr * a_im + pi * a_re
        for k in range(nkb):
            pwr_ref[k, n] = pr[k * per_blk:(k + 1) * per_blk, :]
            pwi_ref[k, n] = pi[k * per_blk:(k + 1) * per_blk, :]


def _ssm_params(lam_re, lam_im, log_dt, b_re, b_im, c_re, c_im):
    g, p = lam_re.shape
    h = b_re.shape[-1]
    nkb = g // SSM_BLOCK_GROUPS
    bl = SSM_BLOCK_GROUPS
    per_blk = bl * p // LANES
    slab = (g * p // LANES, LANES)
    pwr, pwi, bbr, bbi = pl.pallas_call(
        _ssm_params_kernel,
        out_shape=(jax.ShapeDtypeStruct((nkb, SEG_LEN, per_blk, LANES), F32),
                   jax.ShapeDtypeStruct((nkb, SEG_LEN, per_blk, LANES), F32),
                   jax.ShapeDtypeStruct((g, h, p), F32),
                   jax.ShapeDtypeStruct((g, h, p), F32)),
        name="ssm_params",
    )(lam_re, lam_im, log_dt.reshape(g, 1),
      lam_re.reshape(slab), lam_im.reshape(slab), jnp.broadcast_to(log_dt[:, None], (g, p)).reshape(slab),
      jnp.transpose(b_re, (0, 2, 1)), jnp.transpose(b_im, (0, 2, 1)))

    def c_rows(c):
        return jnp.transpose(c.reshape(nkb, bl * h, p), (0, 2, 1))

    bq = jnp.stack([bbr.reshape(nkb, bl * h, p), bbi.reshape(nkb, bl * h, p)], axis=1)
    cq = jnp.stack([c_rows(c_re), -c_rows(c_im)], axis=1)
    a_row = jnp.concatenate([pwr[:, 0].reshape(nkb, 1, bl * p), pwi[:, 0].reshape(nkb, 1, bl * p)], axis=-1)
    return bq, cq, pwr, pwi, a_row


def _expand_blockdiag(bq_ref, cq_ref, bcat_ref, ccat_ref):
    kw, p = bq_ref.shape[2], bq_ref.shape[3]
    half = bcat_ref.shape[1] // 2
    hch = kw * p // half
    log_p, log_h = p.bit_length() - 1, hch.bit_length() - 1
    assert (1 << log_p) == p and (1 << log_h) == hch

    def iota(shape, axis):
        return lax.broadcasted_iota(jnp.int32, shape, axis)

    def same(a, b):
        return a == b

    tile_b = same(iota((p, half), 1) & (p - 1), iota((p, half), 0)).astype(BF16)
    mask_b = same(iota((kw, half), 0) >> log_h, iota((kw, half), 1) >> log_p)
    tile_c = same(iota((half, p), 0) & (p - 1), iota((half, p), 1)).astype(BF16)
    mask_c = same(iota((half, kw), 0) >> log_p, iota((half, kw), 1) >> log_h)
    for s in range(2):
        full = _dot(bq_ref[0, s].astype(BF16), tile_b)
        bcat_ref[:, s * half:(s + 1) * half] = jnp.where(mask_b, full, 0.0).astype(BF16)
        full = _dot(tile_c, cq_ref[0, s].astype(BF16))
        ccat_ref[s * half:(s + 1) * half, :] = jnp.where(mask_c, full, 0.0).astype(BF16)


def _norm_matmul_kernel(x_ref, g_ref, w_ref, o_ref, *rest):
    *wb_ref, h_ref = rest

    @pl.when(pl.program_id(1) == 0)
    def _():
        h_ref[...] = _rmsnorm(x_ref[...], g_ref[...]).astype(BF16)

    w = w_ref[...]
    if wb_ref:
        w = w.astype(BF16)
        wb_ref[0][...] = w
    o_ref[...] = _dot(h_ref[...], w)


def _norm_matmul(x, g, w, tm, tn):
    n, d = x.shape
    dout = w.shape[1]
    convert = w.dtype == F32
    assert not convert or n == tm, "each weight tile must be visited exactly once to be converted"
    nbytes = 2 * tm * d * 4 + tm * d * 2 + 2 * d * tn * w.dtype.itemsize + 2 * tm * tn * 4 + convert * 3 * d * tn * 2
    z_spec = pl.BlockSpec((tm, tn), lambda i, j: (i, j))
    w_spec = pl.BlockSpec((d, tn), lambda i, j: (0, j))
    out = pl.pallas_call(
        _norm_matmul_kernel,
        grid=(n // tm, dout // tn),
        in_specs=[pl.BlockSpec((tm, d), lambda i, j: (i, 0)),
                  pl.BlockSpec((1, d), lambda i, j: (0, 0)),
                  w_spec],
        out_specs=[z_spec] + [w_spec] * convert,
        out_shape=[jax.ShapeDtypeStruct((n, dout), F32)] + [jax.ShapeDtypeStruct(w.shape, BF16)] * convert,
        scratch_shapes=[pltpu.VMEM((tm, d), BF16)],
        compiler_params=_params(("arbitrary", "arbitrary"), nbytes),
        name="norm_matmul",
    )(x, g.reshape(1, d), w)
    return out if convert else out[0]


def _pool_project(diffs, wp_ref, scale_ref, wb_ref, y_ref):
    gw = wp_ref.shape[1]
    for k, diff in enumerate(diffs):
        yk = _dot(diff.astype(BF16), wp_ref[k]) * scale_ref[:, k * gw:(k + 1) * gw]
        y_ref[:, k * gw:(k + 1) * gw] = yk.astype(BF16)
    return _dot(y_ref[...], wb_ref[...])


def _pool_seq_kernel(u_ref, wp_ref, scale_ref, wb_ref, o_ref, new_ref, ext_ref, y_ref):
    tc, dp = u_ref.shape
    gw = wp_ref.shape[1]
    t = pl.program_id(1)

    @pl.when(t == 0)
    def _():
        ext_ref[0:POOL_HALO, :] = jnp.zeros((POOL_HALO, dp), F32)

    ext_ref[POOL_HALO:POOL_HALO + tc, :] = u_ref[...]
    pos = (t * tc + 1 + lax.broadcasted_iota(jnp.int32, (tc, 1), 0)).astype(F32)
    diffs = []
    for k, w in enumerate(POOL_WINDOWS):
        cols = slice(k * gw, (k + 1) * gw)
        u = ext_ref[POOL_HALO:POOL_HALO + tc, cols]
        s = u
        for j in range(1, w):
            s = s + ext_ref[POOL_HALO - j:POOL_HALO - j + tc, cols]
        count = jnp.minimum(pos, float(w))
        diffs.append(s / count - u)
    o_ref[...] = _pool_project(diffs, wp_ref, scale_ref, wb_ref, y_ref)
    nb = new_ref.shape[1]
    new_ref[0] = ext_ref[POOL_HALO + tc - nb:POOL_HALO + tc, :]
    ext_ref[0:POOL_HALO, :] = ext_ref[tc:tc + POOL_HALO, :]


def _pool_seq(z, w_pool, pool_scale, w_branch, nb, t_len, tc, pool_buf):
    n = z.shape[0]
    ng, gw, _ = w_pool.shape
    dp = ng * gw
    dm = w_branch.shape[1]
    nt = t_len // tc
    nbytes = (2 * tc * dp * 4 + 2 * ng * gw * gw * 2 + 2 * dp * dm * 2 + 2 * tc * dm * 4
              + (tc + POOL_HALO) * dp * 4 + tc * dp * 2 + 4 * tc * gw * 4)
    return pl.pallas_call(
        _pool_seq_kernel,
        grid=(nb, nt),
        in_specs=[pl.BlockSpec((tc, dp), lambda b, t: (b * nt + t, 0)),
                  pl.BlockSpec((ng, gw, gw), lambda b, t: (0, 0, 0)),
                  pl.BlockSpec((1, dp), lambda b, t: (0, 0)),
                  pl.BlockSpec((dp, dm), lambda b, t: (0, 0))],
        out_specs=(pl.BlockSpec((tc, dm), lambda b, t: (b * nt + t, 0)),
                   pl.BlockSpec((1, pool_buf, dp), lambda b, t: (b, 0, 0))),
        out_shape=(jax.ShapeDtypeStruct((n, dm), F32),
                   jax.ShapeDtypeStruct((nb, pool_buf, dp), F32)),
        scratch_shapes=[pltpu.VMEM((tc + POOL_HALO, dp), F32), pltpu.VMEM((tc, dp), BF16)],
        compiler_params=_params(("arbitrary", "arbitrary"), nbytes),
        name="pool_seq",
    )(z, w_pool, pool_scale.reshape(1, dp), w_branch)


def _pool_step_kernel(u_ref, cache_ref, wp_ref, scale_ref, wb_ref, o_ref, y_ref):
    dp = u_ref.shape[1]
    gw = wp_ref.shape[1]
    lb = cache_ref.shape[1] // dp
    diffs = []
    for k, w in enumerate(POOL_WINDOWS):
        u = u_ref[:, k * gw:(k + 1) * gw]
        s = u
        for j in range(1, w):
            s = s + cache_ref[:, (lb - j) * dp + k * gw:(lb - j) * dp + (k + 1) * gw]
        diffs.append(s / float(w) - u)
    o_ref[...] = _pool_project(diffs, wp_ref, scale_ref, wb_ref, y_ref)


def _pool_step(z, cache, w_pool, pool_scale, w_branch):
    n = z.shape[0]
    ng, gw, _ = w_pool.shape
    dp = ng * gw
    dm = w_branch.shape[1]
    lbdp = cache.shape[1]
    nbytes = 2 * (n * dp * 4 + n * lbdp * 4 + ng * gw * gw * 2 + dp * dm * 2 + n * dm * 4) + n * dp * 2
    return pl.pallas_call(
        _pool_step_kernel,
        grid=(1,),
        in_specs=[pl.BlockSpec((n, dp), lambda i: (0, 0)),
                  pl.BlockSpec((n, lbdp), lambda i: (0, 0)),
                  pl.BlockSpec((ng, gw, gw), lambda i: (0, 0, 0)),
                  pl.BlockSpec((1, dp), lambda i: (0, 0)),
                  pl.BlockSpec((dp, dm), lambda i: (0, 0))],
        out_specs=pl.BlockSpec((n, dm), lambda i: (0, 0)),
        out_shape=jax.ShapeDtypeStruct((n, dm), F32),
        scratch_shapes=[pltpu.VMEM((n, dp), BF16)],
        compiler_params=_params(("arbitrary",), nbytes),
        name="pool_step",
    )(z, cache, w_pool, pool_scale.reshape(1, dp), w_branch)


def _cmul_add(xr, xi, ar, ai, br, bi):
    return xr + ar * br - ai * bi, xi + ar * bi + ai * br


def _ssm_seq_kernel(n_cast, u_ref, bq_ref, cq_ref, pwr_ref, pwi_ref, d_ref, *refs):
    cast_in, (gy_ref, st_ref), refs = refs[:n_cast], refs[n_cast:n_cast + 2], refs[n_cast + 2:]
    cast_out, refs = refs[:n_cast], refs[n_cast:]
    bcat_ref, ccat_ref, us_ref, up_ref, h_ref, hb_ref, ys_ref, seed_ref, carry_ref = refs
    for src_ref, dst_ref in zip(cast_in, cast_out):
        dst_ref[...] = src_ref[...].astype(BF16)

    kw = u_ref.shape[1]
    nslab = h_ref.shape[1] // LANES
    npair = nslab // 2
    t = pl.program_id(2)

    def lanes(j):
        return slice(j * LANES, (j + 1) * LANES)

    def step_rows(n, count=1):
        return slice(n * SUBLANES, (n + count) * SUBLANES)

    def apow(n, j):
        return pwr_ref[0, n, j:j + 1, :], pwi_ref[0, n, j:j + 1, :]

    @pl.when(jnp.logical_and(pl.program_id(1) == 0, t == 0))
    def _():
        _expand_blockdiag(bq_ref, cq_ref, bcat_ref, ccat_ref)

    @pl.when(t == 0)
    def _():
        carry_ref[...] = jnp.zeros_like(carry_ref)

    for k in range(kw // LANES):
        for r in range(SUBLANES):
            us_ref[k, r * SEG_PITCH:r * SEG_PITCH + SEG_LEN, :] = u_ref[r * SEG_LEN:(r + 1) * SEG_LEN, lanes(k)]
    for n in range(SEG_LEN):
        for k in range(kw // LANES):
            up_ref[step_rows(n), lanes(k)] = us_ref[k, pl.ds(n, SUBLANES, stride=SEG_PITCH), :]
    u = up_ref[...]
    h_ref[...] = _dot(u.astype(BF16), bcat_ref[...])

    state = [jnp.zeros((SUBLANES, LANES), F32)] * nslab
    for n in range(SEG_LEN):
        for j in range(npair):
            sr, si = _cmul_add(h_ref[step_rows(n), lanes(j)], h_ref[step_rows(n), lanes(npair + j)],
                               *apow(0, j), state[j], state[npair + j])
            h_ref[step_rows(n), lanes(j)] = sr
            h_ref[step_rows(n), lanes(npair + j)] = si
            state[j], state[npair + j] = sr, si

    for j in range(npair):
        cr, ci = carry_ref[j], carry_ref[npair + j]
        for r in range(SUBLANES):
            seed_ref[j, r:r + 1, :] = cr
            seed_ref[npair + j, r:r + 1, :] = ci
            cr, ci = _cmul_add(state[j][r:r + 1, :], state[npair + j][r:r + 1, :],
                               *apow(SEG_LEN - 1, j), cr, ci)
        carry_ref[j] = cr
        carry_ref[npair + j] = ci
    st_ref[0, 0] = carry_ref[...]

    @pl.when(t >= 0)
    def _():
        for i in range(SEG_LEN // 2):
            for j in range(npair):
                parts = []
                for n in (2 * i, 2 * i + 1):
                    parts.append(_cmul_add(h_ref[step_rows(n), lanes(j)], h_ref[step_rows(n), lanes(npair + j)],
                                           *apow(n, j), seed_ref[j], seed_ref[npair + j]))
                hb_ref[step_rows(2 * i, 2), lanes(j)] = (
                    jnp.concatenate([p[0] for p in parts], axis=0).astype(BF16))
                hb_ref[step_rows(2 * i, 2), lanes(npair + j)] = (
                    jnp.concatenate([p[1] for p in parts], axis=0).astype(BF16))

        y = _dot(hb_ref[...], ccat_ref[...]) + d_ref[...] * up_ref[...]
        g = jax.nn.gelu(y)
        for n in range(SEG_LEN):
            for k in range(kw // LANES):
                ys_ref[k, pl.ds(n, SUBLANES, stride=SEG_PITCH), :] = g[step_rows(n), lanes(k)]
        for k in range(kw // LANES):
            for r in range(SUBLANES):
                gy_ref[r * SEG_LEN:(r + 1) * SEG_LEN, lanes(k)] = (
                    ys_ref[k, r * SEG_PITCH:r * SEG_PITCH + SEG_LEN, :].astype(BF16))


def _ssm_seq(z, col0, bq, cq, pwr, pwi, d_skip, nb, t_len, side_casts):
    n = z.shape[0]
    nkb, _, kw, p = bq.shape
    sw = 2 * SSM_BLOCK_GROUPS * p
    nslab = sw // LANES
    tc = SUBLANES * SEG_LEN
    nt = t_len // tc
    cb = col0 // kw
    n_steps = nkb * nb * nt
    cast_rows = [w.shape[0] // n_steps for w in side_casts]
    assert all(r % (2 * SUBLANES) == 0 and r * n_steps == w.shape[0] for r, w in zip(cast_rows, side_casts))
    nbytes = (2 * tc * kw * 4 + 4 * kw * sw * 2 + 2 * (SEG_LEN + SUBLANES) * sw * 4 + 2 * tc * kw * 2
              + 2 * SUBLANES * SEG_PITCH * kw * 4 + tc * kw * 4 + tc * sw * 4 + tc * sw * 2 + tc * sw * 4
              + sum(2 * r * w.shape[1] * 6 for r, w in zip(cast_rows, side_casts)))

    def cast_spec(r, w):
        return pl.BlockSpec((r, w.shape[1]), lambda k, b, t: ((k * nb + b) * nt + t, 0))

    cast_specs = [cast_spec(r, w) for r, w in zip(cast_rows, side_casts)]
    gy, st, *cast = pl.pallas_call(
        functools.partial(_ssm_seq_kernel, len(side_casts)),
        grid=(nkb, nb, nt),
        in_specs=[pl.BlockSpec((tc, kw), lambda k, b, t: (b * nt + t, cb + k)),
                  pl.BlockSpec((1, 2, kw, p), lambda k, b, t: (k, 0, 0, 0)),
                  pl.BlockSpec((1, 2, p, kw), lambda k, b, t: (k, 0, 0, 0)),
                  pl.BlockSpec((1, SEG_LEN, nslab // 2, LANES), lambda k, b, t: (k, 0, 0, 0)),
                  pl.BlockSpec((1, SEG_LEN, nslab // 2, LANES), lambda k, b, t: (k, 0, 0, 0)),
                  pl.BlockSpec((1, kw), lambda k, b, t: (0, k))] + cast_specs,
        out_specs=[pl.BlockSpec((tc, kw), lambda k, b, t: (b * nt + t, k)),
                   pl.BlockSpec((1, 1, nslab, 1, LANES), lambda k, b, t: (b, k, 0, 0, 0))] + cast_specs,
        out_shape=[jax.ShapeDtypeStruct((n, nkb * kw), BF16),
                   jax.ShapeDtypeStruct((nb, nkb, nslab, 1, LANES), F32)]
                  + [jax.ShapeDtypeStruct(w.shape, BF16) for w in side_casts],
        scratch_shapes=[pltpu.VMEM((kw, sw), BF16), pltpu.VMEM((sw, kw), BF16),
                        pltpu.VMEM((kw // LANES, SUBLANES * SEG_PITCH, LANES), F32), pltpu.VMEM((tc, kw), F32),
                        pltpu.VMEM((tc, sw), F32), pltpu.VMEM((tc, sw), BF16),
                        pltpu.VMEM((kw // LANES, SUBLANES * SEG_PITCH, LANES), F32),
                        pltpu.VMEM((nslab, SUBLANES, LANES), F32), pltpu.VMEM((nslab, 1, LANES), F32)],
        compiler_params=_params(("arbitrary", "arbitrary", "arbitrary"), nbytes),
        name="ssm_seq",
    )(z, bq, cq, pwr, pwi, d_skip.reshape(1, nkb * kw), *side_casts)
    st = st.reshape(nb, nkb, 2, sw // 2)
    return gy, st[:, :, 0].reshape(nb, -1), st[:, :, 1].reshape(nb, -1), cast


def _ssm_step_kernel(u_ref, h0r_ref, h0i_ref, bq_ref, cq_ref, a_ref, d_ref,
                     gy_ref, h1r_ref, h1i_ref, h_ref, bcat_ref, ccat_ref):
    half = h0r_ref.shape[1]
    _expand_blockdiag(bq_ref, cq_ref, bcat_ref, ccat_ref)
    u = u_ref[...]
    bu = _dot(u.astype(BF16), bcat_ref[...])
    hr, hi = _cmul_add(bu[:, 0:half], bu[:, half:2 * half],
                       a_ref[0, :, 0:half], a_ref[0, :, half:2 * half],
                       h0r_ref[...], h0i_ref[...])
    h1r_ref[...] = hr
    h1i_ref[...] = hi
    h_ref[:, 0:half] = hr.astype(BF16)
    h_ref[:, half:2 * half] = hi.astype(BF16)
    y = _dot(h_ref[...], ccat_ref[...]) + d_ref[...] * u
    gy_ref[...] = jax.nn.gelu(y).astype(BF16)


def _ssm_step(z, col0, h0_re, h0_im, bq, cq, a_row, d_skip):
    n = z.shape[0]
    nkb, _, kw, p = bq.shape
    half = SSM_BLOCK_GROUPS * p
    sw = 2 * half
    cb = col0 // kw
    nbytes = 2 * (n * kw * 4 + 4 * n * half * 4 + 2 * kw * sw * 2 + sw * 4 + n * kw * 2) + n * sw * 6
    return pl.pallas_call(
        _ssm_step_kernel,
        grid=(nkb,),
        in_specs=[pl.BlockSpec((n, kw), lambda k: (0, cb + k)),
                  pl.BlockSpec((n, half), lambda k: (0, k)),
                  pl.BlockSpec((n, half), lambda k: (0, k)),
                  pl.BlockSpec((1, 2, kw, p), lambda k: (k, 0, 0, 0)),
                  pl.BlockSpec((1, 2, p, kw), lambda k: (k, 0, 0, 0)),
                  pl.BlockSpec((1, 1, sw), lambda k: (k, 0, 0)),
                  pl.BlockSpec((1, kw), lambda k: (0, k))],
        out_specs=(pl.BlockSpec((n, kw), lambda k: (0, k)),
                   pl.BlockSpec((n, half), lambda k: (0, k)),
                   pl.BlockSpec((n, half), lambda k: (0, k))),
        out_shape=(jax.ShapeDtypeStruct((n, nkb * kw), BF16),
                   jax.ShapeDtypeStruct((n, nkb * half), F32),
                   jax.ShapeDtypeStruct((n, nkb * half), F32)),
        scratch_shapes=[pltpu.VMEM((n, sw), BF16), pltpu.VMEM((kw, sw), BF16), pltpu.VMEM((sw, kw), BF16)],
        compiler_params=_params(("arbitrary",), nbytes),
        name="ssm_step",
    )(z, h0_re, h0_im, bq, cq, a_row, d_skip.reshape(1, nkb * kw))


def _mix_out_kernel(gy_ref, bp_ref, gp_ref, gs_ref, x_ref, wg_ref, wb_ref, wo_ref, o_ref):
    ds = wb_ref.shape[0]
    g = _dot(gy_ref[...], wg_ref[...])
    y = g[:, 0:ds] * jax.nn.sigmoid(g[:, ds:2 * ds])
    bs = _dot(y.astype(BF16), wb_ref[...])
    merged = jax.nn.sigmoid(gp_ref[...]) * bp_ref[...] + jax.nn.sigmoid(gs_ref[...]) * bs
    o_ref[...] = x_ref[...] + _dot(merged.astype(BF16), wo_ref[...])


def _mix_out(gy, bp, z, gate_col0, x, w_glu, w_branch, w_out, tm):
    n, ds = gy.shape
    dm = w_branch.shape[1]
    gb = gate_col0 // dm
    w_bytes = (ds * 2 * ds + ds * dm + dm * dm) * 2
    nbytes = 2 * (tm * ds * 2 + 5 * tm * dm * 4) + w_bytes + 6 * tm * dm * 4

    def resident(shape):
        return pl.BlockSpec(shape, lambda i: (0, 0), pipeline_mode=pl.Buffered(1))

    return pl.pallas_call(
        _mix_out_kernel,
        grid=(n // tm,),
        in_specs=[pl.BlockSpec((tm, ds), lambda i: (i, 0)),
                  pl.BlockSpec((tm, dm), lambda i: (i, 0)),
                  pl.BlockSpec((tm, dm), lambda i: (i, gb)),
                  pl.BlockSpec((tm, dm), lambda i: (i, gb + 1)),
                  pl.BlockSpec((tm, dm), lambda i: (i, 0)),
                  resident((ds, 2 * ds)), resident((ds, dm)), resident((dm, dm))],
        out_specs=pl.BlockSpec((tm, dm), lambda i: (i, 0)),
        out_shape=jax.ShapeDtypeStruct((n, dm), F32),
        compiler_params=_params(("arbitrary",), nbytes),
        name="mix_out",
    )(gy, bp, z, z, x, w_glu, w_branch, w_out)


def _ffn_gate_down(x_ref, conv, v, wd_ref, o_ref, acc_ref):
    c = pl.program_id(1)
    part = _dot((jax.nn.gelu(conv) * v).astype(BF16), wd_ref[...])

    @pl.when(c == 0)
    def _():
        acc_ref[...] = part

    @pl.when(c > 0)
    def _():
        acc_ref[...] += part

    @pl.when(c == pl.num_programs(1) - 1)
    def _():
        o_ref[...] = x_ref[...] + acc_ref[...]


def _ffn_seq_kernel(tiles_per_seq, x_ref, g_ref, wa_ref, wv_ref, wc_ref, bc_ref, wd_ref,
                    o_ref, new_ref, h_ref, ext_ref, carry_ref, gate_ref):
    tm = x_ref.shape[0]
    tf = gate_ref.shape[1]
    i = pl.program_id(0)
    c = pl.program_id(1)

    @pl.when(jnp.logical_and(i == 0, c == 0))
    def _():
        o_ref[...] = jnp.zeros(o_ref.shape, F32)
        carry_ref[...] = jnp.zeros(carry_ref.shape, F32)

    @pl.when(c == 0)
    def _():
        h_ref[...] = _rmsnorm(x_ref[...], g_ref[...]).astype(BF16)

    seq_start = i % tiles_per_seq == 0
    nb = new_ref.shape[1]
    for q in range(tf // FFN_SLICE):
        cols = slice(q * FFN_SLICE, (q + 1) * FFN_SLICE)
        a = _dot(h_ref[...], wa_ref[:, cols])
        v = _dot(h_ref[...], wv_ref[:, cols])
        ext_ref[0:CONV_HALO, cols] = jnp.where(seq_start, 0.0, carry_ref[c, :, cols])
        ext_ref[CONV_HALO:CONV_HALO + tm, cols] = a
        conv = bc_ref[:, cols] + wc_ref[CONV_W - 1:CONV_W, cols] * a
        for j in range(CONV_W - 1):
            off = CONV_HALO - (CONV_W - 1) + j
            conv = conv + wc_ref[j:j + 1, cols] * ext_ref[off:off + tm, cols]
        carry_ref[c, :, cols] = ext_ref[tm:tm + CONV_HALO, cols]
        new_ref[0, :, cols] = ext_ref[CONV_HALO + tm - nb:CONV_HALO + tm, cols]
        gate_ref[:, cols] = (jax.nn.gelu(conv) * v).astype(BF16)
    o_ref[...] = _dot(gate_ref[...], wd_ref[...]) + jnp.where(c == 0, x_ref[...], o_ref[...])


def _ffn_seq(x, g, w_up, w_conv, b_conv, w_down, nb, t_len, tm, tf, conv_buf):
    n, d = x.shape
    dff = w_down.shape[0]
    nc = dff // tf
    tps = t_len // tm
    nbytes = (4 * tm * d * 4 + 4 * d * tf * 2 + 2 * tf * d * 2 + tm * d * 2 + tm * tf * 2
              + (tm + CONV_HALO) * tf * 4 + nc * CONV_HALO * tf * 4 + 6 * tm * FFN_SLICE * 4)
    out, new_tail = pl.pallas_call(
        functools.partial(_ffn_seq_kernel, tps),
        grid=(n // tm, nc),
        in_specs=[pl.BlockSpec((tm, d), lambda i, c: (i, 0)),
                  pl.BlockSpec((1, d), lambda i, c: (0, 0)),
                  pl.BlockSpec((d, tf), lambda i, c: (0, c)),
                  pl.BlockSpec((d, tf), lambda i, c: (0, nc + c)),
                  pl.BlockSpec((CONV_W, tf), lambda i, c: (0, c)),
                  pl.BlockSpec((1, tf), lambda i, c: (0, c)),
                  pl.BlockSpec((tf, d), lambda i, c: (c, 0))],
        out_specs=(pl.BlockSpec((tm, d), lambda i, c: (i, 0)),
                   pl.BlockSpec((1, conv_buf, tf), lambda i, c: (i, 0, c))),
        out_shape=(jax.ShapeDtypeStruct((n, d), F32),
                   jax.ShapeDtypeStruct((n // tm, conv_buf, dff), F32)),
        scratch_shapes=[pltpu.VMEM((tm, d), BF16), pltpu.VMEM((tm + CONV_HALO, tf), F32),
                        pltpu.VMEM((nc, CONV_HALO, tf), F32), pltpu.VMEM((tm, tf), BF16)],
        compiler_params=_params(("arbitrary", "arbitrary"), nbytes),
        name="ffn_seq",
    )(x, g.reshape(1, d), w_up, w_up, w_conv, b_conv.reshape(1, dff), w_down)
    return out, new_tail[tps - 1::tps]


def _ffn_step_kernel(x_ref, g_ref, wa_ref, wv_ref, wc_ref, bc_ref, wd_ref, p0_ref, p1_ref,
                     o_ref, a_ref, h_ref, acc_ref):
    @pl.when(pl.program_id(1) == 0)
    def _():
        h_ref[...] = _rmsnorm(x_ref[...], g_ref[...]).astype(BF16)

    a = _dot(h_ref[...], wa_ref[...])
    v = _dot(h_ref[...], wv_ref[...])
    conv = bc_ref[...] + wc_ref[2:3, :] * a + wc_ref[1:2, :] * p1_ref[...] + wc_ref[0:1, :] * p0_ref[...]
    a_ref[...] = a
    _ffn_gate_down(x_ref, conv, v, wd_ref, o_ref, acc_ref)


def _ffn_step(x, g, w_up, w_conv, b_conv, w_down, cache, tf):
    n, d = x.shape
    dff = w_down.shape[0]
    nc = dff // tf
    nbytes = 4 * n * d * 4 + 4 * d * tf * 2 + 2 * tf * d * 2 + n * d * 6 + 12 * n * tf * 4
    return pl.pallas_call(
        _ffn_step_kernel,
        grid=(1, nc),
        in_specs=[pl.BlockSpec((n, d), lambda i, c: (0, 0)),
                  pl.BlockSpec((1, d), lambda i, c: (0, 0)),
                  pl.BlockSpec((d, tf), lambda i, c: (0, c)),
                  pl.BlockSpec((d, tf), lambda i, c: (0, nc + c)),
                  pl.BlockSpec((CONV_W, tf), lambda i, c: (0, c)),
                  pl.BlockSpec((1, tf), lambda i, c: (0, c)),
                  pl.BlockSpec((tf, d), lambda i, c: (c, 0)),
                  pl.BlockSpec((n, tf), lambda i, c: (0, c)),
                  pl.BlockSpec((n, tf), lambda i, c: (0, nc + c))],
        out_specs=(pl.BlockSpec((n, d), lambda i, c: (0, 0)),
                   pl.BlockSpec((n, tf), lambda i, c: (0, c))),
        out_shape=(jax.ShapeDtypeStruct((n, d), F32),
                   jax.ShapeDtypeStruct((n, dff), F32)),
        scratch_shapes=[pltpu.VMEM((n, d), BF16), pltpu.VMEM((n, d), F32)],
        compiler_params=_params(("arbitrary", "arbitrary"), nbytes),
        name="ffn_step",
    )(x, g.reshape(1, d), w_up, w_up, w_conv, b_conv.reshape(1, dff), w_down, cache, cache)


def _ple_final_kernel(x_ref, p_ref, gp_ref, wg_ref, wp_ref, gf_ref, o_ref):
    x = x_ref[...]
    gate = jax.nn.sigmoid(_dot(_rmsnorm(x, gp_ref[...]).astype(BF16), wg_ref[...]))
    x = x + gate * _dot(p_ref[...].astype(BF16), wp_ref[...])
    o_ref[...] = _rmsnorm(x, gf_ref[...])


def _ple_final(x, p, g_ple, w_gate, w_ple, g_final, tm):
    n, d = x.shape
    dp = p.shape[1]
    nbytes = 2 * (2 * tm * d * 4 + tm * dp * 4 + d * d * 2 + dp * d * 2) + 4 * tm * d * 4
    return pl.pallas_call(
        _ple_final_kernel,
        grid=(n // tm,),
        in_specs=[pl.BlockSpec((tm, d), lambda i: (i, 0)),
                  pl.BlockSpec((tm, dp), lambda i: (i, 0)),
                  pl.BlockSpec((1, d), lambda i: (0, 0)),
                  pl.BlockSpec((d, d), lambda i: (0, 0)),
                  pl.BlockSpec((dp, d), lambda i: (0, 0)),
                  pl.BlockSpec((1, d), lambda i: (0, 0))],
        out_specs=pl.BlockSpec((tm, d), lambda i: (i, 0)),
        out_shape=jax.ShapeDtypeStruct((n, d), F32),
        compiler_params=_params(("arbitrary",), nbytes),
        name="ple_final",
    )(x, p, g_ple.reshape(1, d), w_gate, w_ple, g_final.reshape(1, d))


def kernel(x_prompt, x_sample, cache_pool, state_ssm_re, state_ssm_im, cache_conv, p_prompt, p_sample, g_mix, w_in, w_pool, pool_scale, ssm_lam_re, ssm_lam_im, ssm_log_dt, ssm_b_re, ssm_b_im, ssm_c_re, ssm_c_im, ssm_d, w_glu, w_branch_pool, w_branch_ssm, w_out, g_ffn, w_up, w_conv, b_conv, w_down, g_ple, w_ple_gate, w_ple, g_final):
    depth = g_mix.shape[0]
    nb, t_len, d = x_prompt.shape
    ns = x_sample.shape[0]
    assert x_sample.shape[1] == 1, "the sample group advances one step per call"
    pool_buf, d_pool = cache_pool.shape[2], cache_pool.shape[3]
    conv_buf, d_ff = cache_conv.shape[2], cache_conv.shape[3]
    n_grp, n_state = ssm_lam_re.shape[1], ssm_lam_re.shape[2]
    d_ssm = ssm_d.shape[1]
    assert pool_buf == max(POOL_WINDOWS) - 1 and conv_buf == CONV_W - 1
    assert n_state == SSM_STATE and d_ssm == n_grp * SSM_GROUP

    xp = x_prompt.reshape(nb * t_len, d)
    xs = x_sample.reshape(ns, d)
    outs = [[] for _ in range(8)]
    for i in range(depth):
        w_pool_b, w_ple_b = w_pool[i].astype(BF16), w_ple[i].astype(BF16)
        bq, cq, a_pow_re, a_pow_im, a_row = _ssm_params(ssm_lam_re[i], ssm_lam_im[i], ssm_log_dt[i],
                                                        ssm_b_re[i], ssm_b_im[i], ssm_c_re[i], ssm_c_im[i])

        z_sample, w_in_b = _norm_matmul(xs, g_mix[i], w_in[i], ns, 3 * W_TILE)

        z = _norm_matmul(xp, g_mix[i], w_in_b, 1024, 3 * W_TILE)
        gy, st_re, st_im, (w_up_b, w_down_b, w_out_b, w_pg_b, w_glu_b, w_bs_b, w_bp_b) = _ssm_seq(
            z, d_pool, bq, cq, a_pow_re, a_pow_im, ssm_d[i], nb, t_len,
            [w_up[i], w_down[i], w_out[i], w_ple_gate[i], w_glu[i], w_branch_ssm[i], w_branch_pool[i]])

        def mix_tail(x, z, bp, gy, tm):
            return _mix_out(gy, bp, z, d_pool + d_ssm, x, w_glu_b, w_bs_b, w_out_b, tm)

        bp, pool_new = _pool_seq(z, w_pool_b, pool_scale[i], w_bp_b, nb, t_len, 1024, pool_buf)
        xp = mix_tail(xp, z, bp, gy, 256)
        xp, conv_new = _ffn_seq(xp, g_ffn[i], w_up_b, w_conv[i], b_conv[i], w_down_b, nb, t_len, 1024, W_TILE,
                                conv_buf)
        xp_out = _ple_final(xp, p_prompt[i].reshape(nb * t_len, -1), g_ple[i], w_pg_b, w_ple_b, g_final, 512)
        for lst, val in zip(outs[:4], (pool_new, st_re.reshape(nb, n_grp, n_state),
                                       st_im.reshape(nb, n_grp, n_state), conv_new)):
            lst.append(val)

        z = z_sample
        bp = _pool_step(z, cache_pool[i].reshape(ns, pool_buf * d_pool), w_pool_b, pool_scale[i], w_bp_b)
        pool_new = jnp.concatenate([cache_pool[i][:, 1:], z[:, None, :d_pool]], axis=1)
        gy, st_re, st_im = _ssm_step(z, d_pool, state_ssm_re[i].reshape(ns, -1), state_ssm_im[i].reshape(ns, -1),
                                     bq, cq, a_row, ssm_d[i])
        xs = mix_tail(xs, z, bp, gy, ns)
        xs, a_new = _ffn_step(xs, g_ffn[i], w_up_b, w_conv[i], b_conv[i], w_down_b,
                              cache_conv[i].reshape(ns, conv_buf * d_ff), 2 * W_TILE)
        conv_new = jnp.concatenate([cache_conv[i][:, 1:], a_new[:, None, :]], axis=1)
        xs_out = _ple_final(xs, p_sample[i].reshape(ns, -1), g_ple[i], w_pg_b, w_ple_b, g_final, ns)
        for lst, val in zip(outs[4:], (pool_new, st_re.reshape(ns, n_grp, n_state),
                                       st_im.reshape(ns, n_grp, n_state), conv_new)):
            lst.append(val)

    assert depth == 1
    y_prompt = xp_out.reshape(nb, t_len, d)
    y_sample = xs_out.reshape(ns, 1, d)
    return (y_prompt, y_sample) + tuple(jnp.stack(o, axis=0) for o in outs)
```

```python
import functools

import jax
import jax.numpy as jnp
from jax import lax
from jax.experimental import pallas as pl
from jax.experimental.pallas import tpu as pltpu

F32 = jnp.float32
BF16 = jnp.bfloat16

EPS = 1e-6
POOL_WINDOWS = (2, 4, 8, 16)
POOL_HALO = 32
SSM_GROUP = 16
SSM_STATE = 64
SSM_BLOCK_GROUPS = 16
LANES = 128
SUBLANES = 8
SEG_LEN = 128
SEG_PITCH = SEG_LEN + 8
CONV_W = 3
CONV_HALO = 8
FFN_SLICE = 256
W_TILE = 512
V7X_VMEM_BYTES = 64 * 1024 * 1024


def _vmem_limit(nbytes):
    return int(min(nbytes * 1.25 + (8 << 20), V7X_VMEM_BYTES - (6 << 20)))


def _params(sem, nbytes):
    return pltpu.CompilerParams(dimension_semantics=sem, vmem_limit_bytes=_vmem_limit(nbytes))


def _rmsnorm(x, g):
    return x * lax.rsqrt(jnp.mean(x * x, axis=-1, keepdims=True) + EPS) * g


def _dot(a, b):
    return jnp.dot(a, b, preferred_element_type=F32)


def _abar(lr, li, logdt):
    dt = jnp.exp(logdt)
    mag = jnp.exp(lr * dt)
    return mag * jnp.cos(li * dt), mag * jnp.sin(li * dt)


def _ssm_params_kernel(lr_ref, li_ref, logdt_ref, lrs_ref, lis_ref, logdts_ref, br_ref, bi_ref,
                       pwr_ref, pwi_ref, bbr_ref, bbi_ref):
    lr = lr_ref[...]
    li = li_ref[...]
    a_re, a_im = _abar(lr, li, logdt_ref[...])
    nr = a_re - 1.0
    ni = a_im
    den = lr * lr + li * li
    coef_re = ((nr * lr + ni * li) / den)[:, None, :]
    coef_im = ((ni * lr - nr * li) / den)[:, None, :]
    br = br_ref[...]
    bi = bi_ref[...]
    bbr_ref[...] = coef_re * br - coef_im * bi
    bbi_ref[...] = coef_re * bi + coef_im * br
    a_re, a_im = _abar(lrs_ref[...], lis_ref[...], logdts_ref[...])
    nkb, _, per_blk, _ = pwr_ref.shape
    pr, pi = a_re, a_im
    for n in range(SEG_LEN):
        if n:
            pr, pi = pr * a_re - pi * a_im, pr * a_im + pi * a_re
        for k in range(nkb):
            pwr_ref[k, n] = pr[k * per_blk:(k + 1) * per_blk, :]
            pwi_ref[k, n] = pi[k * per_blk:(k + 1) * per_blk, :]


def _ssm_params(lam_re, lam_im, log_dt, b_re, b_im, c_re, c_im):
    g, p = lam_re.shape
    h = b_re.shape[-1]
    nkb = g // SSM_BLOCK_GROUPS
    bl = SSM_BLOCK_GROUPS
    per_blk = bl * p // LANES
    slab = (g * p // LANES, LANES)
    pwr, pwi, bbr, bbi = pl.pallas_call(
        _ssm_params_kernel,
        out_shape=(jax.ShapeDtypeStruct((nkb, SEG_LEN, per_blk, LANES), F32),
                   jax.ShapeDtypeStruct((nkb, SEG_LEN, per_blk, LANES), F32),
                   jax.ShapeDtypeStruct((g, h, p), F32),
                   jax.ShapeDtypeStruct((g, h, p), F32)),
        name="ssm_params",
    )(lam_re, lam_im, log_dt.reshape(g, 1),
      lam_re.reshape(slab), lam_im.reshape(slab), jnp.broadcast_to(log_dt[:, None], (g, p)).reshape(slab),
      jnp.transpose(b_re, (0, 2, 1)), jnp.transpose(b_im, (0, 2, 1)))

    def c_rows(c):
        return jnp.transpose(c.reshape(nkb, bl * h, p), (0, 2, 1))

    bq = jnp.stack([bbr.reshape(nkb, bl * h, p), bbi.reshape(nkb, bl * h, p)], axis=1)
    cq = jnp.stack([c_rows(c_re), -c_rows(c_im)], axis=1)
    a_row = jnp.concatenate([pwr[:, 0].reshape(nkb, 1, bl * p), pwi[:, 0].reshape(nkb, 1, bl * p)], axis=-1)
    return bq, cq, pwr, pwi, a_row


def _expand_blockdiag(bq_ref, cq_ref, bcat_ref, ccat_ref):
    kw, p = bq_ref.shape[2], bq_ref.shape[3]
    half = bcat_ref.shape[1] // 2
    hch = kw * p // half
    log_p, log_h = p.bit_length() - 1, hch.bit_length() - 1
    assert (1 << log_p) == p and (1 << log_h) == hch

    def iota(shape, axis):
        return lax.broadcasted_iota(jnp.int32, shape, axis)

    def same(a, b):
        return a == b

    tile_b = same(iota((p, half), 1) & (p - 1), iota((p, half), 0)).astype(BF16)
    mask_b = same(iota((kw, half), 0) >> log_h, iota((kw, half), 1) >> log_p)
    tile_c = same(iota((half, p), 0) & (p - 1), iota((half, p), 1)).astype(BF16)
    mask_c = same(iota((half, kw), 0) >> log_p, iota((half, kw), 1) >> log_h)
    for s in range(2):
        full = _dot(bq_ref[0, s].astype(BF16), tile_b)
        bcat_ref[:, s * half:(s + 1) * half] = jnp.where(mask_b, full, 0.0).astype(BF16)
        full = _dot(tile_c, cq_ref[0, s].astype(BF16))
        ccat_ref[s * half:(s + 1) * half, :] = jnp.where(mask_c, full, 0.0).astype(BF16)


def _norm_matmul_kernel(x_ref, g_ref, w_ref, o_ref, *rest):
    *wb_ref, h_ref = rest

    @pl.when(pl.program_id(1) == 0)
    def _():
        h_ref[...] = _rmsnorm(x_ref[...], g_ref[...]).astype(BF16)

    w = w_ref[...]
    if wb_ref:
        w = w.astype(BF16)
        wb_ref[0][...] = w
    o_ref[...] = _dot(h_ref[...], w)


def _norm_matmul(x, g, w, tm, tn):
    n, d = x.shape
    dout = w.shape[1]
    convert = w.dtype == F32
    assert not convert or n == tm, "each weight tile must be visited exactly once to be converted"
    nbytes = 2 * tm * d * 4 + tm * d * 2 + 2 * d * tn * w.dtype.itemsize + 2 * tm * tn * 4 + convert * 3 * d * tn * 2
    z_spec = pl.BlockSpec((tm, tn), lambda i, j: (i, j))
    w_spec = pl.BlockSpec((d, tn), lambda i, j: (0, j))
    out = pl.pallas_call(
        _norm_matmul_kernel,
        grid=(n // tm, dout // tn),
        in_specs=[pl.BlockSpec((tm, d), lambda i, j: (i, 0)),
                  pl.BlockSpec((1, d), lambda i, j: (0, 0)),
                  w_spec],
        out_specs=[z_spec] + [w_spec] * convert,
        out_shape=[jax.ShapeDtypeStruct((n, dout), F32)] + [jax.ShapeDtypeStruct(w.shape, BF16)] * convert,
        scratch_shapes=[pltpu.VMEM((tm, d), BF16)],
        compiler_params=_params(("arbitrary", "arbitrary"), nbytes),
        name="norm_matmul",
    )(x, g.reshape(1, d), w)
    return out if convert else out[0]


def _pool_project(diffs, wp_ref, scale_ref, wb_ref, y_ref):
    gw = wp_ref.shape[1]
    for k, diff in enumerate(diffs):
        yk = _dot(diff.astype(BF16), wp_ref[k]) * scale_ref[:, k * gw:(k + 1) * gw]
        y_ref[:, k * gw:(k + 1) * gw] = yk.astype(BF16)
    return _dot(y_ref[...], wb_ref[...])


def _pool_seq_kernel(u_ref, wp_ref, scale_ref, wb_ref, o_ref, new_ref, ext_ref, y_ref, tmp_ref):
    tc, dp = u_ref.shape
    gw = wp_ref.shape[1]
    t = pl.program_id(1)
    end = POOL_HALO + tc

    @pl.when(t == 0)
    def _():
        ext_ref[0:POOL_HALO, :] = jnp.zeros((POOL_HALO, dp), F32)

    ext_ref[POOL_HALO:end, :] = u_ref[...]
    pos = (t * tc + 1 + lax.broadcasted_iota(jnp.int32, (tc, 1), 0)).astype(F32)
    diffs = []
    for k, w in enumerate(POOL_WINDOWS):
        cols = slice(k * gw, (k + 1) * gw)
        u = ext_ref[POOL_HALO:end, cols]
        src, m, start = (ext_ref, cols), 1, 0
        while m < w:
            ref, c = src
            start += SUBLANES
            s = ref[start:end, c] + ref[start - m:end - m, c]
            m *= 2
            if m < w:
                level = (m.bit_length() & 1, slice(None))
                tmp_ref[level[0], start:end, :] = s
                src = (tmp_ref.at[level[0]], level[1])
        s = s[POOL_HALO - start:, :]
        count = jnp.minimum(pos, float(w))
        diffs.append(s / count - u)
    o_ref[...] = _pool_project(diffs, wp_ref, scale_ref, wb_ref, y_ref)
    nb = new_ref.shape[1]
    new_ref[0] = ext_ref[POOL_HALO + tc - nb:POOL_HALO + tc, :]
    ext_ref[0:POOL_HALO, :] = ext_ref[tc:tc + POOL_HALO, :]


def _pool_seq(z, w_pool, pool_scale, w_branch, nb, t_len, tc, pool_buf):
    n = z.shape[0]
    ng, gw, _ = w_pool.shape
    dp = ng * gw
    dm = w_branch.shape[1]
    nt = t_len // tc
    nbytes = (2 * tc * dp * 4 + 2 * ng * gw * gw * 2 + 2 * dp * dm * 2 + 2 * tc * dm * 4
              + (tc + POOL_HALO) * dp * 4 + tc * dp * 2 + 4 * tc * gw * 4)
    return pl.pallas_call(
        _pool_seq_kernel,
        grid=(nb, nt),
        in_specs=[pl.BlockSpec((tc, dp), lambda b, t: (b * nt + t, 0)),
                  pl.BlockSpec((ng, gw, gw), lambda b, t: (0, 0, 0)),
                  pl.BlockSpec((1, dp), lambda b, t: (0, 0)),
                  pl.BlockSpec((dp, dm), lambda b, t: (0, 0))],
        out_specs=(pl.BlockSpec((tc, dm), lambda b, t: (b * nt + t, 0)),
                   pl.BlockSpec((1, pool_buf, dp), lambda b, t: (b, 0, 0))),
        out_shape=(jax.ShapeDtypeStruct((n, dm), F32),
                   jax.ShapeDtypeStruct((nb, pool_buf, dp), F32)),
        scratch_shapes=[pltpu.VMEM((tc + POOL_HALO, dp), F32), pltpu.VMEM((tc, dp), BF16),
                        pltpu.VMEM((2, tc + POOL_HALO, gw), F32)],
        compiler_params=_params(("arbitrary", "arbitrary"), nbytes),
        name="pool_seq",
    )(z, w_pool, pool_scale.reshape(1, dp), w_branch)


def _pool_step_kernel(u_ref, cache_ref, wp_ref, scale_ref, wb_ref, o_ref, new_ref, y_ref):
    dp = u_ref.shape[1]
    gw = wp_ref.shape[1]
    lb = cache_ref.shape[1] // dp
    diffs = []
    for k, w in enumerate(POOL_WINDOWS):
        u = u_ref[:, k * gw:(k + 1) * gw]
        s = u
        for j in range(1, w):
            s = s + cache_ref[:, (lb - j) * dp + k * gw:(lb - j) * dp + (k + 1) * gw]
        diffs.append(s / float(w) - u)
    o_ref[...] = _pool_project(diffs, wp_ref, scale_ref, wb_ref, y_ref)
    new_ref[:, 0:(lb - 1) * dp] = cache_ref[:, dp:lb * dp]
    new_ref[:, (lb - 1) * dp:lb * dp] = u_ref[...]


def _pool_step(z, cache, w_pool, pool_scale, w_branch):
    n = z.shape[0]
    ng, gw, _ = w_pool.shape
    dp = ng * gw
    dm = w_branch.shape[1]
    lbdp = cache.shape[1]
    nbytes = 2 * (n * dp * 4 + 2 * n * lbdp * 4 + ng * gw * gw * 2 + dp * dm * 2 + n * dm * 4) + n * dp * 2
    return pl.pallas_call(
        _pool_step_kernel,
        grid=(1,),
        in_specs=[pl.BlockSpec((n, dp), lambda i: (0, 0)),
                  pl.BlockSpec((n, lbdp), lambda i: (0, 0)),
                  pl.BlockSpec((ng, gw, gw), lambda i: (0, 0, 0)),
                  pl.BlockSpec((1, dp), lambda i: (0, 0)),
                  pl.BlockSpec((dp, dm), lambda i: (0, 0))],
        out_specs=(pl.BlockSpec((n, dm), lambda i: (0, 0)),
                   pl.BlockSpec((n, lbdp), lambda i: (0, 0))),
        out_shape=(jax.ShapeDtypeStruct((n, dm), F32),
                   jax.ShapeDtypeStruct((n, lbdp), F32)),
        scratch_shapes=[pltpu.VMEM((n, dp), BF16)],
        compiler_params=_params(("arbitrary",), nbytes),
        name="pool_step",
    )(z, cache, w_pool, pool_scale.reshape(1, dp), w_branch)


def _cmul_add(xr, xi, ar, ai, br, bi):
    return xr + ar * br - ai * bi, xi + ar * bi + ai * br


def _ssm_seq_kernel(n_cast, u_ref, bq_ref, cq_ref, pwr_ref, pwi_ref, d_ref, *refs):
    cast_in, (gy_ref, st_ref), refs = refs[:n_cast], refs[n_cast:n_cast + 2], refs[n_cast + 2:]
    cast_out, refs = refs[:n_cast], refs[n_cast:]
    bcat_ref, ccat_ref, us_ref, up_ref, h_ref, hb_ref, ys_ref, seed_ref, carry_ref = refs
    for src_ref, dst_ref in zip(cast_in, cast_out):
        dst_ref[...] = src_ref[...].astype(BF16)

    kw = u_ref.shape[1]
    nslab = h_ref.shape[1] // LANES
    npair = nslab // 2
    t = pl.program_id(2)

    def lanes(j):
        return slice(j * LANES, (j + 1) * LANES)

    def step_rows(n, count=1):
        return slice(n * SUBLANES, (n + count) * SUBLANES)

    def apow(n, j):
        return pwr_ref[0, n, j:j + 1, :], pwi_ref[0, n, j:j + 1, :]

    @pl.when(jnp.logical_and(pl.program_id(1) == 0, t == 0))
    def _():
        _expand_blockdiag(bq_ref, cq_ref, bcat_ref, ccat_ref)

    @pl.when(t == 0)
    def _():
        carry_ref[...] = jnp.zeros_like(carry_ref)

    for k in range(kw // LANES):
        for r in range(SUBLANES):
            us_ref[k, r * SEG_PITCH:r * SEG_PITCH + SEG_LEN, :] = u_ref[r * SEG_LEN:(r + 1) * SEG_LEN, lanes(k)]
    for n in range(SEG_LEN):
        for k in range(kw // LANES):
            up_ref[step_rows(n), lanes(k)] = us_ref[k, pl.ds(n, SUBLANES, stride=SEG_PITCH), :]
    u = up_ref[...]
    h_ref[...] = _dot(u.astype(BF16), bcat_ref[...])

    state = [jnp.zeros((SUBLANES, LANES), F32)] * nslab
    for n in range(SEG_LEN):
        for j in range(npair):
            sr, si = _cmul_add(h_ref[step_rows(n), lanes(j)], h_ref[step_rows(n), lanes(npair + j)],
                               *apow(0, j), state[j], state[npair + j])
            h_ref[step_rows(n), lanes(j)] = sr
            h_ref[step_rows(n), lanes(npair + j)] = si
            state[j], state[npair + j] = sr, si

    for j in range(npair):
        cr, ci = carry_ref[j], carry_ref[npair + j]
        for r in range(SUBLANES):
            seed_ref[j, r:r + 1, :] = cr
            seed_ref[npair + j, r:r + 1, :] = ci
            cr, ci = _cmul_add(state[j][r:r + 1, :], state[npair + j][r:r + 1, :],
                               *apow(SEG_LEN - 1, j), cr, ci)
        carry_ref[j] = cr
        carry_ref[npair + j] = ci
    st_ref[0, 0] = carry_ref[...]

    @pl.when(t >= 0)
    def _():
        for i in range(SEG_LEN // 2):
            for j in range(npair):
                parts = []
                for n in (2 * i, 2 * i + 1):
                    parts.append(_cmul_add(h_ref[step_rows(n), lanes(j)], h_ref[step_rows(n), lanes(npair + j)],
                                           *apow(n, j), seed_ref[j], seed_ref[npair + j]))
                hb_ref[step_rows(2 * i, 2), lanes(j)] = (
                    jnp.concatenate([p[0] for p in parts], axis=0).astype(BF16))
                hb_ref[step_rows(2 * i, 2), lanes(npair + j)] = (
                    jnp.concatenate([p[1] for p in parts], axis=0).astype(BF16))

        y = _dot(hb_ref[...], ccat_ref[...]) + d_ref[...] * up_ref[...]
        g = jax.nn.gelu(y)
        for n in range(SEG_LEN):
            for k in range(kw // LANES):
                ys_ref[k, pl.ds(n, SUBLANES, stride=SEG_PITCH), :] = g[step_rows(n), lanes(k)]
        for k in range(kw // LANES):
            for r in range(SUBLANES):
                gy_ref[r * SEG_LEN:(r + 1) * SEG_LEN, lanes(k)] = (
                    ys_ref[k, r * SEG_PITCH:r * SEG_PITCH + SEG_LEN, :].astype(BF16))


def _ssm_seq(z, col0, bq, cq, pwr, pwi, d_skip, nb, t_len, side_casts):
    n = z.shape[0]
    nkb, _, kw, p = bq.shape
    sw = 2 * SSM_BLOCK_GROUPS * p
    nslab = sw // LANES
    tc = SUBLANES * SEG_LEN
    nt = t_len // tc
    cb = col0 // kw
    n_steps = nkb * nb * nt
    cast_rows = [w.shape[0] // n_steps for w in side_casts]
    assert all(r % (2 * SUBLANES) == 0 and r * n_steps == w.shape[0] for r, w in zip(cast_rows, side_casts))
    nbytes = (2 * tc * kw * 4 + 4 * kw * sw * 2 + 2 * (SEG_LEN + SUBLANES) * sw * 4 + 2 * tc * kw * 2
              + 2 * SUBLANES * SEG_PITCH * kw * 4 + tc * kw * 4 + tc * sw * 4 + tc * sw * 2 + tc * sw * 4
              + sum(2 * r * w.shape[1] * 6 for r, w in zip(cast_rows, side_casts)))

    def cast_spec(r, w):
        return pl.BlockSpec((r, w.shape[1]), lambda k, b, t: ((k * nb + b) * nt + t, 0))

    cast_specs = [cast_spec(r, w) for r, w in zip(cast_rows, side_casts)]
    gy, st, *cast = pl.pallas_call(
        functools.partial(_ssm_seq_kernel, len(side_casts)),
        grid=(nkb, nb, nt),
        in_specs=[pl.BlockSpec((tc, kw), lambda k, b, t: (b * nt + t, cb + k)),
                  pl.BlockSpec((1, 2, kw, p), lambda k, b, t: (k, 0, 0, 0)),
                  pl.BlockSpec((1, 2, p, kw), lambda k, b, t: (k, 0, 0, 0)),
                  pl.BlockSpec((1, SEG_LEN, nslab // 2, LANES), lambda k, b, t: (k, 0, 0, 0)),
                  pl.BlockSpec((1, SEG_LEN, nslab // 2, LANES), lambda k, b, t: (k, 0, 0, 0)),
                  pl.BlockSpec((1, kw), lambda k, b, t: (0, k))] + cast_specs,
        out_specs=[pl.BlockSpec((tc, kw), lambda k, b, t: (b * nt + t, k)),
                   pl.BlockSpec((1, 1, nslab, 1, LANES), lambda k, b, t: (b, k, 0, 0, 0))] + cast_specs,
        out_shape=[jax.ShapeDtypeStruct((n, nkb * kw), BF16),
                   jax.ShapeDtypeStruct((nb, nkb, nslab, 1, LANES), F32)]
                  + [jax.ShapeDtypeStruct(w.shape, BF16) for w in side_casts],
        scratch_shapes=[pltpu.VMEM((kw, sw), BF16), pltpu.VMEM((sw, kw), BF16),
                        pltpu.VMEM((kw // LANES, SUBLANES * SEG_PITCH, LANES), F32), pltpu.VMEM((tc, kw), F32),
                        pltpu.VMEM((tc, sw), F32), pltpu.VMEM((tc, sw), BF16),
                        pltpu.VMEM((kw // LANES, SUBLANES * SEG_PITCH, LANES), F32),
                        pltpu.VMEM((nslab, SUBLANES, LANES), F32), pltpu.VMEM((nslab, 1, LANES), F32)],
        compiler_params=_params(("arbitrary", "arbitrary", "arbitrary"), nbytes),
        name="ssm_seq",
    )(z, bq, cq, pwr, pwi, d_skip.reshape(1, nkb * kw), *side_casts)
    st = st.reshape(nb, nkb, 2, sw // 2)
    return gy, st[:, :, 0].reshape(nb, -1), st[:, :, 1].reshape(nb, -1), cast


def _ssm_step_kernel(u_ref, h0r_ref, h0i_ref, bq_ref, cq_ref, a_ref, d_ref,
                     gy_ref, h1r_ref, h1i_ref, h_ref, bcat_ref, ccat_ref):
    half = h0r_ref.shape[1]
    _expand_blockdiag(bq_ref, cq_ref, bcat_ref, ccat_ref)
    u = u_ref[...]
    bu = _dot(u.astype(BF16), bcat_ref[...])
    hr, hi = _cmul_add(bu[:, 0:half], bu[:, half:2 * half],
                       a_ref[0, :, 0:half], a_ref[0, :, half:2 * half],
                       h0r_ref[...], h0i_ref[...])
    h1r_ref[...] = hr
    h1i_ref[...] = hi
    h_ref[:, 0:half] = hr.astype(BF16)
    h_ref[:, half:2 * half] = hi.astype(BF16)
    y = _dot(h_ref[...], ccat_ref[...]) + d_ref[...] * u
    gy_ref[...] = jax.nn.gelu(y).astype(BF16)


def _ssm_step(z, col0, h0_re, h0_im, bq, cq, a_row, d_skip):
    n = z.shape[0]
    nkb, _, kw, p = bq.shape
    half = SSM_BLOCK_GROUPS * p
    sw = 2 * half
    cb = col0 // kw
    nbytes = 2 * (n * kw * 4 + 4 * n * half * 4 + 2 * kw * sw * 2 + sw * 4 + n * kw * 2) + n * sw * 6
    return pl.pallas_call(
        _ssm_step_kernel,
        grid=(nkb,),
        in_specs=[pl.BlockSpec((n, kw), lambda k: (0, cb + k)),
                  pl.BlockSpec((n, half), lambda k: (0, k)),
                  pl.BlockSpec((n, half), lambda k: (0, k)),
                  pl.BlockSpec((1, 2, kw, p), lambda k: (k, 0, 0, 0)),
                  pl.BlockSpec((1, 2, p, kw), lambda k: (k, 0, 0, 0)),
                  pl.BlockSpec((1, 1, sw), lambda k: (k, 0, 0)),
                  pl.BlockSpec((1, kw), lambda k: (0, k))],
        out_specs=(pl.BlockSpec((n, kw), lambda k: (0, k)),
                   pl.BlockSpec((n, half), lambda k: (0, k)),
                   pl.BlockSpec((n, half), lambda k: (0, k))),
        out_shape=(jax.ShapeDtypeStruct((n, nkb * kw), BF16),
                   jax.ShapeDtypeStruct((n, nkb * half), F32),
                   jax.ShapeDtypeStruct((n, nkb * half), F32)),
        scratch_shapes=[pltpu.VMEM((n, sw), BF16), pltpu.VMEM((kw, sw), BF16), pltpu.VMEM((sw, kw), BF16)],
        compiler_params=_params(("arbitrary",), nbytes),
        name="ssm_step",
    )(z, h0_re, h0_im, bq, cq, a_row, d_skip.reshape(1, nkb * kw))


def _mix_out_kernel(gy_ref, bp_ref, gp_ref, gs_ref, x_ref, wg_ref, wb_ref, wo_ref, o_ref):
    ds = wb_ref.shape[0]
    g = _dot(gy_ref[...], wg_ref[...])
    y = g[:, 0:ds] * jax.nn.sigmoid(g[:, ds:2 * ds])
    bs = _dot(y.astype(BF16), wb_ref[...])
    merged = jax.nn.sigmoid(gp_ref[...]) * bp_ref[...] + jax.nn.sigmoid(gs_ref[...]) * bs
    o_ref[...] = x_ref[...] + _dot(merged.astype(BF16), wo_ref[...])


def _mix_out(gy, bp, z, gate_col0, x, w_glu, w_branch, w_out, tm):
    n, ds = gy.shape
    dm = w_branch.shape[1]
    gb = gate_col0 // dm
    w_bytes = (ds * 2 * ds + ds * dm + dm * dm) * 2
    nbytes = 2 * (tm * ds * 2 + 5 * tm * dm * 4) + w_bytes + 6 * tm * dm * 4

    def resident(shape):
        return pl.BlockSpec(shape, lambda i: (0, 0), pipeline_mode=pl.Buffered(1))

    return pl.pallas_call(
        _mix_out_kernel,
        grid=(n // tm,),
        in_specs=[pl.BlockSpec((tm, ds), lambda i: (i, 0)),
                  pl.BlockSpec((tm, dm), lambda i: (i, 0)),
                  pl.BlockSpec((tm, dm), lambda i: (i, gb)),
                  pl.BlockSpec((tm, dm), lambda i: (i, gb + 1)),
                  pl.BlockSpec((tm, dm), lambda i: (i, 0)),
                  resident((ds, 2 * ds)), resident((ds, dm)), resident((dm, dm))],
        out_specs=pl.BlockSpec((tm, dm), lambda i: (i, 0)),
        out_shape=jax.ShapeDtypeStruct((n, dm), F32),
        compiler_params=_params(("arbitrary",), nbytes),
        name="mix_out",
    )(gy, bp, z, z, x, w_glu, w_branch, w_out)


def _ffn_gate_down(x_ref, conv, v, wd_ref, o_ref, acc_ref):
    c = pl.program_id(1)
    part = _dot((jax.nn.gelu(conv) * v).astype(BF16), wd_ref[...])

    @pl.when(c == 0)
    def _():
        acc_ref[...] = part

    @pl.when(c > 0)
    def _():
        acc_ref[...] += part

    @pl.when(c == pl.num_programs(1) - 1)
    def _():
        o_ref[...] = x_ref[...] + acc_ref[...]


def _ffn_seq_kernel(tiles_per_seq, x_ref, g_ref, wa_ref, wv_ref, wc_ref, bc_ref, wd_ref,
                    o_ref, new_ref, h_ref, ext_ref, carry_ref, gate_ref):
    tm = x_ref.shape[0]
    tf = gate_ref.shape[1]
    i = pl.program_id(0)
    c = pl.program_id(1)

    @pl.when(jnp.logical_and(i == 0, c == 0))
    def _():
        o_ref[...] = jnp.zeros(o_ref.shape, F32)
        carry_ref[...] = jnp.zeros(carry_ref.shape, F32)

    @pl.when(c == 0)
    def _():
        h_ref[...] = _rmsnorm(x_ref[...], g_ref[...]).astype(BF16)

    seq_start = i % tiles_per_seq == 0
    nb = new_ref.shape[1]
    for q in range(tf // FFN_SLICE):
        cols = slice(q * FFN_SLICE, (q + 1) * FFN_SLICE)
        a = _dot(h_ref[...], wa_ref[:, cols])
        v = _dot(h_ref[...], wv_ref[:, cols])
        ext_ref[0:CONV_HALO, cols] = jnp.where(seq_start, 0.0, carry_ref[c, :, cols])
        ext_ref[CONV_HALO:CONV_HALO + tm, cols] = a
        conv = bc_ref[:, cols] + wc_ref[CONV_W - 1:CONV_W, cols] * a
        for j in range(CONV_W - 1):
            off = CONV_HALO - (CONV_W - 1) + j
            conv = conv + wc_ref[j:j + 1, cols] * ext_ref[off:off + tm, cols]
        carry_ref[c, :, cols] = ext_ref[tm:tm + CONV_HALO, cols]
        new_ref[0, :, cols] = ext_ref[CONV_HALO + tm - nb:CONV_HALO + tm, cols]
        gate_ref[:, cols] = (jax.nn.gelu(conv) * v).astype(BF16)
    o_ref[...] = _dot(gate_ref[...], wd_ref[...]) + jnp.where(c == 0, x_ref[...], o_ref[...])


def _ffn_seq(x, g, w_up, w_conv, b_conv, w_down, nb, t_len, tm, tf, conv_buf):
    n, d = x.shape
    dff = w_down.shape[0]
    nc = dff // tf
    tps = t_len // tm
    nbytes = (4 * tm * d * 4 + 4 * d * tf * 2 + 2 * tf * d * 2 + tm * d * 2 + tm * tf * 2
              + (tm + CONV_HALO) * tf * 4 + nc * CONV_HALO * tf * 4 + 6 * tm * FFN_SLICE * 4)
    out, new_tail = pl.pallas_call(
        functools.partial(_ffn_seq_kernel, tps),
        grid=(n // tm, nc),
        in_specs=[pl.BlockSpec((tm, d), lambda i, c: (i, 0)),
                  pl.BlockSpec((1, d), lambda i, c: (0, 0)),
                  pl.BlockSpec((d, tf), lambda i, c: (0, c)),
                  pl.BlockSpec((d, tf), lambda i, c: (0, nc + c)),
                  pl.BlockSpec((CONV_W, tf), lambda i, c: (0, c)),
                  pl.BlockSpec((1, tf), lambda i, c: (0, c)),
                  pl.BlockSpec((tf, d), lambda i, c: (c, 0))],
        out_specs=(pl.BlockSpec((tm, d), lambda i, c: (i, 0)),
                   pl.BlockSpec((1, conv_buf, tf), lambda i, c: (i, 0, c))),
        out_shape=(jax.ShapeDtypeStruct((n, d), F32),
                   jax.ShapeDtypeStruct((n // tm, conv_buf, dff), F32)),
        scratch_shapes=[pltpu.VMEM((tm, d), BF16), pltpu.VMEM((tm + CONV_HALO, tf), F32),
                        pltpu.VMEM((nc, CONV_HALO, tf), F32), pltpu.VMEM((tm, tf), BF16)],
        compiler_params=_params(("arbitrary", "arbitrary"), nbytes),
        name="ffn_seq",
    )(x, g.reshape(1, d), w_up, w_up, w_conv, b_conv.reshape(1, dff), w_down)
    return out, new_tail[tps - 1::tps]


def _ffn_step_kernel(x_ref, g_ref, wa_ref, wv_ref, wc_ref, bc_ref, wd_ref, p0_ref, p1_ref,
                     o_ref, a_ref, h_ref, acc_ref):
    @pl.when(pl.program_id(1) == 0)
    def _():
        h_ref[...] = _rmsnorm(x_ref[...], g_ref[...]).astype(BF16)

    a = _dot(h_ref[...], wa_ref[...])
    v = _dot(h_ref[...], wv_ref[...])
    conv = bc_ref[...] + wc_ref[2:3, :] * a + wc_ref[1:2, :] * p1_ref[...] + wc_ref[0:1, :] * p0_ref[...]
    a_ref[...] = a
    _ffn_gate_down(x_ref, conv, v, wd_ref, o_ref, acc_ref)


def _ffn_step(x, g, w_up, w_conv, b_conv, w_down, cache, tf):
    n, d = x.shape
    dff = w_down.shape[0]
    nc = dff // tf
    nbytes = 4 * n * d * 4 + 4 * d * tf * 2 + 2 * tf * d * 2 + n * d * 6 + 12 * n * tf * 4
    return pl.pallas_call(
        _ffn_step_kernel,
        grid=(1, nc),
        in_specs=[pl.BlockSpec((n, d), lambda i, c: (0, 0)),
                  pl.BlockSpec((1, d), lambda i, c: (0, 0)),
                  pl.BlockSpec((d, tf), lambda i, c: (0, c)),
                  pl.BlockSpec((d, tf), lambda i, c: (0, nc + c)),
                  pl.BlockSpec((CONV_W, tf), lambda i, c: (0, c)),
                  pl.BlockSpec((1, tf), lambda i, c: (0, c)),
                  pl.BlockSpec((tf, d), lambda i, c: (c, 0)),
                  pl.BlockSpec((n, tf), lambda i, c: (0, c)),
                  pl.BlockSpec((n, tf), lambda i, c: (0, nc + c))],
        out_specs=(pl.BlockSpec((n, d), lambda i, c: (0, 0)),
                   pl.BlockSpec((n, tf), lambda i, c: (0, c))),
        out_shape=(jax.ShapeDtypeStruct((n, d), F32),
                   jax.ShapeDtypeStruct((n, dff), F32)),
        scratch_shapes=[pltpu.VMEM((n, d), BF16), pltpu.VMEM((n, d), F32)],
        compiler_params=_params(("arbitrary", "arbitrary"), nbytes),
        name="ffn_step",
    )(x, g.reshape(1, d), w_up, w_up, w_conv, b_conv.reshape(1, dff), w_down, cache, cache)


def _ple_final_kernel(x_ref, p_ref, gp_ref, wg_ref, wp_ref, gf_ref, o_ref):
    x = x_ref[...]
    gate = jax.nn.sigmoid(_dot(_rmsnorm(x, gp_ref[...]).astype(BF16), wg_ref[...]))
    x = x + gate * _dot(p_ref[...].astype(BF16), wp_ref[...])
    o_ref[...] = _rmsnorm(x, gf_ref[...])


def _ple_final(x, p, g_ple, w_gate, w_ple, g_final, tm):
    n, d = x.shape
    dp = p.shape[1]
    nbytes = 2 * (2 * tm * d * 4 + tm * dp * 4 + d * d * 2 + dp * d * 2) + 4 * tm * d * 4
    return pl.pallas_call(
        _ple_final_kernel,
        grid=(n // tm,),
        in_specs=[pl.BlockSpec((tm, d), lambda i: (i, 0)),
                  pl.BlockSpec((tm, dp), lambda i: (i, 0)),
                  pl.BlockSpec((1, d), lambda i: (0, 0)),
                  pl.BlockSpec((d, d), lambda i: (0, 0)),
                  pl.BlockSpec((dp, d), lambda i: (0, 0)),
                  pl.BlockSpec((1, d), lambda i: (0, 0))],
        out_specs=pl.BlockSpec((tm, d), lambda i: (i, 0)),
        out_shape=jax.ShapeDtypeStruct((n, d), F32),
        compiler_params=_params(("arbitrary",), nbytes),
        name="ple_final",
    )(x, p, g_ple.reshape(1, d), w_gate, w_ple, g_final.reshape(1, d))


def kernel(x_prompt, x_sample, cache_pool, state_ssm_re, state_ssm_im, cache_conv, p_prompt, p_sample, g_mix, w_in, w_pool, pool_scale, ssm_lam_re, ssm_lam_im, ssm_log_dt, ssm_b_re, ssm_b_im, ssm_c_re, ssm_c_im, ssm_d, w_glu, w_branch_pool, w_branch_ssm, w_out, g_ffn, w_up, w_conv, b_conv, w_down, g_ple, w_ple_gate, w_ple, g_final):
    depth = g_mix.shape[0]
    nb, t_len, d = x_prompt.shape
    ns = x_sample.shape[0]
    assert x_sample.shape[1] == 1, "the sample group advances one step per call"
    pool_buf, d_pool = cache_pool.shape[2], cache_pool.shape[3]
    conv_buf, d_ff = cache_conv.shape[2], cache_conv.shape[3]
    n_grp, n_state = ssm_lam_re.shape[1], ssm_lam_re.shape[2]
    d_ssm = ssm_d.shape[1]
    assert pool_buf == max(POOL_WINDOWS) - 1 and conv_buf == CONV_W - 1
    assert POOL_HALO == SUBLANES * (max(POOL_WINDOWS).bit_length() - 1) and pool_buf <= POOL_HALO
    assert n_state == SSM_STATE and d_ssm == n_grp * SSM_GROUP

    xp = x_prompt.reshape(nb * t_len, d)
    xs = x_sample.reshape(ns, d)
    outs = [[] for _ in range(8)]
    for i in range(depth):
        w_pool_b, w_ple_b = w_pool[i].astype(BF16), w_ple[i].astype(BF16)
        bq, cq, a_pow_re, a_pow_im, a_row = _ssm_params(ssm_lam_re[i], ssm_lam_im[i], ssm_log_dt[i],
                                                        ssm_b_re[i], ssm_b_im[i], ssm_c_re[i], ssm_c_im[i])

        z_sample, w_in_b = _norm_matmul(xs, g_mix[i], w_in[i], ns, 3 * W_TILE)

        z = _norm_matmul(xp, g_mix[i], w_in_b, 1024, 3 * W_TILE)
        gy, st_re, st_im, (w_up_b, w_down_b, w_out_b, w_pg_b, w_glu_b, w_bs_b, w_bp_b) = _ssm_seq(
            z, d_pool, bq, cq, a_pow_re, a_pow_im, ssm_d[i], nb, t_len,
            [w_up[i], w_down[i], w_out[i], w_ple_gate[i], w_glu[i], w_branch_ssm[i], w_branch_pool[i]])

        def mix_tail(x, z, bp, gy, tm):
            return _mix_out(gy, bp, z, d_pool + d_ssm, x, w_glu_b, w_bs_b, w_out_b, tm)

        bp, pool_new = _pool_seq(z, w_pool_b, pool_scale[i], w_bp_b, nb, t_len, 1024, pool_buf)
        xp = mix_tail(xp, z, bp, gy, 256)
        xp, conv_new = _ffn_seq(xp, g_ffn[i], w_up_b, w_conv[i], b_conv[i], w_down_b, nb, t_len, 1024, W_TILE,
                                conv_buf)
        xp_out = _ple_final(xp, p_prompt[i].reshape(nb * t_len, -1), g_ple[i], w_pg_b, w_ple_b, g_final, 512)
        for lst, val in zip(outs[:4], (pool_new, st_re.reshape(nb, n_grp, n_state),
                                       st_im.reshape(nb, n_grp, n_state), conv_new)):
            lst.append(val)

        z = z_sample
        bp, pool_new = _pool_step(z, cache_pool[i].reshape(ns, pool_buf * d_pool), w_pool_b, pool_scale[i], w_bp_b)
        gy, st_re, st_im = _ssm_step(z, d_pool, state_ssm_re[i].reshape(ns, -1), state_ssm_im[i].reshape(ns, -1),
                                     bq, cq, a_row, ssm_d[i])
        xs = mix_tail(xs, z, bp, gy, ns)
        xs, a_new = _ffn_step(xs, g_ffn[i], w_up_b, w_conv[i], b_conv[i], w_down_b,
                              cache_conv[i].reshape(ns, conv_buf * d_ff), 2 * W_TILE)
        conv_new = jnp.concatenate([cache_conv[i][:, 1:], a_new[:, None, :]], axis=1)
        xs_out = _ple_final(xs, p_sample[i].reshape(ns, -1), g_ple[i], w_pg_b, w_ple_b, g_final, ns)
        for lst, val in zip(outs[4:], (pool_new.reshape(ns, pool_buf, d_pool), st_re.reshape(ns, n_grp, n_state),
                                       st_im.reshape(ns, n_grp, n_state), conv_new)):
            lst.append(val)

    assert depth == 1
    y_prompt = xp_out.reshape(nb, t_len, d)
    y_sample = xs_out.reshape(ns, 1, d)
    return (y_prompt, y_sample) + tuple(jnp.stack(o, axis=0) for o in outs)
```

```python
import functools

import jax
import jax.numpy as jnp
from jax import lax
from jax.experimental import pallas as pl
from jax.experimental.pallas import tpu as pltpu

F32 = jnp.float32
BF16 = jnp.bfloat16

EPS = 1e-6
POOL_WINDOWS = (2, 4, 8, 16)
POOL_HALO = 32
SSM_GROUP = 16
SSM_STATE = 64
SSM_BLOCK_GROUPS = 16
LANES = 128
SUBLANES = 8
SEG_LEN = 128
SEG_PITCH = SEG_LEN + 8
CONV_W = 3
CONV_HALO = 8
V7X_MXU_COLS = 256
FFN_SLICE = V7X_MXU_COLS
W_TILE = 2 * V7X_MXU_COLS
V7X_VMEM_BYTES = 64 * 1024 * 1024
VMEM_UNSCOPED_BYTES = 6 << 20
VMEM_TEMP_BYTES = 8 << 20

IN_ROWS, IN_COLS = 1024, 3 * W_TILE
POOL_ROWS = 1024
MIX_ROWS = 256
FFN_ROWS, FFN_COLS = 1024, W_TILE
FFN_STEP_COLS = 2 * W_TILE
PLE_ROWS = 512


def _vmem_limit(nbytes):
    return int(min(nbytes * 1.25 + VMEM_TEMP_BYTES, V7X_VMEM_BYTES - VMEM_UNSCOPED_BYTES))


def _params(sem, nbytes):
    return pltpu.CompilerParams(dimension_semantics=sem, vmem_limit_bytes=_vmem_limit(nbytes))


def _rmsnorm(x, g):
    return x * lax.rsqrt(jnp.mean(x * x, axis=-1, keepdims=True) + EPS) * g


def _dot(a, b):
    return jnp.dot(a, b, preferred_element_type=F32)


def _abar(lr, li, logdt):
    dt = jnp.exp(logdt)
    mag = jnp.exp(lr * dt)
    return mag * jnp.cos(li * dt), mag * jnp.sin(li * dt)


def _ssm_params_kernel(lr_ref, li_ref, logdt_ref, lrs_ref, lis_ref, logdts_ref, br_ref, bi_ref,
                       pwr_ref, pwi_ref, bbr_ref, bbi_ref):
    lr = lr_ref[...]
    li = li_ref[...]
    a_re, a_im = _abar(lr, li, logdt_ref[...])
    nr = a_re - 1.0
    ni = a_im
    den = lr * lr + li * li
    coef_re = ((nr * lr + ni * li) / den)[:, None, :]
    coef_im = ((ni * lr - nr * li) / den)[:, None, :]
    br = br_ref[...]
    bi = bi_ref[...]
    bbr_ref[...] = coef_re * br - coef_im * bi
    bbi_ref[...] = coef_re * bi + coef_im * br
    a_re, a_im = _abar(lrs_ref[...], lis_ref[...], logdts_ref[...])
    nkb, _, per_blk, _ = pwr_ref.shape
    pr, pi = a_re, a_im
    for n in range(SEG_LEN):
        if n:
            pr, pi = pr * a_re - pi * a_im, pr * a_im + pi * a_re
        for k in range(nkb):
            pwr_ref[k, n] = pr[k * per_blk:(k + 1) * per_blk, :]
            pwi_ref[k, n] = pi[k * per_blk:(k + 1) * per_blk, :]


def _ssm_params(lam_re, lam_im, log_dt, b_re, b_im, c_re, c_im):
    g, p = lam_re.shape
    h = b_re.shape[-1]
    nkb = g // SSM_BLOCK_GROUPS
    bl = SSM_BLOCK_GROUPS
    per_blk = bl * p // LANES
    slab = (g * p // LANES, LANES)
    pwr, pwi, bbr, bbi = pl.pallas_call(
        _ssm_params_kernel,
        out_shape=(jax.ShapeDtypeStruct((nkb, SEG_LEN, per_blk, LANES), F32),
                   jax.ShapeDtypeStruct((nkb, SEG_LEN, per_blk, LANES), F32),
                   jax.ShapeDtypeStruct((g, h, p), F32),
                   jax.ShapeDtypeStruct((g, h, p), F32)),
        name="ssm_params",
    )(lam_re, lam_im, log_dt.reshape(g, 1),
      lam_re.reshape(slab), lam_im.reshape(slab), jnp.broadcast_to(log_dt[:, None], (g, p)).reshape(slab),
      jnp.transpose(b_re, (0, 2, 1)), jnp.transpose(b_im, (0, 2, 1)))

    def c_rows(c):
        return jnp.transpose(c.reshape(nkb, bl * h, p), (0, 2, 1))

    bq = jnp.stack([bbr.reshape(nkb, bl * h, p), bbi.reshape(nkb, bl * h, p)], axis=1)
    cq = jnp.stack([c_rows(c_re), -c_rows(c_im)], axis=1)
    a_row = jnp.concatenate([pwr[:, 0].reshape(nkb, 1, bl * p), pwi[:, 0].reshape(nkb, 1, bl * p)], axis=-1)
    return bq, cq, pwr, pwi, a_row


def _expand_blockdiag(bq_ref, cq_ref, bcat_ref, ccat_ref):
    kw, p = bq_ref.shape[2], bq_ref.shape[3]
    half = bcat_ref.shape[1] // 2
    hch = kw * p // half
    log_p, log_h = p.bit_length() - 1, hch.bit_length() - 1
    assert (1 << log_p) == p and (1 << log_h) == hch

    def iota(shape, axis):
        return lax.broadcasted_iota(jnp.int32, shape, axis)

    def same(a, b):
        return a == b

    tile_b = same(iota((p, half), 1) & (p - 1), iota((p, half), 0)).astype(BF16)
    mask_b = same(iota((kw, half), 0) >> log_h, iota((kw, half), 1) >> log_p)
    tile_c = same(iota((half, p), 0) & (p - 1), iota((half, p), 1)).astype(BF16)
    mask_c = same(iota((half, kw), 0) >> log_p, iota((half, kw), 1) >> log_h)
    for s in range(2):
        full = _dot(bq_ref[0, s].astype(BF16), tile_b)
        bcat_ref[:, s * half:(s + 1) * half] = jnp.where(mask_b, full, 0.0).astype(BF16)
        full = _dot(tile_c, cq_ref[0, s].astype(BF16))
        ccat_ref[s * half:(s + 1) * half, :] = jnp.where(mask_c, full, 0.0).astype(BF16)


def _norm_matmul_kernel(x_ref, g_ref, w_ref, o_ref, *rest):
    *wb_ref, h_ref = rest

    @pl.when(pl.program_id(1) == 0)
    def _():
        h_ref[...] = _rmsnorm(x_ref[...], g_ref[...]).astype(BF16)

    w = w_ref[...]
    if wb_ref:
        w = w.astype(BF16)
        wb_ref[0][...] = w
    o_ref[...] = _dot(h_ref[...], w)


def _norm_matmul(x, g, w, tm, tn):
    n, d = x.shape
    dout = w.shape[1]
    convert = w.dtype == F32
    assert not convert or n == tm, "each weight tile must be visited exactly once to be converted"
    nbytes = 2 * tm * d * 4 + tm * d * 2 + 2 * d * tn * w.dtype.itemsize + 2 * tm * tn * 4 + convert * 3 * d * tn * 2
    z_spec = pl.BlockSpec((tm, tn), lambda i, j: (i, j))
    w_spec = pl.BlockSpec((d, tn), lambda i, j: (0, j))
    out = pl.pallas_call(
        _norm_matmul_kernel,
        grid=(n // tm, dout // tn),
        in_specs=[pl.BlockSpec((tm, d), lambda i, j: (i, 0)),
                  pl.BlockSpec((1, d), lambda i, j: (0, 0)),
                  w_spec],
        out_specs=[z_spec] + [w_spec] * convert,
        out_shape=[jax.ShapeDtypeStruct((n, dout), F32)] + [jax.ShapeDtypeStruct(w.shape, BF16)] * convert,
        scratch_shapes=[pltpu.VMEM((tm, d), BF16)],
        compiler_params=_params(("arbitrary", "arbitrary"), nbytes),
        name="norm_matmul",
    )(x, g.reshape(1, d), w)
    return out if convert else out[0]


def _pool_project(diffs, wp_ref, scale_ref, wb_ref, y_ref):
    gw = wp_ref.shape[1]
    for k, diff in enumerate(diffs):
        yk = _dot(diff.astype(BF16), wp_ref[k]) * scale_ref[:, k * gw:(k + 1) * gw]
        y_ref[:, k * gw:(k + 1) * gw] = yk.astype(BF16)
    return _dot(y_ref[...], wb_ref[...])


def _pool_seq_kernel(u_ref, wp_ref, scale_ref, wb_ref, o_ref, new_ref, ext_ref, y_ref, tmp_ref):
    tc, dp = u_ref.shape
    gw = wp_ref.shape[1]
    t = pl.program_id(1)
    end = POOL_HALO + tc

    @pl.when(t == 0)
    def _():
        ext_ref[0:POOL_HALO, :] = jnp.zeros((POOL_HALO, dp), F32)

    ext_ref[POOL_HALO:end, :] = u_ref[...]
    pos = (t * tc + 1 + lax.broadcasted_iota(jnp.int32, (tc, 1), 0)).astype(F32)
    diffs = []
    for k, w in enumerate(POOL_WINDOWS):
        cols = slice(k * gw, (k + 1) * gw)
        u = ext_ref[POOL_HALO:end, cols]
        src, m, start = (ext_ref, cols), 1, 0
        while m < w:
            ref, c = src
            start += SUBLANES
            s = ref[start:end, c] + ref[start - m:end - m, c]
            m *= 2
            if m < w:
                level = (m.bit_length() & 1, slice(None))
                tmp_ref[level[0], start:end, :] = s
                src = (tmp_ref.at[level[0]], level[1])
        s = s[POOL_HALO - start:, :]
        count = jnp.minimum(pos, float(w))
        diffs.append(s / count - u)
    o_ref[...] = _pool_project(diffs, wp_ref, scale_ref, wb_ref, y_ref)
    nb = new_ref.shape[1]
    new_ref[0] = ext_ref[POOL_HALO + tc - nb:POOL_HALO + tc, :]
    ext_ref[0:POOL_HALO, :] = ext_ref[tc:tc + POOL_HALO, :]


def _pool_seq(z, w_pool, pool_scale, w_branch, nb, t_len, tc, pool_buf):
    n = z.shape[0]
    ng, gw, _ = w_pool.shape
    dp = ng * gw
    dm = w_branch.shape[1]
    nt = t_len // tc
    nbytes = (2 * tc * dp * 4 + 2 * ng * gw * gw * 2 + 2 * dp * dm * 2 + 2 * tc * dm * 4
              + (tc + POOL_HALO) * dp * 4 + tc * dp * 2 + 4 * tc * gw * 4)
    return pl.pallas_call(
        _pool_seq_kernel,
        grid=(nb, nt),
        in_specs=[pl.BlockSpec((tc, dp), lambda b, t: (b * nt + t, 0)),
                  pl.BlockSpec((ng, gw, gw), lambda b, t: (0, 0, 0)),
                  pl.BlockSpec((1, dp), lambda b, t: (0, 0)),
                  pl.BlockSpec((dp, dm), lambda b, t: (0, 0))],
        out_specs=(pl.BlockSpec((tc, dm), lambda b, t: (b * nt + t, 0)),
                   pl.BlockSpec((1, pool_buf, dp), lambda b, t: (b, 0, 0))),
        out_shape=(jax.ShapeDtypeStruct((n, dm), F32),
                   jax.ShapeDtypeStruct((nb, pool_buf, dp), F32)),
        scratch_shapes=[pltpu.VMEM((tc + POOL_HALO, dp), F32), pltpu.VMEM((tc, dp), BF16),
                        pltpu.VMEM((2, tc + POOL_HALO, gw), F32)],
        compiler_params=_params(("arbitrary", "arbitrary"), nbytes),
        name="pool_seq",
    )(z, w_pool, pool_scale.reshape(1, dp), w_branch)


def _pool_step_kernel(u_ref, cache_ref, wp_ref, scale_ref, wb_ref, o_ref, new_ref, y_ref):
    dp = u_ref.shape[1]
    gw = wp_ref.shape[1]
    lb = cache_ref.shape[1] // dp
    diffs = []
    for k, w in enumerate(POOL_WINDOWS):
        u = u_ref[:, k * gw:(k + 1) * gw]
        s = u
        for j in range(1, w):
            s = s + cache_ref[:, (lb - j) * dp + k * gw:(lb - j) * dp + (k + 1) * gw]
        diffs.append(s / float(w) - u)
    o_ref[...] = _pool_project(diffs, wp_ref, scale_ref, wb_ref, y_ref)
    new_ref[:, 0:(lb - 1) * dp] = cache_ref[:, dp:lb * dp]
    new_ref[:, (lb - 1) * dp:lb * dp] = u_ref[...]


def _pool_step(z, cache, w_pool, pool_scale, w_branch):
    n = z.shape[0]
    ng, gw, _ = w_pool.shape
    dp = ng * gw
    dm = w_branch.shape[1]
    lbdp = cache.shape[1]
    nbytes = 2 * (n * dp * 4 + 2 * n * lbdp * 4 + ng * gw * gw * 2 + dp * dm * 2 + n * dm * 4) + n * dp * 2
    return pl.pallas_call(
        _pool_step_kernel,
        grid=(1,),
        in_specs=[pl.BlockSpec((n, dp), lambda i: (0, 0)),
                  pl.BlockSpec((n, lbdp), lambda i: (0, 0)),
                  pl.BlockSpec((ng, gw, gw), lambda i: (0, 0, 0)),
                  pl.BlockSpec((1, dp), lambda i: (0, 0)),
                  pl.BlockSpec((dp, dm), lambda i: (0, 0))],
        out_specs=(pl.BlockSpec((n, dm), lambda i: (0, 0)),
                   pl.BlockSpec((n, lbdp), lambda i: (0, 0))),
        out_shape=(jax.ShapeDtypeStruct((n, dm), F32),
                   jax.ShapeDtypeStruct((n, lbdp), F32)),
        scratch_shapes=[pltpu.VMEM((n, dp), BF16)],
        compiler_params=_params(("arbitrary",), nbytes),
        name="pool_step",
    )(z, cache, w_pool, pool_scale.reshape(1, dp), w_branch)


def _cmul_add(xr, xi, ar, ai, br, bi):
    return xr + ar * br - ai * bi, xi + ar * bi + ai * br


def _ssm_seq_kernel(n_cast, u_ref, bq_ref, cq_ref, pwr_ref, pwi_ref, d_ref, *refs):
    cast_in, (gy_ref, st_ref), refs = refs[:n_cast], refs[n_cast:n_cast + 2], refs[n_cast + 2:]
    cast_out, refs = refs[:n_cast], refs[n_cast:]
    bcat_ref, ccat_ref, us_ref, up_ref, h_ref, hb_ref, ys_ref, seed_ref, carry_ref = refs
    for src_ref, dst_ref in zip(cast_in, cast_out):
        dst_ref[...] = src_ref[...].astype(BF16)

    kw = u_ref.shape[1]
    nslab = h_ref.shape[1] // LANES
    npair = nslab // 2
    t = pl.program_id(2)

    def lanes(j):
        return slice(j * LANES, (j + 1) * LANES)

    def step_rows(n, count=1):
        return slice(n * SUBLANES, (n + count) * SUBLANES)

    def apow(n, j):
        return pwr_ref[0, n, j:j + 1, :], pwi_ref[0, n, j:j + 1, :]

    @pl.when(jnp.logical_and(pl.program_id(1) == 0, t == 0))
    def _():
        _expand_blockdiag(bq_ref, cq_ref, bcat_ref, ccat_ref)

    @pl.when(t == 0)
    def _():
        carry_ref[...] = jnp.zeros_like(carry_ref)

    for k in range(kw // LANES):
        for r in range(SUBLANES):
            us_ref[k, r * SEG_PITCH:r * SEG_PITCH + SEG_LEN, :] = u_ref[r * SEG_LEN:(r + 1) * SEG_LEN, lanes(k)]
    for n in range(SEG_LEN):
        for k in range(kw // LANES):
            up_ref[step_rows(n), lanes(k)] = us_ref[k, pl.ds(n, SUBLANES, stride=SEG_PITCH), :]
    u = up_ref[...]
    h_ref[...] = _dot(u.astype(BF16), bcat_ref[...])

    state = [jnp.zeros((SUBLANES, LANES), F32)] * nslab
    for n in range(SEG_LEN):
        for j in range(npair):
            sr, si = _cmul_add(h_ref[step_rows(n), lanes(j)], h_ref[step_rows(n), lanes(npair + j)],
                               *apow(0, j), state[j], state[npair + j])
            h_ref[step_rows(n), lanes(j)] = sr
            h_ref[step_rows(n), lanes(npair + j)] = si
            state[j], state[npair + j] = sr, si

    for j in range(npair):
        cr, ci = carry_ref[j], carry_ref[npair + j]
        for r in range(SUBLANES):
            seed_ref[j, r:r + 1, :] = cr
            seed_ref[npair + j, r:r + 1, :] = ci
            cr, ci = _cmul_add(state[j][r:r + 1, :], state[npair + j][r:r + 1, :],
                               *apow(SEG_LEN - 1, j), cr, ci)
        carry_ref[j] = cr
        carry_ref[npair + j] = ci
    st_ref[0, 0] = carry_ref[...]

    @pl.when(t >= 0)
    def _():
        for i in range(SEG_LEN // 2):
            for j in range(npair):
                parts = []
                for n in (2 * i, 2 * i + 1):
                    parts.append(_cmul_add(h_ref[step_rows(n), lanes(j)], h_ref[step_rows(n), lanes(npair + j)],
                                           *apow(n, j), seed_ref[j], seed_ref[npair + j]))
                hb_ref[step_rows(2 * i, 2), lanes(j)] = (
                    jnp.concatenate([p[0] for p in parts], axis=0).astype(BF16))
                hb_ref[step_rows(2 * i, 2), lanes(npair + j)] = (
                    jnp.concatenate([p[1] for p in parts], axis=0).astype(BF16))

        y = _dot(hb_ref[...], ccat_ref[...]) + d_ref[...] * up_ref[...]
        g = jax.nn.gelu(y)
        for n in range(SEG_LEN):
            for k in range(kw // LANES):
                ys_ref[k, pl.ds(n, SUBLANES, stride=SEG_PITCH), :] = g[step_rows(n), lanes(k)]
        for k in range(kw // LANES):
            for r in range(SUBLANES):
                gy_ref[r * SEG_LEN:(r + 1) * SEG_LEN, lanes(k)] = (
                    ys_ref[k, r * SEG_PITCH:r * SEG_PITCH + SEG_LEN, :].astype(BF16))


def _ssm_seq(z, col0, bq, cq, pwr, pwi, d_skip, nb, t_len, side_casts):
    n = z.shape[0]
    nkb, _, kw, p = bq.shape
    sw = 2 * SSM_BLOCK_GROUPS * p
    nslab = sw // LANES
    tc = SUBLANES * SEG_LEN
    nt = t_len // tc
    cb = col0 // kw
    n_steps = nkb * nb * nt
    cast_rows = [w.shape[0] // n_steps for w in side_casts]
    assert all(r % (2 * SUBLANES) == 0 and r * n_steps == w.shape[0] for r, w in zip(cast_rows, side_casts))
    nbytes = (2 * tc * kw * 4 + 4 * kw * sw * 2 + 2 * (SEG_LEN + SUBLANES) * sw * 4 + 2 * tc * kw * 2
              + 2 * SUBLANES * SEG_PITCH * kw * 4 + tc * kw * 4 + tc * sw * 4 + tc * sw * 2 + tc * sw * 4
              + sum(2 * r * w.shape[1] * 6 for r, w in zip(cast_rows, side_casts)))

    def cast_spec(r, w):
        return pl.BlockSpec((r, w.shape[1]), lambda k, b, t: ((k * nb + b) * nt + t, 0))

    cast_specs = [cast_spec(r, w) for r, w in zip(cast_rows, side_casts)]
    gy, st, *cast = pl.pallas_call(
        functools.partial(_ssm_seq_kernel, len(side_casts)),
        grid=(nkb, nb, nt),
        in_specs=[pl.BlockSpec((tc, kw), lambda k, b, t: (b * nt + t, cb + k)),
                  pl.BlockSpec((1, 2, kw, p), lambda k, b, t: (k, 0, 0, 0)),
                  pl.BlockSpec((1, 2, p, kw), lambda k, b, t: (k, 0, 0, 0)),
                  pl.BlockSpec((1, SEG_LEN, nslab // 2, LANES), lambda k, b, t: (k, 0, 0, 0)),
                  pl.BlockSpec((1, SEG_LEN, nslab // 2, LANES), lambda k, b, t: (k, 0, 0, 0)),
                  pl.BlockSpec((1, kw), lambda k, b, t: (0, k))] + cast_specs,
        out_specs=[pl.BlockSpec((tc, kw), lambda k, b, t: (b * nt + t, k)),
                   pl.BlockSpec((1, 1, nslab, 1, LANES), lambda k, b, t: (b, k, 0, 0, 0))] + cast_specs,
        out_shape=[jax.ShapeDtypeStruct((n, nkb * kw), BF16),
                   jax.ShapeDtypeStruct((nb, nkb, nslab, 1, LANES), F32)]
                  + [jax.ShapeDtypeStruct(w.shape, BF16) for w in side_casts],
        scratch_shapes=[pltpu.VMEM((kw, sw), BF16), pltpu.VMEM((sw, kw), BF16),
                        pltpu.VMEM((kw // LANES, SUBLANES * SEG_PITCH, LANES), F32), pltpu.VMEM((tc, kw), F32),
                        pltpu.VMEM((tc, sw), F32), pltpu.VMEM((tc, sw), BF16),
                        pltpu.VMEM((kw // LANES, SUBLANES * SEG_PITCH, LANES), F32),
                        pltpu.VMEM((nslab, SUBLANES, LANES), F32), pltpu.VMEM((nslab, 1, LANES), F32)],
        compiler_params=_params(("arbitrary", "arbitrary", "arbitrary"), nbytes),
        name="ssm_seq",
    )(z, bq, cq, pwr, pwi, d_skip.reshape(1, nkb * kw), *side_casts)
    st = st.reshape(nb, nkb, 2, sw // 2)
    return gy, st[:, :, 0].reshape(nb, -1), st[:, :, 1].reshape(nb, -1), cast


def _ssm_step_kernel(u_ref, h0r_ref, h0i_ref, bq_ref, cq_ref, a_ref, d_ref,
                     gy_ref, h1r_ref, h1i_ref, h_ref, bcat_ref, ccat_ref):
    half = h0r_ref.shape[1]
    _expand_blockdiag(bq_ref, cq_ref, bcat_ref, ccat_ref)
    u = u_ref[...]
    bu = _dot(u.astype(BF16), bcat_ref[...])
    hr, hi = _cmul_add(bu[:, 0:half], bu[:, half:2 * half],
                       a_ref[0, :, 0:half], a_ref[0, :, half:2 * half],
                       h0r_ref[...], h0i_ref[...])
    h1r_ref[...] = hr
    h1i_ref[...] = hi
    h_ref[:, 0:half] = hr.astype(BF16)
    h_ref[:, half:2 * half] = hi.astype(BF16)
    y = _dot(h_ref[...], ccat_ref[...]) + d_ref[...] * u
    gy_ref[...] = jax.nn.gelu(y).astype(BF16)


def _ssm_step(z, col0, h0_re, h0_im, bq, cq, a_row, d_skip):
    n = z.shape[0]
    nkb, _, kw, p = bq.shape
    half = SSM_BLOCK_GROUPS * p
    sw = 2 * half
    cb = col0 // kw
    nbytes = 2 * (n * kw * 4 + 4 * n * half * 4 + 2 * kw * sw * 2 + sw * 4 + n * kw * 2) + n * sw * 6
    return pl.pallas_call(
        _ssm_step_kernel,
        grid=(nkb,),
        in_specs=[pl.BlockSpec((n, kw), lambda k: (0, cb + k)),
                  pl.BlockSpec((n, half), lambda k: (0, k)),
                  pl.BlockSpec((n, half), lambda k: (0, k)),
                  pl.BlockSpec((1, 2, kw, p), lambda k: (k, 0, 0, 0)),
                  pl.BlockSpec((1, 2, p, kw), lambda k: (k, 0, 0, 0)),
                  pl.BlockSpec((1, 1, sw), lambda k: (k, 0, 0)),
                  pl.BlockSpec((1, kw), lambda k: (0, k))],
        out_specs=(pl.BlockSpec((n, kw), lambda k: (0, k)),
                   pl.BlockSpec((n, half), lambda k: (0, k)),
                   pl.BlockSpec((n, half), lambda k: (0, k))),
        out_shape=(jax.ShapeDtypeStruct((n, nkb * kw), BF16),
                   jax.ShapeDtypeStruct((n, nkb * half), F32),
                   jax.ShapeDtypeStruct((n, nkb * half), F32)),
        scratch_shapes=[pltpu.VMEM((n, sw), BF16), pltpu.VMEM((kw, sw), BF16), pltpu.VMEM((sw, kw), BF16)],
        compiler_params=_params(("arbitrary",), nbytes),
        name="ssm_step",
    )(z, h0_re, h0_im, bq, cq, a_row, d_skip.reshape(1, nkb * kw))


def _mix_out_kernel(gy_ref, bp_ref, gp_ref, gs_ref, x_ref, wg_ref, wb_ref, wo_ref, o_ref):
    ds = wb_ref.shape[0]
    g = _dot(gy_ref[...], wg_ref[...])
    y = g[:, 0:ds] * jax.nn.sigmoid(g[:, ds:2 * ds])
    bs = _dot(y.astype(BF16), wb_ref[...])
    merged = jax.nn.sigmoid(gp_ref[...]) * bp_ref[...] + jax.nn.sigmoid(gs_ref[...]) * bs
    o_ref[...] = x_ref[...] + _dot(merged.astype(BF16), wo_ref[...])


def _mix_out(gy, bp, z, gate_col0, x, w_glu, w_branch, w_out, tm):
    n, ds = gy.shape
    dm = w_branch.shape[1]
    gb = gate_col0 // dm
    w_bytes = (ds * 2 * ds + ds * dm + dm * dm) * 2
    nbytes = 2 * (tm * ds * 2 + 5 * tm * dm * 4) + w_bytes + 6 * tm * dm * 4

    def resident(shape):
        return pl.BlockSpec(shape, lambda i: (0, 0), pipeline_mode=pl.Buffered(1))

    return pl.pallas_call(
        _mix_out_kernel,
        grid=(n // tm,),
        in_specs=[pl.BlockSpec((tm, ds), lambda i: (i, 0)),
                  pl.BlockSpec((tm, dm), lambda i: (i, 0)),
                  pl.BlockSpec((tm, dm), lambda i: (i, gb)),
                  pl.BlockSpec((tm, dm), lambda i: (i, gb + 1)),
                  pl.BlockSpec((tm, dm), lambda i: (i, 0)),
                  resident((ds, 2 * ds)), resident((ds, dm)), resident((dm, dm))],
        out_specs=pl.BlockSpec((tm, dm), lambda i: (i, 0)),
        out_shape=jax.ShapeDtypeStruct((n, dm), F32),
        compiler_params=_params(("arbitrary",), nbytes),
        name="mix_out",
    )(gy, bp, z, z, x, w_glu, w_branch, w_out)


def _ffn_gate_down(x_ref, conv, v, wd_ref, o_ref, acc_ref):
    c = pl.program_id(1)
    part = _dot((jax.nn.gelu(conv) * v).astype(BF16), wd_ref[...])

    @pl.when(c == 0)
    def _():
        acc_ref[...] = part

    @pl.when(c > 0)
    def _():
        acc_ref[...] += part

    @pl.when(c == pl.num_programs(1) - 1)
    def _():
        o_ref[...] = x_ref[...] + acc_ref[...]


def _ffn_seq_kernel(tiles_per_seq, x_ref, g_ref, wa_ref, wv_ref, wc_ref, bc_ref, wd_ref,
                    o_ref, new_ref, h_ref, ext_ref, carry_ref, gate_ref):
    tm = x_ref.shape[0]
    tf = gate_ref.shape[1]
    i = pl.program_id(0)
    c = pl.program_id(1)

    @pl.when(jnp.logical_and(i == 0, c == 0))
    def _():
        o_ref[...] = jnp.zeros(o_ref.shape, F32)
        carry_ref[...] = jnp.zeros(carry_ref.shape, F32)

    @pl.when(c == 0)
    def _():
        h_ref[...] = _rmsnorm(x_ref[...], g_ref[...]).astype(BF16)

    seq_start = i % tiles_per_seq == 0
    nb = new_ref.shape[1]
    for q in range(tf // FFN_SLICE):
        cols = slice(q * FFN_SLICE, (q + 1) * FFN_SLICE)
        a = _dot(h_ref[...], wa_ref[:, cols])
        v = _dot(h_ref[...], wv_ref[:, cols])
        ext_ref[0:CONV_HALO, cols] = jnp.where(seq_start, 0.0, carry_ref[c, :, cols])
        ext_ref[CONV_HALO:CONV_HALO + tm, cols] = a
        conv = bc_ref[:, cols] + wc_ref[CONV_W - 1:CONV_W, cols] * a
        for j in range(CONV_W - 1):
            off = CONV_HALO - (CONV_W - 1) + j
            conv = conv + wc_ref[j:j + 1, cols] * ext_ref[off:off + tm, cols]
        carry_ref[c, :, cols] = ext_ref[tm:tm + CONV_HALO, cols]
        new_ref[0, :, cols] = ext_ref[CONV_HALO + tm - nb:CONV_HALO + tm, cols]
        gate_ref[:, cols] = (jax.nn.gelu(conv) * v).astype(BF16)
    o_ref[...] = _dot(gate_ref[...], wd_ref[...]) + jnp.where(c == 0, x_ref[...], o_ref[...])


def _ffn_seq(x, g, w_up, w_conv, b_conv, w_down, nb, t_len, tm, tf, conv_buf):
    n, d = x.shape
    dff = w_down.shape[0]
    nc = dff // tf
    tps = t_len // tm
    nbytes = (4 * tm * d * 4 + 4 * d * tf * 2 + 2 * tf * d * 2 + tm * d * 2 + tm * tf * 2
              + (tm + CONV_HALO) * tf * 4 + nc * CONV_HALO * tf * 4 + 6 * tm * FFN_SLICE * 4)
    out, new_tail = pl.pallas_call(
        functools.partial(_ffn_seq_kernel, tps),
        grid=(n // tm, nc),
        in_specs=[pl.BlockSpec((tm, d), lambda i, c: (i, 0)),
                  pl.BlockSpec((1, d), lambda i, c: (0, 0)),
                  pl.BlockSpec((d, tf), lambda i, c: (0, c)),
                  pl.BlockSpec((d, tf), lambda i, c: (0, nc + c)),
                  pl.BlockSpec((CONV_W, tf), lambda i, c: (0, c)),
                  pl.BlockSpec((1, tf), lambda i, c: (0, c)),
                  pl.BlockSpec((tf, d), lambda i, c: (c, 0))],
        out_specs=(pl.BlockSpec((tm, d), lambda i, c: (i, 0)),
                   pl.BlockSpec((1, conv_buf, tf), lambda i, c: (i, 0, c))),
        out_shape=(jax.ShapeDtypeStruct((n, d), F32),
                   jax.ShapeDtypeStruct((n // tm, conv_buf, dff), F32)),
        scratch_shapes=[pltpu.VMEM((tm, d), BF16), pltpu.VMEM((tm + CONV_HALO, tf), F32),
                        pltpu.VMEM((nc, CONV_HALO, tf), F32), pltpu.VMEM((tm, tf), BF16)],
        compiler_params=_params(("arbitrary", "arbitrary"), nbytes),
        name="ffn_seq",
    )(x, g.reshape(1, d), w_up, w_up, w_conv, b_conv.reshape(1, dff), w_down)
    return out, new_tail[tps - 1::tps]


def _ffn_step_kernel(x_ref, g_ref, wa_ref, wv_ref, wc_ref, bc_ref, wd_ref, p0_ref, p1_ref,
                     o_ref, a_ref, h_ref, acc_ref):
    @pl.when(pl.program_id(1) == 0)
    def _():
        h_ref[...] = _rmsnorm(x_ref[...], g_ref[...]).astype(BF16)

    a = _dot(h_ref[...], wa_ref[...])
    v = _dot(h_ref[...], wv_ref[...])
    conv = bc_ref[...] + wc_ref[2:3, :] * a + wc_ref[1:2, :] * p1_ref[...] + wc_ref[0:1, :] * p0_ref[...]
    a_ref[...] = a
    _ffn_gate_down(x_ref, conv, v, wd_ref, o_ref, acc_ref)


def _ffn_step(x, g, w_up, w_conv, b_conv, w_down, cache, tf):
    n, d = x.shape
    dff = w_down.shape[0]
    nc = dff // tf
    nbytes = 4 * n * d * 4 + 4 * d * tf * 2 + 2 * tf * d * 2 + n * d * 6 + 12 * n * tf * 4
    return pl.pallas_call(
        _ffn_step_kernel,
        grid=(1, nc),
        in_specs=[pl.BlockSpec((n, d), lambda i, c: (0, 0)),
                  pl.BlockSpec((1, d), lambda i, c: (0, 0)),
                  pl.BlockSpec((d, tf), lambda i, c: (0, c)),
                  pl.BlockSpec((d, tf), lambda i, c: (0, nc + c)),
                  pl.BlockSpec((CONV_W, tf), lambda i, c: (0, c)),
                  pl.BlockSpec((1, tf), lambda i, c: (0, c)),
                  pl.BlockSpec((tf, d), lambda i, c: (c, 0)),
                  pl.BlockSpec((n, tf), lambda i, c: (0, c)),
                  pl.BlockSpec((n, tf), lambda i, c: (0, nc + c))],
        out_specs=(pl.BlockSpec((n, d), lambda i, c: (0, 0)),
                   pl.BlockSpec((n, tf), lambda i, c: (0, c))),
        out_shape=(jax.ShapeDtypeStruct((n, d), F32),
                   jax.ShapeDtypeStruct((n, dff), F32)),
        scratch_shapes=[pltpu.VMEM((n, d), BF16), pltpu.VMEM((n, d), F32)],
        compiler_params=_params(("arbitrary", "arbitrary"), nbytes),
        name="ffn_step",
    )(x, g.reshape(1, d), w_up, w_up, w_conv, b_conv.reshape(1, dff), w_down, cache, cache)


def _ple_final_kernel(x_ref, p_ref, gp_ref, wg_ref, wp_ref, gf_ref, o_ref):
    x = x_ref[...]
    gate = jax.nn.sigmoid(_dot(_rmsnorm(x, gp_ref[...]).astype(BF16), wg_ref[...]))
    x = x + gate * _dot(p_ref[...].astype(BF16), wp_ref[...])
    o_ref[...] = _rmsnorm(x, gf_ref[...])


def _ple_final(x, p, g_ple, w_gate, w_ple, g_final, tm):
    n, d = x.shape
    dp = p.shape[1]
    nbytes = 2 * (2 * tm * d * 4 + tm * dp * 4 + d * d * 2 + dp * d * 2) + 4 * tm * d * 4
    return pl.pallas_call(
        _ple_final_kernel,
        grid=(n // tm,),
        in_specs=[pl.BlockSpec((tm, d), lambda i: (i, 0)),
                  pl.BlockSpec((tm, dp), lambda i: (i, 0)),
                  pl.BlockSpec((1, d), lambda i: (0, 0)),
                  pl.BlockSpec((d, d), lambda i: (0, 0)),
                  pl.BlockSpec((dp, d), lambda i: (0, 0)),
                  pl.BlockSpec((1, d), lambda i: (0, 0))],
        out_specs=pl.BlockSpec((tm, d), lambda i: (i, 0)),
        out_shape=jax.ShapeDtypeStruct((n, d), F32),
        compiler_params=_params(("arbitrary",), nbytes),
        name="ple_final",
    )(x, p, g_ple.reshape(1, d), w_gate, w_ple, g_final.reshape(1, d))


def kernel(x_prompt, x_sample, cache_pool, state_ssm_re, state_ssm_im, cache_conv, p_prompt, p_sample, g_mix, w_in, w_pool, pool_scale, ssm_lam_re, ssm_lam_im, ssm_log_dt, ssm_b_re, ssm_b_im, ssm_c_re, ssm_c_im, ssm_d, w_glu, w_branch_pool, w_branch_ssm, w_out, g_ffn, w_up, w_conv, b_conv, w_down, g_ple, w_ple_gate, w_ple, g_final):
    depth = g_mix.shape[0]
    nb, t_len, d = x_prompt.shape
    ns = x_sample.shape[0]
    assert x_sample.shape[1] == 1, "the sample group advances one step per call"
    pool_buf, d_pool = cache_pool.shape[2], cache_pool.shape[3]
    conv_buf, d_ff = cache_conv.shape[2], cache_conv.shape[3]
    n_grp, n_state = ssm_lam_re.shape[1], ssm_lam_re.shape[2]
    d_ssm = ssm_d.shape[1]
    assert pool_buf == max(POOL_WINDOWS) - 1 and conv_buf == CONV_W - 1
    assert POOL_HALO == SUBLANES * (max(POOL_WINDOWS).bit_length() - 1) and pool_buf <= POOL_HALO
    assert n_state == SSM_STATE and d_ssm == n_grp * SSM_GROUP

    xp = x_prompt.reshape(nb * t_len, d)
    xs = x_sample.reshape(ns, d)
    outs = [[] for _ in range(8)]
    for i in range(depth):
        w_pool_b, w_ple_b = w_pool[i].astype(BF16), w_ple[i].astype(BF16)
        bq, cq, a_pow_re, a_pow_im, a_row = _ssm_params(ssm_lam_re[i], ssm_lam_im[i], ssm_log_dt[i],
                                                        ssm_b_re[i], ssm_b_im[i], ssm_c_re[i], ssm_c_im[i])

        z_sample, w_in_b = _norm_matmul(xs, g_mix[i], w_in[i], ns, IN_COLS)

        z = _norm_matmul(xp, g_mix[i], w_in_b, IN_ROWS, IN_COLS)
        gy, st_re, st_im, (w_up_b, w_down_b, w_out_b, w_pg_b, w_glu_b, w_bs_b, w_bp_b) = _ssm_seq(
            z, d_pool, bq, cq, a_pow_re, a_pow_im, ssm_d[i], nb, t_len,
            [w_up[i], w_down[i], w_out[i], w_ple_gate[i], w_glu[i], w_branch_ssm[i], w_branch_pool[i]])

        def mix_tail(x, z, bp, gy, tm):
            return _mix_out(gy, bp, z, d_pool + d_ssm, x, w_glu_b, w_bs_b, w_out_b, tm)

        bp, pool_new = _pool_seq(z, w_pool_b, pool_scale[i], w_bp_b, nb, t_len, POOL_ROWS, pool_buf)
        xp = mix_tail(xp, z, bp, gy, MIX_ROWS)
        xp, conv_new = _ffn_seq(xp, g_ffn[i], w_up_b, w_conv[i], b_conv[i], w_down_b, nb, t_len, FFN_ROWS, FFN_COLS,
                                conv_buf)
        xp_out = _ple_final(xp, p_prompt[i].reshape(nb * t_len, -1), g_ple[i], w_pg_b, w_ple_b, g_final, PLE_ROWS)
        for lst, val in zip(outs[:4], (pool_new, st_re.reshape(nb, n_grp, n_state),
                                       st_im.reshape(nb, n_grp, n_state), conv_new)):
            lst.append(val)

        z = z_sample
        bp, pool_new = _pool_step(z, cache_pool[i].reshape(ns, pool_buf * d_pool), w_pool_b, pool_scale[i], w_bp_b)
        gy, st_re, st_im = _ssm_step(z, d_pool, state_ssm_re[i].reshape(ns, -1), state_ssm_im[i].reshape(ns, -1),
                                     bq, cq, a_row, ssm_d[i])
        xs = mix_tail(xs, z, bp, gy, ns)
        xs, a_new = _ffn_step(xs, g_ffn[i], w_up_b, w_conv[i], b_conv[i], w_down_b,
                              cache_conv[i].reshape(ns, conv_buf * d_ff), FFN_STEP_COLS)
        conv_new = jnp.concatenate([cache_conv[i][:, 1:], a_new[:, None, :]], axis=1)
        xs_out = _ple_final(xs, p_sample[i].reshape(ns, -1), g_ple[i], w_pg_b, w_ple_b, g_final, ns)
        for lst, val in zip(outs[4:], (pool_new.reshape(ns, pool_buf, d_pool), st_re.reshape(ns, n_grp, n_state),
                                       st_im.reshape(ns, n_grp, n_state), conv_new)):
            lst.append(val)

    assert depth == 1
    y_prompt = xp_out.reshape(nb, t_len, d)
    y_sample = xs_out.reshape(ns, 1, d)
    return (y_prompt, y_sample) + tuple(jnp.stack(o, axis=0) for o in outs)
```

```python
import functools

import jax
import jax.numpy as jnp
from jax import lax
from jax.experimental import pallas as pl
from jax.experimental.pallas import tpu as pltpu

F32 = jnp.float32
BF16 = jnp.bfloat16

EPS = 1e-6
POOL_WINDOWS = (2, 4, 8, 16)
POOL_HALO = 32
SSM_GROUP = 16
SSM_STATE = 64
SSM_BLOCK_GROUPS = 16
LANES = 128
SUBLANES = 8
SEG_LEN = 128
SEG_PITCH = SEG_LEN + 8
CONV_W = 3
CONV_HALO = 8
V7X_MXU_COLS = 256
FFN_SLICE = V7X_MXU_COLS
W_TILE = 2 * V7X_MXU_COLS
V7X_VMEM_BYTES = 64 * 1024 * 1024
VMEM_UNSCOPED_BYTES = 6 << 20
VMEM_TEMP_BYTES = 8 << 20

IN_ROWS, IN_COLS = 1024, 3 * W_TILE
POOL_ROWS = 1024
MIX_ROWS = 256
FFN_ROWS, FFN_COLS = 1024, W_TILE
FFN_STEP_COLS = 2 * W_TILE
PLE_ROWS = 512


def _vmem_limit(nbytes):
    return int(min(nbytes * 1.25 + VMEM_TEMP_BYTES, V7X_VMEM_BYTES - VMEM_UNSCOPED_BYTES))


def _params(sem, nbytes):
    return pltpu.CompilerParams(dimension_semantics=sem, vmem_limit_bytes=_vmem_limit(nbytes))


def _rmsnorm(x, g):
    return x * lax.rsqrt(jnp.mean(x * x, axis=-1, keepdims=True) + EPS) * g


def _dot(a, b):
    return jnp.dot(a, b, preferred_element_type=F32)


def _abar(lr, li, logdt):
    dt = jnp.exp(logdt)
    mag = jnp.exp(lr * dt)
    return mag * jnp.cos(li * dt), mag * jnp.sin(li * dt)


def _ssm_params_kernel(lr_ref, li_ref, logdt_ref, lrs_ref, lis_ref, logdts_ref, br_ref, bi_ref,
                       pwr_ref, pwi_ref, bbr_ref, bbi_ref):
    lr = lr_ref[...]
    li = li_ref[...]
    a_re, a_im = _abar(lr, li, logdt_ref[...])
    nr = a_re - 1.0
    ni = a_im
    den = lr * lr + li * li
    coef_re = ((nr * lr + ni * li) / den)[:, None, :]
    coef_im = ((ni * lr - nr * li) / den)[:, None, :]
    br = br_ref[...]
    bi = bi_ref[...]
    bbr_ref[...] = coef_re * br - coef_im * bi
    bbi_ref[...] = coef_re * bi + coef_im * br
    a_re, a_im = _abar(lrs_ref[...], lis_ref[...], logdts_ref[...])
    nkb, _, per_blk, _ = pwr_ref.shape
    pr, pi = a_re, a_im
    for n in range(SEG_LEN):
        if n:
            pr, pi = pr * a_re - pi * a_im, pr * a_im + pi * a_re
        for k in range(nkb):
            pwr_ref[k, n] = pr[k * per_blk:(k + 1) * per_blk, :]
            pwi_ref[k, n] = pi[k * per_blk:(k + 1) * per_blk, :]


def _ssm_params(lam_re, lam_im, log_dt, b_re, b_im, c_re, c_im):
    g, p = lam_re.shape
    h = b_re.shape[-1]
    nkb = g // SSM_BLOCK_GROUPS
    bl = SSM_BLOCK_GROUPS
    per_blk = bl * p // LANES
    slab = (g * p // LANES, LANES)
    pwr, pwi, bbr, bbi = pl.pallas_call(
        _ssm_params_kernel,
        out_shape=(jax.ShapeDtypeStruct((nkb, SEG_LEN, per_blk, LANES), F32),
                   jax.ShapeDtypeStruct((nkb, SEG_LEN, per_blk, LANES), F32),
                   jax.ShapeDtypeStruct((g, h, p), F32),
                   jax.ShapeDtypeStruct((g, h, p), F32)),
        name="ssm_params",
    )(lam_re, lam_im, log_dt.reshape(g, 1),
      lam_re.reshape(slab), lam_im.reshape(slab), jnp.broadcast_to(log_dt[:, None], (g, p)).reshape(slab),
      jnp.transpose(b_re, (0, 2, 1)), jnp.transpose(b_im, (0, 2, 1)))

    def c_rows(c):
        return jnp.transpose(c.reshape(nkb, bl * h, p), (0, 2, 1))

    bq = jnp.stack([bbr.reshape(nkb, bl * h, p), bbi.reshape(nkb, bl * h, p)], axis=1)
    cq = jnp.stack([c_rows(c_re), -c_rows(c_im)], axis=1)
    a_row = jnp.concatenate([pwr[:, 0].reshape(nkb, 1, bl * p), pwi[:, 0].reshape(nkb, 1, bl * p)], axis=-1)
    return bq, cq, pwr, pwi, a_row


def _expand_blockdiag(bq_ref, cq_ref, bcat_ref, ccat_ref):
    kw, p = bq_ref.shape[2], bq_ref.shape[3]
    half = bcat_ref.shape[1] // 2
    hch = kw * p // half
    log_p, log_h = p.bit_length() - 1, hch.bit_length() - 1
    assert (1 << log_p) == p and (1 << log_h) == hch

    def iota(shape, axis):
        return lax.broadcasted_iota(jnp.int32, shape, axis)

    def same(a, b):
        return a == b

    tile_b = same(iota((p, half), 1) & (p - 1), iota((p, half), 0)).astype(BF16)
    mask_b = same(iota((kw, half), 0) >> log_h, iota((kw, half), 1) >> log_p)
    tile_c = same(iota((half, p), 0) & (p - 1), iota((half, p), 1)).astype(BF16)
    mask_c = same(iota((half, kw), 0) >> log_p, iota((half, kw), 1) >> log_h)
    for s in range(2):
        full = _dot(bq_ref[0, s].astype(BF16), tile_b)
        bcat_ref[:, s * half:(s + 1) * half] = jnp.where(mask_b, full, 0.0).astype(BF16)
        full = _dot(tile_c, cq_ref[0, s].astype(BF16))
        ccat_ref[s * half:(s + 1) * half, :] = jnp.where(mask_c, full, 0.0).astype(BF16)


def _norm_matmul_kernel(x_ref, g_ref, w_ref, o_ref, *rest):
    *wb_ref, h_ref = rest

    @pl.when(pl.program_id(1) == 0)
    def _():
        h_ref[...] = _rmsnorm(x_ref[...], g_ref[...]).astype(BF16)

    w = w_ref[...]
    if wb_ref:
        w = w.astype(BF16)
        wb_ref[0][...] = w
    o_ref[...] = _dot(h_ref[...], w)


def _norm_matmul(x, g, w, tm, tn):
    n, d = x.shape
    dout = w.shape[1]
    convert = w.dtype == F32
    assert not convert or n == tm, "each weight tile must be visited exactly once to be converted"
    nbytes = 2 * tm * d * 4 + tm * d * 2 + 2 * d * tn * w.dtype.itemsize + 2 * tm * tn * 4 + convert * 3 * d * tn * 2
    z_spec = pl.BlockSpec((tm, tn), lambda i, j: (i, j))
    w_spec = pl.BlockSpec((d, tn), lambda i, j: (0, j))
    out = pl.pallas_call(
        _norm_matmul_kernel,
        grid=(n // tm, dout // tn),
        in_specs=[pl.BlockSpec((tm, d), lambda i, j: (i, 0)),
                  pl.BlockSpec((1, d), lambda i, j: (0, 0)),
                  w_spec],
        out_specs=[z_spec] + [w_spec] * convert,
        out_shape=[jax.ShapeDtypeStruct((n, dout), F32)] + [jax.ShapeDtypeStruct(w.shape, BF16)] * convert,
        scratch_shapes=[pltpu.VMEM((tm, d), BF16)],
        compiler_params=_params(("arbitrary", "arbitrary"), nbytes),
        name="norm_matmul",
    )(x, g.reshape(1, d), w)
    return out if convert else out[0]


def _pool_project(diffs, wp_ref, scale_ref, wb_ref, y_ref):
    gw = wp_ref.shape[1]
    for k, diff in enumerate(diffs):
        yk = _dot(diff.astype(BF16), wp_ref[k]) * scale_ref[:, k * gw:(k + 1) * gw]
        y_ref[:, k * gw:(k + 1) * gw] = yk.astype(BF16)
    return _dot(y_ref[...], wb_ref[...])


def _pool_seq_kernel(u_ref, wp_ref, scale_ref, wb_ref, o_ref, new_ref, ext_ref, y_ref, tmp_ref):
    tc, dp = u_ref.shape
    gw = wp_ref.shape[1]
    t = pl.program_id(1)
    end = POOL_HALO + tc

    @pl.when(t == 0)
    def _():
        ext_ref[0:POOL_HALO, :] = jnp.zeros((POOL_HALO, dp), F32)

    ext_ref[POOL_HALO:end, :] = u_ref[...]
    pos = (t * tc + 1 + lax.broadcasted_iota(jnp.int32, (tc, 1), 0)).astype(F32)
    diffs = []
    for k, w in enumerate(POOL_WINDOWS):
        cols = slice(k * gw, (k + 1) * gw)
        u = ext_ref[POOL_HALO:end, cols]
        src, m, start = (ext_ref, cols), 1, 0
        while m < w:
            ref, c = src
            start += SUBLANES
            s = ref[start:end, c] + ref[start - m:end - m, c]
            m *= 2
            if m < w:
                level = (m.bit_length() & 1, slice(None))
                tmp_ref[level[0], start:end, :] = s
                src = (tmp_ref.at[level[0]], level[1])
        s = s[POOL_HALO - start:, :]
        count = jnp.minimum(pos, float(w))
        diffs.append(s / count - u)
    o_ref[...] = _pool_project(diffs, wp_ref, scale_ref, wb_ref, y_ref)
    nb = new_ref.shape[1]
    new_ref[0] = ext_ref[POOL_HALO + tc - nb:POOL_HALO + tc, :]
    ext_ref[0:POOL_HALO, :] = ext_ref[tc:tc + POOL_HALO, :]


def _pool_seq(z, w_pool, pool_scale, w_branch, nb, t_len, tc, pool_buf):
    n = z.shape[0]
    ng, gw, _ = w_pool.shape
    dp = ng * gw
    dm = w_branch.shape[1]
    nt = t_len // tc
    nbytes = (2 * tc * dp * 4 + 2 * ng * gw * gw * 2 + 2 * dp * dm * 2 + 2 * tc * dm * 4
              + (tc + POOL_HALO) * dp * 4 + tc * dp * 2 + 4 * tc * gw * 4)
    return pl.pallas_call(
        _pool_seq_kernel,
        grid=(nb, nt),
        in_specs=[pl.BlockSpec((tc, dp), lambda b, t: (b * nt + t, 0)),
                  pl.BlockSpec((ng, gw, gw), lambda b, t: (0, 0, 0)),
                  pl.BlockSpec((1, dp), lambda b, t: (0, 0)),
                  pl.BlockSpec((dp, dm), lambda b, t: (0, 0))],
        out_specs=(pl.BlockSpec((tc, dm), lambda b, t: (b * nt + t, 0)),
                   pl.BlockSpec((1, pool_buf, dp), lambda b, t: (b, 0, 0))),
        out_shape=(jax.ShapeDtypeStruct((n, dm), F32),
                   jax.ShapeDtypeStruct((nb, pool_buf, dp), F32)),
        scratch_shapes=[pltpu.VMEM((tc + POOL_HALO, dp), F32), pltpu.VMEM((tc, dp), BF16),
                        pltpu.VMEM((2, tc + POOL_HALO, gw), F32)],
        compiler_params=_params(("arbitrary", "arbitrary"), nbytes),
        name="pool_seq",
    )(z, w_pool, pool_scale.reshape(1, dp), w_branch)


def _pool_step_kernel(u_ref, cache_ref, wp_ref, scale_ref, wb_ref, o_ref, new_ref, y_ref):
    dp = u_ref.shape[1]
    gw = wp_ref.shape[1]
    lb = cache_ref.shape[1] // dp
    diffs = []
    for k, w in enumerate(POOL_WINDOWS):
        u = u_ref[:, k * gw:(k + 1) * gw]
        s = u
        for j in range(1, w):
            s = s + cache_ref[:, (lb - j) * dp + k * gw:(lb - j) * dp + (k + 1) * gw]
        diffs.append(s / float(w) - u)
    o_ref[...] = _pool_project(diffs, wp_ref, scale_ref, wb_ref, y_ref)
    new_ref[:, 0:(lb - 1) * dp] = cache_ref[:, dp:lb * dp]
    new_ref[:, (lb - 1) * dp:lb * dp] = u_ref[...]


def _pool_step(z, cache, w_pool, pool_scale, w_branch):
    n = z.shape[0]
    ng, gw, _ = w_pool.shape
    dp = ng * gw
    dm = w_branch.shape[1]
    lbdp = cache.shape[1]
    nbytes = 2 * (n * dp * 4 + 2 * n * lbdp * 4 + ng * gw * gw * 2 + dp * dm * 2 + n * dm * 4) + n * dp * 2
    return pl.pallas_call(
        _pool_step_kernel,
        grid=(1,),
        in_specs=[pl.BlockSpec((n, dp), lambda i: (0, 0)),
                  pl.BlockSpec((n, lbdp), lambda i: (0, 0)),
                  pl.BlockSpec((ng, gw, gw), lambda i: (0, 0, 0)),
                  pl.BlockSpec((1, dp), lambda i: (0, 0)),
                  pl.BlockSpec((dp, dm), lambda i: (0, 0))],
        out_specs=(pl.BlockSpec((n, dm), lambda i: (0, 0)),
                   pl.BlockSpec((n, lbdp), lambda i: (0, 0))),
        out_shape=(jax.ShapeDtypeStruct((n, dm), F32),
                   jax.ShapeDtypeStruct((n, lbdp), F32)),
        scratch_shapes=[pltpu.VMEM((n, dp), BF16)],
        compiler_params=_params(("arbitrary",), nbytes),
        name="pool_step",
    )(z, cache, w_pool, pool_scale.reshape(1, dp), w_branch)


def _cmul_add(xr, xi, ar, ai, br, bi):
    return xr + ar * br - ai * bi, xi + ar * bi + ai * br


def _ssm_seq_kernel(n_cast, u_ref, bq_ref, cq_ref, pwr_ref, pwi_ref, d_ref, *refs):
    cast_in, (gy_ref, st_ref), refs = refs[:n_cast], refs[n_cast:n_cast + 2], refs[n_cast + 2:]
    cast_out, refs = refs[:n_cast], refs[n_cast:]
    bcat_ref, ccat_ref, us_ref, up_ref, h_ref, hb_ref, ys_ref, seed_ref, carry_ref = refs
    for src_ref, dst_ref in zip(cast_in, cast_out):
        dst_ref[...] = src_ref[...].astype(BF16)

    kw = u_ref.shape[1]
    nslab = h_ref.shape[1] // LANES
    npair = nslab // 2
    t = pl.program_id(2)

    def lanes(j):
        return slice(j * LANES, (j + 1) * LANES)

    def step_rows(n, count=1):
        return slice(n * SUBLANES, (n + count) * SUBLANES)

    def apow(n, j):
        return pwr_ref[0, n, j:j + 1, :], pwi_ref[0, n, j:j + 1, :]

    @pl.when(jnp.logical_and(pl.program_id(1) == 0, t == 0))
    def _():
        _expand_blockdiag(bq_ref, cq_ref, bcat_ref, ccat_ref)

    @pl.when(t == 0)
    def _():
        carry_ref[...] = jnp.zeros_like(carry_ref)

    for k in range(kw // LANES):
        for r in range(SUBLANES):
            us_ref[k, r * SEG_PITCH:r * SEG_PITCH + SEG_LEN, :] = u_ref[r * SEG_LEN:(r + 1) * SEG_LEN, lanes(k)]
    for n in range(SEG_LEN):
        for k in range(kw // LANES):
            up_ref[step_rows(n), lanes(k)] = us_ref[k, pl.ds(n, SUBLANES, stride=SEG_PITCH), :]
    u = up_ref[...]
    h_ref[...] = _dot(u.astype(BF16), bcat_ref[...])

    state = [jnp.zeros((SUBLANES, LANES), F32)] * nslab
    for n in range(SEG_LEN):
        for j in range(npair):
            sr, si = _cmul_add(h_ref[step_rows(n), lanes(j)], h_ref[step_rows(n), lanes(npair + j)],
                               *apow(0, j), state[j], state[npair + j])
            h_ref[step_rows(n), lanes(j)] = sr
            h_ref[step_rows(n), lanes(npair + j)] = si
            state[j], state[npair + j] = sr, si

    for j in range(npair):
        cr, ci = carry_ref[j], carry_ref[npair + j]
        for r in range(SUBLANES):
            seed_ref[j, r:r + 1, :] = cr
            seed_ref[npair + j, r:r + 1, :] = ci
            cr, ci = _cmul_add(state[j][r:r + 1, :], state[npair + j][r:r + 1, :],
                               *apow(SEG_LEN - 1, j), cr, ci)
        carry_ref[j] = cr
        carry_ref[npair + j] = ci
    st_ref[0, 0] = carry_ref[...]

    @pl.when(t >= 0)
    def _():
        for i in range(SEG_LEN // 2):
            for j in range(npair):
                parts = []
                for n in (2 * i, 2 * i + 1):
                    parts.append(_cmul_add(h_ref[step_rows(n), lanes(j)], h_ref[step_rows(n), lanes(npair + j)],
                                           *apow(n, j), seed_ref[j], seed_ref[npair + j]))
                hb_ref[step_rows(2 * i, 2), lanes(j)] = (
                    jnp.concatenate([p[0] for p in parts], axis=0).astype(BF16))
                hb_ref[step_rows(2 * i, 2), lanes(npair + j)] = (
                    jnp.concatenate([p[1] for p in parts], axis=0).astype(BF16))

        y = _dot(hb_ref[...], ccat_ref[...]) + d_ref[...] * up_ref[...]
        g = jax.nn.gelu(y)
        for n in range(SEG_LEN):
            for k in range(kw // LANES):
                ys_ref[k, pl.ds(n, SUBLANES, stride=SEG_PITCH), :] = g[step_rows(n), lanes(k)]
        for k in range(kw // LANES):
            for r in range(SUBLANES):
                gy_ref[r * SEG_LEN:(r + 1) * SEG_LEN, lanes(k)] = (
                    ys_ref[k, r * SEG_PITCH:r * SEG_PITCH + SEG_LEN, :].astype(BF16))


def _ssm_seq(z, col0, bq, cq, pwr, pwi, d_skip, nb, t_len, side_casts):
    n = z.shape[0]
    nkb, _, kw, p = bq.shape
    sw = 2 * SSM_BLOCK_GROUPS * p
    nslab = sw // LANES
    tc = SUBLANES * SEG_LEN
    nt = t_len // tc
    cb = col0 // kw
    n_steps = nkb * nb * nt
    cast_rows = [w.shape[0] // n_steps for w in side_casts]
    assert all(r % (2 * SUBLANES) == 0 and r * n_steps == w.shape[0] for r, w in zip(cast_rows, side_casts))
    nbytes = (2 * tc * kw * 4 + 4 * kw * sw * 2 + 2 * (SEG_LEN + SUBLANES) * sw * 4 + 2 * tc * kw * 2
              + 2 * SUBLANES * SEG_PITCH * kw * 4 + tc * kw * 4 + tc * sw * 4 + tc * sw * 2 + tc * sw * 4
              + sum(2 * r * w.shape[1] * 6 for r, w in zip(cast_rows, side_casts)))

    def cast_spec(r, w):
        return pl.BlockSpec((r, w.shape[1]), lambda k, b, t: ((k * nb + b) * nt + t, 0))

    cast_specs = [cast_spec(r, w) for r, w in zip(cast_rows, side_casts)]
    gy, st, *cast = pl.pallas_call(
        functools.partial(_ssm_seq_kernel, len(side_casts)),
        grid=(nkb, nb, nt),
        in_specs=[pl.BlockSpec((tc, kw), lambda k, b, t: (b * nt + t, cb + k)),
                  pl.BlockSpec((1, 2, kw, p), lambda k, b, t: (k, 0, 0, 0)),
                  pl.BlockSpec((1, 2, p, kw), lambda k, b, t: (k, 0, 0, 0)),
                  pl.BlockSpec((1, SEG_LEN, nslab // 2, LANES), lambda k, b, t: (k, 0, 0, 0)),
                  pl.BlockSpec((1, SEG_LEN, nslab // 2, LANES), lambda k, b, t: (k, 0, 0, 0)),
                  pl.BlockSpec((1, kw), lambda k, b, t: (0, k))] + cast_specs,
        out_specs=[pl.BlockSpec((tc, kw), lambda k, b, t: (b * nt + t, k)),
                   pl.BlockSpec((1, 1, nslab, 1, LANES), lambda k, b, t: (b, k, 0, 0, 0))] + cast_specs,
        out_shape=[jax.ShapeDtypeStruct((n, nkb * kw), BF16),
                   jax.ShapeDtypeStruct((nb, nkb, nslab, 1, LANES), F32)]
                  + [jax.ShapeDtypeStruct(w.shape, BF16) for w in side_casts],
        scratch_shapes=[pltpu.VMEM((kw, sw), BF16), pltpu.VMEM((sw, kw), BF16),
                        pltpu.VMEM((kw // LANES, SUBLANES * SEG_PITCH, LANES), F32), pltpu.VMEM((tc, kw), F32),
                        pltpu.VMEM((tc, sw), F32), pltpu.VMEM((tc, sw), BF16),
                        pltpu.VMEM((kw // LANES, SUBLANES * SEG_PITCH, LANES), F32),
                        pltpu.VMEM((nslab, SUBLANES, LANES), F32), pltpu.VMEM((nslab, 1, LANES), F32)],
        compiler_params=_params(("arbitrary", "arbitrary", "arbitrary"), nbytes),
        name="ssm_seq",
    )(z, bq, cq, pwr, pwi, d_skip.reshape(1, nkb * kw), *side_casts)
    st = st.reshape(nb, nkb, 2, sw // 2)
    return gy, st[:, :, 0].reshape(nb, -1), st[:, :, 1].reshape(nb, -1), cast


def _ssm_step_kernel(u_ref, h0r_ref, h0i_ref, bq_ref, cq_ref, a_ref, d_ref,
                     gy_ref, h1r_ref, h1i_ref, h_ref, bcat_ref, ccat_ref):
    half = h0r_ref.shape[1]
    _expand_blockdiag(bq_ref, cq_ref, bcat_ref, ccat_ref)
    u = u_ref[...]
    bu = _dot(u.astype(BF16), bcat_ref[...])
    hr, hi = _cmul_add(bu[:, 0:half], bu[:, half:2 * half],
                       a_ref[0, :, 0:half], a_ref[0, :, half:2 * half],
                       h0r_ref[...], h0i_ref[...])
    h1r_ref[...] = hr
    h1i_ref[...] = hi
    h_ref[:, 0:half] = hr.astype(BF16)
    h_ref[:, half:2 * half] = hi.astype(BF16)
    y = _dot(h_ref[...], ccat_ref[...]) + d_ref[...] * u
    gy_ref[...] = jax.nn.gelu(y).astype(BF16)


def _ssm_step(z, col0, h0_re, h0_im, bq, cq, a_row, d_skip):
    n = z.shape[0]
    nkb, _, kw, p = bq.shape
    half = SSM_BLOCK_GROUPS * p
    sw = 2 * half
    cb = col0 // kw
    nbytes = 2 * (n * kw * 4 + 4 * n * half * 4 + 2 * kw * sw * 2 + sw * 4 + n * kw * 2) + n * sw * 6
    return pl.pallas_call(
        _ssm_step_kernel,
        grid=(nkb,),
        in_specs=[pl.BlockSpec((n, kw), lambda k: (0, cb + k)),
                  pl.BlockSpec((n, half), lambda k: (0, k)),
                  pl.BlockSpec((n, half), lambda k: (0, k)),
                  pl.BlockSpec((1, 2, kw, p), lambda k: (k, 0, 0, 0)),
                  pl.BlockSpec((1, 2, p, kw), lambda k: (k, 0, 0, 0)),
                  pl.BlockSpec((1, 1, sw), lambda k: (k, 0, 0)),
                  pl.BlockSpec((1, kw), lambda k: (0, k))],
        out_specs=(pl.BlockSpec((n, kw), lambda k: (0, k)),
                   pl.BlockSpec((n, half), lambda k: (0, k)),
                   pl.BlockSpec((n, half), lambda k: (0, k))),
        out_shape=(jax.ShapeDtypeStruct((n, nkb * kw), BF16),
                   jax.ShapeDtypeStruct((n, nkb * half), F32),
                   jax.ShapeDtypeStruct((n, nkb * half), F32)),
        scratch_shapes=[pltpu.VMEM((n, sw), BF16), pltpu.VMEM((kw, sw), BF16), pltpu.VMEM((sw, kw), BF16)],
        compiler_params=_params(("arbitrary",), nbytes),
        name="ssm_step",
    )(z, h0_re, h0_im, bq, cq, a_row, d_skip.reshape(1, nkb * kw))


def _mix_out_kernel(gy_ref, bp_ref, gp_ref, gs_ref, x_ref, wg_ref, wb_ref, wo_ref, o_ref):
    ds = wb_ref.shape[0]
    g = _dot(gy_ref[...], wg_ref[...])
    y = g[:, 0:ds] * jax.nn.sigmoid(g[:, ds:2 * ds])
    bs = _dot(y.astype(BF16), wb_ref[...])
    merged = jax.nn.sigmoid(gp_ref[...]) * bp_ref[...] + jax.nn.sigmoid(gs_ref[...]) * bs
    o_ref[...] = x_ref[...] + _dot(merged.astype(BF16), wo_ref[...])


def _mix_out(gy, bp, z, gate_col0, x, w_glu, w_branch, w_out, tm):
    n, ds = gy.shape
    dm = w_branch.shape[1]
    gb = gate_col0 // dm
    w_bytes = (ds * 2 * ds + ds * dm + dm * dm) * 2
    nbytes = 2 * (tm * ds * 2 + 5 * tm * dm * 4) + w_bytes + 6 * tm * dm * 4

    def resident(shape):
        return pl.BlockSpec(shape, lambda i: (0, 0), pipeline_mode=pl.Buffered(1))

    return pl.pallas_call(
        _mix_out_kernel,
        grid=(n // tm,),
        in_specs=[pl.BlockSpec((tm, ds), lambda i: (i, 0)),
                  pl.BlockSpec((tm, dm), lambda i: (i, 0)),
                  pl.BlockSpec((tm, dm), lambda i: (i, gb)),
                  pl.BlockSpec((tm, dm), lambda i: (i, gb + 1)),
                  pl.BlockSpec((tm, dm), lambda i: (i, 0)),
                  resident((ds, 2 * ds)), resident((ds, dm)), resident((dm, dm))],
        out_specs=pl.BlockSpec((tm, dm), lambda i: (i, 0)),
        out_shape=jax.ShapeDtypeStruct((n, dm), F32),
        compiler_params=_params(("arbitrary",), nbytes),
        name="mix_out",
    )(gy, bp, z, z, x, w_glu, w_branch, w_out)


def _ffn_gate_down(x_ref, conv, v, wd_ref, o_ref, acc_ref):
    c = pl.program_id(1)
    part = _dot((jax.nn.gelu(conv) * v).astype(BF16), wd_ref[...])

    @pl.when(c == 0)
    def _():
        acc_ref[...] = part

    @pl.when(c > 0)
    def _():
        acc_ref[...] += part

    @pl.when(c == pl.num_programs(1) - 1)
    def _():
        o_ref[...] = x_ref[...] + acc_ref[...]


def _ffn_seq_kernel(tiles_per_seq, x_ref, g_ref, wa_ref, wv_ref, wc_ref, bc_ref, wd_ref,
                    o_ref, new_ref, h_ref, ext_ref, carry_ref, gate_ref, v_ref):
    tm = x_ref.shape[0]
    tf = gate_ref.shape[1]
    i = pl.program_id(0)
    c = pl.program_id(1)

    @pl.when(jnp.logical_and(i == 0, c == 0))
    def _():
        o_ref[...] = jnp.zeros(o_ref.shape, F32)
        carry_ref[...] = jnp.zeros(carry_ref.shape, F32)

    @pl.when(c == 0)
    def _():
        h_ref[...] = _rmsnorm(x_ref[...], g_ref[...]).astype(BF16)

    seq_start = i % tiles_per_seq == 0
    nb = new_ref.shape[1]
    slot = jnp.minimum(c, 0)
    for q in range(tf // FFN_SLICE):
        cols = slice(q * FFN_SLICE, (q + 1) * FFN_SLICE)
        ext_ref[slot, 0:CONV_HALO, cols] = jnp.where(seq_start, 0.0, carry_ref[c, :, cols])
        ext_ref[slot, CONV_HALO:CONV_HALO + tm, cols] = _dot(h_ref[...], wa_ref[:, cols])
        v_ref[slot, :, cols] = _dot(h_ref[...], wv_ref[:, cols])
        a = ext_ref[slot, CONV_HALO:CONV_HALO + tm, cols]
        v = v_ref[slot, :, cols]
        conv = bc_ref[:, cols] + wc_ref[CONV_W - 1:CONV_W, cols] * a
        for j in range(CONV_W - 1):
            off = CONV_HALO - (CONV_W - 1) + j
            conv = conv + wc_ref[j:j + 1, cols] * ext_ref[slot, off:off + tm, cols]
        carry_ref[c, :, cols] = ext_ref[slot, tm:tm + CONV_HALO, cols]
        new_ref[0, :, cols] = ext_ref[slot, CONV_HALO + tm - nb:CONV_HALO + tm, cols]
        gate_ref[:, cols] = (jax.nn.gelu(conv) * v).astype(BF16)
    o_ref[...] = _dot(gate_ref[...], wd_ref[...]) + jnp.where(c == 0, x_ref[...], o_ref[...])


def _ffn_seq(x, g, w_up, w_conv, b_conv, w_down, nb, t_len, tm, tf, conv_buf):
    n, d = x.shape
    dff = w_down.shape[0]
    nc = dff // tf
    tps = t_len // tm
    nbytes = (4 * tm * d * 4 + 4 * d * tf * 2 + 2 * tf * d * 2 + tm * d * 2 + tm * tf * 2
              + (tm + CONV_HALO) * tf * 4 + nc * CONV_HALO * tf * 4 + 6 * tm * FFN_SLICE * 4)
    out, new_tail = pl.pallas_call(
        functools.partial(_ffn_seq_kernel, tps),
        grid=(n // tm, nc),
        in_specs=[pl.BlockSpec((tm, d), lambda i, c: (i, 0)),
                  pl.BlockSpec((1, d), lambda i, c: (0, 0)),
                  pl.BlockSpec((d, tf), lambda i, c: (0, c)),
                  pl.BlockSpec((d, tf), lambda i, c: (0, nc + c)),
                  pl.BlockSpec((CONV_W, tf), lambda i, c: (0, c)),
                  pl.BlockSpec((1, tf), lambda i, c: (0, c)),
                  pl.BlockSpec((tf, d), lambda i, c: (c, 0))],
        out_specs=(pl.BlockSpec((tm, d), lambda i, c: (i, 0)),
                   pl.BlockSpec((1, conv_buf, tf), lambda i, c: (i, 0, c))),
        out_shape=(jax.ShapeDtypeStruct((n, d), F32),
                   jax.ShapeDtypeStruct((n // tm, conv_buf, dff), F32)),
        scratch_shapes=[pltpu.VMEM((tm, d), BF16), pltpu.VMEM((1, tm + CONV_HALO, tf), F32),
                        pltpu.VMEM((nc, CONV_HALO, tf), F32), pltpu.VMEM((tm, tf), BF16),
                        pltpu.VMEM((1, tm, tf), F32)],
        compiler_params=_params(("arbitrary", "arbitrary"), nbytes),
        name="ffn_seq",
    )(x, g.reshape(1, d), w_up, w_up, w_conv, b_conv.reshape(1, dff), w_down)
    return out, new_tail[tps - 1::tps]


def _ffn_step_kernel(x_ref, g_ref, wa_ref, wv_ref, wc_ref, bc_ref, wd_ref, p0_ref, p1_ref,
                     o_ref, a_ref, h_ref, acc_ref):
    @pl.when(pl.program_id(1) == 0)
    def _():
        h_ref[...] = _rmsnorm(x_ref[...], g_ref[...]).astype(BF16)

    a = _dot(h_ref[...], wa_ref[...])
    v = _dot(h_ref[...], wv_ref[...])
    conv = bc_ref[...] + wc_ref[2:3, :] * a + wc_ref[1:2, :] * p1_ref[...] + wc_ref[0:1, :] * p0_ref[...]
    a_ref[...] = a
    _ffn_gate_down(x_ref, conv, v, wd_ref, o_ref, acc_ref)


def _ffn_step(x, g, w_up, w_conv, b_conv, w_down, cache, tf):
    n, d = x.shape
    dff = w_down.shape[0]
    nc = dff // tf
    nbytes = 4 * n * d * 4 + 4 * d * tf * 2 + 2 * tf * d * 2 + n * d * 6 + 12 * n * tf * 4
    return pl.pallas_call(
        _ffn_step_kernel,
        grid=(1, nc),
        in_specs=[pl.BlockSpec((n, d), lambda i, c: (0, 0)),
                  pl.BlockSpec((1, d), lambda i, c: (0, 0)),
                  pl.BlockSpec((d, tf), lambda i, c: (0, c)),
                  pl.BlockSpec((d, tf), lambda i, c: (0, nc + c)),
                  pl.BlockSpec((CONV_W, tf), lambda i, c: (0, c)),
                  pl.BlockSpec((1, tf), lambda i, c: (0, c)),
                  pl.BlockSpec((tf, d), lambda i, c: (c, 0)),
                  pl.BlockSpec((n, tf), lambda i, c: (0, c)),
                  pl.BlockSpec((n, tf), lambda i, c: (0, nc + c))],
        out_specs=(pl.BlockSpec((n, d), lambda i, c: (0, 0)),
                   pl.BlockSpec((n, tf), lambda i, c: (0, c))),
        out_shape=(jax.ShapeDtypeStruct((n, d), F32),
                   jax.ShapeDtypeStruct((n, dff), F32)),
        scratch_shapes=[pltpu.VMEM((n, d), BF16), pltpu.VMEM((n, d), F32)],
        compiler_params=_params(("arbitrary", "arbitrary"), nbytes),
        name="ffn_step",
    )(x, g.reshape(1, d), w_up, w_up, w_conv, b_conv.reshape(1, dff), w_down, cache, cache)


def _ple_final_kernel(x_ref, p_ref, gp_ref, wg_ref, wp_ref, gf_ref, o_ref):
    x = x_ref[...]
    gate = jax.nn.sigmoid(_dot(_rmsnorm(x, gp_ref[...]).astype(BF16), wg_ref[...]))
    x = x + gate * _dot(p_ref[...].astype(BF16), wp_ref[...])
    o_ref[...] = _rmsnorm(x, gf_ref[...])


def _ple_final(x, p, g_ple, w_gate, w_ple, g_final, tm):
    n, d = x.shape
    dp = p.shape[1]
    nbytes = 2 * (2 * tm * d * 4 + tm * dp * 4 + d * d * 2 + dp * d * 2) + 4 * tm * d * 4
    return pl.pallas_call(
        _ple_final_kernel,
        grid=(n // tm,),
        in_specs=[pl.BlockSpec((tm, d), lambda i: (i, 0)),
                  pl.BlockSpec((tm, dp), lambda i: (i, 0)),
                  pl.BlockSpec((1, d), lambda i: (0, 0)),
                  pl.BlockSpec((d, d), lambda i: (0, 0)),
                  pl.BlockSpec((dp, d), lambda i: (0, 0)),
                  pl.BlockSpec((1, d), lambda i: (0, 0))],
        out_specs=pl.BlockSpec((tm, d), lambda i: (i, 0)),
        out_shape=jax.ShapeDtypeStruct((n, d), F32),
        compiler_params=_params(("arbitrary",), nbytes),
        name="ple_final",
    )(x, p, g_ple.reshape(1, d), w_gate, w_ple, g_final.reshape(1, d))


def kernel(x_prompt, x_sample, cache_pool, state_ssm_re, state_ssm_im, cache_conv, p_prompt, p_sample, g_mix, w_in, w_pool, pool_scale, ssm_lam_re, ssm_lam_im, ssm_log_dt, ssm_b_re, ssm_b_im, ssm_c_re, ssm_c_im, ssm_d, w_glu, w_branch_pool, w_branch_ssm, w_out, g_ffn, w_up, w_conv, b_conv, w_down, g_ple, w_ple_gate, w_ple, g_final):
    depth = g_mix.shape[0]
    nb, t_len, d = x_prompt.shape
    ns = x_sample.shape[0]
    assert x_sample.shape[1] == 1, "the sample group advances one step per call"
    pool_buf, d_pool = cache_pool.shape[2], cache_pool.shape[3]
    conv_buf, d_ff = cache_conv.shape[2], cache_conv.shape[3]
    n_grp, n_state = ssm_lam_re.shape[1], ssm_lam_re.shape[2]
    d_ssm = ssm_d.shape[1]
    assert pool_buf == max(POOL_WINDOWS) - 1 and conv_buf == CONV_W - 1
    assert POOL_HALO == SUBLANES * (max(POOL_WINDOWS).bit_length() - 1) and pool_buf <= POOL_HALO
    assert n_state == SSM_STATE and d_ssm == n_grp * SSM_GROUP

    xp = x_prompt.reshape(nb * t_len, d)
    xs = x_sample.reshape(ns, d)
    outs = [[] for _ in range(8)]
    for i in range(depth):
        w_pool_b, w_ple_b = w_pool[i].astype(BF16), w_ple[i].astype(BF16)
        bq, cq, a_pow_re, a_pow_im, a_row = _ssm_params(ssm_lam_re[i], ssm_lam_im[i], ssm_log_dt[i],
                                                        ssm_b_re[i], ssm_b_im[i], ssm_c_re[i], ssm_c_im[i])

        z_sample, w_in_b = _norm_matmul(xs, g_mix[i], w_in[i], ns, IN_COLS)

        z = _norm_matmul(xp, g_mix[i], w_in_b, IN_ROWS, IN_COLS)
        gy, st_re, st_im, (w_up_b, w_down_b, w_out_b, w_pg_b, w_glu_b, w_bs_b, w_bp_b) = _ssm_seq(
            z, d_pool, bq, cq, a_pow_re, a_pow_im, ssm_d[i], nb, t_len,
            [w_up[i], w_down[i], w_out[i], w_ple_gate[i], w_glu[i], w_branch_ssm[i], w_branch_pool[i]])

        def mix_tail(x, z, bp, gy, tm):
            return _mix_out(gy, bp, z, d_pool + d_ssm, x, w_glu_b, w_bs_b, w_out_b, tm)

        bp, pool_new = _pool_seq(z, w_pool_b, pool_scale[i], w_bp_b, nb, t_len, POOL_ROWS, pool_buf)
        xp = mix_tail(xp, z, bp, gy, MIX_ROWS)
        xp, conv_new = _ffn_seq(xp, g_ffn[i], w_up_b, w_conv[i], b_conv[i], w_down_b, nb, t_len, FFN_ROWS, FFN_COLS,
                                conv_buf)
        xp_out = _ple_final(xp, p_prompt[i].reshape(nb * t_len, -1), g_ple[i], w_pg_b, w_ple_b, g_final, PLE_ROWS)
        for lst, val in zip(outs[:4], (pool_new, st_re.reshape(nb, n_grp, n_state),
                                       st_im.reshape(nb, n_grp, n_state), conv_new)):
            lst.append(val)

        z = z_sample
        bp, pool_new = _pool_step(z, cache_pool[i].reshape(ns, pool_buf * d_pool), w_pool_b, pool_scale[i], w_bp_b)
        gy, st_re, st_im = _ssm_step(z, d_pool, state_ssm_re[i].reshape(ns, -1), state_ssm_im[i].reshape(ns, -1),
                                     bq, cq, a_row, ssm_d[i])
        xs = mix_tail(xs, z, bp, gy, ns)
        xs, a_new = _ffn_step(xs, g_ffn[i], w_up_b, w_conv[i], b_conv[i], w_down_b,
                              cache_conv[i].reshape(ns, conv_buf * d_ff), FFN_STEP_COLS)
        conv_new = jnp.concatenate([cache_conv[i][:, 1:], a_new[:, None, :]], axis=1)
        xs_out = _ple_final(xs, p_sample[i].reshape(ns, -1), g_ple[i], w_pg_b, w_ple_b, g_final, ns)
        for lst, val in zip(outs[4:], (pool_new.reshape(ns, pool_buf, d_pool), st_re.reshape(ns, n_grp, n_state),
                                       st_im.reshape(ns, n_grp, n_state), conv_new)):
            lst.append(val)

    assert depth == 1
    y_prompt = xp_out.reshape(nb, t_len, d)
    y_sample = xs_out.reshape(ns, 1, d)
    return (y_prompt, y_sample) + tuple(jnp.stack(o, axis=0) for o in outs)
```

```python
import functools

import jax
import jax.numpy as jnp
from jax import lax
from jax.experimental import pallas as pl
from jax.experimental.pallas import tpu as pltpu

F32 = jnp.float32
BF16 = jnp.bfloat16

EPS = 1e-6
POOL_WINDOWS = (2, 4, 8, 16)
POOL_HALO = 32
SSM_GROUP = 16
SSM_STATE = 64
SSM_BLOCK_GROUPS = 16
LANES = 128
SUBLANES = 8
SEG_LEN = 128
SEG_PITCH = SEG_LEN + 8
CONV_W = 3
CONV_HALO = 8
V7X_MXU_COLS = 256
FFN_SLICE = V7X_MXU_COLS
W_TILE = 2 * V7X_MXU_COLS
V7X_VMEM_BYTES = 64 * 1024 * 1024
VMEM_UNSCOPED_BYTES = 6 << 20
VMEM_TEMP_BYTES = 8 << 20

IN_ROWS, IN_COLS = 1024, 4 * W_TILE
POOL_ROWS = 1024
MIX_ROWS = 256
FFN_ROWS, FFN_COLS = 1024, W_TILE
FFN_STEP_COLS = 2 * W_TILE
PLE_ROWS = 1024


def _vmem_limit(nbytes):
    return int(min(nbytes * 1.25 + VMEM_TEMP_BYTES, V7X_VMEM_BYTES - VMEM_UNSCOPED_BYTES))


def _params(sem, nbytes):
    return pltpu.CompilerParams(dimension_semantics=sem, vmem_limit_bytes=_vmem_limit(nbytes))


def _rmsnorm(x, g):
    return x * lax.rsqrt(jnp.mean(x * x, axis=-1, keepdims=True) + EPS) * g


def _dot(a, b):
    return jnp.dot(a, b, preferred_element_type=F32)


def _abar(lr, li, logdt):
    dt = jnp.exp(logdt)
    mag = jnp.exp(lr * dt)
    return mag * jnp.cos(li * dt), mag * jnp.sin(li * dt)


def _ssm_params_kernel(lr_ref, li_ref, logdt_ref, lrs_ref, lis_ref, logdts_ref, br_ref, bi_ref,
                       pwr_ref, pwi_ref, bbr_ref, bbi_ref):
    lr = lr_ref[...]
    li = li_ref[...]
    a_re, a_im = _abar(lr, li, logdt_ref[...])
    nr = a_re - 1.0
    ni = a_im
    den = lr * lr + li * li
    coef_re = ((nr * lr + ni * li) / den)[:, None, :]
    coef_im = ((ni * lr - nr * li) / den)[:, None, :]
    br = br_ref[...]
    bi = bi_ref[...]
    bbr_ref[...] = coef_re * br - coef_im * bi
    bbi_ref[...] = coef_re * bi + coef_im * br
    a_re, a_im = _abar(lrs_ref[...], lis_ref[...], logdts_ref[...])
    nkb, _, per_blk, _ = pwr_ref.shape
    pr, pi = a_re, a_im
    for n in range(SEG_LEN):
        if n:
            pr, pi = pr * a_re - pi * a_im, pr * a_im + pi * a_re
        for k in range(nkb):
            pwr_ref[k, n] = pr[k * per_blk:(k + 1) * per_blk, :]
            pwi_ref[k, n] = pi[k * per_blk:(k + 1) * per_blk, :]


def _ssm_params(lam_re, lam_im, log_dt, b_re, b_im, c_re, c_im):
    g, p = lam_re.shape
    h = b_re.shape[-1]
    nkb = g // SSM_BLOCK_GROUPS
    bl = SSM_BLOCK_GROUPS
    per_blk = bl * p // LANES
    slab = (g * p // LANES, LANES)
    pwr, pwi, bbr, bbi = pl.pallas_call(
        _ssm_params_kernel,
        out_shape=(jax.ShapeDtypeStruct((nkb, SEG_LEN, per_blk, LANES), F32),
                   jax.ShapeDtypeStruct((nkb, SEG_LEN, per_blk, LANES), F32),
                   jax.ShapeDtypeStruct((g, h, p), F32),
                   jax.ShapeDtypeStruct((g, h, p), F32)),
        name="ssm_params",
    )(lam_re, lam_im, log_dt.reshape(g, 1),
      lam_re.reshape(slab), lam_im.reshape(slab), jnp.broadcast_to(log_dt[:, None], (g, p)).reshape(slab),
      jnp.transpose(b_re, (0, 2, 1)), jnp.transpose(b_im, (0, 2, 1)))

    def c_rows(c):
        return jnp.transpose(c.reshape(nkb, bl * h, p), (0, 2, 1))

    bq = jnp.stack([bbr.reshape(nkb, bl * h, p), bbi.reshape(nkb, bl * h, p)], axis=1)
    cq = jnp.stack([c_rows(c_re), -c_rows(c_im)], axis=1)
    a_row = jnp.concatenate([pwr[:, 0].reshape(nkb, 1, bl * p), pwi[:, 0].reshape(nkb, 1, bl * p)], axis=-1)
    return bq, cq, pwr, pwi, a_row


def _expand_blockdiag(bq_ref, cq_ref, bcat_ref, ccat_ref):
    kw, p = bq_ref.shape[2], bq_ref.shape[3]
    half = bcat_ref.shape[1] // 2
    hch = kw * p // half
    log_p, log_h = p.bit_length() - 1, hch.bit_length() - 1
    assert (1 << log_p) == p and (1 << log_h) == hch

    def iota(shape, axis):
        return lax.broadcasted_iota(jnp.int32, shape, axis)

    def same(a, b):
        return a == b

    tile_b = same(iota((p, half), 1) & (p - 1), iota((p, half), 0)).astype(BF16)
    mask_b = same(iota((kw, half), 0) >> log_h, iota((kw, half), 1) >> log_p)
    tile_c = same(iota((half, p), 0) & (p - 1), iota((half, p), 1)).astype(BF16)
    mask_c = same(iota((half, kw), 0) >> log_p, iota((half, kw), 1) >> log_h)
    for s in range(2):
        full = _dot(bq_ref[0, s].astype(BF16), tile_b)
        bcat_ref[:, s * half:(s + 1) * half] = jnp.where(mask_b, full, 0.0).astype(BF16)
        full = _dot(tile_c, cq_ref[0, s].astype(BF16))
        ccat_ref[s * half:(s + 1) * half, :] = jnp.where(mask_c, full, 0.0).astype(BF16)


def _norm_matmul_kernel(x_ref, g_ref, w_ref, o_ref, *rest):
    *wb_ref, h_ref = rest

    @pl.when(pl.program_id(1) == 0)
    def _():
        h_ref[...] = _rmsnorm(x_ref[...], g_ref[...]).astype(BF16)

    w = w_ref[...]
    if wb_ref:
        w = w.astype(BF16)
        wb_ref[0][...] = w
    o_ref[...] = _dot(h_ref[...], w)


def _norm_matmul(x, g, w, tm, tn):
    n, d = x.shape
    dout = w.shape[1]
    convert = w.dtype == F32
    assert not convert or n == tm, "each weight tile must be visited exactly once to be converted"
    nbytes = 2 * tm * d * 4 + tm * d * 2 + 2 * d * tn * w.dtype.itemsize + 2 * tm * tn * 4 + convert * 3 * d * tn * 2
    z_spec = pl.BlockSpec((tm, tn), lambda i, j: (i, j))
    w_spec = pl.BlockSpec((d, tn), lambda i, j: (0, j))
    out = pl.pallas_call(
        _norm_matmul_kernel,
        grid=(n // tm, dout // tn),
        in_specs=[pl.BlockSpec((tm, d), lambda i, j: (i, 0)),
                  pl.BlockSpec((1, d), lambda i, j: (0, 0)),
                  w_spec],
        out_specs=[z_spec] + [w_spec] * convert,
        out_shape=[jax.ShapeDtypeStruct((n, dout), F32)] + [jax.ShapeDtypeStruct(w.shape, BF16)] * convert,
        scratch_shapes=[pltpu.VMEM((tm, d), BF16)],
        compiler_params=_params(("arbitrary", "arbitrary"), nbytes),
        name="norm_matmul",
    )(x, g.reshape(1, d), w)
    return out if convert else out[0]


def _pool_project(diffs, wp_ref, scale_ref, wb_ref, y_ref):
    gw = wp_ref.shape[1]
    for k, diff in enumerate(diffs):
        yk = _dot(diff.astype(BF16), wp_ref[k]) * scale_ref[:, k * gw:(k + 1) * gw]
        y_ref[:, k * gw:(k + 1) * gw] = yk.astype(BF16)
    return _dot(y_ref[...], wb_ref[...])


def _pool_seq_kernel(u_ref, wp_ref, scale_ref, wb_ref, o_ref, new_ref, ext_ref, y_ref, tmp_ref):
    tc, dp = u_ref.shape
    gw = wp_ref.shape[1]
    t = pl.program_id(1)
    end = POOL_HALO + tc

    @pl.when(t == 0)
    def _():
        ext_ref[0:POOL_HALO, :] = jnp.zeros((POOL_HALO, dp), F32)

    ext_ref[POOL_HALO:end, :] = u_ref[...]
    pos = (t * tc + 1 + lax.broadcasted_iota(jnp.int32, (tc, 1), 0)).astype(F32)
    diffs = []
    for k, w in enumerate(POOL_WINDOWS):
        cols = slice(k * gw, (k + 1) * gw)
        u = ext_ref[POOL_HALO:end, cols]
        src, m, start = (ext_ref, cols), 1, 0
        while m < w:
            ref, c = src
            start += SUBLANES
            s = ref[start:end, c] + ref[start - m:end - m, c]
            m *= 2
            if m < w:
                level = (m.bit_length() & 1, slice(None))
                tmp_ref[level[0], start:end, :] = s
                src = (tmp_ref.at[level[0]], level[1])
        s = s[POOL_HALO - start:, :]
        count = jnp.minimum(pos, float(w))
        diffs.append(s / count - u)
    o_ref[...] = _pool_project(diffs, wp_ref, scale_ref, wb_ref, y_ref)
    nb = new_ref.shape[1]
    new_ref[0] = ext_ref[POOL_HALO + tc - nb:POOL_HALO + tc, :]
    ext_ref[0:POOL_HALO, :] = ext_ref[tc:tc + POOL_HALO, :]


def _pool_seq(z, w_pool, pool_scale, w_branch, nb, t_len, tc, pool_buf):
    n = z.shape[0]
    ng, gw, _ = w_pool.shape
    dp = ng * gw
    dm = w_branch.shape[1]
    nt = t_len // tc
    nbytes = (2 * tc * dp * 4 + 2 * ng * gw * gw * 2 + 2 * dp * dm * 2 + 2 * tc * dm * 4
              + (tc + POOL_HALO) * dp * 4 + tc * dp * 2 + 4 * tc * gw * 4)
    return pl.pallas_call(
        _pool_seq_kernel,
        grid=(nb, nt),
        in_specs=[pl.BlockSpec((tc, dp), lambda b, t: (b * nt + t, 0)),
                  pl.BlockSpec((ng, gw, gw), lambda b, t: (0, 0, 0)),
                  pl.BlockSpec((1, dp), lambda b, t: (0, 0)),
                  pl.BlockSpec((dp, dm), lambda b, t: (0, 0))],
        out_specs=(pl.BlockSpec((tc, dm), lambda b, t: (b * nt + t, 0)),
                   pl.BlockSpec((1, pool_buf, dp), lambda b, t: (b, 0, 0))),
        out_shape=(jax.ShapeDtypeStruct((n, dm), F32),
                   jax.ShapeDtypeStruct((nb, pool_buf, dp), F32)),
        scratch_shapes=[pltpu.VMEM((tc + POOL_HALO, dp), F32), pltpu.VMEM((tc, dp), BF16),
                        pltpu.VMEM((2, tc + POOL_HALO, gw), F32)],
        compiler_params=_params(("arbitrary", "arbitrary"), nbytes),
        name="pool_seq",
    )(z, w_pool, pool_scale.reshape(1, dp), w_branch)


def _pool_step_kernel(u_ref, cache_ref, wp_ref, scale_ref, wb_ref, o_ref, new_ref, y_ref):
    dp = u_ref.shape[1]
    gw = wp_ref.shape[1]
    lb = cache_ref.shape[1] // dp
    diffs = []
    for k, w in enumerate(POOL_WINDOWS):
        u = u_ref[:, k * gw:(k + 1) * gw]
        s = u
        for j in range(1, w):
            s = s + cache_ref[:, (lb - j) * dp + k * gw:(lb - j) * dp + (k + 1) * gw]
        diffs.append(s / float(w) - u)
    o_ref[...] = _pool_project(diffs, wp_ref, scale_ref, wb_ref, y_ref)
    new_ref[:, 0:(lb - 1) * dp] = cache_ref[:, dp:lb * dp]
    new_ref[:, (lb - 1) * dp:lb * dp] = u_ref[...]


def _pool_step(z, cache, w_pool, pool_scale, w_branch):
    n = z.shape[0]
    ng, gw, _ = w_pool.shape
    dp = ng * gw
    dm = w_branch.shape[1]
    lbdp = cache.shape[1]
    nbytes = 2 * (n * dp * 4 + 2 * n * lbdp * 4 + ng * gw * gw * 2 + dp * dm * 2 + n * dm * 4) + n * dp * 2
    return pl.pallas_call(
        _pool_step_kernel,
        grid=(1,),
        in_specs=[pl.BlockSpec((n, dp), lambda i: (0, 0)),
                  pl.BlockSpec((n, lbdp), lambda i: (0, 0)),
                  pl.BlockSpec((ng, gw, gw), lambda i: (0, 0, 0)),
                  pl.BlockSpec((1, dp), lambda i: (0, 0)),
                  pl.BlockSpec((dp, dm), lambda i: (0, 0))],
        out_specs=(pl.BlockSpec((n, dm), lambda i: (0, 0)),
                   pl.BlockSpec((n, lbdp), lambda i: (0, 0))),
        out_shape=(jax.ShapeDtypeStruct((n, dm), F32),
                   jax.ShapeDtypeStruct((n, lbdp), F32)),
        scratch_shapes=[pltpu.VMEM((n, dp), BF16)],
        compiler_params=_params(("arbitrary",), nbytes),
        name="pool_step",
    )(z, cache, w_pool, pool_scale.reshape(1, dp), w_branch)


def _cmul_add(xr, xi, ar, ai, br, bi):
    return xr + ar * br - ai * bi, xi + ar * bi + ai * br


def _ssm_seq_kernel(n_cast, u_ref, bq_ref, cq_ref, pwr_ref, pwi_ref, d_ref, *refs):
    cast_in, (gy_ref, st_ref), refs = refs[:n_cast], refs[n_cast:n_cast + 2], refs[n_cast + 2:]
    cast_out, refs = refs[:n_cast], refs[n_cast:]
    bcat_ref, ccat_ref, us_ref, up_ref, h_ref, hb_ref, ys_ref, seed_ref, carry_ref = refs
    for src_ref, dst_ref in zip(cast_in, cast_out):
        dst_ref[...] = src_ref[...].astype(BF16)

    kw = u_ref.shape[1]
    nslab = h_ref.shape[1] // LANES
    npair = nslab // 2
    t = pl.program_id(2)

    def lanes(j):
        return slice(j * LANES, (j + 1) * LANES)

    def step_rows(n, count=1):
        return slice(n * SUBLANES, (n + count) * SUBLANES)

    def apow(n, j):
        return pwr_ref[0, n, j:j + 1, :], pwi_ref[0, n, j:j + 1, :]

    @pl.when(jnp.logical_and(pl.program_id(1) == 0, t == 0))
    def _():
        _expand_blockdiag(bq_ref, cq_ref, bcat_ref, ccat_ref)

    @pl.when(t == 0)
    def _():
        carry_ref[...] = jnp.zeros_like(carry_ref)

    for k in range(kw // LANES):
        for r in range(SUBLANES):
            us_ref[k, r * SEG_PITCH:r * SEG_PITCH + SEG_LEN, :] = u_ref[r * SEG_LEN:(r + 1) * SEG_LEN, lanes(k)]
    for n in range(SEG_LEN):
        for k in range(kw // LANES):
            up_ref[step_rows(n), lanes(k)] = us_ref[k, pl.ds(n, SUBLANES, stride=SEG_PITCH), :]
    u = up_ref[...]
    h_ref[...] = _dot(u.astype(BF16), bcat_ref[...])

    state = [jnp.zeros((SUBLANES, LANES), F32)] * nslab
    for n in range(SEG_LEN):
        for j in range(npair):
            sr, si = _cmul_add(h_ref[step_rows(n), lanes(j)], h_ref[step_rows(n), lanes(npair + j)],
                               *apow(0, j), state[j], state[npair + j])
            h_ref[step_rows(n), lanes(j)] = sr
            h_ref[step_rows(n), lanes(npair + j)] = si
            state[j], state[npair + j] = sr, si

    for j in range(npair):
        cr, ci = carry_ref[j], carry_ref[npair + j]
        for r in range(SUBLANES):
            seed_ref[j, r:r + 1, :] = cr
            seed_ref[npair + j, r:r + 1, :] = ci
            cr, ci = _cmul_add(state[j][r:r + 1, :], state[npair + j][r:r + 1, :],
                               *apow(SEG_LEN - 1, j), cr, ci)
        carry_ref[j] = cr
        carry_ref[npair + j] = ci
    st_ref[0, 0] = carry_ref[...]

    @pl.when(t >= 0)
    def _():
        for i in range(SEG_LEN // 2):
            for j in range(npair):
                parts = []
                for n in (2 * i, 2 * i + 1):
                    parts.append(_cmul_add(h_ref[step_rows(n), lanes(j)], h_ref[step_rows(n), lanes(npair + j)],
                                           *apow(n, j), seed_ref[j], seed_ref[npair + j]))
                hb_ref[step_rows(2 * i, 2), lanes(j)] = (
                    jnp.concatenate([p[0] for p in parts], axis=0).astype(BF16))
                hb_ref[step_rows(2 * i, 2), lanes(npair + j)] = (
                    jnp.concatenate([p[1] for p in parts], axis=0).astype(BF16))

        y = _dot(hb_ref[...], ccat_ref[...]) + d_ref[...] * up_ref[...]
        g = jax.nn.gelu(y)
        for n in range(SEG_LEN):
            for k in range(kw // LANES):
                ys_ref[k, pl.ds(n, SUBLANES, stride=SEG_PITCH), :] = g[step_rows(n), lanes(k)]
        for k in range(kw // LANES):
            for r in range(SUBLANES):
                gy_ref[r * SEG_LEN:(r + 1) * SEG_LEN, lanes(k)] = (
                    ys_ref[k, r * SEG_PITCH:r * SEG_PITCH + SEG_LEN, :].astype(BF16))


def _ssm_seq(z, col0, bq, cq, pwr, pwi, d_skip, nb, t_len, side_casts):
    n = z.shape[0]
    nkb, _, kw, p = bq.shape
    sw = 2 * SSM_BLOCK_GROUPS * p
    nslab = sw // LANES
    tc = SUBLANES * SEG_LEN
    nt = t_len // tc
    cb = col0 // kw
    n_steps = nkb * nb * nt
    cast_rows = [w.shape[0] // n_steps for w in side_casts]
    assert all(r % (2 * SUBLANES) == 0 and r * n_steps == w.shape[0] for r, w in zip(cast_rows, side_casts))
    nbytes = (2 * tc * kw * 4 + 4 * kw * sw * 2 + 2 * (SEG_LEN + SUBLANES) * sw * 4 + 2 * tc * kw * 2
              + 2 * SUBLANES * SEG_PITCH * kw * 4 + tc * kw * 4 + tc * sw * 4 + tc * sw * 2 + tc * sw * 4
              + sum(2 * r * w.shape[1] * 6 for r, w in zip(cast_rows, side_casts)))

    def cast_spec(r, w):
        return pl.BlockSpec((r, w.shape[1]), lambda k, b, t: ((k * nb + b) * nt + t, 0))

    cast_specs = [cast_spec(r, w) for r, w in zip(cast_rows, side_casts)]
    gy, st, *cast = pl.pallas_call(
        functools.partial(_ssm_seq_kernel, len(side_casts)),
        grid=(nkb, nb, nt),
        in_specs=[pl.BlockSpec((tc, kw), lambda k, b, t: (b * nt + t, cb + k)),
                  pl.BlockSpec((1, 2, kw, p), lambda k, b, t: (k, 0, 0, 0)),
                  pl.BlockSpec((1, 2, p, kw), lambda k, b, t: (k, 0, 0, 0)),
                  pl.BlockSpec((1, SEG_LEN, nslab // 2, LANES), lambda k, b, t: (k, 0, 0, 0)),
                  pl.BlockSpec((1, SEG_LEN, nslab // 2, LANES), lambda k, b, t: (k, 0, 0, 0)),
                  pl.BlockSpec((1, kw), lambda k, b, t: (0, k))] + cast_specs,
        out_specs=[pl.BlockSpec((tc, kw), lambda k, b, t: (b * nt + t, k)),
                   pl.BlockSpec((1, 1, nslab, 1, LANES), lambda k, b, t: (b, k, 0, 0, 0))] + cast_specs,
        out_shape=[jax.ShapeDtypeStruct((n, nkb * kw), BF16),
                   jax.ShapeDtypeStruct((nb, nkb, nslab, 1, LANES), F32)]
                  + [jax.ShapeDtypeStruct(w.shape, BF16) for w in side_casts],
        scratch_shapes=[pltpu.VMEM((kw, sw), BF16), pltpu.VMEM((sw, kw), BF16),
                        pltpu.VMEM((kw // LANES, SUBLANES * SEG_PITCH, LANES), F32), pltpu.VMEM((tc, kw), F32),
                        pltpu.VMEM((tc, sw), F32), pltpu.VMEM((tc, sw), BF16),
                        pltpu.VMEM((kw // LANES, SUBLANES * SEG_PITCH, LANES), F32),
                        pltpu.VMEM((nslab, SUBLANES, LANES), F32), pltpu.VMEM((nslab, 1, LANES), F32)],
        compiler_params=_params(("arbitrary", "arbitrary", "arbitrary"), nbytes),
        name="ssm_seq",
    )(z, bq, cq, pwr, pwi, d_skip.reshape(1, nkb * kw), *side_casts)
    st = st.reshape(nb, nkb, 2, sw // 2)
    return gy, st[:, :, 0].reshape(nb, -1), st[:, :, 1].reshape(nb, -1), cast


def _ssm_step_kernel(u_ref, h0r_ref, h0i_ref, bq_ref, cq_ref, a_ref, d_ref,
                     gy_ref, h1r_ref, h1i_ref, h_ref, bcat_ref, ccat_ref):
    half = h0r_ref.shape[1]
    _expand_blockdiag(bq_ref, cq_ref, bcat_ref, ccat_ref)
    u = u_ref[...]
    bu = _dot(u.astype(BF16), bcat_ref[...])
    hr, hi = _cmul_add(bu[:, 0:half], bu[:, half:2 * half],
                       a_ref[0, :, 0:half], a_ref[0, :, half:2 * half],
                       h0r_ref[...], h0i_ref[...])
    h1r_ref[...] = hr
    h1i_ref[...] = hi
    h_ref[:, 0:half] = hr.astype(BF16)
    h_ref[:, half:2 * half] = hi.astype(BF16)
    y = _dot(h_ref[...], ccat_ref[...]) + d_ref[...] * u
    gy_ref[...] = jax.nn.gelu(y).astype(BF16)


def _ssm_step(z, col0, h0_re, h0_im, bq, cq, a_row, d_skip):
    n = z.shape[0]
    nkb, _, kw, p = bq.shape
    half = SSM_BLOCK_GROUPS * p
    sw = 2 * half
    cb = col0 // kw
    nbytes = 2 * (n * kw * 4 + 4 * n * half * 4 + 2 * kw * sw * 2 + sw * 4 + n * kw * 2) + n * sw * 6
    return pl.pallas_call(
        _ssm_step_kernel,
        grid=(nkb,),
        in_specs=[pl.BlockSpec((n, kw), lambda k: (0, cb + k)),
                  pl.BlockSpec((n, half), lambda k: (0, k)),
                  pl.BlockSpec((n, half), lambda k: (0, k)),
                  pl.BlockSpec((1, 2, kw, p), lambda k: (k, 0, 0, 0)),
                  pl.BlockSpec((1, 2, p, kw), lambda k: (k, 0, 0, 0)),
                  pl.BlockSpec((1, 1, sw), lambda k: (k, 0, 0)),
                  pl.BlockSpec((1, kw), lambda k: (0, k))],
        out_specs=(pl.BlockSpec((n, kw), lambda k: (0, k)),
                   pl.BlockSpec((n, half), lambda k: (0, k)),
                   pl.BlockSpec((n, half), lambda k: (0, k))),
        out_shape=(jax.ShapeDtypeStruct((n, nkb * kw), BF16),
                   jax.ShapeDtypeStruct((n, nkb * half), F32),
                   jax.ShapeDtypeStruct((n, nkb * half), F32)),
        scratch_shapes=[pltpu.VMEM((n, sw), BF16), pltpu.VMEM((kw, sw), BF16), pltpu.VMEM((sw, kw), BF16)],
        compiler_params=_params(("arbitrary",), nbytes),
        name="ssm_step",
    )(z, h0_re, h0_im, bq, cq, a_row, d_skip.reshape(1, nkb * kw))


def _mix_out_kernel(gy_ref, bp_ref, gp_ref, gs_ref, x_ref, wg_ref, wb_ref, wo_ref, o_ref):
    ds = wb_ref.shape[0]
    g = _dot(gy_ref[...], wg_ref[...])
    y = g[:, 0:ds] * jax.nn.sigmoid(g[:, ds:2 * ds])
    bs = _dot(y.astype(BF16), wb_ref[...])
    merged = jax.nn.sigmoid(gp_ref[...]) * bp_ref[...] + jax.nn.sigmoid(gs_ref[...]) * bs
    o_ref[...] = x_ref[...] + _dot(merged.astype(BF16), wo_ref[...])


def _mix_out(gy, bp, z, gate_col0, x, w_glu, w_branch, w_out, tm):
    n, ds = gy.shape
    dm = w_branch.shape[1]
    gb = gate_col0 // dm
    w_bytes = (ds * 2 * ds + ds * dm + dm * dm) * 2
    nbytes = 2 * (tm * ds * 2 + 5 * tm * dm * 4) + w_bytes + 6 * tm * dm * 4

    def resident(shape):
        return pl.BlockSpec(shape, lambda i: (0, 0), pipeline_mode=pl.Buffered(1))

    return pl.pallas_call(
        _mix_out_kernel,
        grid=(n // tm,),
        in_specs=[pl.BlockSpec((tm, ds), lambda i: (i, 0)),
                  pl.BlockSpec((tm, dm), lambda i: (i, 0)),
                  pl.BlockSpec((tm, dm), lambda i: (i, gb)),
                  pl.BlockSpec((tm, dm), lambda i: (i, gb + 1)),
                  pl.BlockSpec((tm, dm), lambda i: (i, 0)),
                  resident((ds, 2 * ds)), resident((ds, dm)), resident((dm, dm))],
        out_specs=pl.BlockSpec((tm, dm), lambda i: (i, 0)),
        out_shape=jax.ShapeDtypeStruct((n, dm), F32),
        compiler_params=_params(("arbitrary",), nbytes),
        name="mix_out",
    )(gy, bp, z, z, x, w_glu, w_branch, w_out)


def _ffn_gate_down(x_ref, conv, v, wd_ref, o_ref, acc_ref):
    c = pl.program_id(1)
    part = _dot((jax.nn.gelu(conv) * v).astype(BF16), wd_ref[...])

    @pl.when(c == 0)
    def _():
        acc_ref[...] = part

    @pl.when(c > 0)
    def _():
        acc_ref[...] += part

    @pl.when(c == pl.num_programs(1) - 1)
    def _():
        o_ref[...] = x_ref[...] + acc_ref[...]


def _ffn_seq_kernel(tiles_per_seq, x_ref, g_ref, wa_ref, wv_ref, wc_ref, bc_ref, wd_ref,
                    o_ref, new_ref, h_ref, ext_ref, carry_ref, gate_ref, v_ref):
    tm = x_ref.shape[0]
    tf = gate_ref.shape[1]
    i = pl.program_id(0)
    c = pl.program_id(1)

    @pl.when(jnp.logical_and(i == 0, c == 0))
    def _():
        o_ref[...] = jnp.zeros(o_ref.shape, F32)
        carry_ref[...] = jnp.zeros(carry_ref.shape, F32)

    @pl.when(c == 0)
    def _():
        h_ref[...] = _rmsnorm(x_ref[...], g_ref[...]).astype(BF16)

    seq_start = i % tiles_per_seq == 0
    nb = new_ref.shape[1]
    slot = jnp.minimum(c, 0)
    for q in range(tf // FFN_SLICE):
        cols = slice(q * FFN_SLICE, (q + 1) * FFN_SLICE)
        ext_ref[slot, 0:CONV_HALO, cols] = jnp.where(seq_start, 0.0, carry_ref[c, :, cols])
        ext_ref[slot, CONV_HALO:CONV_HALO + tm, cols] = _dot(h_ref[...], wa_ref[:, cols])
        v_ref[slot, :, cols] = _dot(h_ref[...], wv_ref[:, cols])
        a = ext_ref[slot, CONV_HALO:CONV_HALO + tm, cols]
        v = v_ref[slot, :, cols]
        conv = bc_ref[:, cols] + wc_ref[CONV_W - 1:CONV_W, cols] * a
        for j in range(CONV_W - 1):
            off = CONV_HALO - (CONV_W - 1) + j
            conv = conv + wc_ref[j:j + 1, cols] * ext_ref[slot, off:off + tm, cols]
        carry_ref[c, :, cols] = ext_ref[slot, tm:tm + CONV_HALO, cols]
        new_ref[0, :, cols] = ext_ref[slot, CONV_HALO + tm - nb:CONV_HALO + tm, cols]
        gate_ref[:, cols] = (jax.nn.gelu(conv) * v).astype(BF16)
    o_ref[...] = _dot(gate_ref[...], wd_ref[...]) + jnp.where(c == 0, x_ref[...], o_ref[...])


def _ffn_seq(x, g, w_up, w_conv, b_conv, w_down, nb, t_len, tm, tf, conv_buf):
    n, d = x.shape
    dff = w_down.shape[0]
    nc = dff // tf
    tps = t_len // tm
    nbytes = (4 * tm * d * 4 + 4 * d * tf * 2 + 2 * tf * d * 2 + tm * d * 2 + tm * tf * 2
              + (tm + CONV_HALO) * tf * 4 + nc * CONV_HALO * tf * 4 + 6 * tm * FFN_SLICE * 4)
    out, new_tail = pl.pallas_call(
        functools.partial(_ffn_seq_kernel, tps),
        grid=(n // tm, nc),
        in_specs=[pl.BlockSpec((tm, d), lambda i, c: (i, 0)),
                  pl.BlockSpec((1, d), lambda i, c: (0, 0)),
                  pl.BlockSpec((d, tf), lambda i, c: (0, c)),
                  pl.BlockSpec((d, tf), lambda i, c: (0, nc + c)),
                  pl.BlockSpec((CONV_W, tf), lambda i, c: (0, c)),
                  pl.BlockSpec((1, tf), lambda i, c: (0, c)),
                  pl.BlockSpec((tf, d), lambda i, c: (c, 0))],
        out_specs=(pl.BlockSpec((tm, d), lambda i, c: (i, 0)),
                   pl.BlockSpec((1, conv_buf, tf), lambda i, c: (i, 0, c))),
        out_shape=(jax.ShapeDtypeStruct((n, d), F32),
                   jax.ShapeDtypeStruct((n // tm, conv_buf, dff), F32)),
        scratch_shapes=[pltpu.VMEM((tm, d), BF16), pltpu.VMEM((1, tm + CONV_HALO, tf), F32),
                        pltpu.VMEM((nc, CONV_HALO, tf), F32), pltpu.VMEM((tm, tf), BF16),
                        pltpu.VMEM((1, tm, tf), F32)],
        compiler_params=_params(("arbitrary", "arbitrary"), nbytes),
        name="ffn_seq",
    )(x, g.reshape(1, d), w_up, w_up, w_conv, b_conv.reshape(1, dff), w_down)
    return out, new_tail[tps - 1::tps]


def _ffn_step_kernel(x_ref, g_ref, wa_ref, wv_ref, wc_ref, bc_ref, wd_ref, p0_ref, p1_ref,
                     o_ref, a_ref, h_ref, acc_ref):
    @pl.when(pl.program_id(1) == 0)
    def _():
        h_ref[...] = _rmsnorm(x_ref[...], g_ref[...]).astype(BF16)

    a = _dot(h_ref[...], wa_ref[...])
    v = _dot(h_ref[...], wv_ref[...])
    conv = bc_ref[...] + wc_ref[2:3, :] * a + wc_ref[1:2, :] * p1_ref[...] + wc_ref[0:1, :] * p0_ref[...]
    a_ref[...] = a
    _ffn_gate_down(x_ref, conv, v, wd_ref, o_ref, acc_ref)


def _ffn_step(x, g, w_up, w_conv, b_conv, w_down, cache, tf):
    n, d = x.shape
    dff = w_down.shape[0]
    nc = dff // tf
    nbytes = 4 * n * d * 4 + 4 * d * tf * 2 + 2 * tf * d * 2 + n * d * 6 + 12 * n * tf * 4
    return pl.pallas_call(
        _ffn_step_kernel,
        grid=(1, nc),
        in_specs=[pl.BlockSpec((n, d), lambda i, c: (0, 0)),
                  pl.BlockSpec((1, d), lambda i, c: (0, 0)),
                  pl.BlockSpec((d, tf), lambda i, c: (0, c)),
                  pl.BlockSpec((d, tf), lambda i, c: (0, nc + c)),
                  pl.BlockSpec((CONV_W, tf), lambda i, c: (0, c)),
                  pl.BlockSpec((1, tf), lambda i, c: (0, c)),
                  pl.BlockSpec((tf, d), lambda i, c: (c, 0)),
                  pl.BlockSpec((n, tf), lambda i, c: (0, c)),
                  pl.BlockSpec((n, tf), lambda i, c: (0, nc + c))],
        out_specs=(pl.BlockSpec((n, d), lambda i, c: (0, 0)),
                   pl.BlockSpec((n, tf), lambda i, c: (0, c))),
        out_shape=(jax.ShapeDtypeStruct((n, d), F32),
                   jax.ShapeDtypeStruct((n, dff), F32)),
        scratch_shapes=[pltpu.VMEM((n, d), BF16), pltpu.VMEM((n, d), F32)],
        compiler_params=_params(("arbitrary", "arbitrary"), nbytes),
        name="ffn_step",
    )(x, g.reshape(1, d), w_up, w_up, w_conv, b_conv.reshape(1, dff), w_down, cache, cache)


def _ple_final_kernel(x_ref, p_ref, gp_ref, wg_ref, wp_ref, gf_ref, o_ref):
    x = x_ref[...]
    gate = jax.nn.sigmoid(_dot(_rmsnorm(x, gp_ref[...]).astype(BF16), wg_ref[...]))
    x = x + gate * _dot(p_ref[...].astype(BF16), wp_ref[...])
    o_ref[...] = _rmsnorm(x, gf_ref[...])


def _ple_final(x, p, g_ple, w_gate, w_ple, g_final, tm):
    n, d = x.shape
    dp = p.shape[1]
    nbytes = 2 * (2 * tm * d * 4 + tm * dp * 4) + d * d * 2 + dp * d * 2 + 3 * tm * d * 4
    return pl.pallas_call(
        _ple_final_kernel,
        grid=(n // tm,),
        in_specs=[pl.BlockSpec((tm, d), lambda i: (i, 0)),
                  pl.BlockSpec((tm, dp), lambda i: (i, 0)),
                  pl.BlockSpec((1, d), lambda i: (0, 0)),
                  pl.BlockSpec((d, d), lambda i: (0, 0), pipeline_mode=pl.Buffered(1)),
                  pl.BlockSpec((dp, d), lambda i: (0, 0), pipeline_mode=pl.Buffered(1)),
                  pl.BlockSpec((1, d), lambda i: (0, 0))],
        out_specs=pl.BlockSpec((tm, d), lambda i: (i, 0)),
        out_shape=jax.ShapeDtypeStruct((n, d), F32),
        compiler_params=_params(("arbitrary",), nbytes),
        name="ple_final",
    )(x, p, g_ple.reshape(1, d), w_gate, w_ple, g_final.reshape(1, d))


def kernel(x_prompt, x_sample, cache_pool, state_ssm_re, state_ssm_im, cache_conv, p_prompt, p_sample, g_mix, w_in, w_pool, pool_scale, ssm_lam_re, ssm_lam_im, ssm_log_dt, ssm_b_re, ssm_b_im, ssm_c_re, ssm_c_im, ssm_d, w_glu, w_branch_pool, w_branch_ssm, w_out, g_ffn, w_up, w_conv, b_conv, w_down, g_ple, w_ple_gate, w_ple, g_final):
    depth = g_mix.shape[0]
    nb, t_len, d = x_prompt.shape
    ns = x_sample.shape[0]
    assert x_sample.shape[1] == 1, "the sample group advances one step per call"
    pool_buf, d_pool = cache_pool.shape[2], cache_pool.shape[3]
    conv_buf, d_ff = cache_conv.shape[2], cache_conv.shape[3]
    n_grp, n_state = ssm_lam_re.shape[1], ssm_lam_re.shape[2]
    d_ssm = ssm_d.shape[1]
    assert pool_buf == max(POOL_WINDOWS) - 1 and conv_buf == CONV_W - 1
    assert POOL_HALO == SUBLANES * (max(POOL_WINDOWS).bit_length() - 1) and pool_buf <= POOL_HALO
    assert n_state == SSM_STATE and d_ssm == n_grp * SSM_GROUP

    xp = x_prompt.reshape(nb * t_len, d)
    xs = x_sample.reshape(ns, d)
    outs = [[] for _ in range(8)]
    for i in range(depth):
        w_pool_b, w_ple_b = w_pool[i].astype(BF16), w_ple[i].astype(BF16)
        bq, cq, a_pow_re, a_pow_im, a_row = _ssm_params(ssm_lam_re[i], ssm_lam_im[i], ssm_log_dt[i],
                                                        ssm_b_re[i], ssm_b_im[i], ssm_c_re[i], ssm_c_im[i])

        z_sample, w_in_b = _norm_matmul(xs, g_mix[i], w_in[i], ns, IN_COLS)

        z = _norm_matmul(xp, g_mix[i], w_in_b, IN_ROWS, IN_COLS)
        gy, st_re, st_im, (w_up_b, w_down_b, w_out_b, w_pg_b, w_glu_b, w_bs_b, w_bp_b) = _ssm_seq(
            z, d_pool, bq, cq, a_pow_re, a_pow_im, ssm_d[i], nb, t_len,
            [w_up[i], w_down[i], w_out[i], w_ple_gate[i], w_glu[i], w_branch_ssm[i], w_branch_pool[i]])

        def mix_tail(x, z, bp, gy, tm):
            return _mix_out(gy, bp, z, d_pool + d_ssm, x, w_glu_b, w_bs_b, w_out_b, tm)

        bp, pool_new = _pool_seq(z, w_pool_b, pool_scale[i], w_bp_b, nb, t_len, POOL_ROWS, pool_buf)
        xp = mix_tail(xp, z, bp, gy, MIX_ROWS)
        xp, conv_new = _ffn_seq(xp, g_ffn[i], w_up_b, w_conv[i], b_conv[i], w_down_b, nb, t_len, FFN_ROWS, FFN_COLS,
                                conv_buf)
        xp_out = _ple_final(xp, p_prompt[i].reshape(nb * t_len, -1), g_ple[i], w_pg_b, w_ple_b, g_final, PLE_ROWS)
        for lst, val in zip(outs[:4], (pool_new, st_re.reshape(nb, n_grp, n_state),
                                       st_im.reshape(nb, n_grp, n_state), conv_new)):
            lst.append(val)

        z = z_sample
        bp, pool_new = _pool_step(z, cache_pool[i].reshape(ns, pool_buf * d_pool), w_pool_b, pool_scale[i], w_bp_b)
        gy, st_re, st_im = _ssm_step(z, d_pool, state_ssm_re[i].reshape(ns, -1), state_ssm_im[i].reshape(ns, -1),
                                     bq, cq, a_row, ssm_d[i])
        xs = mix_tail(xs, z, bp, gy, ns)
        xs, a_new = _ffn_step(xs, g_ffn[i], w_up_b, w_conv[i], b_conv[i], w_down_b,
                              cache_conv[i].reshape(ns, conv_buf * d_ff), FFN_STEP_COLS)
        conv_new = jnp.concatenate([cache_conv[i][:, 1:], a_new[:, None, :]], axis=1)
        xs_out = _ple_final(xs, p_sample[i].reshape(ns, -1), g_ple[i], w_pg_b, w_ple_b, g_final, ns)
        for lst, val in zip(outs[4:], (pool_new.reshape(ns, pool_buf, d_pool), st_re.reshape(ns, n_grp, n_state),
                                       st_im.reshape(ns, n_grp, n_state), conv_new)):
            lst.append(val)

    assert depth == 1
    y_prompt = xp_out.reshape(nb, t_len, d)
    y_sample = xs_out.reshape(ns, 1, d)
    return (y_prompt, y_sample) + tuple(jnp.stack(o, axis=0) for o in outs)
```

```python
import functools

import jax
import jax.numpy as jnp
from jax import lax
from jax.experimental import pallas as pl
from jax.experimental.pallas import tpu as pltpu

F32 = jnp.float32
BF16 = jnp.bfloat16

EPS = 1e-6
POOL_WINDOWS = (2, 4, 8, 16)
POOL_HALO = 32
SSM_GROUP = 16
SSM_STATE = 64
SSM_BLOCK_GROUPS = 16
LANES = 128
SUBLANES = 8
SEG_LEN = 128
SEG_PITCH = SEG_LEN + 8
CONV_W = 3
CONV_HALO = 8
V7X_MXU_COLS = 256
FFN_SLICE = V7X_MXU_COLS
W_TILE = 2 * V7X_MXU_COLS
V7X_VMEM_BYTES = 64 * 1024 * 1024
VMEM_UNSCOPED_BYTES = 6 << 20
VMEM_TEMP_BYTES = 8 << 20

IN_ROWS, IN_COLS = 1024, 4 * W_TILE
POOL_ROWS = 1024
IN_COLS_F32 = 3 * W_TILE
MIX_ROWS = 256
FFN_ROWS, FFN_COLS = 1024, W_TILE
FFN_STEP_COLS = 2 * W_TILE
PLE_ROWS = 1024


def _vmem_limit(nbytes):
    return int(min(nbytes * 1.25 + VMEM_TEMP_BYTES, V7X_VMEM_BYTES - VMEM_UNSCOPED_BYTES))


def _params(sem, nbytes):
    return pltpu.CompilerParams(dimension_semantics=sem, vmem_limit_bytes=_vmem_limit(nbytes))


def _rmsnorm(x, g):
    return x * lax.rsqrt(jnp.mean(x * x, axis=-1, keepdims=True) + EPS) * g


def _dot(a, b):
    return jnp.dot(a, b, preferred_element_type=F32)


def _abar(lr, li, logdt):
    dt = jnp.exp(logdt)
    mag = jnp.exp(lr * dt)
    return mag * jnp.cos(li * dt), mag * jnp.sin(li * dt)


def _ssm_params_kernel(lr_ref, li_ref, logdt_ref, lrs_ref, lis_ref, logdts_ref, br_ref, bi_ref,
                       pwr_ref, pwi_ref, bbr_ref, bbi_ref):
    lr = lr_ref[...]
    li = li_ref[...]
    a_re, a_im = _abar(lr, li, logdt_ref[...])
    nr = a_re - 1.0
    ni = a_im
    den = lr * lr + li * li
    coef_re = ((nr * lr + ni * li) / den)[:, None, :]
    coef_im = ((ni * lr - nr * li) / den)[:, None, :]
    br = br_ref[...]
    bi = bi_ref[...]
    bbr_ref[...] = coef_re * br - coef_im * bi
    bbi_ref[...] = coef_re * bi + coef_im * br
    a_re, a_im = _abar(lrs_ref[...], lis_ref[...], logdts_ref[...])
    nkb, _, per_blk, _ = pwr_ref.shape
    pr, pi = a_re, a_im
    for n in range(SEG_LEN):
        if n:
            pr, pi = pr * a_re - pi * a_im, pr * a_im + pi * a_re
        for k in range(nkb):
            pwr_ref[k, n] = pr[k * per_blk:(k + 1) * per_blk, :]
            pwi_ref[k, n] = pi[k * per_blk:(k + 1) * per_blk, :]


def _ssm_params(lam_re, lam_im, log_dt, b_re, b_im, c_re, c_im):
    g, p = lam_re.shape
    h = b_re.shape[-1]
    nkb = g // SSM_BLOCK_GROUPS
    bl = SSM_BLOCK_GROUPS
    per_blk = bl * p // LANES
    slab = (g * p // LANES, LANES)
    pwr, pwi, bbr, bbi = pl.pallas_call(
        _ssm_params_kernel,
        out_shape=(jax.ShapeDtypeStruct((nkb, SEG_LEN, per_blk, LANES), F32),
                   jax.ShapeDtypeStruct((nkb, SEG_LEN, per_blk, LANES), F32),
                   jax.ShapeDtypeStruct((g, h, p), F32),
                   jax.ShapeDtypeStruct((g, h, p), F32)),
        name="ssm_params",
    )(lam_re, lam_im, log_dt.reshape(g, 1),
      lam_re.reshape(slab), lam_im.reshape(slab), jnp.broadcast_to(log_dt[:, None], (g, p)).reshape(slab),
      jnp.transpose(b_re, (0, 2, 1)), jnp.transpose(b_im, (0, 2, 1)))

    def c_rows(c):
        return jnp.transpose(c.reshape(nkb, bl * h, p), (0, 2, 1))

    bq = jnp.stack([bbr.reshape(nkb, bl * h, p), bbi.reshape(nkb, bl * h, p)], axis=1)
    cq = jnp.stack([c_rows(c_re), -c_rows(c_im)], axis=1)
    a_row = jnp.concatenate([pwr[:, 0].reshape(nkb, 1, bl * p), pwi[:, 0].reshape(nkb, 1, bl * p)], axis=-1)
    return bq, cq, pwr, pwi, a_row


def _expand_blockdiag(bq_ref, cq_ref, bcat_ref, ccat_ref):
    kw, p = bq_ref.shape[2], bq_ref.shape[3]
    half = bcat_ref.shape[1] // 2
    hch = kw * p // half
    log_p, log_h = p.bit_length() - 1, hch.bit_length() - 1
    assert (1 << log_p) == p and (1 << log_h) == hch

    def iota(shape, axis):
        return lax.broadcasted_iota(jnp.int32, shape, axis)

    def same(a, b):
        return a == b

    tile_b = same(iota((p, half), 1) & (p - 1), iota((p, half), 0)).astype(BF16)
    mask_b = same(iota((kw, half), 0) >> log_h, iota((kw, half), 1) >> log_p)
    tile_c = same(iota((half, p), 0) & (p - 1), iota((half, p), 1)).astype(BF16)
    mask_c = same(iota((half, kw), 0) >> log_p, iota((half, kw), 1) >> log_h)
    for s in range(2):
        full = _dot(bq_ref[0, s].astype(BF16), tile_b)
        bcat_ref[:, s * half:(s + 1) * half] = jnp.where(mask_b, full, 0.0).astype(BF16)
        full = _dot(tile_c, cq_ref[0, s].astype(BF16))
        ccat_ref[s * half:(s + 1) * half, :] = jnp.where(mask_c, full, 0.0).astype(BF16)


def _norm_matmul_kernel(x_ref, g_ref, w_ref, o_ref, *rest):
    *wb_ref, h_ref = rest

    @pl.when(pl.program_id(1) == 0)
    def _():
        h_ref[...] = _rmsnorm(x_ref[...], g_ref[...]).astype(BF16)

    w = w_ref[...]
    if wb_ref:
        w = w.astype(BF16)
        wb_ref[0][...] = w
    o_ref[...] = _dot(h_ref[...], w)


def _norm_matmul(x, g, w, tm, tn):
    n, d = x.shape
    dout = w.shape[1]
    convert = w.dtype == F32
    assert not convert or n == tm, "each weight tile must be visited exactly once to be converted"
    nbytes = 2 * tm * d * 4 + tm * d * 2 + 2 * d * tn * w.dtype.itemsize + 2 * tm * tn * 4 + convert * 3 * d * tn * 2
    z_spec = pl.BlockSpec((tm, tn), lambda i, j: (i, j))
    w_spec = pl.BlockSpec((d, tn), lambda i, j: (0, j))
    out = pl.pallas_call(
        _norm_matmul_kernel,
        grid=(n // tm, dout // tn),
        in_specs=[pl.BlockSpec((tm, d), lambda i, j: (i, 0)),
                  pl.BlockSpec((1, d), lambda i, j: (0, 0)),
                  w_spec],
        out_specs=[z_spec] + [w_spec] * convert,
        out_shape=[jax.ShapeDtypeStruct((n, dout), F32)] + [jax.ShapeDtypeStruct(w.shape, BF16)] * convert,
        scratch_shapes=[pltpu.VMEM((tm, d), BF16)],
        compiler_params=_params(("arbitrary", "arbitrary"), nbytes),
        name="norm_matmul",
    )(x, g.reshape(1, d), w)
    return out if convert else out[0]


def _pool_project(diffs, wp_ref, scale_ref, wb_ref, y_ref):
    gw = wp_ref.shape[1]
    for k, diff in enumerate(diffs):
        yk = _dot(diff.astype(BF16), wp_ref[k]) * scale_ref[:, k * gw:(k + 1) * gw]
        y_ref[:, k * gw:(k + 1) * gw] = yk.astype(BF16)
    return _dot(y_ref[...], wb_ref[...])


def _pool_seq_kernel(u_ref, wp_ref, scale_ref, wb_ref, o_ref, new_ref, ext_ref, y_ref, tmp_ref):
    tc, dp = u_ref.shape
    gw = wp_ref.shape[1]
    t = pl.program_id(1)
    end = POOL_HALO + tc

    @pl.when(t == 0)
    def _():
        ext_ref[0:POOL_HALO, :] = jnp.zeros((POOL_HALO, dp), F32)

    ext_ref[POOL_HALO:end, :] = u_ref[...]
    pos = (t * tc + 1 + lax.broadcasted_iota(jnp.int32, (tc, 1), 0)).astype(F32)
    diffs = []
    for k, w in enumerate(POOL_WINDOWS):
        cols = slice(k * gw, (k + 1) * gw)
        u = ext_ref[POOL_HALO:end, cols]
        src, m, start = (ext_ref, cols), 1, 0
        while m < w:
            ref, c = src
            start += SUBLANES
            s = ref[start:end, c] + ref[start - m:end - m, c]
            m *= 2
            if m < w:
                level = (m.bit_length() & 1, slice(None))
                tmp_ref[level[0], start:end, :] = s
                src = (tmp_ref.at[level[0]], level[1])
        s = s[POOL_HALO - start:, :]
        count = jnp.minimum(pos, float(w))
        diffs.append(s / count - u)
    o_ref[...] = _pool_project(diffs, wp_ref, scale_ref, wb_ref, y_ref)
    nb = new_ref.shape[1]
    new_ref[0] = ext_ref[POOL_HALO + tc - nb:POOL_HALO + tc, :]
    ext_ref[0:POOL_HALO, :] = ext_ref[tc:tc + POOL_HALO, :]


def _pool_seq(z, w_pool, pool_scale, w_branch, nb, t_len, tc, pool_buf):
    n = z.shape[0]
    ng, gw, _ = w_pool.shape
    dp = ng * gw
    dm = w_branch.shape[1]
    nt = t_len // tc
    nbytes = (2 * tc * dp * 4 + 2 * ng * gw * gw * 2 + 2 * dp * dm * 2 + 2 * tc * dm * 4
              + (tc + POOL_HALO) * dp * 4 + tc * dp * 2 + 4 * tc * gw * 4)
    return pl.pallas_call(
        _pool_seq_kernel,
        grid=(nb, nt),
        in_specs=[pl.BlockSpec((tc, dp), lambda b, t: (b * nt + t, 0)),
                  pl.BlockSpec((ng, gw, gw), lambda b, t: (0, 0, 0)),
                  pl.BlockSpec((1, dp), lambda b, t: (0, 0)),
                  pl.BlockSpec((dp, dm), lambda b, t: (0, 0))],
        out_specs=(pl.BlockSpec((tc, dm), lambda b, t: (b * nt + t, 0)),
                   pl.BlockSpec((1, pool_buf, dp), lambda b, t: (b, 0, 0))),
        out_shape=(jax.ShapeDtypeStruct((n, dm), F32),
                   jax.ShapeDtypeStruct((nb, pool_buf, dp), F32)),
        scratch_shapes=[pltpu.VMEM((tc + POOL_HALO, dp), F32), pltpu.VMEM((tc, dp), BF16),
                        pltpu.VMEM((2, tc + POOL_HALO, gw), F32)],
        compiler_params=_params(("arbitrary", "arbitrary"), nbytes),
        name="pool_seq",
    )(z, w_pool, pool_scale.reshape(1, dp), w_branch)


def _pool_step_kernel(u_ref, cache_ref, wp_ref, scale_ref, wb_ref, o_ref, new_ref, y_ref):
    dp = u_ref.shape[1]
    gw = wp_ref.shape[1]
    lb = cache_ref.shape[1] // dp
    diffs = []
    for k, w in enumerate(POOL_WINDOWS):
        u = u_ref[:, k * gw:(k + 1) * gw]
        s = u
        for j in range(1, w):
            s = s + cache_ref[:, (lb - j) * dp + k * gw:(lb - j) * dp + (k + 1) * gw]
        diffs.append(s / float(w) - u)
    o_ref[...] = _pool_project(diffs, wp_ref, scale_ref, wb_ref, y_ref)
    new_ref[:, 0:(lb - 1) * dp] = cache_ref[:, dp:lb * dp]
    new_ref[:, (lb - 1) * dp:lb * dp] = u_ref[...]


def _pool_step(z, cache, w_pool, pool_scale, w_branch):
    n = z.shape[0]
    ng, gw, _ = w_pool.shape
    dp = ng * gw
    dm = w_branch.shape[1]
    lbdp = cache.shape[1]
    nbytes = 2 * (n * dp * 4 + 2 * n * lbdp * 4 + ng * gw * gw * 2 + dp * dm * 2 + n * dm * 4) + n * dp * 2
    return pl.pallas_call(
        _pool_step_kernel,
        grid=(1,),
        in_specs=[pl.BlockSpec((n, dp), lambda i: (0, 0)),
                  pl.BlockSpec((n, lbdp), lambda i: (0, 0)),
                  pl.BlockSpec((ng, gw, gw), lambda i: (0, 0, 0)),
                  pl.BlockSpec((1, dp), lambda i: (0, 0)),
                  pl.BlockSpec((dp, dm), lambda i: (0, 0))],
        out_specs=(pl.BlockSpec((n, dm), lambda i: (0, 0)),
                   pl.BlockSpec((n, lbdp), lambda i: (0, 0))),
        out_shape=(jax.ShapeDtypeStruct((n, dm), F32),
                   jax.ShapeDtypeStruct((n, lbdp), F32)),
        scratch_shapes=[pltpu.VMEM((n, dp), BF16)],
        compiler_params=_params(("arbitrary",), nbytes),
        name="pool_step",
    )(z, cache, w_pool, pool_scale.reshape(1, dp), w_branch)


def _cmul_add(xr, xi, ar, ai, br, bi):
    return xr + ar * br - ai * bi, xi + ar * bi + ai * br


def _ssm_seq_kernel(n_cast, u_ref, bq_ref, cq_ref, pwr_ref, pwi_ref, d_ref, *refs):
    cast_in, (gy_ref, st_ref), refs = refs[:n_cast], refs[n_cast:n_cast + 2], refs[n_cast + 2:]
    cast_out, refs = refs[:n_cast], refs[n_cast:]
    bcat_ref, ccat_ref, us_ref, up_ref, h_ref, hb_ref, ys_ref, seed_ref, carry_ref = refs
    for src_ref, dst_ref in zip(cast_in, cast_out):
        dst_ref[...] = src_ref[...].astype(BF16)

    kw = u_ref.shape[1]
    nslab = h_ref.shape[1] // LANES
    npair = nslab // 2
    t = pl.program_id(2)

    def lanes(j):
        return slice(j * LANES, (j + 1) * LANES)

    def step_rows(n, count=1):
        return slice(n * SUBLANES, (n + count) * SUBLANES)

    def apow(n, j):
        return pwr_ref[0, n, j:j + 1, :], pwi_ref[0, n, j:j + 1, :]

    @pl.when(jnp.logical_and(pl.program_id(1) == 0, t == 0))
    def _():
        _expand_blockdiag(bq_ref, cq_ref, bcat_ref, ccat_ref)

    @pl.when(t == 0)
    def _():
        carry_ref[...] = jnp.zeros_like(carry_ref)

    for k in range(kw // LANES):
        for r in range(SUBLANES):
            us_ref[k, r * SEG_PITCH:r * SEG_PITCH + SEG_LEN, :] = u_ref[r * SEG_LEN:(r + 1) * SEG_LEN, lanes(k)]
    for n in range(SEG_LEN):
        for k in range(kw // LANES):
            up_ref[step_rows(n), lanes(k)] = us_ref[k, pl.ds(n, SUBLANES, stride=SEG_PITCH), :]
    u = up_ref[...]
    h_ref[...] = _dot(u.astype(BF16), bcat_ref[...])

    state = [jnp.zeros((SUBLANES, LANES), F32)] * nslab
    for n in range(SEG_LEN):
        for j in range(npair):
            sr, si = _cmul_add(h_ref[step_rows(n), lanes(j)], h_ref[step_rows(n), lanes(npair + j)],
                               *apow(0, j), state[j], state[npair + j])
            h_ref[step_rows(n), lanes(j)] = sr
            h_ref[step_rows(n), lanes(npair + j)] = si
            state[j], state[npair + j] = sr, si

    for j in range(npair):
        cr, ci = carry_ref[j], carry_ref[npair + j]
        for r in range(SUBLANES):
            seed_ref[j, r:r + 1, :] = cr
            seed_ref[npair + j, r:r + 1, :] = ci
            cr, ci = _cmul_add(state[j][r:r + 1, :], state[npair + j][r:r + 1, :],
                               *apow(SEG_LEN - 1, j), cr, ci)
        carry_ref[j] = cr
        carry_ref[npair + j] = ci
    st_ref[0, 0] = carry_ref[...]

    @pl.when(t >= 0)
    def _():
        for i in range(SEG_LEN // 2):
            for j in range(npair):
                parts = []
                for n in (2 * i, 2 * i + 1):
                    parts.append(_cmul_add(h_ref[step_rows(n), lanes(j)], h_ref[step_rows(n), lanes(npair + j)],
                                           *apow(n, j), seed_ref[j], seed_ref[npair + j]))
                hb_ref[step_rows(2 * i, 2), lanes(j)] = (
                    jnp.concatenate([p[0] for p in parts], axis=0).astype(BF16))
                hb_ref[step_rows(2 * i, 2), lanes(npair + j)] = (
                    jnp.concatenate([p[1] for p in parts], axis=0).astype(BF16))

        y = _dot(hb_ref[...], ccat_ref[...]) + d_ref[...] * up_ref[...]
        g = jax.nn.gelu(y)
        for n in range(SEG_LEN):
            for k in range(kw // LANES):
                ys_ref[k, pl.ds(n, SUBLANES, stride=SEG_PITCH), :] = g[step_rows(n), lanes(k)]
        for k in range(kw // LANES):
            for r in range(SUBLANES):
                gy_ref[r * SEG_LEN:(r + 1) * SEG_LEN, lanes(k)] = (
                    ys_ref[k, r * SEG_PITCH:r * SEG_PITCH + SEG_LEN, :].astype(BF16))


def _ssm_seq(z, col0, bq, cq, pwr, pwi, d_skip, nb, t_len, side_casts):
    n = z.shape[0]
    nkb, _, kw, p = bq.shape
    sw = 2 * SSM_BLOCK_GROUPS * p
    nslab = sw // LANES
    tc = SUBLANES * SEG_LEN
    nt = t_len // tc
    cb = col0 // kw
    n_steps = nkb * nb * nt
    cast_rows = [w.shape[0] // n_steps for w in side_casts]
    assert all(r % (2 * SUBLANES) == 0 and r * n_steps == w.shape[0] for r, w in zip(cast_rows, side_casts))
    nbytes = (2 * tc * kw * 4 + 4 * kw * sw * 2 + 2 * (SEG_LEN + SUBLANES) * sw * 4 + 2 * tc * kw * 2
              + 2 * SUBLANES * SEG_PITCH * kw * 4 + tc * kw * 4 + tc * sw * 4 + tc * sw * 2 + tc * sw * 4
              + sum(2 * r * w.shape[1] * 6 for r, w in zip(cast_rows, side_casts)))

    def cast_spec(r, w):
        return pl.BlockSpec((r, w.shape[1]), lambda k, b, t: ((k * nb + b) * nt + t, 0))

    cast_specs = [cast_spec(r, w) for r, w in zip(cast_rows, side_casts)]
    gy, st, *cast = pl.pallas_call(
        functools.partial(_ssm_seq_kernel, len(side_casts)),
        grid=(nkb, nb, nt),
        in_specs=[pl.BlockSpec((tc, kw), lambda k, b, t: (b * nt + t, cb + k)),
                  pl.BlockSpec((1, 2, kw, p), lambda k, b, t: (k, 0, 0, 0)),
                  pl.BlockSpec((1, 2, p, kw), lambda k, b, t: (k, 0, 0, 0)),
                  pl.BlockSpec((1, SEG_LEN, nslab // 2, LANES), lambda k, b, t: (k, 0, 0, 0)),
                  pl.BlockSpec((1, SEG_LEN, nslab // 2, LANES), lambda k, b, t: (k, 0, 0, 0)),
                  pl.BlockSpec((1, kw), lambda k, b, t: (0, k))] + cast_specs,
        out_specs=[pl.BlockSpec((tc, kw), lambda k, b, t: (b * nt + t, k)),
                   pl.BlockSpec((1, 1, nslab, 1, LANES), lambda k, b, t: (b, k, 0, 0, 0))] + cast_specs,
        out_shape=[jax.ShapeDtypeStruct((n, nkb * kw), BF16),
                   jax.ShapeDtypeStruct((nb, nkb, nslab, 1, LANES), F32)]
                  + [jax.ShapeDtypeStruct(w.shape, BF16) for w in side_casts],
        scratch_shapes=[pltpu.VMEM((kw, sw), BF16), pltpu.VMEM((sw, kw), BF16),
                        pltpu.VMEM((kw // LANES, SUBLANES * SEG_PITCH, LANES), F32), pltpu.VMEM((tc, kw), F32),
                        pltpu.VMEM((tc, sw), F32), pltpu.VMEM((tc, sw), BF16),
                        pltpu.VMEM((kw // LANES, SUBLANES * SEG_PITCH, LANES), F32),
                        pltpu.VMEM((nslab, SUBLANES, LANES), F32), pltpu.VMEM((nslab, 1, LANES), F32)],
        compiler_params=_params(("arbitrary", "arbitrary", "arbitrary"), nbytes),
        name="ssm_seq",
    )(z, bq, cq, pwr, pwi, d_skip.reshape(1, nkb * kw), *side_casts)
    st = st.reshape(nb, nkb, 2, sw // 2)
    return gy, st[:, :, 0].reshape(nb, -1), st[:, :, 1].reshape(nb, -1), cast


def _ssm_step_kernel(u_ref, h0r_ref, h0i_ref, bq_ref, cq_ref, a_ref, d_ref,
                     gy_ref, h1r_ref, h1i_ref, h_ref, bcat_ref, ccat_ref):
    half = h0r_ref.shape[1]
    _expand_blockdiag(bq_ref, cq_ref, bcat_ref, ccat_ref)
    u = u_ref[...]
    bu = _dot(u.astype(BF16), bcat_ref[...])
    hr, hi = _cmul_add(bu[:, 0:half], bu[:, half:2 * half],
                       a_ref[0, :, 0:half], a_ref[0, :, half:2 * half],
                       h0r_ref[...], h0i_ref[...])
    h1r_ref[...] = hr
    h1i_ref[...] = hi
    h_ref[:, 0:half] = hr.astype(BF16)
    h_ref[:, half:2 * half] = hi.astype(BF16)
    y = _dot(h_ref[...], ccat_ref[...]) + d_ref[...] * u
    gy_ref[...] = jax.nn.gelu(y).astype(BF16)


def _ssm_step(z, col0, h0_re, h0_im, bq, cq, a_row, d_skip):
    n = z.shape[0]
    nkb, _, kw, p = bq.shape
    half = SSM_BLOCK_GROUPS * p
    sw = 2 * half
    cb = col0 // kw
    nbytes = 2 * (n * kw * 4 + 4 * n * half * 4 + 2 * kw * sw * 2 + sw * 4 + n * kw * 2) + n * sw * 6
    return pl.pallas_call(
        _ssm_step_kernel,
        grid=(nkb,),
        in_specs=[pl.BlockSpec((n, kw), lambda k: (0, cb + k)),
                  pl.BlockSpec((n, half), lambda k: (0, k)),
                  pl.BlockSpec((n, half), lambda k: (0, k)),
                  pl.BlockSpec((1, 2, kw, p), lambda k: (k, 0, 0, 0)),
                  pl.BlockSpec((1, 2, p, kw), lambda k: (k, 0, 0, 0)),
                  pl.BlockSpec((1, 1, sw), lambda k: (k, 0, 0)),
                  pl.BlockSpec((1, kw), lambda k: (0, k))],
        out_specs=(pl.BlockSpec((n, kw), lambda k: (0, k)),
                   pl.BlockSpec((n, half), lambda k: (0, k)),
                   pl.BlockSpec((n, half), lambda k: (0, k))),
        out_shape=(jax.ShapeDtypeStruct((n, nkb * kw), BF16),
                   jax.ShapeDtypeStruct((n, nkb * half), F32),
                   jax.ShapeDtypeStruct((n, nkb * half), F32)),
        scratch_shapes=[pltpu.VMEM((n, sw), BF16), pltpu.VMEM((kw, sw), BF16), pltpu.VMEM((sw, kw), BF16)],
        compiler_params=_params(("arbitrary",), nbytes),
        name="ssm_step",
    )(z, h0_re, h0_im, bq, cq, a_row, d_skip.reshape(1, nkb * kw))


def _mix_out_kernel(gy_ref, bp_ref, gp_ref, gs_ref, x_ref, wg_ref, wb_ref, wo_ref, o_ref):
    ds = wb_ref.shape[0]
    g = _dot(gy_ref[...], wg_ref[...])
    y = g[:, 0:ds] * jax.nn.sigmoid(g[:, ds:2 * ds])
    bs = _dot(y.astype(BF16), wb_ref[...])
    merged = jax.nn.sigmoid(gp_ref[...]) * bp_ref[...] + jax.nn.sigmoid(gs_ref[...]) * bs
    o_ref[...] = x_ref[...] + _dot(merged.astype(BF16), wo_ref[...])


def _mix_out(gy, bp, z, gate_col0, x, w_glu, w_branch, w_out, tm):
    n, ds = gy.shape
    dm = w_branch.shape[1]
    gb = gate_col0 // dm
    w_bytes = (ds * 2 * ds + ds * dm + dm * dm) * 2
    nbytes = 2 * (tm * ds * 2 + 5 * tm * dm * 4) + w_bytes + 6 * tm * dm * 4

    def resident(shape):
        return pl.BlockSpec(shape, lambda i: (0, 0), pipeline_mode=pl.Buffered(1))

    return pl.pallas_call(
        _mix_out_kernel,
        grid=(n // tm,),
        in_specs=[pl.BlockSpec((tm, ds), lambda i: (i, 0)),
                  pl.BlockSpec((tm, dm), lambda i: (i, 0)),
                  pl.BlockSpec((tm, dm), lambda i: (i, gb)),
                  pl.BlockSpec((tm, dm), lambda i: (i, gb + 1)),
                  pl.BlockSpec((tm, dm), lambda i: (i, 0)),
                  resident((ds, 2 * ds)), resident((ds, dm)), resident((dm, dm))],
        out_specs=pl.BlockSpec((tm, dm), lambda i: (i, 0)),
        out_shape=jax.ShapeDtypeStruct((n, dm), F32),
        compiler_params=_params(("arbitrary",), nbytes),
        name="mix_out",
    )(gy, bp, z, z, x, w_glu, w_branch, w_out)


def _ffn_gate_down(x_ref, conv, v, wd_ref, o_ref, acc_ref):
    c = pl.program_id(1)
    part = _dot((jax.nn.gelu(conv) * v).astype(BF16), wd_ref[...])

    @pl.when(c == 0)
    def _():
        acc_ref[...] = part

    @pl.when(c > 0)
    def _():
        acc_ref[...] += part

    @pl.when(c == pl.num_programs(1) - 1)
    def _():
        o_ref[...] = x_ref[...] + acc_ref[...]


def _ffn_seq_kernel(tiles_per_seq, x_ref, g_ref, wa_ref, wv_ref, wc_ref, bc_ref, wd_ref,
                    o_ref, new_ref, h_ref, ext_ref, carry_ref, gate_ref, v_ref):
    tm = x_ref.shape[0]
    tf = gate_ref.shape[1]
    i = pl.program_id(0)
    c = pl.program_id(1)

    @pl.when(jnp.logical_and(i == 0, c == 0))
    def _():
        o_ref[...] = jnp.zeros(o_ref.shape, F32)
        carry_ref[...] = jnp.zeros(carry_ref.shape, F32)

    @pl.when(c == 0)
    def _():
        h_ref[...] = _rmsnorm(x_ref[...], g_ref[...]).astype(BF16)

    seq_start = i % tiles_per_seq == 0
    nb = new_ref.shape[1]
    slot = jnp.minimum(c, 0)
    for q in range(tf // FFN_SLICE):
        cols = slice(q * FFN_SLICE, (q + 1) * FFN_SLICE)
        ext_ref[slot, 0:CONV_HALO, cols] = jnp.where(seq_start, 0.0, carry_ref[c, :, cols])
        ext_ref[slot, CONV_HALO:CONV_HALO + tm, cols] = _dot(h_ref[...], wa_ref[:, cols])
        v_ref[slot, :, cols] = _dot(h_ref[...], wv_ref[:, cols])
        a = ext_ref[slot, CONV_HALO:CONV_HALO + tm, cols]
        v = v_ref[slot, :, cols]
        conv = bc_ref[:, cols] + wc_ref[CONV_W - 1:CONV_W, cols] * a
        for j in range(CONV_W - 1):
            off = CONV_HALO - (CONV_W - 1) + j
            conv = conv + wc_ref[j:j + 1, cols] * ext_ref[slot, off:off + tm, cols]
        carry_ref[c, :, cols] = ext_ref[slot, tm:tm + CONV_HALO, cols]
        new_ref[0, :, cols] = ext_ref[slot, CONV_HALO + tm - nb:CONV_HALO + tm, cols]
        gate_ref[:, cols] = (jax.nn.gelu(conv) * v).astype(BF16)
    o_ref[...] = _dot(gate_ref[...], wd_ref[...]) + jnp.where(c == 0, x_ref[...], o_ref[...])


def _ffn_seq(x, g, w_up, w_conv, b_conv, w_down, nb, t_len, tm, tf, conv_buf):
    n, d = x.shape
    dff = w_down.shape[0]
    nc = dff // tf
    tps = t_len // tm
    nbytes = (4 * tm * d * 4 + 4 * d * tf * 2 + 2 * tf * d * 2 + tm * d * 2 + tm * tf * 2
              + (tm + CONV_HALO) * tf * 4 + nc * CONV_HALO * tf * 4 + 6 * tm * FFN_SLICE * 4)
    out, new_tail = pl.pallas_call(
        functools.partial(_ffn_seq_kernel, tps),
        grid=(n // tm, nc),
        in_specs=[pl.BlockSpec((tm, d), lambda i, c: (i, 0)),
                  pl.BlockSpec((1, d), lambda i, c: (0, 0)),
                  pl.BlockSpec((d, tf), lambda i, c: (0, c)),
                  pl.BlockSpec((d, tf), lambda i, c: (0, nc + c)),
                  pl.BlockSpec((CONV_W, tf), lambda i, c: (0, c)),
                  pl.BlockSpec((1, tf), lambda i, c: (0, c)),
                  pl.BlockSpec((tf, d), lambda i, c: (c, 0))],
        out_specs=(pl.BlockSpec((tm, d), lambda i, c: (i, 0)),
                   pl.BlockSpec((1, conv_buf, tf), lambda i, c: (i, 0, c))),
        out_shape=(jax.ShapeDtypeStruct((n, d), F32),
                   jax.ShapeDtypeStruct((n // tm, conv_buf, dff), F32)),
        scratch_shapes=[pltpu.VMEM((tm, d), BF16), pltpu.VMEM((1, tm + CONV_HALO, tf), F32),
                        pltpu.VMEM((nc, CONV_HALO, tf), F32), pltpu.VMEM((tm, tf), BF16),
                        pltpu.VMEM((1, tm, tf), F32)],
        compiler_params=_params(("arbitrary", "arbitrary"), nbytes),
        name="ffn_seq",
    )(x, g.reshape(1, d), w_up, w_up, w_conv, b_conv.reshape(1, dff), w_down)
    return out, new_tail[tps - 1::tps]


def _ffn_step_kernel(x_ref, g_ref, wa_ref, wv_ref, wc_ref, bc_ref, wd_ref, p0_ref, p1_ref,
                     o_ref, a_ref, h_ref, acc_ref):
    @pl.when(pl.program_id(1) == 0)
    def _():
        h_ref[...] = _rmsnorm(x_ref[...], g_ref[...]).astype(BF16)

    a = _dot(h_ref[...], wa_ref[...])
    v = _dot(h_ref[...], wv_ref[...])
    conv = bc_ref[...] + wc_ref[2:3, :] * a + wc_ref[1:2, :] * p1_ref[...] + wc_ref[0:1, :] * p0_ref[...]
    a_ref[...] = a
    _ffn_gate_down(x_ref, conv, v, wd_ref, o_ref, acc_ref)


def _ffn_step(x, g, w_up, w_conv, b_conv, w_down, cache, tf):
    n, d = x.shape
    dff = w_down.shape[0]
    nc = dff // tf
    nbytes = 4 * n * d * 4 + 4 * d * tf * 2 + 2 * tf * d * 2 + n * d * 6 + 12 * n * tf * 4
    return pl.pallas_call(
        _ffn_step_kernel,
        grid=(1, nc),
        in_specs=[pl.BlockSpec((n, d), lambda i, c: (0, 0)),
                  pl.BlockSpec((1, d), lambda i, c: (0, 0)),
                  pl.BlockSpec((d, tf), lambda i, c: (0, c)),
                  pl.BlockSpec((d, tf), lambda i, c: (0, nc + c)),
                  pl.BlockSpec((CONV_W, tf), lambda i, c: (0, c)),
                  pl.BlockSpec((1, tf), lambda i, c: (0, c)),
                  pl.BlockSpec((tf, d), lambda i, c: (c, 0)),
                  pl.BlockSpec((n, tf), lambda i, c: (0, c)),
                  pl.BlockSpec((n, tf), lambda i, c: (0, nc + c))],
        out_specs=(pl.BlockSpec((n, d), lambda i, c: (0, 0)),
                   pl.BlockSpec((n, tf), lambda i, c: (0, c))),
        out_shape=(jax.ShapeDtypeStruct((n, d), F32),
                   jax.ShapeDtypeStruct((n, dff), F32)),
        scratch_shapes=[pltpu.VMEM((n, d), BF16), pltpu.VMEM((n, d), F32)],
        compiler_params=_params(("arbitrary", "arbitrary"), nbytes),
        name="ffn_step",
    )(x, g.reshape(1, d), w_up, w_up, w_conv, b_conv.reshape(1, dff), w_down, cache, cache)


def _ple_final_kernel(x_ref, p_ref, gp_ref, wg_ref, wp_ref, gf_ref, o_ref):
    x = x_ref[...]
    gate = jax.nn.sigmoid(_dot(_rmsnorm(x, gp_ref[...]).astype(BF16), wg_ref[...]))
    x = x + gate * _dot(p_ref[...].astype(BF16), wp_ref[...])
    o_ref[...] = _rmsnorm(x, gf_ref[...])


def _ple_final(x, p, g_ple, w_gate, w_ple, g_final, tm):
    n, d = x.shape
    dp = p.shape[1]
    nbytes = 2 * (2 * tm * d * 4 + tm * dp * 4) + d * d * 2 + dp * d * 2 + 3 * tm * d * 4
    return pl.pallas_call(
        _ple_final_kernel,
        grid=(n // tm,),
        in_specs=[pl.BlockSpec((tm, d), lambda i: (i, 0)),
                  pl.BlockSpec((tm, dp), lambda i: (i, 0)),
                  pl.BlockSpec((1, d), lambda i: (0, 0)),
                  pl.BlockSpec((d, d), lambda i: (0, 0), pipeline_mode=pl.Buffered(1)),
                  pl.BlockSpec((dp, d), lambda i: (0, 0), pipeline_mode=pl.Buffered(1)),
                  pl.BlockSpec((1, d), lambda i: (0, 0))],
        out_specs=pl.BlockSpec((tm, d), lambda i: (i, 0)),
        out_shape=jax.ShapeDtypeStruct((n, d), F32),
        compiler_params=_params(("arbitrary",), nbytes),
        name="ple_final",
    )(x, p, g_ple.reshape(1, d), w_gate, w_ple, g_final.reshape(1, d))


def kernel(x_prompt, x_sample, cache_pool, state_ssm_re, state_ssm_im, cache_conv, p_prompt, p_sample, g_mix, w_in, w_pool, pool_scale, ssm_lam_re, ssm_lam_im, ssm_log_dt, ssm_b_re, ssm_b_im, ssm_c_re, ssm_c_im, ssm_d, w_glu, w_branch_pool, w_branch_ssm, w_out, g_ffn, w_up, w_conv, b_conv, w_down, g_ple, w_ple_gate, w_ple, g_final):
    depth = g_mix.shape[0]
    nb, t_len, d = x_prompt.shape
    ns = x_sample.shape[0]
    assert x_sample.shape[1] == 1, "the sample group advances one step per call"
    pool_buf, d_pool = cache_pool.shape[2], cache_pool.shape[3]
    conv_buf, d_ff = cache_conv.shape[2], cache_conv.shape[3]
    n_grp, n_state = ssm_lam_re.shape[1], ssm_lam_re.shape[2]
    d_ssm = ssm_d.shape[1]
    assert pool_buf == max(POOL_WINDOWS) - 1 and conv_buf == CONV_W - 1
    assert POOL_HALO == SUBLANES * (max(POOL_WINDOWS).bit_length() - 1) and pool_buf <= POOL_HALO
    assert n_state == SSM_STATE and d_ssm == n_grp * SSM_GROUP

    xp = x_prompt.reshape(nb * t_len, d)
    xs = x_sample.reshape(ns, d)
    outs = [[] for _ in range(8)]
    for i in range(depth):
        w_pool_b, w_ple_b = w_pool[i].astype(BF16), w_ple[i].astype(BF16)
        bq, cq, a_pow_re, a_pow_im, a_row = _ssm_params(ssm_lam_re[i], ssm_lam_im[i], ssm_log_dt[i],
                                                        ssm_b_re[i], ssm_b_im[i], ssm_c_re[i], ssm_c_im[i])

        z_sample, w_in_b = _norm_matmul(xs, g_mix[i], w_in[i], ns, IN_COLS_F32)

        z = _norm_matmul(xp, g_mix[i], w_in_b, IN_ROWS, IN_COLS)
        gy, st_re, st_im, (w_up_b, w_down_b, w_out_b, w_pg_b, w_glu_b, w_bs_b, w_bp_b) = _ssm_seq(
            z, d_pool, bq, cq, a_pow_re, a_pow_im, ssm_d[i], nb, t_len,
            [w_up[i], w_down[i], w_out[i], w_ple_gate[i], w_glu[i], w_branch_ssm[i], w_branch_pool[i]])

        def mix_tail(x, z, bp, gy, tm):
            return _mix_out(gy, bp, z, d_pool + d_ssm, x, w_glu_b, w_bs_b, w_out_b, tm)

        bp, pool_new = _pool_seq(z, w_pool_b, pool_scale[i], w_bp_b, nb, t_len, POOL_ROWS, pool_buf)
        xp = mix_tail(xp, z, bp, gy, MIX_ROWS)
        xp, conv_new = _ffn_seq(xp, g_ffn[i], w_up_b, w_conv[i], b_conv[i], w_down_b, nb, t_len, FFN_ROWS, FFN_COLS,
                                conv_buf)
        xp_out = _ple_final(xp, p_prompt[i].reshape(nb * t_len, -1), g_ple[i], w_pg_b, w_ple_b, g_final, PLE_ROWS)
        for lst, val in zip(outs[:4], (pool_new, st_re.reshape(nb, n_grp, n_state),
                                       st_im.reshape(nb, n_grp, n_state), conv_new)):
            lst.append(val)

        z = z_sample
        bp, pool_new = _pool_step(z, cache_pool[i].reshape(ns, pool_buf * d_pool), w_pool_b, pool_scale[i], w_bp_b)
        gy, st_re, st_im = _ssm_step(z, d_pool, state_ssm_re[i].reshape(ns, -1), state_ssm_im[i].reshape(ns, -1),
                                     bq, cq, a_row, ssm_d[i])
        xs = mix_tail(xs, z, bp, gy, ns)
        xs, a_new = _ffn_step(xs, g_ffn[i], w_up_b, w_conv[i], b_conv[i], w_down_b,
                              cache_conv[i].reshape(ns, conv_buf * d_ff), FFN_STEP_COLS)
        conv_new = jnp.concatenate([cache_conv[i][:, 1:], a_new[:, None, :]], axis=1)
        xs_out = _ple_final(xs, p_sample[i].reshape(ns, -1), g_ple[i], w_pg_b, w_ple_b, g_final, ns)
        for lst, val in zip(outs[4:], (pool_new.reshape(ns, pool_buf, d_pool), st_re.reshape(ns, n_grp, n_state),
                                       st_im.reshape(ns, n_grp, n_state), conv_new)):
            lst.append(val)

    assert depth == 1
    y_prompt = xp_out.reshape(nb, t_len, d)
    y_sample = xs_out.reshape(ns, 1, d)
    return (y_prompt, y_sample) + tuple(jnp.stack(o, axis=0) for o in outs)
```

```python
import functools

import jax
import jax.numpy as jnp
from jax import lax
from jax.experimental import pallas as pl
from jax.experimental.pallas import tpu as pltpu

F32 = jnp.float32
BF16 = jnp.bfloat16

EPS = 1e-6
POOL_WINDOWS = (2, 4, 8, 16)
POOL_HALO = 32
SSM_GROUP = 16
SSM_STATE = 64
SSM_BLOCK_GROUPS = 16
LANES = 128
SUBLANES = 8
SEG_LEN = 128
SEG_PITCH = SEG_LEN + 8
CONV_W = 3
CONV_HALO = 8
V7X_MXU_COLS = 256
FFN_SLICE = 2 * V7X_MXU_COLS
W_TILE = 2 * V7X_MXU_COLS
V7X_VMEM_BYTES = 64 * 1024 * 1024
VMEM_UNSCOPED_BYTES = 6 << 20
VMEM_TEMP_BYTES = 8 << 20

IN_ROWS, IN_COLS = 1024, 4 * W_TILE
POOL_ROWS = 1024
IN_COLS_F32 = 3 * W_TILE
MIX_ROWS = 256
FFN_ROWS, FFN_COLS = 1024, W_TILE
FFN_STEP_COLS = 2 * W_TILE
PLE_ROWS = 1024


def _vmem_limit(nbytes):
    return int(min(nbytes * 1.25 + VMEM_TEMP_BYTES, V7X_VMEM_BYTES - VMEM_UNSCOPED_BYTES))


def _params(sem, nbytes):
    return pltpu.CompilerParams(dimension_semantics=sem, vmem_limit_bytes=_vmem_limit(nbytes))


def _rmsnorm(x, g):
    return x * lax.rsqrt(jnp.mean(x * x, axis=-1, keepdims=True) + EPS) * g


def _dot(a, b):
    return jnp.dot(a, b, preferred_element_type=F32)


def _abar(lr, li, logdt):
    dt = jnp.exp(logdt)
    mag = jnp.exp(lr * dt)
    return mag * jnp.cos(li * dt), mag * jnp.sin(li * dt)


def _ssm_params_kernel(lr_ref, li_ref, logdt_ref, lrs_ref, lis_ref, logdts_ref, br_ref, bi_ref,
                       pwr_ref, pwi_ref, bbr_ref, bbi_ref):
    lr = lr_ref[...]
    li = li_ref[...]
    a_re, a_im = _abar(lr, li, logdt_ref[...])
    nr = a_re - 1.0
    ni = a_im
    den = lr * lr + li * li
    coef_re = ((nr * lr + ni * li) / den)[:, None, :]
    coef_im = ((ni * lr - nr * li) / den)[:, None, :]
    br = br_ref[...]
    bi = bi_ref[...]
    bbr_ref[...] = coef_re * br - coef_im * bi
    bbi_ref[...] = coef_re * bi + coef_im * br
    a_re, a_im = _abar(lrs_ref[...], lis_ref[...], logdts_ref[...])
    nkb, _, per_blk, _ = pwr_ref.shape
    pr, pi = a_re, a_im
    for n in range(SEG_LEN):
        if n:
            pr, pi = pr * a_re - pi * a_im, pr * a_im + pi * a_re
        for k in range(nkb):
            pwr_ref[k, n] = pr[k * per_blk:(k + 1) * per_blk, :]
            pwi_ref[k, n] = pi[k * per_blk:(k + 1) * per_blk, :]


def _ssm_params(lam_re, lam_im, log_dt, b_re, b_im, c_re, c_im):
    g, p = lam_re.shape
    h = b_re.shape[-1]
    nkb = g // SSM_BLOCK_GROUPS
    bl = SSM_BLOCK_GROUPS
    per_blk = bl * p // LANES
    slab = (g * p // LANES, LANES)
    pwr, pwi, bbr, bbi = pl.pallas_call(
        _ssm_params_kernel,
        out_shape=(jax.ShapeDtypeStruct((nkb, SEG_LEN, per_blk, LANES), F32),
                   jax.ShapeDtypeStruct((nkb, SEG_LEN, per_blk, LANES), F32),
                   jax.ShapeDtypeStruct((g, h, p), F32),
                   jax.ShapeDtypeStruct((g, h, p), F32)),
        name="ssm_params",
    )(lam_re, lam_im, log_dt.reshape(g, 1),
      lam_re.reshape(slab), lam_im.reshape(slab), jnp.broadcast_to(log_dt[:, None], (g, p)).reshape(slab),
      jnp.transpose(b_re, (0, 2, 1)), jnp.transpose(b_im, (0, 2, 1)))

    def c_rows(c):
        return jnp.transpose(c.reshape(nkb, bl * h, p), (0, 2, 1))

    bq = jnp.stack([bbr.reshape(nkb, bl * h, p), bbi.reshape(nkb, bl * h, p)], axis=1)
    cq = jnp.stack([c_rows(c_re), -c_rows(c_im)], axis=1)
    a_row = jnp.concatenate([pwr[:, 0].reshape(nkb, 1, bl * p), pwi[:, 0].reshape(nkb, 1, bl * p)], axis=-1)
    return bq, cq, pwr, pwi, a_row


def _expand_blockdiag(bq_ref, cq_ref, bcat_ref, ccat_ref):
    kw, p = bq_ref.shape[2], bq_ref.shape[3]
    half = bcat_ref.shape[1] // 2
    hch = kw * p // half
    log_p, log_h = p.bit_length() - 1, hch.bit_length() - 1
    assert (1 << log_p) == p and (1 << log_h) == hch

    def iota(shape, axis):
        return lax.broadcasted_iota(jnp.int32, shape, axis)

    def same(a, b):
        return a == b

    tile_b = same(iota((p, half), 1) & (p - 1), iota((p, half), 0)).astype(BF16)
    mask_b = same(iota((kw, half), 0) >> log_h, iota((kw, half), 1) >> log_p)
    tile_c = same(iota((half, p), 0) & (p - 1), iota((half, p), 1)).astype(BF16)
    mask_c = same(iota((half, kw), 0) >> log_p, iota((half, kw), 1) >> log_h)
    for s in range(2):
        full = _dot(bq_ref[0, s].astype(BF16), tile_b)
        bcat_ref[:, s * half:(s + 1) * half] = jnp.where(mask_b, full, 0.0).astype(BF16)
        full = _dot(tile_c, cq_ref[0, s].astype(BF16))
        ccat_ref[s * half:(s + 1) * half, :] = jnp.where(mask_c, full, 0.0).astype(BF16)


def _norm_matmul_kernel(x_ref, g_ref, w_ref, o_ref, *rest):
    *wb_ref, h_ref = rest

    @pl.when(pl.program_id(1) == 0)
    def _():
        h_ref[...] = _rmsnorm(x_ref[...], g_ref[...]).astype(BF16)

    w = w_ref[...]
    if wb_ref:
        w = w.astype(BF16)
        wb_ref[0][...] = w
    o_ref[...] = _dot(h_ref[...], w)


def _norm_matmul(x, g, w, tm, tn):
    n, d = x.shape
    dout = w.shape[1]
    convert = w.dtype == F32
    assert not convert or n == tm, "each weight tile must be visited exactly once to be converted"
    nbytes = 2 * tm * d * 4 + tm * d * 2 + 2 * d * tn * w.dtype.itemsize + 2 * tm * tn * 4 + convert * 3 * d * tn * 2
    z_spec = pl.BlockSpec((tm, tn), lambda i, j: (i, j))
    w_spec = pl.BlockSpec((d, tn), lambda i, j: (0, j))
    out = pl.pallas_call(
        _norm_matmul_kernel,
        grid=(n // tm, dout // tn),
        in_specs=[pl.BlockSpec((tm, d), lambda i, j: (i, 0)),
                  pl.BlockSpec((1, d), lambda i, j: (0, 0)),
                  w_spec],
        out_specs=[z_spec] + [w_spec] * convert,
        out_shape=[jax.ShapeDtypeStruct((n, dout), F32)] + [jax.ShapeDtypeStruct(w.shape, BF16)] * convert,
        scratch_shapes=[pltpu.VMEM((tm, d), BF16)],
        compiler_params=_params(("arbitrary", "arbitrary"), nbytes),
        name="norm_matmul",
    )(x, g.reshape(1, d), w)
    return out if convert else out[0]


def _pool_project(diffs, wp_ref, scale_ref, wb_ref, y_ref):
    gw = wp_ref.shape[1]
    for k, diff in enumerate(diffs):
        yk = _dot(diff.astype(BF16), wp_ref[k]) * scale_ref[:, k * gw:(k + 1) * gw]
        y_ref[:, k * gw:(k + 1) * gw] = yk.astype(BF16)
    return _dot(y_ref[...], wb_ref[...])


def _pool_seq_kernel(u_ref, wp_ref, scale_ref, wb_ref, o_ref, new_ref, ext_ref, y_ref, tmp_ref):
    tc, dp = u_ref.shape
    gw = wp_ref.shape[1]
    t = pl.program_id(1)
    end = POOL_HALO + tc

    @pl.when(t == 0)
    def _():
        ext_ref[0:POOL_HALO, :] = jnp.zeros((POOL_HALO, dp), F32)

    ext_ref[POOL_HALO:end, :] = u_ref[...]
    pos = (t * tc + 1 + lax.broadcasted_iota(jnp.int32, (tc, 1), 0)).astype(F32)
    diffs = []
    for k, w in enumerate(POOL_WINDOWS):
        cols = slice(k * gw, (k + 1) * gw)
        u = ext_ref[POOL_HALO:end, cols]
        src, m, start = (ext_ref, cols), 1, 0
        while m < w:
            ref, c = src
            start += SUBLANES
            s = ref[start:end, c] + ref[start - m:end - m, c]
            m *= 2
            if m < w:
                level = (m.bit_length() & 1, slice(None))
                tmp_ref[level[0], start:end, :] = s
                src = (tmp_ref.at[level[0]], level[1])
        s = s[POOL_HALO - start:, :]
        count = jnp.minimum(pos, float(w))
        diffs.append(s / count - u)
    o_ref[...] = _pool_project(diffs, wp_ref, scale_ref, wb_ref, y_ref)
    nb = new_ref.shape[1]
    new_ref[0] = ext_ref[POOL_HALO + tc - nb:POOL_HALO + tc, :]
    ext_ref[0:POOL_HALO, :] = ext_ref[tc:tc + POOL_HALO, :]


def _pool_seq(z, w_pool, pool_scale, w_branch, nb, t_len, tc, pool_buf):
    n = z.shape[0]
    ng, gw, _ = w_pool.shape
    dp = ng * gw
    dm = w_branch.shape[1]
    nt = t_len // tc
    nbytes = (2 * tc * dp * 4 + 2 * ng * gw * gw * 2 + 2 * dp * dm * 2 + 2 * tc * dm * 4
              + (tc + POOL_HALO) * dp * 4 + tc * dp * 2 + 4 * tc * gw * 4)
    return pl.pallas_call(
        _pool_seq_kernel,
        grid=(nb, nt),
        in_specs=[pl.BlockSpec((tc, dp), lambda b, t: (b * nt + t, 0)),
                  pl.BlockSpec((ng, gw, gw), lambda b, t: (0, 0, 0)),
                  pl.BlockSpec((1, dp), lambda b, t: (0, 0)),
                  pl.BlockSpec((dp, dm), lambda b, t: (0, 0))],
        out_specs=(pl.BlockSpec((tc, dm), lambda b, t: (b * nt + t, 0)),
                   pl.BlockSpec((1, pool_buf, dp), lambda b, t: (b, 0, 0))),
        out_shape=(jax.ShapeDtypeStruct((n, dm), F32),
                   jax.ShapeDtypeStruct((nb, pool_buf, dp), F32)),
        scratch_shapes=[pltpu.VMEM((tc + POOL_HALO, dp), F32), pltpu.VMEM((tc, dp), BF16),
                        pltpu.VMEM((2, tc + POOL_HALO, gw), F32)],
        compiler_params=_params(("arbitrary", "arbitrary"), nbytes),
        name="pool_seq",
    )(z, w_pool, pool_scale.reshape(1, dp), w_branch)


def _pool_step_kernel(u_ref, cache_ref, wp_ref, scale_ref, wb_ref, o_ref, new_ref, y_ref):
    dp = u_ref.shape[1]
    gw = wp_ref.shape[1]
    lb = cache_ref.shape[1] // dp
    diffs = []
    for k, w in enumerate(POOL_WINDOWS):
        u = u_ref[:, k * gw:(k + 1) * gw]
        s = u
        for j in range(1, w):
            s = s + cache_ref[:, (lb - j) * dp + k * gw:(lb - j) * dp + (k + 1) * gw]
        diffs.append(s / float(w) - u)
    o_ref[...] = _pool_project(diffs, wp_ref, scale_ref, wb_ref, y_ref)
    new_ref[:, 0:(lb - 1) * dp] = cache_ref[:, dp:lb * dp]
    new_ref[:, (lb - 1) * dp:lb * dp] = u_ref[...]


def _pool_step(z, cache, w_pool, pool_scale, w_branch):
    n = z.shape[0]
    ng, gw, _ = w_pool.shape
    dp = ng * gw
    dm = w_branch.shape[1]
    lbdp = cache.shape[1]
    nbytes = 2 * (n * dp * 4 + 2 * n * lbdp * 4 + ng * gw * gw * 2 + dp * dm * 2 + n * dm * 4) + n * dp * 2
    return pl.pallas_call(
        _pool_step_kernel,
        grid=(1,),
        in_specs=[pl.BlockSpec((n, dp), lambda i: (0, 0)),
                  pl.BlockSpec((n, lbdp), lambda i: (0, 0)),
                  pl.BlockSpec((ng, gw, gw), lambda i: (0, 0, 0)),
                  pl.BlockSpec((1, dp), lambda i: (0, 0)),
                  pl.BlockSpec((dp, dm), lambda i: (0, 0))],
        out_specs=(pl.BlockSpec((n, dm), lambda i: (0, 0)),
                   pl.BlockSpec((n, lbdp), lambda i: (0, 0))),
        out_shape=(jax.ShapeDtypeStruct((n, dm), F32),
                   jax.ShapeDtypeStruct((n, lbdp), F32)),
        scratch_shapes=[pltpu.VMEM((n, dp), BF16)],
        compiler_params=_params(("arbitrary",), nbytes),
        name="pool_step",
    )(z, cache, w_pool, pool_scale.reshape(1, dp), w_branch)


def _cmul_add(xr, xi, ar, ai, br, bi):
    return xr + ar * br - ai * bi, xi + ar * bi + ai * br


def _ssm_seq_kernel(n_cast, u_ref, bq_ref, cq_ref, pwr_ref, pwi_ref, d_ref, *refs):
    cast_in, (gy_ref, st_ref), refs = refs[:n_cast], refs[n_cast:n_cast + 2], refs[n_cast + 2:]
    cast_out, refs = refs[:n_cast], refs[n_cast:]
    bcat_ref, ccat_ref, us_ref, up_ref, h_ref, hb_ref, ys_ref, seed_ref, carry_ref = refs
    for src_ref, dst_ref in zip(cast_in, cast_out):
        dst_ref[...] = src_ref[...].astype(BF16)

    kw = u_ref.shape[1]
    nslab = h_ref.shape[1] // LANES
    npair = nslab // 2
    t = pl.program_id(2)

    def lanes(j):
        return slice(j * LANES, (j + 1) * LANES)

    def step_rows(n, count=1):
        return slice(n * SUBLANES, (n + count) * SUBLANES)

    def apow(n, j):
        return pwr_ref[0, n, j:j + 1, :], pwi_ref[0, n, j:j + 1, :]

    @pl.when(jnp.logical_and(pl.program_id(1) == 0, t == 0))
    def _():
        _expand_blockdiag(bq_ref, cq_ref, bcat_ref, ccat_ref)

    @pl.when(t == 0)
    def _():
        carry_ref[...] = jnp.zeros_like(carry_ref)

    for k in range(kw // LANES):
        for r in range(SUBLANES):
            us_ref[k, r * SEG_PITCH:r * SEG_PITCH + SEG_LEN, :] = u_ref[r * SEG_LEN:(r + 1) * SEG_LEN, lanes(k)]
    for n in range(SEG_LEN):
        for k in range(kw // LANES):
            up_ref[step_rows(n), lanes(k)] = us_ref[k, pl.ds(n, SUBLANES, stride=SEG_PITCH), :]
    u = up_ref[...]
    h_ref[...] = _dot(u.astype(BF16), bcat_ref[...])

    state = [jnp.zeros((SUBLANES, LANES), F32)] * nslab
    for n in range(SEG_LEN):
        for j in range(npair):
            sr, si = _cmul_add(h_ref[step_rows(n), lanes(j)], h_ref[step_rows(n), lanes(npair + j)],
                               *apow(0, j), state[j], state[npair + j])
            h_ref[step_rows(n), lanes(j)] = sr
            h_ref[step_rows(n), lanes(npair + j)] = si
            state[j], state[npair + j] = sr, si

    for j in range(npair):
        cr, ci = carry_ref[j], carry_ref[npair + j]
        for r in range(SUBLANES):
            seed_ref[j, r:r + 1, :] = cr
            seed_ref[npair + j, r:r + 1, :] = ci
            cr, ci = _cmul_add(state[j][r:r + 1, :], state[npair + j][r:r + 1, :],
                               *apow(SEG_LEN - 1, j), cr, ci)
        carry_ref[j] = cr
        carry_ref[npair + j] = ci
    st_ref[0, 0] = carry_ref[...]

    @pl.when(t >= 0)
    def _():
        for i in range(SEG_LEN // 2):
            for j in range(npair):
                parts = []
                for n in (2 * i, 2 * i + 1):
                    parts.append(_cmul_add(h_ref[step_rows(n), lanes(j)], h_ref[step_rows(n), lanes(npair + j)],
                                           *apow(n, j), seed_ref[j], seed_ref[npair + j]))
                hb_ref[step_rows(2 * i, 2), lanes(j)] = (
                    jnp.concatenate([p[0] for p in parts], axis=0).astype(BF16))
                hb_ref[step_rows(2 * i, 2), lanes(npair + j)] = (
                    jnp.concatenate([p[1] for p in parts], axis=0).astype(BF16))

        y = _dot(hb_ref[...], ccat_ref[...]) + d_ref[...] * up_ref[...]
        g = jax.nn.gelu(y)
        for n in range(SEG_LEN):
            for k in range(kw // LANES):
                ys_ref[k, pl.ds(n, SUBLANES, stride=SEG_PITCH), :] = g[step_rows(n), lanes(k)]
        for k in range(kw // LANES):
            for r in range(SUBLANES):
                gy_ref[r * SEG_LEN:(r + 1) * SEG_LEN, lanes(k)] = (
                    ys_ref[k, r * SEG_PITCH:r * SEG_PITCH + SEG_LEN, :].astype(BF16))


def _ssm_seq(z, col0, bq, cq, pwr, pwi, d_skip, nb, t_len, side_casts):
    n = z.shape[0]
    nkb, _, kw, p = bq.shape
    sw = 2 * SSM_BLOCK_GROUPS * p
    nslab = sw // LANES
    tc = SUBLANES * SEG_LEN
    nt = t_len // tc
    cb = col0 // kw
    n_steps = nkb * nb * nt
    cast_rows = [w.shape[0] // n_steps for w in side_casts]
    assert all(r % (2 * SUBLANES) == 0 and r * n_steps == w.shape[0] for r, w in zip(cast_rows, side_casts))
    nbytes = (2 * tc * kw * 4 + 4 * kw * sw * 2 + 2 * (SEG_LEN + SUBLANES) * sw * 4 + 2 * tc * kw * 2
              + 2 * SUBLANES * SEG_PITCH * kw * 4 + tc * kw * 4 + tc * sw * 4 + tc * sw * 2 + tc * sw * 4
              + sum(2 * r * w.shape[1] * 6 for r, w in zip(cast_rows, side_casts)))

    def cast_spec(r, w):
        return pl.BlockSpec((r, w.shape[1]), lambda k, b, t: ((k * nb + b) * nt + t, 0))

    cast_specs = [cast_spec(r, w) for r, w in zip(cast_rows, side_casts)]
    gy, st, *cast = pl.pallas_call(
        functools.partial(_ssm_seq_kernel, len(side_casts)),
        grid=(nkb, nb, nt),
        in_specs=[pl.BlockSpec((tc, kw), lambda k, b, t: (b * nt + t, cb + k)),
                  pl.BlockSpec((1, 2, kw, p), lambda k, b, t: (k, 0, 0, 0)),
                  pl.BlockSpec((1, 2, p, kw), lambda k, b, t: (k, 0, 0, 0)),
                  pl.BlockSpec((1, SEG_LEN, nslab // 2, LANES), lambda k, b, t: (k, 0, 0, 0)),
                  pl.BlockSpec((1, SEG_LEN, nslab // 2, LANES), lambda k, b, t: (k, 0, 0, 0)),
                  pl.BlockSpec((1, kw), lambda k, b, t: (0, k))] + cast_specs,
        out_specs=[pl.BlockSpec((tc, kw), lambda k, b, t: (b * nt + t, k)),
                   pl.BlockSpec((1, 1, nslab, 1, LANES), lambda k, b, t: (b, k, 0, 0, 0))] + cast_specs,
        out_shape=[jax.ShapeDtypeStruct((n, nkb * kw), BF16),
                   jax.ShapeDtypeStruct((nb, nkb, nslab, 1, LANES), F32)]
                  + [jax.ShapeDtypeStruct(w.shape, BF16) for w in side_casts],
        scratch_shapes=[pltpu.VMEM((kw, sw), BF16), pltpu.VMEM((sw, kw), BF16),
                        pltpu.VMEM((kw // LANES, SUBLANES * SEG_PITCH, LANES), F32), pltpu.VMEM((tc, kw), F32),
                        pltpu.VMEM((tc, sw), F32), pltpu.VMEM((tc, sw), BF16),
                        pltpu.VMEM((kw // LANES, SUBLANES * SEG_PITCH, LANES), F32),
                        pltpu.VMEM((nslab, SUBLANES, LANES), F32), pltpu.VMEM((nslab, 1, LANES), F32)],
        compiler_params=_params(("arbitrary", "arbitrary", "arbitrary"), nbytes),
        name="ssm_seq",
    )(z, bq, cq, pwr, pwi, d_skip.reshape(1, nkb * kw), *side_casts)
    st = st.reshape(nb, nkb, 2, sw // 2)
    return gy, st[:, :, 0].reshape(nb, -1), st[:, :, 1].reshape(nb, -1), cast


def _ssm_step_kernel(u_ref, h0r_ref, h0i_ref, bq_ref, cq_ref, a_ref, d_ref,
                     gy_ref, h1r_ref, h1i_ref, h_ref, bcat_ref, ccat_ref):
    half = h0r_ref.shape[1]
    _expand_blockdiag(bq_ref, cq_ref, bcat_ref, ccat_ref)
    u = u_ref[...]
    bu = _dot(u.astype(BF16), bcat_ref[...])
    hr, hi = _cmul_add(bu[:, 0:half], bu[:, half:2 * half],
                       a_ref[0, :, 0:half], a_ref[0, :, half:2 * half],
                       h0r_ref[...], h0i_ref[...])
    h1r_ref[...] = hr
    h1i_ref[...] = hi
    h_ref[:, 0:half] = hr.astype(BF16)
    h_ref[:, half:2 * half] = hi.astype(BF16)
    y = _dot(h_ref[...], ccat_ref[...]) + d_ref[...] * u
    gy_ref[...] = jax.nn.gelu(y).astype(BF16)


def _ssm_step(z, col0, h0_re, h0_im, bq, cq, a_row, d_skip):
    n = z.shape[0]
    nkb, _, kw, p = bq.shape
    half = SSM_BLOCK_GROUPS * p
    sw = 2 * half
    cb = col0 // kw
    nbytes = 2 * (n * kw * 4 + 4 * n * half * 4 + 2 * kw * sw * 2 + sw * 4 + n * kw * 2) + n * sw * 6
    return pl.pallas_call(
        _ssm_step_kernel,
        grid=(nkb,),
        in_specs=[pl.BlockSpec((n, kw), lambda k: (0, cb + k)),
                  pl.BlockSpec((n, half), lambda k: (0, k)),
                  pl.BlockSpec((n, half), lambda k: (0, k)),
                  pl.BlockSpec((1, 2, kw, p), lambda k: (k, 0, 0, 0)),
                  pl.BlockSpec((1, 2, p, kw), lambda k: (k, 0, 0, 0)),
                  pl.BlockSpec((1, 1, sw), lambda k: (k, 0, 0)),
                  pl.BlockSpec((1, kw), lambda k: (0, k))],
        out_specs=(pl.BlockSpec((n, kw), lambda k: (0, k)),
                   pl.BlockSpec((n, half), lambda k: (0, k)),
                   pl.BlockSpec((n, half), lambda k: (0, k))),
        out_shape=(jax.ShapeDtypeStruct((n, nkb * kw), BF16),
                   jax.ShapeDtypeStruct((n, nkb * half), F32),
                   jax.ShapeDtypeStruct((n, nkb * half), F32)),
        scratch_shapes=[pltpu.VMEM((n, sw), BF16), pltpu.VMEM((kw, sw), BF16), pltpu.VMEM((sw, kw), BF16)],
        compiler_params=_params(("arbitrary",), nbytes),
        name="ssm_step",
    )(z, h0_re, h0_im, bq, cq, a_row, d_skip.reshape(1, nkb * kw))


def _mix_out_kernel(gy_ref, bp_ref, gp_ref, gs_ref, x_ref, wg_ref, wb_ref, wo_ref, o_ref):
    ds = wb_ref.shape[0]
    g = _dot(gy_ref[...], wg_ref[...])
    y = g[:, 0:ds] * jax.nn.sigmoid(g[:, ds:2 * ds])
    bs = _dot(y.astype(BF16), wb_ref[...])
    merged = jax.nn.sigmoid(gp_ref[...]) * bp_ref[...] + jax.nn.sigmoid(gs_ref[...]) * bs
    o_ref[...] = x_ref[...] + _dot(merged.astype(BF16), wo_ref[...])


def _mix_out(gy, bp, z, gate_col0, x, w_glu, w_branch, w_out, tm):
    n, ds = gy.shape
    dm = w_branch.shape[1]
    gb = gate_col0 // dm
    w_bytes = (ds * 2 * ds + ds * dm + dm * dm) * 2
    nbytes = 2 * (tm * ds * 2 + 5 * tm * dm * 4) + w_bytes + 6 * tm * dm * 4

    def resident(shape):
        return pl.BlockSpec(shape, lambda i: (0, 0), pipeline_mode=pl.Buffered(1))

    return pl.pallas_call(
        _mix_out_kernel,
        grid=(n // tm,),
        in_specs=[pl.BlockSpec((tm, ds), lambda i: (i, 0)),
                  pl.BlockSpec((tm, dm), lambda i: (i, 0)),
                  pl.BlockSpec((tm, dm), lambda i: (i, gb)),
                  pl.BlockSpec((tm, dm), lambda i: (i, gb + 1)),
                  pl.BlockSpec((tm, dm), lambda i: (i, 0)),
                  resident((ds, 2 * ds)), resident((ds, dm)), resident((dm, dm))],
        out_specs=pl.BlockSpec((tm, dm), lambda i: (i, 0)),
        out_shape=jax.ShapeDtypeStruct((n, dm), F32),
        compiler_params=_params(("arbitrary",), nbytes),
        name="mix_out",
    )(gy, bp, z, z, x, w_glu, w_branch, w_out)


def _ffn_gate_down(x_ref, conv, v, wd_ref, o_ref, acc_ref):
    c = pl.program_id(1)
    part = _dot((jax.nn.gelu(conv) * v).astype(BF16), wd_ref[...])

    @pl.when(c == 0)
    def _():
        acc_ref[...] = part

    @pl.when(c > 0)
    def _():
        acc_ref[...] += part

    @pl.when(c == pl.num_programs(1) - 1)
    def _():
        o_ref[...] = x_ref[...] + acc_ref[...]


def _ffn_seq_kernel(tiles_per_seq, x_ref, g_ref, wa_ref, wv_ref, wc_ref, bc_ref, wd_ref,
                    o_ref, new_ref, h_ref, ext_ref, carry_ref, gate_ref, v_ref):
    tm = x_ref.shape[0]
    tf = gate_ref.shape[1]
    i = pl.program_id(0)
    c = pl.program_id(1)

    @pl.when(jnp.logical_and(i == 0, c == 0))
    def _():
        o_ref[...] = jnp.zeros(o_ref.shape, F32)
        carry_ref[...] = jnp.zeros(carry_ref.shape, F32)

    @pl.when(c == 0)
    def _():
        h_ref[...] = _rmsnorm(x_ref[...], g_ref[...]).astype(BF16)

    seq_start = i % tiles_per_seq == 0
    nb = new_ref.shape[1]
    slot = jnp.minimum(c, 0)
    for q in range(tf // FFN_SLICE):
        cols = slice(q * FFN_SLICE, (q + 1) * FFN_SLICE)
        ext_ref[slot, 0:CONV_HALO, cols] = jnp.where(seq_start, 0.0, carry_ref[c, :, cols])
        ext_ref[slot, CONV_HALO:CONV_HALO + tm, cols] = _dot(h_ref[...], wa_ref[:, cols])
        v_ref[slot, :, cols] = _dot(h_ref[...], wv_ref[:, cols])
        a = ext_ref[slot, CONV_HALO:CONV_HALO + tm, cols]
        v = v_ref[slot, :, cols]
        conv = bc_ref[:, cols] + wc_ref[CONV_W - 1:CONV_W, cols] * a
        for j in range(CONV_W - 1):
            off = CONV_HALO - (CONV_W - 1) + j
            conv = conv + wc_ref[j:j + 1, cols] * ext_ref[slot, off:off + tm, cols]
        carry_ref[c, :, cols] = ext_ref[slot, tm:tm + CONV_HALO, cols]
        new_ref[0, :, cols] = ext_ref[slot, CONV_HALO + tm - nb:CONV_HALO + tm, cols]
        gate_ref[:, cols] = (jax.nn.gelu(conv) * v).astype(BF16)
    o_ref[...] = _dot(gate_ref[...], wd_ref[...]) + jnp.where(c == 0, x_ref[...], o_ref[...])


def _ffn_seq(x, g, w_up, w_conv, b_conv, w_down, nb, t_len, tm, tf, conv_buf):
    n, d = x.shape
    dff = w_down.shape[0]
    nc = dff // tf
    tps = t_len // tm
    nbytes = (4 * tm * d * 4 + 4 * d * tf * 2 + 2 * tf * d * 2 + tm * d * 2 + tm * tf * 2
              + (tm + CONV_HALO) * tf * 4 + nc * CONV_HALO * tf * 4 + 6 * tm * FFN_SLICE * 4)
    out, new_tail = pl.pallas_call(
        functools.partial(_ffn_seq_kernel, tps),
        grid=(n // tm, nc),
        in_specs=[pl.BlockSpec((tm, d), lambda i, c: (i, 0)),
                  pl.BlockSpec((1, d), lambda i, c: (0, 0)),
                  pl.BlockSpec((d, tf), lambda i, c: (0, c)),
                  pl.BlockSpec((d, tf), lambda i, c: (0, nc + c)),
                  pl.BlockSpec((CONV_W, tf), lambda i, c: (0, c)),
                  pl.BlockSpec((1, tf), lambda i, c: (0, c)),
                  pl.BlockSpec((tf, d), lambda i, c: (c, 0))],
        out_specs=(pl.BlockSpec((tm, d), lambda i, c: (i, 0)),
                   pl.BlockSpec((1, conv_buf, tf), lambda i, c: (i, 0, c))),
        out_shape=(jax.ShapeDtypeStruct((n, d), F32),
                   jax.ShapeDtypeStruct((n // tm, conv_buf, dff), F32)),
        scratch_shapes=[pltpu.VMEM((tm, d), BF16), pltpu.VMEM((1, tm + CONV_HALO, tf), F32),
                        pltpu.VMEM((nc, CONV_HALO, tf), F32), pltpu.VMEM((tm, tf), BF16),
                        pltpu.VMEM((1, tm, tf), F32)],
        compiler_params=_params(("arbitrary", "arbitrary"), nbytes),
        name="ffn_seq",
    )(x, g.reshape(1, d), w_up, w_up, w_conv, b_conv.reshape(1, dff), w_down)
    return out, new_tail[tps - 1::tps]


def _ffn_step_kernel(x_ref, g_ref, wa_ref, wv_ref, wc_ref, bc_ref, wd_ref, p0_ref, p1_ref,
                     o_ref, a_ref, h_ref, acc_ref):
    @pl.when(pl.program_id(1) == 0)
    def _():
        h_ref[...] = _rmsnorm(x_ref[...], g_ref[...]).astype(BF16)

    a = _dot(h_ref[...], wa_ref[...])
    v = _dot(h_ref[...], wv_ref[...])
    conv = bc_ref[...] + wc_ref[2:3, :] * a + wc_ref[1:2, :] * p1_ref[...] + wc_ref[0:1, :] * p0_ref[...]
    a_ref[...] = a
    _ffn_gate_down(x_ref, conv, v, wd_ref, o_ref, acc_ref)


def _ffn_step(x, g, w_up, w_conv, b_conv, w_down, cache, tf):
    n, d = x.shape
    dff = w_down.shape[0]
    nc = dff // tf
    nbytes = 4 * n * d * 4 + 4 * d * tf * 2 + 2 * tf * d * 2 + n * d * 6 + 12 * n * tf * 4
    return pl.pallas_call(
        _ffn_step_kernel,
        grid=(1, nc),
        in_specs=[pl.BlockSpec((n, d), lambda i, c: (0, 0)),
                  pl.BlockSpec((1, d), lambda i, c: (0, 0)),
                  pl.BlockSpec((d, tf), lambda i, c: (0, c)),
                  pl.BlockSpec((d, tf), lambda i, c: (0, nc + c)),
                  pl.BlockSpec((CONV_W, tf), lambda i, c: (0, c)),
                  pl.BlockSpec((1, tf), lambda i, c: (0, c)),
                  pl.BlockSpec((tf, d), lambda i, c: (c, 0)),
                  pl.BlockSpec((n, tf), lambda i, c: (0, c)),
                  pl.BlockSpec((n, tf), lambda i, c: (0, nc + c))],
        out_specs=(pl.BlockSpec((n, d), lambda i, c: (0, 0)),
                   pl.BlockSpec((n, tf), lambda i, c: (0, c))),
        out_shape=(jax.ShapeDtypeStruct((n, d), F32),
                   jax.ShapeDtypeStruct((n, dff), F32)),
        scratch_shapes=[pltpu.VMEM((n, d), BF16), pltpu.VMEM((n, d), F32)],
        compiler_params=_params(("arbitrary", "arbitrary"), nbytes),
        name="ffn_step",
    )(x, g.reshape(1, d), w_up, w_up, w_conv, b_conv.reshape(1, dff), w_down, cache, cache)


def _ple_final_kernel(x_ref, p_ref, gp_ref, wg_ref, wp_ref, gf_ref, o_ref):
    x = x_ref[...]
    gate = jax.nn.sigmoid(_dot(_rmsnorm(x, gp_ref[...]).astype(BF16), wg_ref[...]))
    x = x + gate * _dot(p_ref[...].astype(BF16), wp_ref[...])
    o_ref[...] = _rmsnorm(x, gf_ref[...])


def _ple_final(x, p, g_ple, w_gate, w_ple, g_final, tm):
    n, d = x.shape
    dp = p.shape[1]
    nbytes = 2 * (2 * tm * d * 4 + tm * dp * 4) + d * d * 2 + dp * d * 2 + 3 * tm * d * 4
    return pl.pallas_call(
        _ple_final_kernel,
        grid=(n // tm,),
        in_specs=[pl.BlockSpec((tm, d), lambda i: (i, 0)),
                  pl.BlockSpec((tm, dp), lambda i: (i, 0)),
                  pl.BlockSpec((1, d), lambda i: (0, 0)),
                  pl.BlockSpec((d, d), lambda i: (0, 0), pipeline_mode=pl.Buffered(1)),
                  pl.BlockSpec((dp, d), lambda i: (0, 0), pipeline_mode=pl.Buffered(1)),
                  pl.BlockSpec((1, d), lambda i: (0, 0))],
        out_specs=pl.BlockSpec((tm, d), lambda i: (i, 0)),
        out_shape=jax.ShapeDtypeStruct((n, d), F32),
        compiler_params=_params(("arbitrary",), nbytes),
        name="ple_final",
    )(x, p, g_ple.reshape(1, d), w_gate, w_ple, g_final.reshape(1, d))


def kernel(x_prompt, x_sample, cache_pool, state_ssm_re, state_ssm_im, cache_conv, p_prompt, p_sample, g_mix, w_in, w_pool, pool_scale, ssm_lam_re, ssm_lam_im, ssm_log_dt, ssm_b_re, ssm_b_im, ssm_c_re, ssm_c_im, ssm_d, w_glu, w_branch_pool, w_branch_ssm, w_out, g_ffn, w_up, w_conv, b_conv, w_down, g_ple, w_ple_gate, w_ple, g_final):
    depth = g_mix.shape[0]
    nb, t_len, d = x_prompt.shape
    ns = x_sample.shape[0]
    assert x_sample.shape[1] == 1, "the sample group advances one step per call"
    pool_buf, d_pool = cache_pool.shape[2], cache_pool.shape[3]
    conv_buf, d_ff = cache_conv.shape[2], cache_conv.shape[3]
    n_grp, n_state = ssm_lam_re.shape[1], ssm_lam_re.shape[2]
    d_ssm = ssm_d.shape[1]
    assert pool_buf == max(POOL_WINDOWS) - 1 and conv_buf == CONV_W - 1
    assert POOL_HALO == SUBLANES * (max(POOL_WINDOWS).bit_length() - 1) and pool_buf <= POOL_HALO
    assert n_state == SSM_STATE and d_ssm == n_grp * SSM_GROUP

    xp = x_prompt.reshape(nb * t_len, d)
    xs = x_sample.reshape(ns, d)
    outs = [[] for _ in range(8)]
    for i in range(depth):
        w_pool_b, w_ple_b = w_pool[i].astype(BF16), w_ple[i].astype(BF16)
        bq, cq, a_pow_re, a_pow_im, a_row = _ssm_params(ssm_lam_re[i], ssm_lam_im[i], ssm_log_dt[i],
                                                        ssm_b_re[i], ssm_b_im[i], ssm_c_re[i], ssm_c_im[i])

        z_sample, w_in_b = _norm_matmul(xs, g_mix[i], w_in[i], ns, IN_COLS_F32)

        z = _norm_matmul(xp, g_mix[i], w_in_b, IN_ROWS, IN_COLS)
        gy, st_re, st_im, (w_up_b, w_down_b, w_out_b, w_pg_b, w_glu_b, w_bs_b, w_bp_b) = _ssm_seq(
            z, d_pool, bq, cq, a_pow_re, a_pow_im, ssm_d[i], nb, t_len,
            [w_up[i], w_down[i], w_out[i], w_ple_gate[i], w_glu[i], w_branch_ssm[i], w_branch_pool[i]])

        def mix_tail(x, z, bp, gy, tm):
            return _mix_out(gy, bp, z, d_pool + d_ssm, x, w_glu_b, w_bs_b, w_out_b, tm)

        bp, pool_new = _pool_seq(z, w_pool_b, pool_scale[i], w_bp_b, nb, t_len, POOL_ROWS, pool_buf)
        xp = mix_tail(xp, z, bp, gy, MIX_ROWS)
        xp, conv_new = _ffn_seq(xp, g_ffn[i], w_up_b, w_conv[i], b_conv[i], w_down_b, nb, t_len, FFN_ROWS, FFN_COLS,
                                conv_buf)
        xp_out = _ple_final(xp, p_prompt[i].reshape(nb * t_len, -1), g_ple[i], w_pg_b, w_ple_b, g_final, PLE_ROWS)
        for lst, val in zip(outs[:4], (pool_new, st_re.reshape(nb, n_grp, n_state),
                                       st_im.reshape(nb, n_grp, n_state), conv_new)):
            lst.append(val)

        z = z_sample
        bp, pool_new = _pool_step(z, cache_pool[i].reshape(ns, pool_buf * d_pool), w_pool_b, pool_scale[i], w_bp_b)
        gy, st_re, st_im = _ssm_step(z, d_pool, state_ssm_re[i].reshape(ns, -1), state_ssm_im[i].reshape(ns, -1),
                                     bq, cq, a_row, ssm_d[i])
        xs = mix_tail(xs, z, bp, gy, ns)
        xs, a_new = _ffn_step(xs, g_ffn[i], w_up_b, w_conv[i], b_conv[i], w_down_b,
                              cache_conv[i].reshape(ns, conv_buf * d_ff), FFN_STEP_COLS)
        conv_new = jnp.concatenate([cache_conv[i][:, 1:], a_new[:, None, :]], axis=1)
        xs_out = _ple_final(xs, p_sample[i].reshape(ns, -1), g_ple[i], w_pg_b, w_ple_b, g_final, ns)
        for lst, val in zip(outs[4:], (pool_new.reshape(ns, pool_buf, d_pool), st_re.reshape(ns, n_grp, n_state),
                                       st_im.reshape(ns, n_grp, n_state), conv_new)):
            lst.append(val)

    assert depth == 1
    y_prompt = xp_out.reshape(nb, t_len, d)
    y_sample = xs_out.reshape(ns, 1, d)
    return (y_prompt, y_sample) + tuple(jnp.stack(o, axis=0) for o in outs)
```
